```python
import math
import jax, jax.numpy as jnp
from jax import lax
import numpy as np

D_MODEL = 2048
BATCH = 8
SEQ = 4096
DEPTH = 4

N_META = 16
HEAD_DIM = 64
ATTN_WIDTH = D_MODEL // 2
N_HEADS = ATTN_WIDTH // HEAD_DIM
N_KV_HEADS = N_HEADS // 4
KV_GROUP = N_HEADS // N_KV_HEADS
KV_WIDTH = N_KV_HEADS * HEAD_DIM
SSM_WIDTH = D_MODEL - ATTN_WIDTH
SSM_GROUP_CH = 16
SSM_GROUPS = SSM_WIDTH // SSM_GROUP_CH
SSM_STATE = 64
WINDOW = 128
BLOCK = 128
PAD = BLOCK - N_META
D_FF = 4 * D_MODEL
IN_WIDTH = ATTN_WIDTH + 2 * KV_WIDTH + SSM_WIDTH
NORM_EPS = 1e-6
NEG_INF = -1e30
STEP_MIN = 1e-3
STEP_MAX = 1e-1

kernel_name = "hymba_s5_swa_alibi_trunk"


def _rmsnorm(x, g):
    xf = x.astype(jnp.float32)
    y = xf * lax.rsqrt(jnp.mean(xf * xf, axis=-1, keepdims=True) + NORM_EPS)
    return (y * g.astype(jnp.float32)).astype(x.dtype)


def _alibi_slopes():
    h = jnp.arange(1, N_HEADS + 1, dtype=jnp.float32)
    return jnp.exp2(-8.0 * h / N_HEADS)


def _sliding_window_attention(q, k, v, sinks):
    b, L = q.shape[0], q.shape[1]
    dtype = q.dtype
    Lp = L + PAD
    nb = Lp // BLOCK
    qf = q.astype(jnp.float32)
    kf = k.astype(jnp.float32)
    vf = v.astype(jnp.float32)
    pad4 = ((0, 0), (PAD, 0), (0, 0), (0, 0))
    qp = jnp.pad(qf, pad4).reshape(b, nb, BLOCK, N_KV_HEADS, KV_GROUP, HEAD_DIM)
    kp = jnp.pad(kf, pad4).reshape(b, nb, BLOCK, N_KV_HEADS, HEAD_DIM)
    vp = jnp.pad(vf, pad4).reshape(b, nb, BLOCK, N_KV_HEADS, HEAD_DIM)
    pad5 = ((0, 0), (1, 0), (0, 0), (0, 0), (0, 0))
    k_band = jnp.concatenate([jnp.pad(kp, pad5)[:, :-1], kp], axis=2)
    v_band = jnp.concatenate([jnp.pad(vp, pad5)[:, :-1], vp], axis=2)
    k_meta = kf[:, :N_META]
    v_meta = vf[:, :N_META]

    n_idx = jnp.arange(nb)[:, None, None]
    i_idx = jnp.arange(BLOCK)[None, :, None]
    j_idx = jnp.arange(2 * BLOCK)[None, None, :]
    t_pos = n_idx * BLOCK + i_idx - PAD
    s_pos = (n_idx - 1) * BLOCK + j_idx - PAD
    band_mask = (s_pos >= N_META) & (s_pos <= t_pos) & (t_pos - s_pos < WINDOW)
    band_dist = jnp.abs(t_pos - s_pos).astype(jnp.float32)
    m_pos = jnp.arange(N_META)[None, None, :]
    meta_mask = m_pos <= t_pos
    meta_dist = jnp.abs(t_pos - m_pos).astype(jnp.float32)

    slopes = _alibi_slopes().reshape(N_KV_HEADS, KV_GROUP, 1, 1)
    scale = 1.0 / math.sqrt(HEAD_DIM)
    s_band = jnp.einsum('bnqkgd,bnskd->bnkgqs', qp, k_band) * scale
    s_band = s_band - slopes * band_dist[:, None, None]
    s_band = jnp.where(band_mask[:, None, None], s_band, NEG_INF)
    s_meta = jnp.einsum('bnqkgd,bmkd->bnkgqm', qp, k_meta) * scale
    s_meta = s_meta - slopes * meta_dist[:, None, None]
    s_meta = jnp.where(meta_mask[:, None, None], s_meta, NEG_INF)
    sink = jnp.broadcast_to(sinks.astype(jnp.float32).reshape(N_KV_HEADS, KV_GROUP, 1, 1),
                            s_band.shape[:-1] + (1,))
    probs = jax.nn.softmax(jnp.concatenate([s_band, s_meta, sink], axis=-1), axis=-1)
    p_band = probs[..., :2 * BLOCK]
    p_meta = probs[..., 2 * BLOCK:2 * BLOCK + N_META]
    out = (jnp.einsum('bnkgqs,bnskd->bnqkgd', p_band, v_band)
           + jnp.einsum('bnkgqm,bmkd->bnqkgd', p_meta, v_meta))
    out = out.reshape(b, Lp, ATTN_WIDTH)[:, PAD:]
    return out.astype(dtype)


def _ssm_combine(e_i, e_j):
    a_i, b_i = e_i
    a_j, b_j = e_j
    return a_j * a_i, a_j * b_i + b_j


def _s5_mixer(u, lam_re, lam_im, log_step, b_re, b_im, c_re, c_im, d, w_glu, b_glu):
    dtype = u.dtype
    b, L = u.shape[0], u.shape[1]
    ul = jnp.moveaxis(u.astype(jnp.float32).reshape(b, L, SSM_GROUPS, SSM_GROUP_CH), 1, 0)
    lam = lax.complex(lam_re.astype(jnp.float32), lam_im.astype(jnp.float32))
    delta = jnp.exp(log_step.astype(jnp.float32))[:, None]
    lam_bar = jnp.exp(lam * delta)
    b_c = lax.complex(b_re.astype(jnp.float32), b_im.astype(jnp.float32))
    b_bar = ((lam_bar - 1.0) / lam)[..., None] * b_c
    c_c = lax.complex(c_re.astype(jnp.float32), c_im.astype(jnp.float32))
    bu = jnp.einsum('lbgh,gph->lbgp', ul.astype(jnp.complex64), b_bar)
    a = jnp.broadcast_to(lam_bar, (L, 1, SSM_GROUPS, SSM_STATE))
    _, states = lax.associative_scan(_ssm_combine, (a, bu), axis=0)
    y = jnp.real(jnp.einsum('lbgp,ghp->lbgh', states, c_c))
    y = y + d.astype(jnp.float32).reshape(SSM_GROUPS, SSM_GROUP_CH) * ul
    y = jnp.moveaxis(y, 0, 1).reshape(b, L, SSM_WIDTH)
    g = jax.nn.gelu(y)
    out = g * jax.nn.sigmoid(g @ w_glu.astype(jnp.float32) + b_glu.astype(jnp.float32))
    return out.astype(dtype)


def _fwd_setup_inputs(seed: int = 0) -> dict:
    key = jax.random.key(seed)
    ks = jax.random.split(key, 24)
    f32 = jnp.float32
    nrm = lambda k, shape, s: jax.random.normal(k, shape, f32) * s
    x = jax.random.normal(ks[0], (BATCH, SEQ, D_MODEL), f32)
    meta_tokens = nrm(ks[1], (N_META, D_MODEL), 1.0)
    norm_mix_g = 1.0 + nrm(ks[2], (DEPTH, D_MODEL), 0.02)
    w_in = nrm(ks[3], (DEPTH, D_MODEL, IN_WIDTH), D_MODEL ** -0.5)
    q_norm_g = 1.0 + nrm(ks[4], (DEPTH, HEAD_DIM), 0.02)
    k_norm_g = 1.0 + nrm(ks[5], (DEPTH, HEAD_DIM), 0.02)
    attn_sinks = nrm(ks[6], (DEPTH, N_HEADS), 0.5)
    n = jnp.arange(SSM_STATE, dtype=f32)
    ssm_lambda_re = -0.5 + nrm(ks[7], (DEPTH, SSM_GROUPS, SSM_STATE), 1e-3)
    ssm_lambda_im = math.pi * n + nrm(ks[8], (DEPTH, SSM_GROUPS, SSM_STATE), 1e-3)
    ssm_log_step = jax.random.uniform(ks[9], (DEPTH, SSM_GROUPS), f32,
                                      math.log(STEP_MIN), math.log(STEP_MAX))
    bs = (SSM_GROUP_CH ** -0.5) / math.sqrt(2.0)
    cs = (SSM_STATE ** -0.5) / math.sqrt(2.0)
    ssm_b_re = nrm(ks[10], (DEPTH, SSM_GROUPS, SSM_STATE, SSM_GROUP_CH), bs)
    ssm_b_im = nrm(ks[11], (DEPTH, SSM_GROUPS, SSM_STATE, SSM_GROUP_CH), bs)
    ssm_c_re = nrm(ks[12], (DEPTH, SSM_GROUPS, SSM_GROUP_CH, SSM_STATE), cs)
    ssm_c_im = nrm(ks[13], (DEPTH, SSM_GROUPS, SSM_GROUP_CH, SSM_STATE), cs)
    ssm_d = nrm(ks[14], (DEPTH, SSM_WIDTH), 1.0)
    w_glu = nrm(ks[15], (DEPTH, SSM_WIDTH, SSM_WIDTH), SSM_WIDTH ** -0.5)
    b_glu = nrm(ks[16], (DEPTH, SSM_WIDTH), 0.01)
    attn_out_g = 1.0 + nrm(ks[17], (DEPTH, ATTN_WIDTH), 0.02)
    ssm_out_g = 1.0 + nrm(ks[18], (DEPTH, SSM_WIDTH), 0.02)
    w_out = nrm(ks[19], (DEPTH, D_MODEL, D_MODEL), D_MODEL ** -0.5)
    norm_mlp_g = 1.0 + nrm(ks[20], (DEPTH, D_MODEL), 0.02)
    w_up = nrm(ks[21], (DEPTH, D_MODEL, D_FF), D_MODEL ** -0.5)
    w_down = nrm(ks[22], (DEPTH, D_FF, D_MODEL), D_FF ** -0.5)
    return {"x": x, "meta_tokens": meta_tokens, "norm_mix_g": norm_mix_g, "w_in": w_in,
            "q_norm_g": q_norm_g, "k_norm_g": k_norm_g, "attn_sinks": attn_sinks,
            "ssm_lambda_re": ssm_lambda_re, "ssm_lambda_im": ssm_lambda_im,
            "ssm_log_step": ssm_log_step, "ssm_b_re": ssm_b_re, "ssm_b_im": ssm_b_im,
            "ssm_c_re": ssm_c_re, "ssm_c_im": ssm_c_im, "ssm_d": ssm_d,
            "w_glu": w_glu, "b_glu": b_glu, "attn_out_g": attn_out_g, "ssm_out_g": ssm_out_g,
            "w_out": w_out, "norm_mlp_g": norm_mlp_g, "w_up": w_up, "w_down": w_down}


def _fwd_reference(x, meta_tokens, norm_mix_g, w_in, q_norm_g, k_norm_g, attn_sinks,
              ssm_lambda_re, ssm_lambda_im, ssm_log_step, ssm_b_re, ssm_b_im,
              ssm_c_re, ssm_c_im, ssm_d, w_glu, b_glu, attn_out_g, ssm_out_g,
              w_out, norm_mlp_g, w_up, w_down):
    b = x.shape[0]
    meta = jnp.broadcast_to(meta_tokens.astype(x.dtype)[None], (b, N_META, D_MODEL))
    h_res = jnp.concatenate([meta, x], axis=1)
    L = h_res.shape[1]
    for l in range(DEPTH):
        h = _rmsnorm(h_res, norm_mix_g[l])
        proj = h @ w_in[l]
        q = proj[..., :ATTN_WIDTH].reshape(b, L, N_HEADS, HEAD_DIM)
        k = proj[..., ATTN_WIDTH:ATTN_WIDTH + KV_WIDTH].reshape(b, L, N_KV_HEADS, HEAD_DIM)
        v = proj[..., ATTN_WIDTH + KV_WIDTH:ATTN_WIDTH + 2 * KV_WIDTH].reshape(b, L, N_KV_HEADS, HEAD_DIM)
        u = proj[..., ATTN_WIDTH + 2 * KV_WIDTH:]
        q = _rmsnorm(q, q_norm_g[l])
        k = _rmsnorm(k, k_norm_g[l])
        attn = _sliding_window_attention(q, k, v, attn_sinks[l])
        ssm = _s5_mixer(u, ssm_lambda_re[l], ssm_lambda_im[l], ssm_log_step[l],
                        ssm_b_re[l], ssm_b_im[l], ssm_c_re[l], ssm_c_im[l], ssm_d[l],
                        w_glu[l], b_glu[l])
        mix = jnp.concatenate([_rmsnorm(attn, attn_out_g[l]), _rmsnorm(ssm, ssm_out_g[l])], axis=-1)
        h_res = h_res + mix @ w_out[l]
        h2 = _rmsnorm(h_res, norm_mlp_g[l])
        h_res = h_res + jnp.square(jax.nn.relu(h2 @ w_up[l])) @ w_down[l]
    return h_res[:, N_META:]


import jax as _jax
import jax.numpy as _jnp

TWIN_FORMAT = 'train_step'
FWD_PARAMS = ['x', 'meta_tokens', 'norm_mix_g', 'w_in', 'q_norm_g', 'k_norm_g', 'attn_sinks', 'ssm_lambda_re', 'ssm_lambda_im', 'ssm_log_step', 'ssm_b_re', 'ssm_b_im', 'ssm_c_re', 'ssm_c_im', 'ssm_d', 'w_glu', 'b_glu', 'attn_out_g', 'ssm_out_g', 'w_out', 'norm_mlp_g', 'w_up', 'w_down']
TWIN_WEIGHTS = ['meta_tokens', 'norm_mix_g', 'w_in', 'q_norm_g', 'k_norm_g', 'attn_sinks', 'ssm_lambda_re', 'ssm_lambda_im', 'ssm_log_step', 'ssm_b_re', 'ssm_b_im', 'ssm_c_re', 'ssm_c_im', 'ssm_d', 'w_glu', 'b_glu', 'attn_out_g', 'ssm_out_g', 'w_out', 'norm_mlp_g', 'w_up', 'w_down']
TWIN_DIFF_INPUT = 'x'
TWIN_INPUTS = ['x', 'meta_tokens', 'norm_mix_g', 'w_in', 'q_norm_g', 'k_norm_g', 'attn_sinks', 'ssm_lambda_re', 'ssm_lambda_im', 'ssm_log_step', 'ssm_b_re', 'ssm_b_im', 'ssm_c_re', 'ssm_c_im', 'ssm_d', 'w_glu', 'b_glu', 'attn_out_g', 'ssm_out_g', 'w_out', 'norm_mlp_g', 'w_up', 'w_down', 'loss_target', 'm_meta_tokens', 'm_norm_mix_g', 'm_w_in', 'm_q_norm_g', 'm_k_norm_g', 'm_attn_sinks', 'm_ssm_lambda_re', 'm_ssm_lambda_im', 'm_ssm_log_step', 'm_ssm_b_re', 'm_ssm_b_im', 'm_ssm_c_re', 'm_ssm_c_im', 'm_ssm_d', 'm_w_glu', 'm_b_glu', 'm_attn_out_g', 'm_ssm_out_g', 'm_w_out', 'm_norm_mlp_g', 'm_w_up', 'm_w_down', 'v_meta_tokens', 'v_norm_mix_g', 'v_w_in', 'v_q_norm_g', 'v_k_norm_g', 'v_attn_sinks', 'v_ssm_lambda_re', 'v_ssm_lambda_im', 'v_ssm_log_step', 'v_ssm_b_re', 'v_ssm_b_im', 'v_ssm_c_re', 'v_ssm_c_im', 'v_ssm_d', 'v_w_glu', 'v_b_glu', 'v_attn_out_g', 'v_ssm_out_g', 'v_w_out', 'v_norm_mlp_g', 'v_w_up', 'v_w_down']
TWIN_OUTPUTS = ['loss', 'grad_x', 'grad_meta_tokens', 'grad_norm_mix_g', 'grad_w_in', 'grad_q_norm_g', 'grad_k_norm_g', 'grad_attn_sinks', 'grad_ssm_lambda_re', 'grad_ssm_lambda_im', 'grad_ssm_log_step', 'grad_ssm_b_re', 'grad_ssm_b_im', 'grad_ssm_c_re', 'grad_ssm_c_im', 'grad_ssm_d', 'grad_w_glu', 'grad_b_glu', 'grad_attn_out_g', 'grad_ssm_out_g', 'grad_w_out', 'grad_norm_mlp_g', 'grad_w_up', 'grad_w_down', 'delta_meta_tokens', 'delta_norm_mix_g', 'delta_w_in', 'delta_q_norm_g', 'delta_k_norm_g', 'delta_attn_sinks', 'delta_ssm_lambda_re', 'delta_ssm_lambda_im', 'delta_ssm_log_step', 'delta_ssm_b_re', 'delta_ssm_b_im', 'delta_ssm_c_re', 'delta_ssm_c_im', 'delta_ssm_d', 'delta_w_glu', 'delta_b_glu', 'delta_attn_out_g', 'delta_ssm_out_g', 'delta_w_out', 'delta_norm_mlp_g', 'delta_w_up', 'delta_w_down', 'new_m_meta_tokens', 'new_m_norm_mix_g', 'new_m_w_in', 'new_m_q_norm_g', 'new_m_k_norm_g', 'new_m_attn_sinks', 'new_m_ssm_lambda_re', 'new_m_ssm_lambda_im', 'new_m_ssm_log_step', 'new_m_ssm_b_re', 'new_m_ssm_b_im', 'new_m_ssm_c_re', 'new_m_ssm_c_im', 'new_m_ssm_d', 'new_m_w_glu', 'new_m_b_glu', 'new_m_attn_out_g', 'new_m_ssm_out_g', 'new_m_w_out', 'new_m_norm_mlp_g', 'new_m_w_up', 'new_m_w_down', 'new_v_meta_tokens', 'new_v_norm_mix_g', 'new_v_w_in', 'new_v_q_norm_g', 'new_v_k_norm_g', 'new_v_attn_sinks', 'new_v_ssm_lambda_re', 'new_v_ssm_lambda_im', 'new_v_ssm_log_step', 'new_v_ssm_b_re', 'new_v_ssm_b_im', 'new_v_ssm_c_re', 'new_v_ssm_c_im', 'new_v_ssm_d', 'new_v_w_glu', 'new_v_b_glu', 'new_v_attn_out_g', 'new_v_ssm_out_g', 'new_v_w_out', 'new_v_norm_mlp_g', 'new_v_w_up', 'new_v_w_down']
TWIN_LEAF_KINDS = {'loss': 'loss', 'grad_x': 'grad_x', 'grad_meta_tokens': 'grad_w', 'grad_norm_mix_g': 'grad_w', 'grad_w_in': 'grad_w', 'grad_q_norm_g': 'grad_w', 'grad_k_norm_g': 'grad_w', 'grad_attn_sinks': 'grad_w', 'grad_ssm_lambda_re': 'grad_w', 'grad_ssm_lambda_im': 'grad_w', 'grad_ssm_log_step': 'grad_w', 'grad_ssm_b_re': 'grad_w', 'grad_ssm_b_im': 'grad_w', 'grad_ssm_c_re': 'grad_w', 'grad_ssm_c_im': 'grad_w', 'grad_ssm_d': 'grad_w', 'grad_w_glu': 'grad_w', 'grad_b_glu': 'grad_w', 'grad_attn_out_g': 'grad_w', 'grad_ssm_out_g': 'grad_w', 'grad_w_out': 'grad_w', 'grad_norm_mlp_g': 'grad_w', 'grad_w_up': 'grad_w', 'grad_w_down': 'grad_w', 'delta_meta_tokens': 'delta_w', 'delta_norm_mix_g': 'delta_w', 'delta_w_in': 'delta_w', 'delta_q_norm_g': 'delta_w', 'delta_k_norm_g': 'delta_w', 'delta_attn_sinks': 'delta_w', 'delta_ssm_lambda_re': 'delta_w', 'delta_ssm_lambda_im': 'delta_w', 'delta_ssm_log_step': 'delta_w', 'delta_ssm_b_re': 'delta_w', 'delta_ssm_b_im': 'delta_w', 'delta_ssm_c_re': 'delta_w', 'delta_ssm_c_im': 'delta_w', 'delta_ssm_d': 'delta_w', 'delta_w_glu': 'delta_w', 'delta_b_glu': 'delta_w', 'delta_attn_out_g': 'delta_w', 'delta_ssm_out_g': 'delta_w', 'delta_w_out': 'delta_w', 'delta_norm_mlp_g': 'delta_w', 'delta_w_up': 'delta_w', 'delta_w_down': 'delta_w', 'new_m_meta_tokens': 'new_m', 'new_m_norm_mix_g': 'new_m', 'new_m_w_in': 'new_m', 'new_m_q_norm_g': 'new_m', 'new_m_k_norm_g': 'new_m', 'new_m_attn_sinks': 'new_m', 'new_m_ssm_lambda_re': 'new_m', 'new_m_ssm_lambda_im': 'new_m', 'new_m_ssm_log_step': 'new_m', 'new_m_ssm_b_re': 'new_m', 'new_m_ssm_b_im': 'new_m', 'new_m_ssm_c_re': 'new_m', 'new_m_ssm_c_im': 'new_m', 'new_m_ssm_d': 'new_m', 'new_m_w_glu': 'new_m', 'new_m_b_glu': 'new_m', 'new_m_attn_out_g': 'new_m', 'new_m_ssm_out_g': 'new_m', 'new_m_w_out': 'new_m', 'new_m_norm_mlp_g': 'new_m', 'new_m_w_up': 'new_m', 'new_m_w_down': 'new_m', 'new_v_meta_tokens': 'new_v', 'new_v_norm_mix_g': 'new_v', 'new_v_w_in': 'new_v', 'new_v_q_norm_g': 'new_v', 'new_v_k_norm_g': 'new_v', 'new_v_attn_sinks': 'new_v', 'new_v_ssm_lambda_re': 'new_v', 'new_v_ssm_lambda_im': 'new_v', 'new_v_ssm_log_step': 'new_v', 'new_v_ssm_b_re': 'new_v', 'new_v_ssm_b_im': 'new_v', 'new_v_ssm_c_re': 'new_v', 'new_v_ssm_c_im': 'new_v', 'new_v_ssm_d': 'new_v', 'new_v_w_glu': 'new_v', 'new_v_b_glu': 'new_v', 'new_v_attn_out_g': 'new_v', 'new_v_ssm_out_g': 'new_v', 'new_v_w_out': 'new_v', 'new_v_norm_mlp_g': 'new_v', 'new_v_w_up': 'new_v', 'new_v_w_down': 'new_v'}


def _forward(args):
    return _fwd_reference(*[args[k] for k in FWD_PARAMS])


def _output_shape():
    def fwd():
        inp = _fwd_setup_inputs(0)
        return _fwd_reference(*[inp[k] for k in FWD_PARAMS])
    out = _jax.eval_shape(fwd)
    return out.shape, out.dtype

N_MICROBATCH = 1
ADAM_LR = 0.001
ADAM_B1 = 0.9
ADAM_B2 = 0.999
ADAM_EPS = 1e-08
ADAM_WD = 0.01
ADAM_STEP = 10
PER_EXAMPLE_BATCH_AXIS = {'x': 0, 'loss_target': 0}
SHARED_INPUTS = []
_WEIGHT_DTYPES = {'meta_tokens': _jnp.float32, 'norm_mix_g': _jnp.float32, 'w_in': _jnp.float32, 'q_norm_g': _jnp.float32, 'k_norm_g': _jnp.float32, 'attn_sinks': _jnp.float32, 'ssm_lambda_re': _jnp.float32, 'ssm_lambda_im': _jnp.float32, 'ssm_log_step': _jnp.float32, 'ssm_b_re': _jnp.float32, 'ssm_b_im': _jnp.float32, 'ssm_c_re': _jnp.float32, 'ssm_c_im': _jnp.float32, 'ssm_d': _jnp.float32, 'w_glu': _jnp.float32, 'b_glu': _jnp.float32, 'attn_out_g': _jnp.float32, 'ssm_out_g': _jnp.float32, 'w_out': _jnp.float32, 'norm_mlp_g': _jnp.float32, 'w_up': _jnp.float32, 'w_down': _jnp.float32}
MOMENT_SCALE = {'meta_tokens': 3.975365e-01, 'norm_mix_g': 1.922782e+01, 'w_in': 1.711809e+01, 'q_norm_g': 2.923476e+00, 'k_norm_g': 2.936343e+00, 'attn_sinks': 1.810741e+01, 'ssm_lambda_re': 6.531402e-01, 'ssm_lambda_im': 6.402935e-01, 'ssm_log_step': 2.583443e+01, 'ssm_b_re': 5.409112e-01, 'ssm_b_im': 5.489312e-01, 'ssm_c_re': 1.016381e+00, 'ssm_c_im': 9.845825e-01, 'ssm_d': 2.544158e+01, 'w_glu': 3.923010e+00, 'b_glu': 1.069171e+01, 'attn_out_g': 3.324728e+01, 'ssm_out_g': 4.149446e+01, 'w_out': 2.452004e+01, 'norm_mlp_g': 5.263381e+01, 'w_up': 7.947960e+00, 'w_down': 2.516365e+01}


def _to_microbatches(a, axis):
    t = _jnp.moveaxis(a, axis, 0)
    t = t.reshape((N_MICROBATCH, t.shape[0] // N_MICROBATCH) + t.shape[1:])
    return _jnp.moveaxis(t, 1, axis + 1)


def setup_inputs(seed: int = 0) -> dict:
    inp = _fwd_setup_inputs(seed)
    key = _jax.random.fold_in(_jax.random.key(seed), 7919)
    shape, _ = _output_shape()
    out = dict(inp)
    out["loss_target"] = _jax.random.normal(_jax.random.fold_in(key, 0), shape, _jnp.float32)
    for i, name in enumerate(TWIN_WEIGHTS):
        w = inp[name].astype(_jnp.float32)
        if MOMENT_SCALE is None:
            s = _jnp.sqrt(_jnp.mean(_jnp.square(w)) + 1e-30)
        else:
            s = MOMENT_SCALE[name]
        km, kv = _jax.random.split(_jax.random.fold_in(key, i + 1))
        out[name] = w
        out["m_" + name] = s * _jax.random.normal(km, w.shape, _jnp.float32)
        out["v_" + name] = (s * s) * _jax.random.uniform(kv, w.shape, _jnp.float32, 0.5, 1.5)
    if N_MICROBATCH > 1:
        for name, axis in PER_EXAMPLE_BATCH_AXIS.items():
            out[name] = _to_microbatches(out[name], axis)
    return {'x': out['x'], 'meta_tokens': out['meta_tokens'], 'norm_mix_g': out['norm_mix_g'], 'w_in': out['w_in'], 'q_norm_g': out['q_norm_g'], 'k_norm_g': out['k_norm_g'], 'attn_sinks': out['attn_sinks'], 'ssm_lambda_re': out['ssm_lambda_re'], 'ssm_lambda_im': out['ssm_lambda_im'], 'ssm_log_step': out['ssm_log_step'], 'ssm_b_re': out['ssm_b_re'], 'ssm_b_im': out['ssm_b_im'], 'ssm_c_re': out['ssm_c_re'], 'ssm_c_im': out['ssm_c_im'], 'ssm_d': out['ssm_d'], 'w_glu': out['w_glu'], 'b_glu': out['b_glu'], 'attn_out_g': out['attn_out_g'], 'ssm_out_g': out['ssm_out_g'], 'w_out': out['w_out'], 'norm_mlp_g': out['norm_mlp_g'], 'w_up': out['w_up'], 'w_down': out['w_down'], 'loss_target': out['loss_target'], 'm_meta_tokens': out['m_meta_tokens'], 'm_norm_mix_g': out['m_norm_mix_g'], 'm_w_in': out['m_w_in'], 'm_q_norm_g': out['m_q_norm_g'], 'm_k_norm_g': out['m_k_norm_g'], 'm_attn_sinks': out['m_attn_sinks'], 'm_ssm_lambda_re': out['m_ssm_lambda_re'], 'm_ssm_lambda_im': out['m_ssm_lambda_im'], 'm_ssm_log_step': out['m_ssm_log_step'], 'm_ssm_b_re': out['m_ssm_b_re'], 'm_ssm_b_im': out['m_ssm_b_im'], 'm_ssm_c_re': out['m_ssm_c_re'], 'm_ssm_c_im': out['m_ssm_c_im'], 'm_ssm_d': out['m_ssm_d'], 'm_w_glu': out['m_w_glu'], 'm_b_glu': out['m_b_glu'], 'm_attn_out_g': out['m_attn_out_g'], 'm_ssm_out_g': out['m_ssm_out_g'], 'm_w_out': out['m_w_out'], 'm_norm_mlp_g': out['m_norm_mlp_g'], 'm_w_up': out['m_w_up'], 'm_w_down': out['m_w_down'], 'v_meta_tokens': out['v_meta_tokens'], 'v_norm_mix_g': out['v_norm_mix_g'], 'v_w_in': out['v_w_in'], 'v_q_norm_g': out['v_q_norm_g'], 'v_k_norm_g': out['v_k_norm_g'], 'v_attn_sinks': out['v_attn_sinks'], 'v_ssm_lambda_re': out['v_ssm_lambda_re'], 'v_ssm_lambda_im': out['v_ssm_lambda_im'], 'v_ssm_log_step': out['v_ssm_log_step'], 'v_ssm_b_re': out['v_ssm_b_re'], 'v_ssm_b_im': out['v_ssm_b_im'], 'v_ssm_c_re': out['v_ssm_c_re'], 'v_ssm_c_im': out['v_ssm_c_im'], 'v_ssm_d': out['v_ssm_d'], 'v_w_glu': out['v_w_glu'], 'v_b_glu': out['v_b_glu'], 'v_attn_out_g': out['v_attn_out_g'], 'v_ssm_out_g': out['v_ssm_out_g'], 'v_w_out': out['v_w_out'], 'v_norm_mlp_g': out['v_norm_mlp_g'], 'v_w_up': out['v_w_up'], 'v_w_down': out['v_w_down']}


def _loss(weights, diff, rest, loss_target):
    with _jax.named_scope("forward"):
        args = {**rest, TWIN_DIFF_INPUT: diff, **{k: w.astype(_WEIGHT_DTYPES[k]) for k, w in weights.items()}}
        y = _forward(args)
    with _jax.named_scope("loss_head"):
        err = _jnp.square(y.astype(_jnp.float32) - loss_target)
        return 0.5 * _jnp.sum(_jnp.mean(err, axis=-1)) if err.ndim else 0.5 * err


def _adamw(w, g, m, v):
    m = ADAM_B1 * m + (1.0 - ADAM_B1) * g
    v = ADAM_B2 * v + (1.0 - ADAM_B2) * _jnp.square(g)
    m_hat = m / (1.0 - ADAM_B1 ** ADAM_STEP)
    v_hat = v / (1.0 - ADAM_B2 ** ADAM_STEP)
    delta = -ADAM_LR * (m_hat / (_jnp.sqrt(v_hat) + ADAM_EPS) + ADAM_WD * w)
    return delta, m, v


def reference(x, meta_tokens, norm_mix_g, w_in, q_norm_g, k_norm_g, attn_sinks, ssm_lambda_re, ssm_lambda_im, ssm_log_step, ssm_b_re, ssm_b_im, ssm_c_re, ssm_c_im, ssm_d, w_glu, b_glu, attn_out_g, ssm_out_g, w_out, norm_mlp_g, w_up, w_down, loss_target, m_meta_tokens, m_norm_mix_g, m_w_in, m_q_norm_g, m_k_norm_g, m_attn_sinks, m_ssm_lambda_re, m_ssm_lambda_im, m_ssm_log_step, m_ssm_b_re, m_ssm_b_im, m_ssm_c_re, m_ssm_c_im, m_ssm_d, m_w_glu, m_b_glu, m_attn_out_g, m_ssm_out_g, m_w_out, m_norm_mlp_g, m_w_up, m_w_down, v_meta_tokens, v_norm_mix_g, v_w_in, v_q_norm_g, v_k_norm_g, v_attn_sinks, v_ssm_lambda_re, v_ssm_lambda_im, v_ssm_log_step, v_ssm_b_re, v_ssm_b_im, v_ssm_c_re, v_ssm_c_im, v_ssm_d, v_w_glu, v_b_glu, v_attn_out_g, v_ssm_out_g, v_w_out, v_norm_mlp_g, v_w_up, v_w_down):
    given = dict(x=x, meta_tokens=meta_tokens, norm_mix_g=norm_mix_g, w_in=w_in, q_norm_g=q_norm_g, k_norm_g=k_norm_g, attn_sinks=attn_sinks, ssm_lambda_re=ssm_lambda_re, ssm_lambda_im=ssm_lambda_im, ssm_log_step=ssm_log_step, ssm_b_re=ssm_b_re, ssm_b_im=ssm_b_im, ssm_c_re=ssm_c_re, ssm_c_im=ssm_c_im, ssm_d=ssm_d, w_glu=w_glu, b_glu=b_glu, attn_out_g=attn_out_g, ssm_out_g=ssm_out_g, w_out=w_out, norm_mlp_g=norm_mlp_g, w_up=w_up, w_down=w_down, loss_target=loss_target, m_meta_tokens=m_meta_tokens, m_norm_mix_g=m_norm_mix_g, m_w_in=m_w_in, m_q_norm_g=m_q_norm_g, m_k_norm_g=m_k_norm_g, m_attn_sinks=m_attn_sinks, m_ssm_lambda_re=m_ssm_lambda_re, m_ssm_lambda_im=m_ssm_lambda_im, m_ssm_log_step=m_ssm_log_step, m_ssm_b_re=m_ssm_b_re, m_ssm_b_im=m_ssm_b_im, m_ssm_c_re=m_ssm_c_re, m_ssm_c_im=m_ssm_c_im, m_ssm_d=m_ssm_d, m_w_glu=m_w_glu, m_b_glu=m_b_glu, m_attn_out_g=m_attn_out_g, m_ssm_out_g=m_ssm_out_g, m_w_out=m_w_out, m_norm_mlp_g=m_norm_mlp_g, m_w_up=m_w_up, m_w_down=m_w_down, v_meta_tokens=v_meta_tokens, v_norm_mix_g=v_norm_mix_g, v_w_in=v_w_in, v_q_norm_g=v_q_norm_g, v_k_norm_g=v_k_norm_g, v_attn_sinks=v_attn_sinks, v_ssm_lambda_re=v_ssm_lambda_re, v_ssm_lambda_im=v_ssm_lambda_im, v_ssm_log_step=v_ssm_log_step, v_ssm_b_re=v_ssm_b_re, v_ssm_b_im=v_ssm_b_im, v_ssm_c_re=v_ssm_c_re, v_ssm_c_im=v_ssm_c_im, v_ssm_d=v_ssm_d, v_w_glu=v_w_glu, v_b_glu=v_b_glu, v_attn_out_g=v_attn_out_g, v_ssm_out_g=v_ssm_out_g, v_w_out=v_w_out, v_norm_mlp_g=v_norm_mlp_g, v_w_up=v_w_up, v_w_down=v_w_down)
    weights = {n: given[n] for n in TWIN_WEIGHTS}
    shared = {n: given[n] for n in SHARED_INPUTS}
    per_example = {n: given[n] for n in ['x']}
    grad_fn = _jax.value_and_grad(_loss, argnums=(0, 1))

    def one_microbatch(ex, loss_target):
        ex = dict(ex)
        diff = ex.pop(TWIN_DIFF_INPUT)
        return grad_fn(weights, diff, {**shared, **ex}, loss_target)

    if N_MICROBATCH == 1:
        loss, (grad_w, grad_x) = one_microbatch(per_example, given["loss_target"])
    else:
        def body(carry, xs):
            loss_sum, grad_sum = carry
            l_k, (gw_k, gx_k) = one_microbatch(xs[0], xs[1])
            with _jax.named_scope("update"):
                return (loss_sum + l_k, _jax.tree.map(_jnp.add, grad_sum, gw_k)), gx_k

        init = (_jnp.zeros((), _jnp.float32), _jax.tree.map(_jnp.zeros_like, weights))
        (loss, grad_w), grad_x = _jax.lax.scan(body, init, (per_example, given["loss_target"]))
    with _jax.named_scope("update"):
        delta_w, new_m, new_v = {}, {}, {}
        for n in TWIN_WEIGHTS:
            delta_w[n], new_m[n], new_v[n] = _adamw(weights[n], grad_w[n], given["m_" + n], given["v_" + n])
    return (loss, grad_x, *[grad_w[n] for n in TWIN_WEIGHTS], *[delta_w[n] for n in TWIN_WEIGHTS],
            *[new_m[n] for n in TWIN_WEIGHTS], *[new_v[n] for n in TWIN_WEIGHTS])
```

```python
import functools
import math

import jax
import jax.numpy as jnp
from jax import lax
from jax.experimental import pallas as pl
from jax.experimental.pallas import tpu as pltpu

F32 = jnp.float32
BF16 = jnp.bfloat16

N_DEV = 8
D_MODEL = 2048
SEQ = 4096
DEPTH = 4
N_META = 16
HEAD_DIM = 64
ATTN_WIDTH = D_MODEL // 2
N_HEADS = ATTN_WIDTH // HEAD_DIM
N_KV_HEADS = N_HEADS // 4
KV_GROUP = N_HEADS // N_KV_HEADS
KV_WIDTH = N_KV_HEADS * HEAD_DIM
SSM_WIDTH = D_MODEL - ATTN_WIDTH
SSM_GROUP_CH = 16
SSM_GROUPS = SSM_WIDTH // SSM_GROUP_CH
SSM_STATE = 64
WINDOW = 128
BLOCK = 128
PAD = BLOCK - N_META
D_FF = 4 * D_MODEL
IN_WIDTH = ATTN_WIDTH + 2 * KV_WIDTH + SSM_WIDTH
NORM_EPS = 1e-6
NEG_INF = -1e30
ADAM_LR = 0.001
ADAM_B1 = 0.9
ADAM_B2 = 0.999
ADAM_EPS = 1e-08
ADAM_WD = 0.01
ADAM_STEP = 10

VMEM_LIMIT = 56 * 1024 * 1024


def _cparams(sem=None):
    return pltpu.CompilerParams(dimension_semantics=sem, vmem_limit_bytes=VMEM_LIMIT)


def _matmul(a, b, *, mode, tm, tn, tk, outs, epilogue, tiles=(), rows=(), name):
    if mode == "nn":
        (m, k), n = a.shape, b.shape[1]
        a_spec = pl.BlockSpec((tm, tk), lambda i, j, kk: (i, kk))
        b_spec = pl.BlockSpec((tk, tn), lambda i, j, kk: (kk, j))
        dims = (((1,), (0,)), ((), ()))
    elif mode == "nt":
        (m, k), n = a.shape, b.shape[0]
        a_spec = pl.BlockSpec((tm, tk), lambda i, j, kk: (i, kk))
        b_spec = pl.BlockSpec((tn, tk), lambda i, j, kk: (j, kk))
        dims = (((1,), (1,)), ((), ()))
    else:
        (k, m), n = a.shape, b.shape[1]
        a_spec = pl.BlockSpec((tk, tm), lambda i, j, kk: (kk, i))
        b_spec = pl.BlockSpec((tk, tn), lambda i, j, kk: (kk, j))
        dims = (((0,), (0,)), ((), ()))
    assert m % tm == 0 and n % tn == 0 and k % tk == 0, (name, m, n, k, tm, tn, tk)
    nk = k // tk
    n_tiles, n_rows, n_outs = len(tiles), len(rows), len(outs)

    def body(a_ref, b_ref, *rest):
        tile_refs = rest[:n_tiles]
        row_refs = rest[n_tiles:n_tiles + n_rows]
        out_refs = rest[n_tiles + n_rows:n_tiles + n_rows + n_outs]
        acc_ref = rest[-1]
        kk = pl.program_id(2)

        @pl.when(kk == 0)
        def _():
            acc_ref[...] = jnp.zeros_like(acc_ref)

        acc_ref[...] += lax.dot_general(a_ref[...].astype(BF16), b_ref[...].astype(BF16), dims,
                                        preferred_element_type=F32)

        @pl.when(kk == nk - 1)
        def _():
            res = epilogue(acc_ref[...], *[r[...] for r in tile_refs], *[r[...] for r in row_refs])
            for o_ref, o in zip(out_refs, res):
                o_ref[...] = o.astype(o_ref.dtype)

    tile_spec = pl.BlockSpec((tm, tn), lambda i, j, kk: (i, j))
    row_spec = pl.BlockSpec((1, tn), lambda i, j, kk: (0, j))
    return pl.pallas_call(
        body, name=name, grid=(m // tm, n // tn, nk),
        in_specs=[a_spec, b_spec] + [tile_spec] * n_tiles + [row_spec] * n_rows,
        out_specs=[tile_spec] * n_outs,
        out_shape=[jax.ShapeDtypeStruct((m, n), dt) for dt in outs],
        scratch_shapes=[pltpu.VMEM((tm, tn), F32)],
        compiler_params=_cparams(("parallel", "parallel", "arbitrary")),
    )(a, b, *tiles, *rows)


def _ident(acc):
    return (acc,)


def _row_tile(n_rows, cap):
    best = BLOCK
    for t in range(BLOCK, cap + 1, BLOCK):
        if n_rows % t == 0:
            best = t
    return best


def _rmsnorm_fwd(xs, gs, *, name):
    n_rows, width = xs[0].shape
    n = len(xs)
    tr = _row_tile(n_rows, 384)

    def body(*refs):
        o_ref = refs[-1]
        parts = []
        for x_ref, g_ref in zip(refs[:n], refs[n:2 * n]):
            x = x_ref[...]
            r = lax.rsqrt(jnp.mean(x * x, axis=-1, keepdims=True) + NORM_EPS)
            parts.append(x * r * g_ref[...])
        o_ref[...] = (parts[0] if n == 1 else jnp.concatenate(parts, axis=1)).astype(BF16)

    return pl.pallas_call(
        body, name=name, grid=(n_rows // tr,),
        in_specs=[pl.BlockSpec((tr, width), lambda i: (i, 0))] * n + [pl.BlockSpec((1, width), lambda i: (0, 0))] * n,
        out_specs=pl.BlockSpec((tr, n * width), lambda i: (i, 0)),
        out_shape=jax.ShapeDtypeStruct((n_rows, n * width), BF16),
        compiler_params=_cparams(("parallel",)),
    )(*xs, *gs)


def _rmsnorm_bwd(xs, gs, dy, res, *, name):
    n_rows, width = xs[0].shape
    n = len(xs)
    tr = _row_tile(n_rows, 384)
    has_res = res is not None

    def body(*refs):
        x_refs, g_refs, dy_ref = refs[:n], refs[n:2 * n], refs[2 * n]
        res_ref = refs[2 * n + 1] if has_res else None
        outs = refs[2 * n + 1 + int(has_res):]
        dx_refs, dg_refs = outs[:n], outs[n:]
        i = pl.program_id(0)
        for c in range(n):
            x = x_refs[c][...]
            d = dy_ref[:, c * width:(c + 1) * width]
            r = lax.rsqrt(jnp.mean(x * x, axis=-1, keepdims=True) + NORM_EPS)
            xh = x * r
            gd = d * g_refs[c][...]
            dx = r * (gd - xh * jnp.mean(gd * xh, axis=-1, keepdims=True))
            if has_res:
                dx = dx + res_ref[...]
            dx_refs[c][...] = dx
            part = jnp.sum(d * xh, axis=0, keepdims=True)

            @pl.when(i == 0)
            def _():
                dg_refs[c][...] = part

            @pl.when(i > 0)
            def _():
                dg_refs[c][...] += part

    row_spec = pl.BlockSpec((tr, width), lambda i: (i, 0))
    vec_spec = pl.BlockSpec((1, width), lambda i: (0, 0))
    outs = pl.pallas_call(
        body, name=name, grid=(n_rows // tr,),
        in_specs=[row_spec] * n + [vec_spec] * n + [pl.BlockSpec((tr, n * width), lambda i: (i, 0))] + [row_spec] * int(has_res),
        out_specs=[row_spec] * n + [vec_spec] * n,
        out_shape=[jax.ShapeDtypeStruct((n_rows, width), F32)] * n + [jax.ShapeDtypeStruct((1, width), F32)] * n,
        compiler_params=_cparams(("arbitrary",)),
    )(*xs, *gs, dy, *([res] if has_res else []))
    return outs[:n], outs[n:]


_SCALE = 1.0 / math.sqrt(HEAD_DIM)
_DN_NT = (((1,), (1,)), ((), ()))
_DN_TN = (((0,), (0,)), ((), ()))


def _head_norm(x, g):
    r = lax.rsqrt(jnp.mean(x * x, axis=-1, keepdims=True) + NORM_EPS)
    return x * r * g, r


def _attn_geometry(n):
    rows = KV_GROUP * BLOCK
    i = lax.broadcasted_iota(jnp.int32, (rows, 3 * BLOCK), 0) % BLOCK
    j = lax.broadcasted_iota(jnp.int32, (rows, 3 * BLOCK), 1)
    t_pos = n * BLOCK + i - PAD
    is_meta = j < BLOCK
    m_pos = j - PAD
    s_pos = (n - 1) * BLOCK + (j - BLOCK) - PAD
    meta_ok = (j >= PAD) & (m_pos <= t_pos)
    band_ok = (s_pos >= N_META) & (s_pos <= t_pos) & (t_pos - s_pos < WINDOW)
    valid = (is_meta & meta_ok) | (jnp.logical_not(is_meta) & band_ok)
    dist = jnp.abs(t_pos - jnp.where(is_meta, m_pos, s_pos)).astype(F32)
    return valid, dist


def _slope_col(kv):
    g = lax.broadcasted_iota(jnp.int32, (KV_GROUP * BLOCK, 1), 0) // BLOCK
    col = jnp.zeros((KV_GROUP * BLOCK, 1), F32)
    for gi in range(KV_GROUP):
        col = jnp.where(g == gi, 2.0 ** (-8.0 * (kv * KV_GROUP + gi + 1) / N_HEADS), col)
    return col


def _sink_col(sink_ref, kv):
    g = lax.broadcasted_iota(jnp.int32, (KV_GROUP * BLOCK, 1), 0) // BLOCK
    col = jnp.zeros((KV_GROUP * BLOCK, 1), F32)
    for gi in range(KV_GROUP):
        h = kv * KV_GROUP + gi
        col = jnp.where(g == gi, sink_ref[0:1, h:h + 1], col)
    return col


def _stack_heads(x, kv):
    return jnp.concatenate([x[:, (kv * KV_GROUP + g) * HEAD_DIM:(kv * KV_GROUP + g + 1) * HEAD_DIM]
                            for g in range(KV_GROUP)], axis=0)


def _attn_scores(q_ref, k_refs, gq_ref, gk_ref, sink_ref, kv, valid, dist):
    qs = _stack_heads(q_ref[...], kv)
    kcat = jnp.concatenate([r[:, kv * HEAD_DIM:(kv + 1) * HEAD_DIM] for r in k_refs], axis=0)
    qn, rq = _head_norm(qs, gq_ref[...])
    kn, rk = _head_norm(kcat, gk_ref[...])
    s = lax.dot_general(qn.astype(BF16), kn.astype(BF16), _DN_NT, preferred_element_type=F32) * _SCALE
    s = jnp.where(valid, s - _slope_col(kv) * dist, NEG_INF)
    return qs, kcat, qn, kn, rq, rk, s, _sink_col(sink_ref, kv)


def _attn_specs():
    kq = ATTN_WIDTH // KV_WIDTH
    q_spec = pl.BlockSpec((BLOCK, ATTN_WIDTH), lambda n: (n, 0))
    kv_specs = []
    for col in (kq, kq + 1):
        kv_specs += [pl.BlockSpec((BLOCK, KV_WIDTH), lambda n, col=col: (0, col)),
                     pl.BlockSpec((BLOCK, KV_WIDTH), lambda n, col=col: (jnp.maximum(n - 1, 0), col)),
                     pl.BlockSpec((BLOCK, KV_WIDTH), lambda n, col=col: (n, col))]
    small = [pl.BlockSpec((1, HEAD_DIM), lambda n: (0, 0)), pl.BlockSpec((1, HEAD_DIM), lambda n: (0, 0)),
             pl.BlockSpec((1, N_HEADS), lambda n: (0, 0))]
    return q_spec, kv_specs, small


def _attn_fwd(proj, gq, gk, sinks, *, name):
    n_rows = proj.shape[0]
    q_spec, kv_specs, small = _attn_specs()

    def body(q_ref, k0, k1, k2, v0, v1, v2, gq_ref, gk_ref, sink_ref, o_ref, lse_ref):
        valid, dist = _attn_geometry(pl.program_id(0))
        o_parts, lse_parts = [], []
        for kv in range(N_KV_HEADS):
            _, _, _, _, _, _, s, sink = _attn_scores(q_ref, (k0, k1, k2), gq_ref, gk_ref, sink_ref, kv, valid, dist)
            vcat = jnp.concatenate([r[:, kv * HEAD_DIM:(kv + 1) * HEAD_DIM] for r in (v0, v1, v2)], axis=0)
            m = jnp.maximum(jnp.max(s, axis=-1, keepdims=True), sink)
            p = jnp.exp(s - m)
            l = jnp.sum(p, axis=-1, keepdims=True) + jnp.exp(sink - m)
            o = jnp.dot(p.astype(BF16), vcat.astype(BF16), preferred_element_type=F32) / l
            lse = m + jnp.log(l)
            o_parts += [o[g * BLOCK:(g + 1) * BLOCK] for g in range(KV_GROUP)]
            lse_parts += [lse[g * BLOCK:(g + 1) * BLOCK] for g in range(KV_GROUP)]
        o_ref[...] = jnp.concatenate(o_parts, axis=1)
        lse_ref[...] = jnp.concatenate(lse_parts, axis=1)

    return pl.pallas_call(
        body, name=name, grid=(n_rows // BLOCK,),
        in_specs=[q_spec] + kv_specs + small,
        out_specs=[pl.BlockSpec((BLOCK, ATTN_WIDTH), lambda n: (n, 0)), pl.BlockSpec((BLOCK, N_HEADS), lambda n: (n, 0))],
        out_shape=[jax.ShapeDtypeStruct((n_rows, ATTN_WIDTH), F32), jax.ShapeDtypeStruct((n_rows, N_HEADS), F32)],
        compiler_params=_cparams(("parallel",)),
    )(proj, proj, proj, proj, proj, proj, proj, gq, gk, sinks)


def _attn_bwd(proj, gq, gk, sinks, o, lse, do, *, name):
    n_rows = proj.shape[0]
    q_spec, kv_specs, small = _attn_specs()

    def body(q_ref, k0, k1, k2, v0, v1, v2, gq_ref, gk_ref, sink_ref, o_ref, lse_ref, do_ref,
             dq_ref, dk_ref, dv_ref, dgq_ref, dgk_ref, dsink_ref):
        n = pl.program_id(0)

        @pl.when(n == 0)
        def _():
            dk_ref[...] = jnp.zeros_like(dk_ref)
            dv_ref[...] = jnp.zeros_like(dv_ref)
            dgq_ref[...] = jnp.zeros_like(dgq_ref)
            dgk_ref[...] = jnp.zeros_like(dgk_ref)
            dsink_ref[...] = jnp.zeros_like(dsink_ref)

        valid, dist = _attn_geometry(n)
        dq_parts, dk_parts, dv_parts, dsink_parts = [], [], [], []
        dgq = jnp.zeros((1, HEAD_DIM), F32)
        dgk = jnp.zeros((1, HEAD_DIM), F32)
        for kv in range(N_KV_HEADS):
            qs, kcat, qn, kn, rq, rk, s, sink = _attn_scores(q_ref, (k0, k1, k2), gq_ref, gk_ref, sink_ref, kv, valid, dist)
            vcat = jnp.concatenate([r[:, kv * HEAD_DIM:(kv + 1) * HEAD_DIM] for r in (v0, v1, v2)], axis=0)
            os_ = _stack_heads(o_ref[...], kv)
            dos = _stack_heads(do_ref[...], kv)
            lse = jnp.concatenate([lse_ref[:, kv * KV_GROUP + g:kv * KV_GROUP + g + 1] for g in range(KV_GROUP)], axis=0)
            p = jnp.exp(s - lse)
            delta = jnp.sum(dos * os_, axis=-1, keepdims=True)
            dp = lax.dot_general(dos.astype(BF16), vcat.astype(BF16), _DN_NT, preferred_element_type=F32)
            ds = (p * (dp - delta)) * _SCALE
            dsink_rows = -jnp.exp(sink - lse) * delta
            dsink_parts += [jnp.sum(dsink_rows[g * BLOCK:(g + 1) * BLOCK], axis=0, keepdims=True) for g in range(KV_GROUP)]
            dv_parts.append(lax.dot_general(p.astype(BF16), dos.astype(BF16), _DN_TN, preferred_element_type=F32))
            dsb = ds.astype(BF16)
            dqn = jnp.dot(dsb, kn.astype(BF16), preferred_element_type=F32)
            dkn = lax.dot_general(dsb, qn.astype(BF16), _DN_TN, preferred_element_type=F32)
            qh = qs * rq
            gd = dqn * gq_ref[...]
            dqs = rq * (gd - qh * jnp.mean(gd * qh, axis=-1, keepdims=True))
            dgq = dgq + jnp.sum(dqn * qh, axis=0, keepdims=True)
            kh = kcat * rk
            gdk = dkn * gk_ref[...]
            dk_parts.append(rk * (gdk - kh * jnp.mean(gdk * kh, axis=-1, keepdims=True)))
            dgk = dgk + jnp.sum(dkn * kh, axis=0, keepdims=True)
            dq_parts += [dqs[g * BLOCK:(g + 1) * BLOCK] for g in range(KV_GROUP)]
        dq_ref[...] = jnp.concatenate(dq_parts, axis=1)
        dkc = jnp.concatenate(dk_parts, axis=1)
        dvc = jnp.concatenate(dv_parts, axis=1)
        prev = pl.multiple_of(jnp.maximum(n - 1, 0) * BLOCK, BLOCK)
        cur = pl.multiple_of(n * BLOCK, BLOCK)
        for acc_ref, val in ((dk_ref, dkc), (dv_ref, dvc)):
            acc_ref[0:BLOCK, :] += val[0:BLOCK]
            acc_ref[pl.ds(prev, BLOCK), :] += val[BLOCK:2 * BLOCK]
            acc_ref[pl.ds(cur, BLOCK), :] += val[2 * BLOCK:3 * BLOCK]
        dgq_ref[...] += dgq
        dgk_ref[...] += dgk
        dsink_ref[...] += jnp.concatenate(dsink_parts, axis=1)

    blk = lambda w: pl.BlockSpec((BLOCK, w), lambda n: (n, 0))
    full = lambda r, w: pl.BlockSpec((r, w), lambda n: (0, 0))
    return pl.pallas_call(
        body, name=name, grid=(n_rows // BLOCK,),
        in_specs=[q_spec] + kv_specs + small + [blk(ATTN_WIDTH), blk(N_HEADS), blk(ATTN_WIDTH)],
        out_specs=[blk(ATTN_WIDTH), full(n_rows, KV_WIDTH), full(n_rows, KV_WIDTH),
                   full(1, HEAD_DIM), full(1, HEAD_DIM), full(1, N_HEADS)],
        out_shape=[jax.ShapeDtypeStruct((n_rows, ATTN_WIDTH), F32), jax.ShapeDtypeStruct((n_rows, KV_WIDTH), F32),
                   jax.ShapeDtypeStruct((n_rows, KV_WIDTH), F32), jax.ShapeDtypeStruct((1, HEAD_DIM), F32),
                   jax.ShapeDtypeStruct((1, HEAD_DIM), F32), jax.ShapeDtypeStruct((1, N_HEADS), F32)],
        compiler_params=_cparams(("arbitrary",)),
    )(proj, proj, proj, proj, proj, proj, proj, gq, gk, sinks, o, lse, do)


SSM_LAGS = 8
SLAB_G = 128 // SSM_GROUP_CH
N_SLABS = SSM_GROUPS // SLAB_G
SLAB_STATE = SLAB_G * SSM_STATE
U_COL = (ATTN_WIDTH + 2 * KV_WIDTH) // 128


def _ssm_prep(lam_re, lam_im, log_step, b_re, b_im, c_re, c_im):
    lam = lax.complex(lam_re, lam_im)
    delta = jnp.exp(log_step)[:, None]
    lam_bar = jnp.exp(lam * delta)
    b_bar = ((lam_bar - 1.0) / lam)[..., None] * lax.complex(b_re, b_im)
    pw = [jnp.ones_like(lam_bar)]
    for _ in range(SSM_LAGS):
        pw.append(pw[-1] * lam_bar)
    w = jnp.stack(pw[:SSM_LAGS])[..., None] * b_bar[None]
    wri = jnp.stack([jnp.real(w), jnp.imag(w)]).reshape(2, SSM_LAGS, N_SLABS, SLAB_G, SSM_STATE, SSM_GROUP_CH)
    eye = jnp.eye(SLAB_G, dtype=F32)
    wt = jnp.transpose(wri, (2, 1, 3, 5, 0, 4))
    wmat = wt[:, :, :, :, :, None, :] * eye[None, None, :, None, None, :, None]
    wmat = wmat.reshape(N_SLABS, SSM_LAGS * 128, 2 * SLAB_STATE)
    cri = jnp.stack([c_re, -c_im]).reshape(2, N_SLABS, SLAB_G, SSM_GROUP_CH, SSM_STATE)
    ct = jnp.transpose(cri, (1, 0, 2, 4, 3))
    cmat = ct[:, :, :, :, None, :] * eye[None, None, :, None, :, None]
    cmat = cmat.reshape(N_SLABS, 2 * SLAB_STATE, 128)
    l8 = pw[SSM_LAGS]
    lam8 = jnp.concatenate([jnp.real(l8).reshape(N_SLABS, 1, SLAB_STATE), jnp.imag(l8).reshape(N_SLABS, 1, SLAB_STATE)], axis=2)
    return wmat, cmat, lam8


def _lagged(u, up, t_rows):
    ue = jnp.concatenate([up, u], axis=0)
    return jnp.concatenate([ue[SSM_LAGS - tau:SSM_LAGS - tau + t_rows] for tau in range(SSM_LAGS)], axis=1).astype(BF16)


def _ssm_fwd(proj, wmat, cmat, lam8, dvec, *, name):
    n_rows = proj.shape[0]
    tt = _row_tile(n_rows, 384)
    n_t = n_rows // tt
    sw = 2 * SLAB_STATE
    hs = SLAB_STATE

    def body(u_ref, up_ref, w_ref, c_ref, l_ref, d_ref, y_ref, x_ref, carry_ref):
        t = pl.program_id(1)

        @pl.when(t == 0)
        def _():
            carry_ref[...] = jnp.zeros_like(carry_ref)

        u = u_ref[...]
        up = jnp.where(t > 0, up_ref[...], 0.0)
        x_ref[...] = jnp.dot(_lagged(u, up, tt), w_ref[...], preferred_element_type=F32)
        ar = jnp.broadcast_to(l_ref[:, :hs], (8, hs))
        ai = jnp.broadcast_to(l_ref[:, hs:], (8, hs))

        def step(b, c):
            xr, xi = c
            r0 = pl.multiple_of(b * 8, 8)
            w = x_ref[pl.ds(r0, 8), :]
            nr = w[:, :hs] + ar * xr - ai * xi
            ni = w[:, hs:] + ar * xi + ai * xr
            x_ref[pl.ds(r0, 8), :] = jnp.concatenate([nr, ni], axis=1)
            return nr, ni

        xr, xi = lax.fori_loop(0, tt // 8, step, (carry_ref[:, :hs], carry_ref[:, hs:]))
        carry_ref[...] = jnp.concatenate([xr, xi], axis=1)
        y_ref[...] = jnp.dot(x_ref[...].astype(BF16), c_ref[...], preferred_element_type=F32) + d_ref[...] * u

    return pl.pallas_call(
        body, name=name, grid=(N_SLABS, n_t),
        in_specs=[pl.BlockSpec((tt, 128), lambda j, t: (t, U_COL + j)),
                  pl.BlockSpec((8, 128), lambda j, t: (jnp.maximum(t * (tt // 8) - 1, 0), U_COL + j)),
                  pl.BlockSpec((None, SSM_LAGS * 128, sw), lambda j, t: (j, 0, 0)),
                  pl.BlockSpec((None, sw, 128), lambda j, t: (j, 0, 0)),
                  pl.BlockSpec((None, 1, sw), lambda j, t: (j, 0, 0)),
                  pl.BlockSpec((1, 128), lambda j, t: (0, j))],
        out_specs=[pl.BlockSpec((tt, 128), lambda j, t: (t, j)), pl.BlockSpec((tt, sw), lambda j, t: (t, j))],
        out_shape=[jax.ShapeDtypeStruct((n_rows, SSM_WIDTH), F32), jax.ShapeDtypeStruct((n_rows, N_SLABS * sw), F32)],
        scratch_shapes=[pltpu.VMEM((8, sw), F32)],
        compiler_params=_cparams(("parallel", "arbitrary")),
    )(proj, proj, wmat, cmat, lam8, dvec)


def _ssm_bwd(proj, xs, dy, wmat, cmat, lam8, dvec, *, name):
    n_rows = proj.shape[0]
    tt = _row_tile(n_rows, 384)
    n_t = n_rows // tt
    sw = 2 * SLAB_STATE
    hs = SLAB_STATE

    def body(u_ref, up_ref, x_ref, xp_ref, dy_ref, w_ref, c_ref, l_ref, d_ref,
             du_ref, dw_ref, dc_ref, dl_ref, dd_ref, a_ref, carry_ref, head_ref):
        t = pl.program_id(1)
        ti = n_t - 1 - t

        @pl.when(t == 0)
        def _():
            carry_ref[...] = jnp.zeros_like(carry_ref)
            head_ref[...] = jnp.zeros_like(head_ref)
            dw_ref[...] = jnp.zeros_like(dw_ref)
            dc_ref[...] = jnp.zeros_like(dc_ref)
            dl_ref[...] = jnp.zeros_like(dl_ref)
            dd_ref[...] = jnp.zeros_like(dd_ref)

        u = u_ref[...]
        up = jnp.where(ti > 0, up_ref[...], 0.0)
        ucat = _lagged(u, up, tt)
        dyv = dy_ref[...]
        dyb = dyv.astype(BF16)
        a_ref[...] = lax.dot_general(dyb, c_ref[...], _DN_NT, preferred_element_type=F32)
        lr = jnp.broadcast_to(l_ref[:, :hs], (8, hs))
        li = jnp.broadcast_to(l_ref[:, hs:], (8, hs))

        def step(i, c):
            cr, ci = c
            r0 = pl.multiple_of((tt // 8 - 1 - i) * 8, 8)
            g = a_ref[pl.ds(r0, 8), :]
            nr = g[:, :hs] + lr * cr + li * ci
            ni = g[:, hs:] + lr * ci - li * cr
            a_ref[pl.ds(r0, 8), :] = jnp.concatenate([nr, ni], axis=1)
            return nr, ni

        cr, ci = lax.fori_loop(0, tt // 8, step, (carry_ref[:, :hs], carry_ref[:, hs:]))
        carry_ref[...] = jnp.concatenate([cr, ci], axis=1)

        a = a_ref[...]
        xv = x_ref[...]
        xprev = jnp.where(ti > 0, xp_ref[...], 0.0)
        xsh = jnp.concatenate([xprev, xv[:tt - SSM_LAGS]], axis=0)
        a_re, a_im, x_re, x_im = a[:, :hs], a[:, hs:], xsh[:, :hs], xsh[:, hs:]
        dl_ref[...] += jnp.concatenate([jnp.sum(a_re * x_re + a_im * x_im, axis=0, keepdims=True),
                                        jnp.sum(a_im * x_re - a_re * x_im, axis=0, keepdims=True)], axis=1)
        ab = a.astype(BF16)
        dw_ref[...] += lax.dot_general(ucat, ab, _DN_TN, preferred_element_type=F32)
        duc = lax.dot_general(ab, w_ref[...], _DN_NT, preferred_element_type=F32)
        ext = jnp.concatenate([duc, head_ref[...]], axis=0)
        du = d_ref[...] * dyv
        for tau in range(SSM_LAGS):
            du = du + ext[tau:tau + tt, tau * 128:(tau + 1) * 128]
        head_ref[...] = duc[0:8]
        row = ti * tt + lax.broadcasted_iota(jnp.int32, (tt, 128), 0)
        du_ref[...] = jnp.where(row >= PAD, du, 0.0)
        dd_ref[...] += jnp.sum(dyv * u, axis=0, keepdims=True)
        dc_ref[...] += lax.dot_general(xv.astype(BF16), dyb, _DN_TN, preferred_element_type=F32)

    rt = lambda t: n_t - 1 - t
    prev8 = lambda t: jnp.maximum(rt(t) * (tt // 8) - 1, 0)
    return pl.pallas_call(
        body, name=name, grid=(N_SLABS, n_t),
        in_specs=[pl.BlockSpec((tt, 128), lambda j, t: (rt(t), U_COL + j)),
                  pl.BlockSpec((8, 128), lambda j, t: (prev8(t), U_COL + j)),
                  pl.BlockSpec((tt, sw), lambda j, t: (rt(t), j)),
                  pl.BlockSpec((8, sw), lambda j, t: (prev8(t), j)),
                  pl.BlockSpec((tt, 128), lambda j, t: (rt(t), j)),
                  pl.BlockSpec((None, SSM_LAGS * 128, sw), lambda j, t: (j, 0, 0)),
                  pl.BlockSpec((None, sw, 128), lambda j, t: (j, 0, 0)),
                  pl.BlockSpec((None, 1, sw), lambda j, t: (j, 0, 0)),
                  pl.BlockSpec((1, 128), lambda j, t: (0, j))],
        out_specs=[pl.BlockSpec((tt, 128), lambda j, t: (rt(t), j)),
                   pl.BlockSpec((None, SSM_LAGS * 128, sw), lambda j, t: (j, 0, 0)),
                   pl.BlockSpec((None, sw, 128), lambda j, t: (j, 0, 0)),
                   pl.BlockSpec((None, 1, sw), lambda j, t: (j, 0, 0)),
                   pl.BlockSpec((1, 128), lambda j, t: (0, j))],
        out_shape=[jax.ShapeDtypeStruct((n_rows, SSM_WIDTH), F32),
                   jax.ShapeDtypeStruct((N_SLABS, SSM_LAGS * 128, sw), F32),
                   jax.ShapeDtypeStruct((N_SLABS, sw, 128), F32),
                   jax.ShapeDtypeStruct((N_SLABS, 1, sw), F32),
                   jax.ShapeDtypeStruct((1, SSM_WIDTH), F32)],
        scratch_shapes=[pltpu.VMEM((tt, sw), F32), pltpu.VMEM((8, sw), F32), pltpu.VMEM((8, sw), F32)],
        compiler_params=_cparams(("parallel", "arbitrary")),
    )(proj, proj, xs, xs, dy, wmat, cmat, lam8, dvec)


_GELU_C = math.sqrt(2.0 / math.pi)
_GELU_A = 0.044715


def _gelu(y):
    th = jnp.tanh(_GELU_C * (y + _GELU_A * y * y * y))
    return 0.5 * y * (1.0 + th), th


def _glu_fwd(y, w, b, *, name):
    n_rows, width = y.shape
    tr = _row_tile(n_rows, 384)

    def body(y_ref, w_ref, b_ref, o_ref):
        g, _ = _gelu(y_ref[...])
        z = jnp.dot(g.astype(BF16), w_ref[...], preferred_element_type=F32) + b_ref[...]
        o_ref[...] = g * jax.nn.sigmoid(z)

    return pl.pallas_call(
        body, name=name, grid=(n_rows // tr,),
        in_specs=[pl.BlockSpec((tr, width), lambda i: (i, 0)), pl.BlockSpec((width, width), lambda i: (0, 0)),
                  pl.BlockSpec((1, width), lambda i: (0, 0))],
        out_specs=pl.BlockSpec((tr, width), lambda i: (i, 0)),
        out_shape=jax.ShapeDtypeStruct((n_rows, width), F32),
        compiler_params=_cparams(("parallel",)),
    )(y, w, b)


def _glu_bwd(y, w, b, dout, *, name):
    n_rows, width = y.shape
    tr = _row_tile(n_rows, 384)

    def body(y_ref, w_ref, b_ref, do_ref, dy_ref, g_ref, dz_ref, db_ref):
        i = pl.program_id(0)
        yv = y_ref[...]
        g, th = _gelu(yv)
        gb = g.astype(BF16)
        z = jnp.dot(gb, w_ref[...], preferred_element_type=F32) + b_ref[...]
        sg = jax.nn.sigmoid(z)
        do = do_ref[...]
        dz = do * g * sg * (1.0 - sg)
        dzb = dz.astype(BF16)
        dg = do * sg + lax.dot_general(dzb, w_ref[...], _DN_NT, preferred_element_type=F32)
        dgelu = 0.5 * (1.0 + th) + 0.5 * yv * (1.0 - th * th) * _GELU_C * (1.0 + 3.0 * _GELU_A * yv * yv)
        dy_ref[...] = dg * dgelu
        g_ref[...] = gb
        dz_ref[...] = dzb
        part = jnp.sum(dz, axis=0, keepdims=True)

        @pl.when(i == 0)
        def _():
            db_ref[...] = part

        @pl.when(i > 0)
        def _():
            db_ref[...] += part

    row = pl.BlockSpec((tr, width), lambda i: (i, 0))
    vec = pl.BlockSpec((1, width), lambda i: (0, 0))
    return pl.pallas_call(
        body, name=name, grid=(n_rows // tr,),
        in_specs=[row, pl.BlockSpec((width, width), lambda i: (0, 0)), vec, row],
        out_specs=[row, row, row, vec],
        out_shape=[jax.ShapeDtypeStruct((n_rows, width), F32), jax.ShapeDtypeStruct((n_rows, width), BF16),
                   jax.ShapeDtypeStruct((n_rows, width), BF16), jax.ShapeDtypeStruct((1, width), F32)],
        compiler_params=_cparams(("arbitrary",)),
    )(y, w, b, dout)


def _loss_head(h, target, *, name):
    n_rows, width = h.shape

    def body(h_ref, t_ref, dh_ref, loss_ref):
        i = pl.program_id(0)

        @pl.when(i == 0)
        def _():
            dh_ref[...] = jnp.zeros_like(dh_ref)
            loss_ref[...] = jnp.zeros_like(loss_ref)

        @pl.when(i > 0)
        def _():
            err = h_ref[...] - t_ref[...]
            dh_ref[...] = err * (1.0 / width)
            loss_ref[...] += (0.5 / width) * jnp.sum(err * err, keepdims=True)

    return pl.pallas_call(
        body, name=name, grid=(n_rows // BLOCK,),
        in_specs=[pl.BlockSpec((BLOCK, width), lambda i: (i, 0)),
                  pl.BlockSpec((BLOCK, width), lambda i: (jnp.maximum(i - 1, 0), 0))],
        out_specs=[pl.BlockSpec((BLOCK, width), lambda i: (i, 0)), pl.BlockSpec((1, 1), lambda i: (0, 0))],
        out_shape=[jax.ShapeDtypeStruct((n_rows, width), F32), jax.ShapeDtypeStruct((1, 1), F32)],
        compiler_params=_cparams(("arbitrary",)),
    )(h, target)


def _elem_rows(n_rows, n_cols, bytes_per_row_elem):
    cap = max(16, (4 * 1024 * 1024) // (n_cols * bytes_per_row_elem))
    best = None
    for t in range(16, min(n_rows, cap) + 1, 16):
        if n_rows % t == 0:
            best = t
    return best or n_rows


def _cast_bf16(x, *, name):
    n_rows, n_cols = x.shape
    tr = _elem_rows(n_rows, n_cols, 4)

    def body(x_ref, o_ref):
        o_ref[...] = x_ref[...].astype(BF16)

    spec = pl.BlockSpec((tr, n_cols), lambda i: (i, 0))
    return pl.pallas_call(body, name=name, grid=(n_rows // tr,), in_specs=[spec], out_specs=spec,
                          out_shape=jax.ShapeDtypeStruct(x.shape, BF16), compiler_params=_cparams(("parallel",)))(x)


def _adamw(w, m, v, parts, *, name):
    n_rows, n_cols = w.shape
    n_parts = parts.shape[0]
    tr = _elem_rows(n_rows, n_cols, 4 * (8 + n_parts))
    c1 = 1.0 / (1.0 - ADAM_B1 ** ADAM_STEP)
    c2 = 1.0 / (1.0 - ADAM_B2 ** ADAM_STEP)

    def body(w_ref, m_ref, v_ref, p_ref, g_ref, d_ref, nm_ref, nv_ref):
        g = p_ref[0].astype(F32)
        for k in range(1, n_parts):
            g = g + p_ref[k].astype(F32)
        nm = ADAM_B1 * m_ref[...] + (1.0 - ADAM_B1) * g
        nv = ADAM_B2 * v_ref[...] + (1.0 - ADAM_B2) * (g * g)
        g_ref[...] = g
        nm_ref[...] = nm
        nv_ref[...] = nv
        d_ref[...] = -ADAM_LR * ((nm * c1) / (jnp.sqrt(nv * c2) + ADAM_EPS) + ADAM_WD * w_ref[...])

    spec = pl.BlockSpec((tr, n_cols), lambda i: (i, 0))
    return pl.pallas_call(
        body, name=name, grid=(n_rows // tr,),
        in_specs=[spec, spec, spec, pl.BlockSpec((n_parts, tr, n_cols), lambda i: (0, i, 0))],
        out_specs=[spec] * 4, out_shape=[jax.ShapeDtypeStruct(w.shape, F32)] * 4,
        compiler_params=_cparams(("parallel",)),
    )(w, m, v, parts)


def _sum_parts(parts, *, name):
    n_parts, n_rows, n_cols = parts.shape
    tr = _elem_rows(n_rows, n_cols, 4 * (1 + n_parts))

    def body(p_ref, o_ref):
        g = p_ref[0].astype(F32)
        for k in range(1, n_parts):
            g = g + p_ref[k].astype(F32)
        o_ref[...] = g

    return pl.pallas_call(
        body, name=name, grid=(n_rows // tr,),
        in_specs=[pl.BlockSpec((n_parts, tr, n_cols), lambda i: (0, i, 0))],
        out_specs=pl.BlockSpec((tr, n_cols), lambda i: (i, 0)),
        out_shape=jax.ShapeDtypeStruct((n_rows, n_cols), F32), compiler_params=_cparams(("parallel",)),
    )(parts)


BIG = ("w_in", "w_glu", "w_out", "w_up", "w_down")
SMALL = ("norm_mix_g", "q_norm_g", "k_norm_g", "attn_sinks", "ssm_lambda_re", "ssm_lambda_im", "ssm_log_step",
         "ssm_b_re", "ssm_b_im", "ssm_c_re", "ssm_c_im", "ssm_d", "b_glu", "attn_out_g", "ssm_out_g", "norm_mlp_g")
_SSM_NAMES = ("ssm_lambda_re", "ssm_lambda_im", "ssm_log_step", "ssm_b_re", "ssm_b_im", "ssm_c_re", "ssm_c_im")


def _divisor(n, cands):
    for c in cands:
        if n % c == 0:
            return c
    return n


def _mm(a, b, mode, name, outs=(F32,), epilogue=_ident, tiles=()):
    if mode == "nn":
        (m, k), n = a.shape, b.shape[1]
    elif mode == "nt":
        (m, k), n = a.shape, b.shape[0]
    else:
        (k, m), n = a.shape, b.shape[1]
    if mode == "tn":
        tm, tk = _divisor(m, (1024,)), _row_tile(k, 1408)
    else:
        tm, tk = _row_tile(m, 1408), _divisor(k, (1024, 1280, 768, 512))
    tn = _divisor(n, (1024, 1280, 512))
    return _matmul(a, b, mode=mode, tm=tm, tn=tn, tk=tk, outs=list(outs), epilogue=epilogue, tiles=tiles, name=name)


def _add_tile(acc, res):
    return (acc + res,)


def _relu_sq(acc):
    r = jnp.maximum(acc, 0.0)
    return r, r * r


def _relu_sq_bwd(acc, r):
    return (acc * (2.0 * r.astype(F32)),)


def _row(v):
    return v.reshape(1, -1)


def _layer_fwd(hres, wts, sp, l):
    tag = f"_l{l}"
    hb = _rmsnorm_fwd([hres], [_row(sp["norm_mix_g"])], name="norm_mix" + tag)
    proj, = _mm(hb, wts["w_in"], "nn", "proj" + tag)
    gq, gk, sinks = _row(sp["q_norm_g"]), _row(sp["k_norm_g"]), _row(sp["attn_sinks"])
    o, lse = _attn_fwd(proj, gq, gk, sinks, name="attn_fwd" + tag)
    (wmat, cmat, lam8), prep_vjp = jax.vjp(_ssm_prep, *[sp[n] for n in _SSM_NAMES])
    wmat, cmat = wmat.astype(BF16), cmat.astype(BF16)
    y, xs = _ssm_fwd(proj, wmat, cmat, lam8, _row(sp["ssm_d"]), name="ssm_fwd" + tag)
    s = _glu_fwd(y, wts["w_glu"], _row(sp["b_glu"]), name="glu_fwd" + tag)
    mix = _rmsnorm_fwd([o, s], [_row(sp["attn_out_g"]), _row(sp["ssm_out_g"])], name="norm_out" + tag)
    hres2, = _mm(mix, wts["w_out"], "nn", "out_proj" + tag, epilogue=_add_tile, tiles=(hres,))
    h2 = _rmsnorm_fwd([hres2], [_row(sp["norm_mlp_g"])], name="norm_mlp" + tag)
    r, act = _mm(h2, wts["w_up"], "nn", "mlp_up" + tag, outs=(BF16, BF16), epilogue=_relu_sq)
    hres3, = _mm(act, wts["w_down"], "nn", "mlp_down" + tag, epilogue=_add_tile, tiles=(hres2,))
    saved = dict(hres=hres, hb=hb, proj=proj, o=o, lse=lse, wmat=wmat, cmat=cmat, lam8=lam8, prep_vjp=prep_vjp,
                 y=y, xs=xs, s=s, mix=mix, hres2=hres2, h2=h2, r=r, act=act)
    return hres3, saved


def _layer_bwd(dres, wts, sp, sv, l):
    tag = f"_l{l}"
    gb, gs = {}, {}
    d_up, = _mm(dres, wts["w_down"], "nt", "mlp_down_dx" + tag, outs=(BF16,), epilogue=_relu_sq_bwd, tiles=(sv["r"],))
    gb["w_down"], = _mm(sv["act"], dres, "tn", "mlp_down_dw" + tag, outs=(BF16,))
    gb["w_up"], = _mm(sv["h2"], d_up, "tn", "mlp_up_dw" + tag, outs=(BF16,))
    dh2, = _mm(d_up, wts["w_up"], "nt", "mlp_up_dx" + tag)
    (dres2,), (dg,) = _rmsnorm_bwd([sv["hres2"]], [_row(sp["norm_mlp_g"])], dh2, dres, name="norm_mlp_bwd" + tag)
    gs["norm_mlp_g"] = dg
    dmix, = _mm(dres2, wts["w_out"], "nt", "out_proj_dx" + tag)
    gb["w_out"], = _mm(sv["mix"], dres2, "tn", "out_proj_dw" + tag, outs=(BF16,))
    (do, ds), (dga, dgs) = _rmsnorm_bwd([sv["o"], sv["s"]], [_row(sp["attn_out_g"]), _row(sp["ssm_out_g"])], dmix, None,
                                        name="norm_out_bwd" + tag)
    gs["attn_out_g"], gs["ssm_out_g"] = dga, dgs
    dy, g_b, dz_b, db = _glu_bwd(sv["y"], wts["w_glu"], _row(sp["b_glu"]), ds, name="glu_bwd" + tag)
    gs["b_glu"] = db
    gb["w_glu"], = _mm(g_b, dz_b, "tn", "glu_dw" + tag, outs=(BF16,))
    du, dwmat, dcmat, dlam8, dd = _ssm_bwd(sv["proj"], sv["xs"], dy, sv["wmat"], sv["cmat"], sv["lam8"], _row(sp["ssm_d"]),
                                           name="ssm_bwd" + tag)
    gs["ssm_d"] = dd
    for n, g in zip(_SSM_NAMES, sv["prep_vjp"]((dwmat, dcmat, dlam8))):
        gs[n] = g
    dq, dk, dv, dgq, dgk, dsinks = _attn_bwd(sv["proj"], _row(sp["q_norm_g"]), _row(sp["k_norm_g"]), _row(sp["attn_sinks"]),
                                             sv["o"], sv["lse"], do, name="attn_bwd" + tag)
    gs["q_norm_g"], gs["k_norm_g"], gs["attn_sinks"] = dgq, dgk, dsinks
    dproj = jnp.concatenate([dq, dk, dv, du], axis=1)
    gb["w_in"], = _mm(sv["hb"], dproj, "tn", "proj_dw" + tag, outs=(BF16,))
    dh, = _mm(dproj, wts["w_in"], "nt", "proj_dx" + tag)
    (dres_in,), (dg,) = _rmsnorm_bwd([sv["hres"]], [_row(sp["norm_mix_g"])], dh, dres2, name="norm_mix_bwd" + tag)
    gs["norm_mix_g"] = dg
    return dres_in, gb, gs


def _local_step(x, target, meta, wts, sp):
    h = jnp.concatenate([jnp.zeros((PAD, x.shape[1]), F32), meta, x], axis=0)
    saved = []
    for l in range(DEPTH):
        h, sv = _layer_fwd(h, {n: wts[n][l] for n in BIG}, {n: sp[n][l] for n in SMALL}, l)
        saved.append(sv)
    dh, loss = _loss_head(h, target, name="loss_head")
    gbig = {n: [None] * DEPTH for n in BIG}
    gsmall = {n: [None] * DEPTH for n in SMALL}
    for l in reversed(range(DEPTH)):
        dh, gb, gs = _layer_bwd(dh, {n: wts[n][l] for n in BIG}, {n: sp[n][l] for n in SMALL}, saved[l], l)
        for n in BIG:
            gbig[n][l] = gb[n]
        for n in SMALL:
            gsmall[n][l] = gs[n].reshape(sp[n][l].shape)
    return loss, dh, gbig, gsmall


_MESH = pl.DeviceIdType.MESH
_ANY = pl.BlockSpec(memory_space=pl.ANY)


def _all_gather(x, *, name):
    def body(x_ref, out_ref, send_sems, recv_sems, local_sem):
        x, y, c = lax.axis_index("x"), lax.axis_index("y"), lax.axis_index("c")
        me, sibling = (x, y, c), (x, y, 1 - c)
        chips = [(1 - x, y), (x, 1 - y), (1 - x, 1 - y)]

        def slot(px, py, pc):
            return out_ref.at[4 * px + 2 * py + pc]

        def copy(k, block, to, src=None):
            return pltpu.make_async_remote_copy(
                src_ref=slot(*block) if src is None else src, dst_ref=slot(*block),
                send_sem=send_sems.at[k], recv_sem=recv_sems.at[k], device_id=to, device_id_type=_MESH)

        mine = pltpu.make_async_copy(x_ref, slot(*me), local_sem)
        mine.start()
        first = [copy(0, me, sibling, src=x_ref)]
        first += [copy(1 + j, me, (*chip, c), src=x_ref) for j, chip in enumerate(chips)]
        for cp in first:
            cp.start()
        passed = [copy(4 + j, (*chip, c), sibling) for j, chip in enumerate(chips)]
        for j, chip in enumerate(chips):
            copy(1 + j, (*chip, c), me).wait_recv()
            passed[j].start()
        copy(0, sibling, me).wait_recv()
        for j, chip in enumerate(chips):
            copy(4 + j, (*chip, 1 - c), me).wait_recv()
        for cp in first + passed:
            cp.wait_send()
        mine.wait()

    return pl.pallas_call(
        body, name=name, out_shape=jax.ShapeDtypeStruct((N_DEV,) + x.shape, x.dtype),
        in_specs=[_ANY], out_specs=_ANY,
        scratch_shapes=[pltpu.SemaphoreType.DMA((7,)), pltpu.SemaphoreType.DMA((7,)), pltpu.SemaphoreType.DMA],
    )(x)


def _exchange(g, *, name):
    def body(g_ref, r_ref, send_sems, recv_sems, local_sem):
        x, y, c = lax.axis_index("x"), lax.axis_index("y"), lax.axis_index("c")
        me = 4 * x + 2 * y + c
        mine = pltpu.make_async_copy(g_ref.at[me], r_ref.at[me], local_sem)
        mine.start()

        def peer(k):
            px, py, pc = (x + (k >> 2)) % 2, (y + ((k >> 1) & 1)) % 2, (c + (k & 1)) % 2
            return (px, py, pc), 4 * px + 2 * py + pc

        def copy(k, src_block, dst_block):
            to, _ = peer(k)
            return pltpu.make_async_remote_copy(
                src_ref=g_ref.at[src_block], dst_ref=r_ref.at[dst_block],
                send_sem=send_sems.at[k - 1], recv_sem=recv_sems.at[k - 1], device_id=to, device_id_type=_MESH)

        sends = [copy(k, peer(k)[1], me) for k in range(1, N_DEV)]
        for cp in sends:
            cp.start()
        for k in range(1, N_DEV):
            copy(k, me, peer(k)[1]).wait_recv()
        for cp in sends:
            cp.wait_send()
        mine.wait()

    return pl.pallas_call(
        body, name=name, out_shape=jax.ShapeDtypeStruct(g.shape, g.dtype),
        in_specs=[_ANY], out_specs=_ANY,
        scratch_shapes=[pltpu.SemaphoreType.DMA((7,)), pltpu.SemaphoreType.DMA((7,)), pltpu.SemaphoreType.DMA],
    )(g)


_SHARD_AXIS = {"w_in": 1, "w_glu": 0, "w_out": 0, "w_up": 1, "w_down": 0}


def _gather_weight(w, axis, *, name):
    depth, r, c = w.shape
    wb = _cast_bf16(w.reshape(depth * r, c), name="cast_" + name).reshape(depth, r, c)
    allw = _all_gather(wb, name="gather_" + name)
    full = []
    for l in range(depth):
        blk = allw[:, l]
        if axis == 0:
            full.append(blk.reshape(N_DEV * r, c))
        else:
            full.append(jnp.transpose(blk, (1, 0, 2)).reshape(r, N_DEV * c))
    return full


def _scatter_blocks(grads, axis):
    out = []
    for g in grads:
        rows, cols = g.shape
        if axis == 0:
            out.append(g.reshape(N_DEV, rows // N_DEV, cols))
        else:
            out.append(jnp.transpose(g.reshape(rows, N_DEV, cols // N_DEV), (1, 0, 2)))
    return jnp.stack(out, axis=1)


_SMALL_ROWS = 1096


def _pack_small(d):
    flat = jnp.concatenate([d[n].reshape(-1) for n in SMALL])
    total = N_DEV * _SMALL_ROWS * 128
    assert flat.shape[0] <= total
    return jnp.pad(flat, (0, total - flat.shape[0])).reshape(N_DEV * _SMALL_ROWS, 128)


def _unpack_small(packed, like):
    flat = packed.reshape(-1)
    out, off = {}, 0
    for n in SMALL:
        size = like[n].size
        out[n] = flat[off:off + size].reshape(like[n].shape)
        off += size
    return out


def kernel(x, meta_tokens, norm_mix_g, w_in, q_norm_g, k_norm_g, attn_sinks, ssm_lambda_re, ssm_lambda_im, ssm_log_step, ssm_b_re, ssm_b_im, ssm_c_re, ssm_c_im, ssm_d, w_glu, b_glu, attn_out_g, ssm_out_g, w_out, norm_mlp_g, w_up, w_down, loss_target, m_meta_tokens, m_norm_mix_g, m_w_in, m_q_norm_g, m_k_norm_g, m_attn_sinks, m_ssm_lambda_re, m_ssm_lambda_im, m_ssm_log_step, m_ssm_b_re, m_ssm_b_im, m_ssm_c_re, m_ssm_c_im, m_ssm_d, m_w_glu, m_b_glu, m_attn_out_g, m_ssm_out_g, m_w_out, m_norm_mlp_g, m_w_up, m_w_down, v_meta_tokens, v_norm_mix_g, v_w_in, v_q_norm_g, v_k_norm_g, v_attn_sinks, v_ssm_lambda_re, v_ssm_lambda_im, v_ssm_log_step, v_ssm_b_re, v_ssm_b_im, v_ssm_c_re, v_ssm_c_im, v_ssm_d, v_w_glu, v_b_glu, v_attn_out_g, v_ssm_out_g, v_w_out, v_norm_mlp_g, v_w_up, v_w_down):
    a = dict(locals())
    order = ("meta_tokens", "norm_mix_g", "w_in", "q_norm_g", "k_norm_g", "attn_sinks", "ssm_lambda_re", "ssm_lambda_im",
             "ssm_log_step", "ssm_b_re", "ssm_b_im", "ssm_c_re", "ssm_c_im", "ssm_d", "w_glu", "b_glu", "attn_out_g",
             "ssm_out_g", "w_out", "norm_mlp_g", "w_up", "w_down")

    wts = {n: _gather_weight(a[n], _SHARD_AXIS[n], name=n) for n in BIG}
    meta_all = _all_gather(meta_tokens, name="gather_meta")
    meta = jnp.transpose(meta_all, (1, 0, 2)).reshape(N_META, D_MODEL)
    sp = {n: a[n] for n in SMALL}

    loss, dh0, gbig, gsmall = _local_step(x[0], loss_target[0], meta, wts, sp)
    loss = lax.psum(loss[0, 0], ("x", "y", "c"))
    grad, delta, new_m, new_v = {}, {}, {}, {}

    def update(n, parts, shape2d):
        outs = _adamw(a[n].reshape(shape2d), a["m_" + n].reshape(shape2d), a["v_" + n].reshape(shape2d),
                      parts.reshape((parts.shape[0],) + shape2d), name="adamw_" + n)
        grad[n], delta[n], new_m[n], new_v[n] = [o.reshape(a[n].shape) for o in outs]

    for n in BIG:
        depth, r, c = a[n].shape
        recv = _exchange(_scatter_blocks(gbig[n], _SHARD_AXIS[n]), name="exchange_" + n)
        update(n, recv, (depth * r, c))
    dmeta = jnp.transpose(dh0[PAD:BLOCK].reshape(N_META, N_DEV, D_MODEL // N_DEV), (1, 0, 2))
    update("meta_tokens", _exchange(dmeta, name="exchange_meta"), meta_tokens.shape)

    packed = _pack_small({n: jnp.stack(gsmall[n]) for n in SMALL}).reshape(N_DEV, _SMALL_ROWS, 128)
    share = _sum_parts(_exchange(packed, name="exchange_small"), name="sum_small")
    total = _all_gather(share, name="gather_small").reshape(1, N_DEV * _SMALL_ROWS, 128)
    outs = _adamw(_pack_small(sp), _pack_small({n: a["m_" + n] for n in SMALL}), _pack_small({n: a["v_" + n] for n in SMALL}),
                  total, name="adamw_small")
    for dst, o in zip((grad, delta, new_m, new_v), outs):
        dst.update(_unpack_small(o, sp))

    return (loss, dh0[BLOCK:][None], *[grad[n] for n in order], *[delta[n] for n in order],
            *[new_m[n] for n in order], *[new_v[n] for n in order])
```

```python
import functools
import math

import jax
import jax.numpy as jnp
from jax import lax
from jax.experimental import pallas as pl
from jax.experimental.pallas import tpu as pltpu

F32 = jnp.float32
BF16 = jnp.bfloat16

N_DEV = 8
D_MODEL = 2048
SEQ = 4096
DEPTH = 4
N_META = 16
HEAD_DIM = 64
ATTN_WIDTH = D_MODEL // 2
N_HEADS = ATTN_WIDTH // HEAD_DIM
N_KV_HEADS = N_HEADS // 4
KV_GROUP = N_HEADS // N_KV_HEADS
KV_WIDTH = N_KV_HEADS * HEAD_DIM
SSM_WIDTH = D_MODEL - ATTN_WIDTH
SSM_GROUP_CH = 16
SSM_GROUPS = SSM_WIDTH // SSM_GROUP_CH
SSM_STATE = 64
WINDOW = 128
BLOCK = 128
PAD = BLOCK - N_META
D_FF = 4 * D_MODEL
IN_WIDTH = ATTN_WIDTH + 2 * KV_WIDTH + SSM_WIDTH
NORM_EPS = 1e-6
NEG_INF = -1e30
ADAM_LR = 0.001
ADAM_B1 = 0.9
ADAM_B2 = 0.999
ADAM_EPS = 1e-08
ADAM_WD = 0.01
ADAM_STEP = 10

VMEM_LIMIT = 56 * 1024 * 1024
_MESH = pl.DeviceIdType.MESH
_ANY = pl.BlockSpec(memory_space=pl.ANY)


def _cparams(sem=None):
    return pltpu.CompilerParams(dimension_semantics=sem, vmem_limit_bytes=VMEM_LIMIT)


def _matmul(a, b, *, mode, tm, tn, tk, outs, epilogue, tiles=(), rows=(), deps=(), blocked=False, name):
    if mode == "nn":
        m, k = a.shape
        if blocked:
            n = b.shape[0] * b.shape[2]
            assert tn == b.shape[2]
            b_spec = pl.BlockSpec((None, tk, tn), lambda i, j, kk: (j, kk, 0))
        else:
            n = b.shape[1]
            b_spec = pl.BlockSpec((tk, tn), lambda i, j, kk: (kk, j))
        a_spec = pl.BlockSpec((tm, tk), lambda i, j, kk: (i, kk))
        dims = (((1,), (0,)), ((), ()))
    elif mode == "nt":
        m, k = a.shape
        if blocked:
            n = b.shape[1]
            assert tk == b.shape[2] and k == b.shape[0] * b.shape[2]
            b_spec = pl.BlockSpec((None, tn, tk), lambda i, j, kk: (kk, j, 0))
        else:
            n = b.shape[0]
            b_spec = pl.BlockSpec((tn, tk), lambda i, j, kk: (j, kk))
        a_spec = pl.BlockSpec((tm, tk), lambda i, j, kk: (i, kk))
        dims = (((1,), (1,)), ((), ()))
    else:
        (k, m), n = a.shape, b.shape[1]
        a_spec = pl.BlockSpec((tk, tm), lambda i, j, kk: (kk, i))
        b_spec = pl.BlockSpec((tk, tn), lambda i, j, kk: (kk, j))
        dims = (((0,), (0,)), ((), ()))
    assert m % tm == 0 and n % tn == 0 and k % tk == 0, (name, m, n, k, tm, tn, tk)
    nk = k // tk
    n_tiles, n_rows, n_outs, n_deps = len(tiles), len(rows), len(outs), len(deps)

    def body(a_ref, b_ref, *rest):
        tile_refs = rest[:n_tiles]
        row_refs = rest[n_tiles:n_tiles + n_rows]
        out_refs = rest[n_tiles + n_rows + n_deps:n_tiles + n_rows + n_deps + n_outs]
        acc_ref = rest[-1]
        kk = pl.program_id(2)

        @pl.when(kk == 0)
        def _():
            acc_ref[...] = jnp.zeros_like(acc_ref)

        acc_ref[...] += lax.dot_general(a_ref[...].astype(BF16), b_ref[...].astype(BF16), dims,
                                        preferred_element_type=F32)

        @pl.when(kk == nk - 1)
        def _():
            res = epilogue(acc_ref[...], *[r[...] for r in tile_refs], *[r[...] for r in row_refs])
            for o_ref, o in zip(out_refs, res):
                o_ref[...] = o.astype(o_ref.dtype)

    tile_spec = pl.BlockSpec((tm, tn), lambda i, j, kk: (i, j))
    row_spec = pl.BlockSpec((1, tn), lambda i, j, kk: (0, j))
    if mode == "tn" and blocked:
        out_specs = [pl.BlockSpec((None, tm, tn), lambda i, j, kk: (j, i, 0))] * n_outs
        out_shape = [jax.ShapeDtypeStruct((n // tn, m, tn), dt) for dt in outs]
    else:
        out_specs = [tile_spec] * n_outs
        out_shape = [jax.ShapeDtypeStruct((m, n), dt) for dt in outs]
    return pl.pallas_call(
        body, name=name, grid=(m // tm, n // tn, nk),
        in_specs=[a_spec, b_spec] + [tile_spec] * n_tiles + [row_spec] * n_rows + [_ANY] * n_deps,
        out_specs=out_specs, out_shape=out_shape,
        scratch_shapes=[pltpu.VMEM((tm, tn), F32)],
        compiler_params=_cparams(("parallel", "parallel", "arbitrary")),
    )(a, b, *tiles, *rows, *deps)


def _ident(acc):
    return (acc,)


def _row_tile(n_rows, cap):
    best = BLOCK
    for t in range(BLOCK, cap + 1, BLOCK):
        if n_rows % t == 0:
            best = t
    return best


def _rmsnorm_fwd(xs, gs, *, name, deps=()):
    n_rows, width = xs[0].shape
    n = len(xs)
    tr = _row_tile(n_rows, 384)

    def body(*refs):
        o_ref = refs[-1]
        parts = []
        for x_ref, g_ref in zip(refs[:n], refs[n:2 * n]):
            x = x_ref[...]
            r = lax.rsqrt(jnp.mean(x * x, axis=-1, keepdims=True) + NORM_EPS)
            parts.append(x * r * g_ref[...])
        o_ref[...] = (parts[0] if n == 1 else jnp.concatenate(parts, axis=1)).astype(BF16)

    return pl.pallas_call(
        body, name=name, grid=(n_rows // tr,),
        in_specs=[pl.BlockSpec((tr, width), lambda i: (i, 0))] * n + [pl.BlockSpec((1, width), lambda i: (0, 0))] * n
        + [_ANY] * len(deps),
        out_specs=pl.BlockSpec((tr, n * width), lambda i: (i, 0)),
        out_shape=jax.ShapeDtypeStruct((n_rows, n * width), BF16),
        compiler_params=_cparams(("parallel",)),
    )(*xs, *gs, *deps)


def _rmsnorm_bwd(xs, gs, dy, res, *, name):
    n_rows, width = xs[0].shape
    n = len(xs)
    tr = _row_tile(n_rows, 384)
    has_res = res is not None

    def body(*refs):
        x_refs, g_refs, dy_ref = refs[:n], refs[n:2 * n], refs[2 * n]
        res_ref = refs[2 * n + 1] if has_res else None
        outs = refs[2 * n + 1 + int(has_res):]
        dx_refs, dg_refs = outs[:n], outs[n:]
        i = pl.program_id(0)
        for c in range(n):
            x = x_refs[c][...]
            d = dy_ref[:, c * width:(c + 1) * width]
            r = lax.rsqrt(jnp.mean(x * x, axis=-1, keepdims=True) + NORM_EPS)
            xh = x * r
            gd = d * g_refs[c][...]
            dx = r * (gd - xh * jnp.mean(gd * xh, axis=-1, keepdims=True))
            if has_res:
                dx = dx + res_ref[...]
            dx_refs[c][...] = dx
            part = jnp.sum(d * xh, axis=0, keepdims=True)

            @pl.when(i == 0)
            def _():
                dg_refs[c][...] = part

            @pl.when(i > 0)
            def _():
                dg_refs[c][...] += part

    row_spec = pl.BlockSpec((tr, width), lambda i: (i, 0))
    vec_spec = pl.BlockSpec((1, width), lambda i: (0, 0))
    outs = pl.pallas_call(
        body, name=name, grid=(n_rows // tr,),
        in_specs=[row_spec] * n + [vec_spec] * n + [pl.BlockSpec((tr, n * width), lambda i: (i, 0))] + [row_spec] * int(has_res),
        out_specs=[row_spec] * n + [vec_spec] * n,
        out_shape=[jax.ShapeDtypeStruct((n_rows, width), F32)] * n + [jax.ShapeDtypeStruct((1, width), F32)] * n,
        compiler_params=_cparams(("arbitrary",)),
    )(*xs, *gs, dy, *([res] if has_res else []))
    return outs[:n], outs[n:]


_SCALE = 1.0 / math.sqrt(HEAD_DIM)
_DN_NT = (((1,), (1,)), ((), ()))
_DN_TN = (((0,), (0,)), ((), ()))


def _head_norm(x, g):
    r = lax.rsqrt(jnp.mean(x * x, axis=-1, keepdims=True) + NORM_EPS)
    return x * r * g, r


def _attn_geometry(n):
    rows = KV_GROUP * BLOCK
    i = lax.broadcasted_iota(jnp.int32, (rows, 3 * BLOCK), 0) % BLOCK
    j = lax.broadcasted_iota(jnp.int32, (rows, 3 * BLOCK), 1)
    t_pos = n * BLOCK + i - PAD
    is_meta = j < BLOCK
    m_pos = j - PAD
    s_pos = (n - 1) * BLOCK + (j - BLOCK) - PAD
    meta_ok = (j >= PAD) & (m_pos <= t_pos)
    band_ok = (s_pos >= N_META) & (s_pos <= t_pos) & (t_pos - s_pos < WINDOW)
    valid = (is_meta & meta_ok) | (jnp.logical_not(is_meta) & band_ok)
    dist = jnp.abs(t_pos - jnp.where(is_meta, m_pos, s_pos)).astype(F32)
    return valid, dist


def _slope_col(kv):
    g = lax.broadcasted_iota(jnp.int32, (KV_GROUP * BLOCK, 1), 0) // BLOCK
    col = jnp.zeros((KV_GROUP * BLOCK, 1), F32)
    for gi in range(KV_GROUP):
        col = jnp.where(g == gi, 2.0 ** (-8.0 * (kv * KV_GROUP + gi + 1) / N_HEADS), col)
    return col


def _sink_col(sink_ref, kv):
    g = lax.broadcasted_iota(jnp.int32, (KV_GROUP * BLOCK, 1), 0) // BLOCK
    col = jnp.zeros((KV_GROUP * BLOCK, 1), F32)
    for gi in range(KV_GROUP):
        h = kv * KV_GROUP + gi
        col = jnp.where(g == gi, sink_ref[0:1, h:h + 1], col)
    return col


def _stack_heads(x, kv):
    return jnp.concatenate([x[:, (kv * KV_GROUP + g) * HEAD_DIM:(kv * KV_GROUP + g + 1) * HEAD_DIM]
                            for g in range(KV_GROUP)], axis=0)


def _attn_scores(q_ref, k_refs, gq_ref, gk_ref, sink_ref, kv, valid, dist):
    qs = _stack_heads(q_ref[...], kv)
    kcat = jnp.concatenate([r[:, kv * HEAD_DIM:(kv + 1) * HEAD_DIM] for r in k_refs], axis=0)
    qn, rq = _head_norm(qs, gq_ref[...])
    kn, rk = _head_norm(kcat, gk_ref[...])
    s = lax.dot_general(qn.astype(BF16), kn.astype(BF16), _DN_NT, preferred_element_type=F32) * _SCALE
    s = jnp.where(valid, s - _slope_col(kv) * dist, NEG_INF)
    return qs, kcat, qn, kn, rq, rk, s, _sink_col(sink_ref, kv)


def _attn_specs():
    kq = ATTN_WIDTH // KV_WIDTH
    q_spec = pl.BlockSpec((BLOCK, ATTN_WIDTH), lambda n: (n, 0))
    kv_specs = []
    for col in (kq, kq + 1):
        kv_specs += [pl.BlockSpec((BLOCK, KV_WIDTH), lambda n, col=col: (0, col)),
                     pl.BlockSpec((BLOCK, KV_WIDTH), lambda n, col=col: (jnp.maximum(n - 1, 0), col)),
                     pl.BlockSpec((BLOCK, KV_WIDTH), lambda n, col=col: (n, col))]
    small = [pl.BlockSpec((1, HEAD_DIM), lambda n: (0, 0)), pl.BlockSpec((1, HEAD_DIM), lambda n: (0, 0)),
             pl.BlockSpec((1, N_HEADS), lambda n: (0, 0))]
    return q_spec, kv_specs, small


def _attn_fwd(proj, gq, gk, sinks, *, name):
    n_rows = proj.shape[0]
    q_spec, kv_specs, small = _attn_specs()

    def body(q_ref, k0, k1, k2, v0, v1, v2, gq_ref, gk_ref, sink_ref, o_ref, lse_ref):
        valid, dist = _attn_geometry(pl.program_id(0))
        o_parts, lse_parts = [], []
        for kv in range(N_KV_HEADS):
            _, _, _, _, _, _, s, sink = _attn_scores(q_ref, (k0, k1, k2), gq_ref, gk_ref, sink_ref, kv, valid, dist)
            vcat = jnp.concatenate([r[:, kv * HEAD_DIM:(kv + 1) * HEAD_DIM] for r in (v0, v1, v2)], axis=0)
            m = jnp.maximum(jnp.max(s, axis=-1, keepdims=True), sink)
            p = jnp.exp(s - m)
            l = jnp.sum(p, axis=-1, keepdims=True) + jnp.exp(sink - m)
            o = jnp.dot(p.astype(BF16), vcat.astype(BF16), preferred_element_type=F32) / l
            lse = m + jnp.log(l)
            o_parts += [o[g * BLOCK:(g + 1) * BLOCK] for g in range(KV_GROUP)]
            lse_parts += [lse[g * BLOCK:(g + 1) * BLOCK] for g in range(KV_GROUP)]
        o_ref[...] = jnp.concatenate(o_parts, axis=1)
        lse_ref[...] = jnp.concatenate(lse_parts, axis=1)

    return pl.pallas_call(
        body, name=name, grid=(n_rows // BLOCK,),
        in_specs=[q_spec] + kv_specs + small,
        out_specs=[pl.BlockSpec((BLOCK, ATTN_WIDTH), lambda n: (n, 0)), pl.BlockSpec((BLOCK, N_HEADS), lambda n: (n, 0))],
        out_shape=[jax.ShapeDtypeStruct((n_rows, ATTN_WIDTH), F32), jax.ShapeDtypeStruct((n_rows, N_HEADS), F32)],
        compiler_params=_cparams(("parallel",)),
    )(proj, proj, proj, proj, proj, proj, proj, gq, gk, sinks)


def _attn_bwd(proj, gq, gk, sinks, o, lse, do, *, name):
    n_rows = proj.shape[0]
    q_spec, kv_specs, small = _attn_specs()

    def body(q_ref, k0, k1, k2, v0, v1, v2, gq_ref, gk_ref, sink_ref, o_ref, lse_ref, do_ref,
             dq_ref, dk_ref, dv_ref, dgq_ref, dgk_ref, dsink_ref):
        n = pl.program_id(0)

        @pl.when(n == 0)
        def _():
            dk_ref[...] = jnp.zeros_like(dk_ref)
            dv_ref[...] = jnp.zeros_like(dv_ref)
            dgq_ref[...] = jnp.zeros_like(dgq_ref)
            dgk_ref[...] = jnp.zeros_like(dgk_ref)
            dsink_ref[...] = jnp.zeros_like(dsink_ref)

        valid, dist = _attn_geometry(n)
        dq_parts, dk_parts, dv_parts, dsink_parts = [], [], [], []
        dgq = jnp.zeros((1, HEAD_DIM), F32)
        dgk = jnp.zeros((1, HEAD_DIM), F32)
        for kv in range(N_KV_HEADS):
            qs, kcat, qn, kn, rq, rk, s, sink = _attn_scores(q_ref, (k0, k1, k2), gq_ref, gk_ref, sink_ref, kv, valid, dist)
            vcat = jnp.concatenate([r[:, kv * HEAD_DIM:(kv + 1) * HEAD_DIM] for r in (v0, v1, v2)], axis=0)
            os_ = _stack_heads(o_ref[...], kv)
            dos = _stack_heads(do_ref[...], kv)
            lse = jnp.concatenate([lse_ref[:, kv * KV_GROUP + g:kv * KV_GROUP + g + 1] for g in range(KV_GROUP)], axis=0)
            p = jnp.exp(s - lse)
            delta = jnp.sum(dos * os_, axis=-1, keepdims=True)
            dp = lax.dot_general(dos.astype(BF16), vcat.astype(BF16), _DN_NT, preferred_element_type=F32)
            ds = (p * (dp - delta)) * _SCALE
            dsink_rows = -jnp.exp(sink - lse) * delta
            dsink_parts += [jnp.sum(dsink_rows[g * BLOCK:(g + 1) * BLOCK], axis=0, keepdims=True) for g in range(KV_GROUP)]
            dv_parts.append(lax.dot_general(p.astype(BF16), dos.astype(BF16), _DN_TN, preferred_element_type=F32))
            dsb = ds.astype(BF16)
            dqn = jnp.dot(dsb, kn.astype(BF16), preferred_element_type=F32)
            dkn = lax.dot_general(dsb, qn.astype(BF16), _DN_TN, preferred_element_type=F32)
            qh = qs * rq
            gd = dqn * gq_ref[...]
            dqs = rq * (gd - qh * jnp.mean(gd * qh, axis=-1, keepdims=True))
            dgq = dgq + jnp.sum(dqn * qh, axis=0, keepdims=True)
            kh = kcat * rk
            gdk = dkn * gk_ref[...]
            dk_parts.append(rk * (gdk - kh * jnp.mean(gdk * kh, axis=-1, keepdims=True)))
            dgk = dgk + jnp.sum(dkn * kh, axis=0, keepdims=True)
            dq_parts += [dqs[g * BLOCK:(g + 1) * BLOCK] for g in range(KV_GROUP)]
        dq_ref[...] = jnp.concatenate(dq_parts, axis=1)
        dkc = jnp.concatenate(dk_parts, axis=1)
        dvc = jnp.concatenate(dv_parts, axis=1)
        prev = pl.multiple_of(jnp.maximum(n - 1, 0) * BLOCK, BLOCK)
        cur = pl.multiple_of(n * BLOCK, BLOCK)
        for acc_ref, val in ((dk_ref, dkc), (dv_ref, dvc)):
            acc_ref[0:BLOCK, :] += val[0:BLOCK]
            acc_ref[pl.ds(prev, BLOCK), :] += val[BLOCK:2 * BLOCK]
            acc_ref[pl.ds(cur, BLOCK), :] += val[2 * BLOCK:3 * BLOCK]
        dgq_ref[...] += dgq
        dgk_ref[...] += dgk
        dsink_ref[...] += jnp.concatenate(dsink_parts, axis=1)

    blk = lambda w: pl.BlockSpec((BLOCK, w), lambda n: (n, 0))
    full = lambda r, w: pl.BlockSpec((r, w), lambda n: (0, 0))
    return pl.pallas_call(
        body, name=name, grid=(n_rows // BLOCK,),
        in_specs=[q_spec] + kv_specs + small + [blk(ATTN_WIDTH), blk(N_HEADS), blk(ATTN_WIDTH)],
        out_specs=[blk(ATTN_WIDTH), full(n_rows, KV_WIDTH), full(n_rows, KV_WIDTH),
                   full(1, HEAD_DIM), full(1, HEAD_DIM), full(1, N_HEADS)],
        out_shape=[jax.ShapeDtypeStruct((n_rows, ATTN_WIDTH), F32), jax.ShapeDtypeStruct((n_rows, KV_WIDTH), F32),
                   jax.ShapeDtypeStruct((n_rows, KV_WIDTH), F32), jax.ShapeDtypeStruct((1, HEAD_DIM), F32),
                   jax.ShapeDtypeStruct((1, HEAD_DIM), F32), jax.ShapeDtypeStruct((1, N_HEADS), F32)],
        compiler_params=_cparams(("arbitrary",)),
    )(proj, proj, proj, proj, proj, proj, proj, gq, gk, sinks, o, lse, do)


SSM_LAGS = 8
SLAB_G = 128 // SSM_GROUP_CH
N_SLABS = SSM_GROUPS // SLAB_G
SLAB_STATE = SLAB_G * SSM_STATE
U_COL = (ATTN_WIDTH + 2 * KV_WIDTH) // 128


def _ssm_prep(lam_re, lam_im, log_step, b_re, b_im, c_re, c_im):
    lam = lax.complex(lam_re, lam_im)
    delta = jnp.exp(log_step)[:, None]
    lam_bar = jnp.exp(lam * delta)
    b_bar = ((lam_bar - 1.0) / lam)[..., None] * lax.complex(b_re, b_im)
    pw = [jnp.ones_like(lam_bar)]
    for _ in range(SSM_LAGS):
        pw.append(pw[-1] * lam_bar)
    w = jnp.stack(pw[:SSM_LAGS])[..., None] * b_bar[None]
    wri = jnp.stack([jnp.real(w), jnp.imag(w)]).reshape(2, SSM_LAGS, N_SLABS, SLAB_G, SSM_STATE, SSM_GROUP_CH)
    eye = jnp.eye(SLAB_G, dtype=F32)
    wt = jnp.transpose(wri, (2, 1, 3, 5, 0, 4))
    wmat = wt[:, :, :, :, :, None, :] * eye[None, None, :, None, None, :, None]
    wmat = wmat.reshape(N_SLABS, SSM_LAGS * 128, 2 * SLAB_STATE)
    cri = jnp.stack([c_re, -c_im]).reshape(2, N_SLABS, SLAB_G, SSM_GROUP_CH, SSM_STATE)
    ct = jnp.transpose(cri, (1, 0, 2, 4, 3))
    cmat = ct[:, :, :, :, None, :] * eye[None, None, :, None, :, None]
    cmat = cmat.reshape(N_SLABS, 2 * SLAB_STATE, 128)
    l8 = pw[SSM_LAGS]
    lam8 = jnp.concatenate([jnp.real(l8).reshape(N_SLABS, 1, SLAB_STATE), jnp.imag(l8).reshape(N_SLABS, 1, SLAB_STATE)], axis=2)
    return wmat, cmat, lam8


def _lagged(u, up, t_rows):
    ue = jnp.concatenate([up, u], axis=0)
    return jnp.concatenate([ue[SSM_LAGS - tau:SSM_LAGS - tau + t_rows] for tau in range(SSM_LAGS)], axis=1).astype(BF16)


def _ssm_fwd(proj, wmat, cmat, lam8, dvec, *, name):
    n_rows = proj.shape[0]
    tt = _row_tile(n_rows, 384)
    n_t = n_rows // tt
    sw = 2 * SLAB_STATE
    hs = SLAB_STATE

    def body(u_ref, up_ref, w_ref, c_ref, l_ref, d_ref, y_ref, x_ref, carry_ref):
        t = pl.program_id(1)

        @pl.when(t == 0)
        def _():
            carry_ref[...] = jnp.zeros_like(carry_ref)

        u = u_ref[...]
        up = jnp.where(t > 0, up_ref[...], 0.0)
        x_ref[...] = jnp.dot(_lagged(u, up, tt), w_ref[...], preferred_element_type=F32)
        ar = jnp.broadcast_to(l_ref[:, :hs], (8, hs))
        ai = jnp.broadcast_to(l_ref[:, hs:], (8, hs))

        def step(b, c):
            xr, xi = c
            r0 = pl.multiple_of(b * 8, 8)
            w = x_ref[pl.ds(r0, 8), :]
            nr = w[:, :hs] + ar * xr - ai * xi
            ni = w[:, hs:] + ar * xi + ai * xr
            x_ref[pl.ds(r0, 8), :] = jnp.concatenate([nr, ni], axis=1)
            return nr, ni

        xr, xi = lax.fori_loop(0, tt // 8, step, (carry_ref[:, :hs], carry_ref[:, hs:]))
        carry_ref[...] = jnp.concatenate([xr, xi], axis=1)
        y_ref[...] = jnp.dot(x_ref[...].astype(BF16), c_ref[...], preferred_element_type=F32) + d_ref[...] * u

    return pl.pallas_call(
        body, name=name, grid=(N_SLABS, n_t),
        in_specs=[pl.BlockSpec((tt, 128), lambda j, t: (t, U_COL + j)),
                  pl.BlockSpec((8, 128), lambda j, t: (jnp.maximum(t * (tt // 8) - 1, 0), U_COL + j)),
                  pl.BlockSpec((None, SSM_LAGS * 128, sw), lambda j, t: (j, 0, 0)),
                  pl.BlockSpec((None, sw, 128), lambda j, t: (j, 0, 0)),
                  pl.BlockSpec((None, 1, sw), lambda j, t: (j, 0, 0)),
                  pl.BlockSpec((1, 128), lambda j, t: (0, j))],
        out_specs=[pl.BlockSpec((tt, 128), lambda j, t: (t, j)), pl.BlockSpec((tt, sw), lambda j, t: (t, j))],
        out_shape=[jax.ShapeDtypeStruct((n_rows, SSM_WIDTH), F32), jax.ShapeDtypeStruct((n_rows, N_SLABS * sw), F32)],
        scratch_shapes=[pltpu.VMEM((8, sw), F32)],
        compiler_params=_cparams(("parallel", "arbitrary")),
    )(proj, proj, wmat, cmat, lam8, dvec)


def _ssm_bwd(proj, xs, dy, wmat, cmat, lam8, dvec, *, name):
    n_rows = proj.shape[0]
    tt = _row_tile(n_rows, 384)
    n_t = n_rows // tt
    sw = 2 * SLAB_STATE
    hs = SLAB_STATE

    def body(u_ref, up_ref, x_ref, xp_ref, dy_ref, w_ref, c_ref, l_ref, d_ref,
             du_ref, dw_ref, dc_ref, dl_ref, dd_ref, a_ref, carry_ref, head_ref):
        t = pl.program_id(1)
        ti = n_t - 1 - t

        @pl.when(t == 0)
        def _():
            carry_ref[...] = jnp.zeros_like(carry_ref)
            head_ref[...] = jnp.zeros_like(head_ref)
            dw_ref[...] = jnp.zeros_like(dw_ref)
            dc_ref[...] = jnp.zeros_like(dc_ref)
            dl_ref[...] = jnp.zeros_like(dl_ref)
            dd_ref[...] = jnp.zeros_like(dd_ref)

        u = u_ref[...]
        up = jnp.where(ti > 0, up_ref[...], 0.0)
        ucat = _lagged(u, up, tt)
        dyv = dy_ref[...]
        dyb = dyv.astype(BF16)
        a_ref[...] = lax.dot_general(dyb, c_ref[...], _DN_NT, preferred_element_type=F32)
        lr = jnp.broadcast_to(l_ref[:, :hs], (8, hs))
        li = jnp.broadcast_to(l_ref[:, hs:], (8, hs))

        def step(i, c):
            cr, ci = c
            r0 = pl.multiple_of((tt // 8 - 1 - i) * 8, 8)
            g = a_ref[pl.ds(r0, 8), :]
            nr = g[:, :hs] + lr * cr + li * ci
            ni = g[:, hs:] + lr * ci - li * cr
            a_ref[pl.ds(r0, 8), :] = jnp.concatenate([nr, ni], axis=1)
            return nr, ni

        cr, ci = lax.fori_loop(0, tt // 8, step, (carry_ref[:, :hs], carry_ref[:, hs:]))
        carry_ref[...] = jnp.concatenate([cr, ci], axis=1)

        a = a_ref[...]
        xv = x_ref[...]
        xprev = jnp.where(ti > 0, xp_ref[...], 0.0)
        xsh = jnp.concatenate([xprev, xv[:tt - SSM_LAGS]], axis=0)
        a_re, a_im, x_re, x_im = a[:, :hs], a[:, hs:], xsh[:, :hs], xsh[:, hs:]
        dl_ref[...] += jnp.concatenate([jnp.sum(a_re * x_re + a_im * x_im, axis=0, keepdims=True),
                                        jnp.sum(a_im * x_re - a_re * x_im, axis=0, keepdims=True)], axis=1)
        ab = a.astype(BF16)
        dw_ref[...] += lax.dot_general(ucat, ab, _DN_TN, preferred_element_type=F32)
        duc = lax.dot_general(ab, w_ref[...], _DN_NT, preferred_element_type=F32)
        ext = jnp.concatenate([duc, head_ref[...]], axis=0)
        du = d_ref[...] * dyv
        for tau in range(SSM_LAGS):
            du = du + ext[tau:tau + tt, tau * 128:(tau + 1) * 128]
        head_ref[...] = duc[0:8]
        row = ti * tt + lax.broadcasted_iota(jnp.int32, (tt, 128), 0)
        du_ref[...] = jnp.where(row >= PAD, du, 0.0)
        dd_ref[...] += jnp.sum(dyv * u, axis=0, keepdims=True)
        dc_ref[...] += lax.dot_general(xv.astype(BF16), dyb, _DN_TN, preferred_element_type=F32)

    rt = lambda t: n_t - 1 - t
    prev8 = lambda t: jnp.maximum(rt(t) * (tt // 8) - 1, 0)
    return pl.pallas_call(
        body, name=name, grid=(N_SLABS, n_t),
        in_specs=[pl.BlockSpec((tt, 128), lambda j, t: (rt(t), U_COL + j)),
                  pl.BlockSpec((8, 128), lambda j, t: (prev8(t), U_COL + j)),
                  pl.BlockSpec((tt, sw), lambda j, t: (rt(t), j)),
                  pl.BlockSpec((8, sw), lambda j, t: (prev8(t), j)),
                  pl.BlockSpec((tt, 128), lambda j, t: (rt(t), j)),
                  pl.BlockSpec((None, SSM_LAGS * 128, sw), lambda j, t: (j, 0, 0)),
                  pl.BlockSpec((None, sw, 128), lambda j, t: (j, 0, 0)),
                  pl.BlockSpec((None, 1, sw), lambda j, t: (j, 0, 0)),
                  pl.BlockSpec((1, 128), lambda j, t: (0, j))],
        out_specs=[pl.BlockSpec((tt, 128), lambda j, t: (rt(t), j)),
                   pl.BlockSpec((None, SSM_LAGS * 128, sw), lambda j, t: (j, 0, 0)),
                   pl.BlockSpec((None, sw, 128), lambda j, t: (j, 0, 0)),
                   pl.BlockSpec((None, 1, sw), lambda j, t: (j, 0, 0)),
                   pl.BlockSpec((1, 128), lambda j, t: (0, j))],
        out_shape=[jax.ShapeDtypeStruct((n_rows, SSM_WIDTH), F32),
                   jax.ShapeDtypeStruct((N_SLABS, SSM_LAGS * 128, sw), F32),
                   jax.ShapeDtypeStruct((N_SLABS, sw, 128), F32),
                   jax.ShapeDtypeStruct((N_SLABS, 1, sw), F32),
                   jax.ShapeDtypeStruct((1, SSM_WIDTH), F32)],
        scratch_shapes=[pltpu.VMEM((tt, sw), F32), pltpu.VMEM((8, sw), F32), pltpu.VMEM((8, sw), F32)],
        compiler_params=_cparams(("parallel", "arbitrary")),
    )(proj, proj, xs, xs, dy, wmat, cmat, lam8, dvec)


_GELU_C = math.sqrt(2.0 / math.pi)
_GELU_A = 0.044715


def _gelu(y):
    th = jnp.tanh(_GELU_C * (y + _GELU_A * y * y * y))
    return 0.5 * y * (1.0 + th), th


def _glu_fwd(y, w, b, *, name):
    n_rows, width = y.shape
    tr = _row_tile(n_rows, 384)

    def body(y_ref, w_ref, b_ref, o_ref):
        g, _ = _gelu(y_ref[...])
        z = jnp.dot(g.astype(BF16), w_ref[...], preferred_element_type=F32) + b_ref[...]
        o_ref[...] = g * jax.nn.sigmoid(z)

    return pl.pallas_call(
        body, name=name, grid=(n_rows // tr,),
        in_specs=[pl.BlockSpec((tr, width), lambda i: (i, 0)), pl.BlockSpec((width, width), lambda i: (0, 0)),
                  pl.BlockSpec((1, width), lambda i: (0, 0))],
        out_specs=pl.BlockSpec((tr, width), lambda i: (i, 0)),
        out_shape=jax.ShapeDtypeStruct((n_rows, width), F32),
        compiler_params=_cparams(("parallel",)),
    )(y, w, b)


def _glu_bwd(y, w, b, dout, *, name):
    n_rows, width = y.shape
    tr = _row_tile(n_rows, 384)

    def body(y_ref, w_ref, b_ref, do_ref, dy_ref, g_ref, dz_ref, db_ref):
        i = pl.program_id(0)
        yv = y_ref[...]
        g, th = _gelu(yv)
        gb = g.astype(BF16)
        z = jnp.dot(gb, w_ref[...], preferred_element_type=F32) + b_ref[...]
        sg = jax.nn.sigmoid(z)
        do = do_ref[...]
        dz = do * g * sg * (1.0 - sg)
        dzb = dz.astype(BF16)
        dg = do * sg + lax.dot_general(dzb, w_ref[...], _DN_NT, preferred_element_type=F32)
        dgelu = 0.5 * (1.0 + th) + 0.5 * yv * (1.0 - th * th) * _GELU_C * (1.0 + 3.0 * _GELU_A * yv * yv)
        dy_ref[...] = dg * dgelu
        g_ref[...] = gb
        dz_ref[...] = dzb
        part = jnp.sum(dz, axis=0, keepdims=True)

        @pl.when(i == 0)
        def _():
            db_ref[...] = part

        @pl.when(i > 0)
        def _():
            db_ref[...] += part

    row = pl.BlockSpec((tr, width), lambda i: (i, 0))
    vec = pl.BlockSpec((1, width), lambda i: (0, 0))
    return pl.pallas_call(
        body, name=name, grid=(n_rows // tr,),
        in_specs=[row, pl.BlockSpec((width, width), lambda i: (0, 0)), vec, row],
        out_specs=[row, row, row, vec],
        out_shape=[jax.ShapeDtypeStruct((n_rows, width), F32), jax.ShapeDtypeStruct((n_rows, width), BF16),
                   jax.ShapeDtypeStruct((n_rows, width), BF16), jax.ShapeDtypeStruct((1, width), F32)],
        compiler_params=_cparams(("arbitrary",)),
    )(y, w, b, dout)


def _loss_head(h, target, *, name):
    n_rows, width = h.shape

    def body(h_ref, t_ref, dh_ref, loss_ref):
        i = pl.program_id(0)

        @pl.when(i == 0)
        def _():
            dh_ref[...] = jnp.zeros_like(dh_ref)
            loss_ref[...] = jnp.zeros_like(loss_ref)

        @pl.when(i > 0)
        def _():
            err = h_ref[...] - t_ref[...]
            dh_ref[...] = err * (1.0 / width)
            loss_ref[...] += (0.5 / width) * jnp.sum(err * err, keepdims=True)

    return pl.pallas_call(
        body, name=name, grid=(n_rows // BLOCK,),
        in_specs=[pl.BlockSpec((BLOCK, width), lambda i: (i, 0)),
                  pl.BlockSpec((BLOCK, width), lambda i: (jnp.maximum(i - 1, 0), 0))],
        out_specs=[pl.BlockSpec((BLOCK, width), lambda i: (i, 0)), pl.BlockSpec((1, 1), lambda i: (0, 0))],
        out_shape=[jax.ShapeDtypeStruct((n_rows, width), F32), jax.ShapeDtypeStruct((1, 1), F32)],
        compiler_params=_cparams(("arbitrary",)),
    )(h, target)


def _elem_rows(n_rows, n_cols, bytes_per_row_elem):
    cap = max(16, (4 * 1024 * 1024) // (n_cols * bytes_per_row_elem))
    best = None
    for t in range(16, min(n_rows, cap) + 1, 16):
        if n_rows % t == 0:
            best = t
    return best or n_rows


def _cast_bf16(x, *, name):
    n_rows, n_cols = x.shape
    tr = _elem_rows(n_rows, n_cols, 4)

    def body(x_ref, o_ref):
        o_ref[...] = x_ref[...].astype(BF16)

    spec = pl.BlockSpec((tr, n_cols), lambda i: (i, 0))
    return pl.pallas_call(body, name=name, grid=(n_rows // tr,), in_specs=[spec], out_specs=spec,
                          out_shape=jax.ShapeDtypeStruct(x.shape, BF16), compiler_params=_cparams(("parallel",)))(x)


def _adamw(w, m, v, parts, *, name):
    n_rows, n_cols = w.shape
    n_parts = parts.shape[0]
    tr = _elem_rows(n_rows, n_cols, 4 * (8 + n_parts))
    c1 = 1.0 / (1.0 - ADAM_B1 ** ADAM_STEP)
    c2 = 1.0 / (1.0 - ADAM_B2 ** ADAM_STEP)

    def body(w_ref, m_ref, v_ref, p_ref, g_ref, d_ref, nm_ref, nv_ref):
        g = p_ref[0].astype(F32)
        for k in range(1, n_parts):
            g = g + p_ref[k].astype(F32)
        nm = ADAM_B1 * m_ref[...] + (1.0 - ADAM_B1) * g
        nv = ADAM_B2 * v_ref[...] + (1.0 - ADAM_B2) * (g * g)
        g_ref[...] = g
        nm_ref[...] = nm
        nv_ref[...] = nv
        d_ref[...] = -ADAM_LR * ((nm * c1) / (jnp.sqrt(nv * c2) + ADAM_EPS) + ADAM_WD * w_ref[...])

    spec = pl.BlockSpec((tr, n_cols), lambda i: (i, 0))
    return pl.pallas_call(
        body, name=name, grid=(n_rows // tr,),
        in_specs=[spec, spec, spec, pl.BlockSpec((n_parts, tr, n_cols), lambda i: (0, i, 0))],
        out_specs=[spec] * 4, out_shape=[jax.ShapeDtypeStruct(w.shape, F32)] * 4,
        compiler_params=_cparams(("parallel",)),
    )(w, m, v, parts)


def _sum_parts(parts, *, name):
    n_parts, n_rows, n_cols = parts.shape
    tr = _elem_rows(n_rows, n_cols, 4 * (1 + n_parts))

    def body(p_ref, o_ref):
        g = p_ref[0].astype(F32)
        for k in range(1, n_parts):
            g = g + p_ref[k].astype(F32)
        o_ref[...] = g

    return pl.pallas_call(
        body, name=name, grid=(n_rows // tr,),
        in_specs=[pl.BlockSpec((n_parts, tr, n_cols), lambda i: (0, i, 0))],
        out_specs=pl.BlockSpec((tr, n_cols), lambda i: (i, 0)),
        out_shape=jax.ShapeDtypeStruct((n_rows, n_cols), F32), compiler_params=_cparams(("parallel",)),
    )(parts)


BIG = ("w_in", "w_glu", "w_out", "w_up", "w_down")
SMALL = ("norm_mix_g", "q_norm_g", "k_norm_g", "attn_sinks", "ssm_lambda_re", "ssm_lambda_im", "ssm_log_step",
         "ssm_b_re", "ssm_b_im", "ssm_c_re", "ssm_c_im", "ssm_d", "b_glu", "attn_out_g", "ssm_out_g", "norm_mlp_g")
_SSM_NAMES = ("ssm_lambda_re", "ssm_lambda_im", "ssm_log_step", "ssm_b_re", "ssm_b_im", "ssm_c_re", "ssm_c_im")


def _divisor(n, cands):
    for c in cands:
        if n % c == 0:
            return c
    return n


def _mm(a, b, mode, name, outs=(F32,), epilogue=_ident, tiles=(), deps=(), blocked=False):
    if mode == "nn":
        m, k = a.shape
        n = b.shape[0] * b.shape[2] if blocked else b.shape[1]
    elif mode == "nt":
        m, k = a.shape
        n = b.shape[1] if blocked else b.shape[0]
    else:
        (k, m), n = a.shape, b.shape[1]
    if mode == "tn":
        tm, tk = _divisor(m, (1024,)), _row_tile(k, 1408)
    else:
        tm, tk = _row_tile(m, 1408), _divisor(k, (1024, 1280, 768, 512))
    tn = _divisor(n, (1024, 1280, 512))
    return _matmul(a, b, mode=mode, tm=tm, tn=tn, tk=tk, outs=list(outs), epilogue=epilogue, tiles=tiles, deps=deps,
                   blocked=blocked, name=name)


def _add_tile(acc, res):
    return (acc + res,)


def _relu_sq(acc):
    r = jnp.maximum(acc, 0.0)
    return r, r * r


def _relu_sq_bwd(acc, r):
    return (acc * (2.0 * r.astype(F32)),)


def _row(v):
    return v.reshape(1, -1)


def _layer_fwd(hres, wts, sp, l, deps=()):
    tag = f"_l{l}"
    hb = _rmsnorm_fwd([hres], [_row(sp["norm_mix_g"])], name="norm_mix" + tag, deps=deps)
    proj, = _mm(hb, wts["w_in"], "nn", "proj" + tag)
    gq, gk, sinks = _row(sp["q_norm_g"]), _row(sp["k_norm_g"]), _row(sp["attn_sinks"])
    o, lse = _attn_fwd(proj, gq, gk, sinks, name="attn_fwd" + tag)
    (wmat, cmat, lam8), prep_vjp = jax.vjp(_ssm_prep, *[sp[n] for n in _SSM_NAMES])
    wmat, cmat = wmat.astype(BF16), cmat.astype(BF16)
    y, xs = _ssm_fwd(proj, wmat, cmat, lam8, _row(sp["ssm_d"]), name="ssm_fwd" + tag)
    s = _glu_fwd(y, wts["w_glu"], _row(sp["b_glu"]), name="glu_fwd" + tag)
    mix = _rmsnorm_fwd([o, s], [_row(sp["attn_out_g"]), _row(sp["ssm_out_g"])], name="norm_out" + tag)
    hres2, = _mm(mix, wts["w_out"], "nn", "out_proj" + tag, epilogue=_add_tile, tiles=(hres,))
    h2 = _rmsnorm_fwd([hres2], [_row(sp["norm_mlp_g"])], name="norm_mlp" + tag)
    r, act = _mm(h2, wts["w_up"], "nn", "mlp_up" + tag, outs=(BF16, BF16), epilogue=_relu_sq, blocked=True)
    hres3, = _mm(act, wts["w_down"], "nn", "mlp_down" + tag, epilogue=_add_tile, tiles=(hres2,))
    saved = dict(hres=hres, hb=hb, proj=proj, o=o, lse=lse, wmat=wmat, cmat=cmat, lam8=lam8, prep_vjp=prep_vjp,
                 y=y, xs=xs, s=s, mix=mix, hres2=hres2, h2=h2, r=r, act=act)
    return hres3, saved


def _layer_bwd(dres, wts, sp, sv, l, deps=()):
    tag = f"_l{l}"
    gb, gs = {}, {}
    d_up, = _mm(dres, wts["w_down"], "nt", "mlp_down_dx" + tag, outs=(BF16,), epilogue=_relu_sq_bwd, tiles=(sv["r"],),
                deps=deps)
    gb["w_down"], = _mm(sv["act"], dres, "tn", "mlp_down_dw" + tag, outs=(BF16,))
    gb["w_up"], = _mm(sv["h2"], d_up, "tn", "mlp_up_dw" + tag, outs=(BF16,), blocked=True)
    dh2, = _mm(d_up, wts["w_up"], "nt", "mlp_up_dx" + tag, blocked=True)
    (dres2,), (dg,) = _rmsnorm_bwd([sv["hres2"]], [_row(sp["norm_mlp_g"])], dh2, dres, name="norm_mlp_bwd" + tag)
    gs["norm_mlp_g"] = dg
    dmix, = _mm(dres2, wts["w_out"], "nt", "out_proj_dx" + tag)
    gb["w_out"], = _mm(sv["mix"], dres2, "tn", "out_proj_dw" + tag, outs=(BF16,))
    (do, ds), (dga, dgs) = _rmsnorm_bwd([sv["o"], sv["s"]], [_row(sp["attn_out_g"]), _row(sp["ssm_out_g"])], dmix, None,
                                        name="norm_out_bwd" + tag)
    gs["attn_out_g"], gs["ssm_out_g"] = dga, dgs
    dy, g_b, dz_b, db = _glu_bwd(sv["y"], wts["w_glu"], _row(sp["b_glu"]), ds, name="glu_bwd" + tag)
    gs["b_glu"] = db
    gb["w_glu"], = _mm(g_b, dz_b, "tn", "glu_dw" + tag, outs=(BF16,))
    du, dwmat, dcmat, dlam8, dd = _ssm_bwd(sv["proj"], sv["xs"], dy, sv["wmat"], sv["cmat"], sv["lam8"], _row(sp["ssm_d"]),
                                           name="ssm_bwd" + tag)
    gs["ssm_d"] = dd
    for n, g in zip(_SSM_NAMES, sv["prep_vjp"]((dwmat, dcmat, dlam8))):
        gs[n] = g
    dq, dk, dv, dgq, dgk, dsinks = _attn_bwd(sv["proj"], _row(sp["q_norm_g"]), _row(sp["k_norm_g"]), _row(sp["attn_sinks"]),
                                             sv["o"], sv["lse"], do, name="attn_bwd" + tag)
    gs["q_norm_g"], gs["k_norm_g"], gs["attn_sinks"] = dgq, dgk, dsinks
    dproj = jnp.concatenate([dq, dk, dv, du], axis=1)
    gb["w_in"], = _mm(sv["hb"], dproj, "tn", "proj_dw" + tag, outs=(BF16,))
    dh, = _mm(dproj, wts["w_in"], "nt", "proj_dx" + tag)
    (dres_in,), (dg,) = _rmsnorm_bwd([sv["hres"]], [_row(sp["norm_mix_g"])], dh, dres2, name="norm_mix_bwd" + tag)
    gs["norm_mix_g"] = dg
    return dres_in, gb, gs


def _local_step(x, target, meta, sp, weights_for_layer, grads_of_layer):
    h = jnp.concatenate([jnp.zeros((PAD, x.shape[1]), F32), meta, x], axis=0)
    saved, wts = [], []
    for l in range(DEPTH):
        w, deps = weights_for_layer(l, h)
        wts.append(w)
        h, sv = _layer_fwd(h, w, {n: sp[n][l] for n in SMALL}, l, deps)
        saved.append(sv)
    dh, loss = _loss_head(h, target, name="loss_head")
    gsmall = {n: [None] * DEPTH for n in SMALL}
    deps = ()
    for l in reversed(range(DEPTH)):
        dh, gb, gs = _layer_bwd(dh, wts[l], {n: sp[n][l] for n in SMALL}, saved[l], l, deps)
        deps = grads_of_layer(l, gb, dh)
        for n in SMALL:
            gsmall[n][l] = gs[n].reshape(sp[n][l].shape)
    return loss, dh, gsmall


def _all_gather(x, *, name):
    def body(x_ref, out_ref, send_sems, recv_sems, local_sem):
        x, y, c = lax.axis_index("x"), lax.axis_index("y"), lax.axis_index("c")
        me, sibling = (x, y, c), (x, y, 1 - c)
        chips = [(1 - x, y), (x, 1 - y), (1 - x, 1 - y)]

        def slot(px, py, pc):
            return out_ref.at[4 * px + 2 * py + pc]

        def copy(k, block, to, src=None):
            return pltpu.make_async_remote_copy(
                src_ref=slot(*block) if src is None else src, dst_ref=slot(*block),
                send_sem=send_sems.at[k], recv_sem=recv_sems.at[k], device_id=to, device_id_type=_MESH)

        mine = pltpu.make_async_copy(x_ref, slot(*me), local_sem)
        mine.start()
        first = [copy(0, me, sibling, src=x_ref)]
        first += [copy(1 + j, me, (*chip, c), src=x_ref) for j, chip in enumerate(chips)]
        for cp in first:
            cp.start()
        passed = [copy(4 + j, (*chip, c), sibling) for j, chip in enumerate(chips)]
        for j, chip in enumerate(chips):
            copy(1 + j, (*chip, c), me).wait_recv()
            passed[j].start()
        copy(0, sibling, me).wait_recv()
        for j, chip in enumerate(chips):
            copy(4 + j, (*chip, 1 - c), me).wait_recv()
        for cp in first + passed:
            cp.wait_send()
        mine.wait()

    return pl.pallas_call(
        body, name=name, out_shape=jax.ShapeDtypeStruct((N_DEV,) + x.shape, x.dtype),
        in_specs=[_ANY], out_specs=_ANY,
        scratch_shapes=[pltpu.SemaphoreType.DMA((7,)), pltpu.SemaphoreType.DMA((7,)), pltpu.SemaphoreType.DMA],
    )(x)


def _exchange(g, *, name):
    def body(g_ref, r_ref, send_sems, recv_sems, local_sem):
        x, y, c = lax.axis_index("x"), lax.axis_index("y"), lax.axis_index("c")
        me = 4 * x + 2 * y + c
        mine = pltpu.make_async_copy(g_ref.at[me], r_ref.at[me], local_sem)
        mine.start()

        def peer(k):
            px, py, pc = (x + (k >> 2)) % 2, (y + ((k >> 1) & 1)) % 2, (c + (k & 1)) % 2
            return (px, py, pc), 4 * px + 2 * py + pc

        def copy(k, src_block, dst_block):
            to, _ = peer(k)
            return pltpu.make_async_remote_copy(
                src_ref=g_ref.at[src_block], dst_ref=r_ref.at[dst_block],
                send_sem=send_sems.at[k - 1], recv_sem=recv_sems.at[k - 1], device_id=to, device_id_type=_MESH)

        sends = [copy(k, peer(k)[1], me) for k in range(1, N_DEV)]
        for cp in sends:
            cp.start()
        for k in range(1, N_DEV):
            copy(k, me, peer(k)[1]).wait_recv()
        for cp in sends:
            cp.wait_send()
        mine.wait()

    return pl.pallas_call(
        body, name=name, out_shape=jax.ShapeDtypeStruct(g.shape, g.dtype),
        in_specs=[_ANY], out_specs=_ANY,
        scratch_shapes=[pltpu.SemaphoreType.DMA((7,)), pltpu.SemaphoreType.DMA((7,)), pltpu.SemaphoreType.DMA],
    )(g)


_HBM = pl.BlockSpec(memory_space=pltpu.HBM)
_SEM = pl.BlockSpec(memory_space=pltpu.SEMAPHORE)
_EFFECT = pltpu.SideEffectType.DATAFLOW_SIDE_EFFECTING
N_PEERS = N_DEV - 1


def _me_and_peers():
    x, y, c = lax.axis_index("x"), lax.axis_index("y"), lax.axis_index("c")
    peers = []
    for k in range(1, N_DEV):
        px, py, pc = (x + (k >> 2)) % 2, (y + ((k >> 1) & 1)) % 2, (c + (k & 1)) % 2
        peers.append(((px, py, pc), 4 * px + 2 * py + pc))
    return 4 * x + 2 * y + c, peers


def _send_start(srcs, after, *, per_peer, name):
    n_t = len(srcs)
    blks = [s.shape[1:] if per_peer else s.shape for s in srcs]
    lands = [lax.empty((N_DEV,) + b, s.dtype) for b, s in zip(blks, srcs)]

    def body(*refs):
        src_refs, land_refs = refs[:n_t], refs[n_t:2 * n_t]
        send_sems, recv_sems = refs[2 * n_t + 1], refs[2 * n_t + 2]
        token = refs[-1]
        me, peers = _me_and_peers()
        for t in range(n_t):
            for k, (to, idx) in enumerate(peers):
                pltpu.make_async_remote_copy(
                    src_ref=src_refs[t].at[idx] if per_peer else src_refs[t], dst_ref=land_refs[t].at[me],
                    send_sem=send_sems.at[t * N_PEERS + k], recv_sem=recv_sems.at[t * N_PEERS + k],
                    device_id=to, device_id_type=_MESH).start()
        token[...] = jnp.zeros_like(token)

    sems = pltpu.SemaphoreType.DMA((n_t * N_PEERS,))
    outs = pl.pallas_call(
        body, name=name,
        out_shape=(sems, sems, *[pltpu.HBM(s.shape, s.dtype) for s in srcs], *[pltpu.HBM(z.shape, z.dtype) for z in lands],
                   jax.ShapeDtypeStruct((8, 128), F32)),
        in_specs=[_HBM] * (2 * n_t) + [_ANY],
        out_specs=(_SEM, _SEM, *[_HBM] * (2 * n_t), pl.BlockSpec(memory_space=pltpu.VMEM)),
        input_output_aliases={i: 2 + i for i in range(2 * n_t)},
        compiler_params=pltpu.CompilerParams(has_side_effects=_EFFECT),
    )(*[pltpu.with_memory_space_constraint(s, pltpu.HBM) for s in srcs],
      *[pltpu.with_memory_space_constraint(z, pltpu.HBM) for z in lands], after)
    return outs[0], outs[1], list(outs[2:2 + n_t]), list(outs[2 + n_t:2 + 2 * n_t]), outs[-1]


def _send_wait(handles, after, *, per_peer, name):
    send_sems, recv_sems, srcs, lands = handles
    n_t = len(srcs)

    def body(*refs):
        src_refs, land_refs = refs[:n_t], refs[n_t:2 * n_t]
        send_sems, recv_sems = refs[2 * n_t], refs[2 * n_t + 1]
        land_outs = refs[2 * n_t + 3 + n_t:]
        me, peers = _me_and_peers()
        for t in range(n_t):
            for k, (to, idx) in enumerate(peers):
                cp = pltpu.make_async_remote_copy(
                    src_ref=src_refs[t].at[idx] if per_peer else src_refs[t], dst_ref=land_refs[t].at[idx],
                    send_sem=send_sems.at[t * N_PEERS + k], recv_sem=recv_sems.at[t * N_PEERS + k],
                    device_id=to, device_id_type=_MESH)
                cp.wait_send()
                cp.wait_recv()
        for t in range(n_t):
            pltpu.sync_copy(src_refs[t].at[me] if per_peer else src_refs[t], land_outs[t].at[me])

    outs = pl.pallas_call(
        body, name=name,
        out_shape=(*[pltpu.HBM(s.shape, s.dtype) for s in srcs], *[pltpu.HBM(z.shape, z.dtype) for z in lands]),
        in_specs=[_HBM] * (2 * n_t) + [_SEM, _SEM, _ANY], out_specs=tuple([_HBM] * (2 * n_t)),
        input_output_aliases={i: i for i in range(2 * n_t)},
        compiler_params=pltpu.CompilerParams(has_side_effects=_EFFECT),
    )(*srcs, *lands, send_sems, recv_sems, after)
    return list(outs[n_t:])


def _adamw_layer(w, m, v, parts, l, prev, *, name):
    depth, n_rows, n_cols = w.shape
    n_parts = parts.shape[0]
    tr = _elem_rows(n_rows, n_cols, 4 * (8 + n_parts))
    c1 = 1.0 / (1.0 - ADAM_B1 ** ADAM_STEP)
    c2 = 1.0 / (1.0 - ADAM_B2 ** ADAM_STEP)
    n_prev = 0 if prev is None else 4

    def body(w_ref, m_ref, v_ref, p_ref, *rest):
        g_ref, d_ref, nm_ref, nv_ref = rest[n_prev:]
        g = p_ref[0].astype(F32)
        for k in range(1, n_parts):
            g = g + p_ref[k].astype(F32)
        nm = ADAM_B1 * m_ref[...] + (1.0 - ADAM_B1) * g
        nv = ADAM_B2 * v_ref[...] + (1.0 - ADAM_B2) * (g * g)
        g_ref[...] = g
        nm_ref[...] = nm
        nv_ref[...] = nv
        d_ref[...] = -ADAM_LR * ((nm * c1) / (jnp.sqrt(nv * c2) + ADAM_EPS) + ADAM_WD * w_ref[...])

    spec = pl.BlockSpec((None, tr, n_cols), lambda i: (l, i, 0))
    return pl.pallas_call(
        body, name=name, grid=(n_rows // tr,),
        in_specs=[spec, spec, spec, pl.BlockSpec((n_parts, tr, n_cols), lambda i: (0, i, 0))] + [_ANY] * n_prev,
        out_specs=[spec] * 4, out_shape=[jax.ShapeDtypeStruct(w.shape, F32)] * 4,
        input_output_aliases={4 + i: i for i in range(n_prev)},
        compiler_params=_cparams(("parallel",)),
    )(w, m, v, parts, *(prev or ()))


def _full_weights(g):
    out = {}
    for n in BIG:
        _, r, c = g[n].shape
        if n == "w_in":
            out[n] = jnp.transpose(g[n], (1, 0, 2)).reshape(r, N_DEV * c)
        elif n == "w_up":
            out[n] = g[n]
        else:
            out[n] = g[n].reshape(N_DEV * r, c)
    return out


def _grad_blocks(gb):
    out = []
    for n in BIG:
        g = gb[n]
        if n == "w_in":
            rows, cols = g.shape
            out.append(jnp.transpose(g.reshape(rows, N_DEV, cols // N_DEV), (1, 0, 2)))
        elif n == "w_up":
            out.append(g)
        else:
            out.append(g.reshape(N_DEV, g.shape[0] // N_DEV, g.shape[1]))
    return out


_SMALL_ROWS = 1096


def _pack_small(d):
    flat = jnp.concatenate([d[n].reshape(-1) for n in SMALL])
    total = N_DEV * _SMALL_ROWS * 128
    assert flat.shape[0] <= total
    return jnp.pad(flat, (0, total - flat.shape[0])).reshape(N_DEV * _SMALL_ROWS, 128)


def _unpack_small(packed, like):
    flat = packed.reshape(-1)
    out, off = {}, 0
    for n in SMALL:
        size = like[n].size
        out[n] = flat[off:off + size].reshape(like[n].shape)
        off += size
    return out


def kernel(x, meta_tokens, norm_mix_g, w_in, q_norm_g, k_norm_g, attn_sinks, ssm_lambda_re, ssm_lambda_im, ssm_log_step, ssm_b_re, ssm_b_im, ssm_c_re, ssm_c_im, ssm_d, w_glu, b_glu, attn_out_g, ssm_out_g, w_out, norm_mlp_g, w_up, w_down, loss_target, m_meta_tokens, m_norm_mix_g, m_w_in, m_q_norm_g, m_k_norm_g, m_attn_sinks, m_ssm_lambda_re, m_ssm_lambda_im, m_ssm_log_step, m_ssm_b_re, m_ssm_b_im, m_ssm_c_re, m_ssm_c_im, m_ssm_d, m_w_glu, m_b_glu, m_attn_out_g, m_ssm_out_g, m_w_out, m_norm_mlp_g, m_w_up, m_w_down, v_meta_tokens, v_norm_mix_g, v_w_in, v_q_norm_g, v_k_norm_g, v_attn_sinks, v_ssm_lambda_re, v_ssm_lambda_im, v_ssm_log_step, v_ssm_b_re, v_ssm_b_im, v_ssm_c_re, v_ssm_c_im, v_ssm_d, v_w_glu, v_b_glu, v_attn_out_g, v_ssm_out_g, v_w_out, v_norm_mlp_g, v_w_up, v_w_down):
    a = dict(locals())
    order = ("meta_tokens", "norm_mix_g", "w_in", "q_norm_g", "k_norm_g", "attn_sinks", "ssm_lambda_re", "ssm_lambda_im",
             "ssm_log_step", "ssm_b_re", "ssm_b_im", "ssm_c_re", "ssm_c_im", "ssm_d", "w_glu", "b_glu", "attn_out_g",
             "ssm_out_g", "w_out", "norm_mlp_g", "w_up", "w_down")

    sp = {n: a[n] for n in SMALL}
    wb = {}
    for n in BIG:
        depth, r, c = a[n].shape
        wb[n] = _cast_bf16(a[n].reshape(depth * r, c), name="cast_" + n).reshape(depth, r, c)
    meta_all = _all_gather(meta_tokens, name="gather_meta")
    meta = jnp.transpose(meta_all, (1, 0, 2)).reshape(N_META, D_MODEL)

    pending = {}
    updated = {n: None for n in BIG}

    def weights_for_layer(l, h):
        if l == 0:
            g = {n: _all_gather(wb[n][0], name="gather_l0_" + n) for n in BIG}
        else:
            g = dict(zip(BIG, _send_wait(pending.pop("gather"), h, per_peer=False, name=f"gather_wait_l{l}")))
        deps = ()
        if l + 1 < DEPTH:
            *handles, token = _send_start([wb[n][l + 1] for n in BIG], g["w_down"], per_peer=False, name=f"gather_start_l{l + 1}")
            pending["gather"] = handles
            deps = (token,)
        return _full_weights(g), deps

    def update_layer(l, recv):
        for n, parts in zip(BIG, recv):
            updated[n] = _adamw_layer(a[n], a["m_" + n], a["v_" + n], parts, l, updated[n], name=f"adamw_{n}_l{l}")

    def grads_of_layer(l, gb, dh):
        *handles, token = _send_start(_grad_blocks(gb), dh, per_peer=True, name=f"exchange_start_l{l}")
        if "exchange" in pending:
            update_layer(l + 1, _send_wait(pending.pop("exchange"), token, per_peer=True, name=f"exchange_wait_l{l + 1}"))
        pending["exchange"] = handles
        return (token,)

    loss, dh0, gsmall = _local_step(x[0], loss_target[0], meta, sp, weights_for_layer, grads_of_layer)
    loss = lax.psum(loss[0, 0], ("x", "y", "c"))
    grad, delta, new_m, new_v = {}, {}, {}, {}

    dmeta = jnp.transpose(dh0[PAD:BLOCK].reshape(N_META, N_DEV, D_MODEL // N_DEV), (1, 0, 2))
    outs = _adamw(meta_tokens, m_meta_tokens, v_meta_tokens, _exchange(dmeta, name="exchange_meta"), name="adamw_meta_tokens")
    grad["meta_tokens"], delta["meta_tokens"], new_m["meta_tokens"], new_v["meta_tokens"] = outs

    packed = _pack_small({n: jnp.stack(gsmall[n]) for n in SMALL}).reshape(N_DEV, _SMALL_ROWS, 128)
    share = _sum_parts(_exchange(packed, name="exchange_small"), name="sum_small")
    total = _all_gather(share, name="gather_small").reshape(1, N_DEV * _SMALL_ROWS, 128)
    outs = _adamw(_pack_small(sp), _pack_small({n: a["m_" + n] for n in SMALL}), _pack_small({n: a["v_" + n] for n in SMALL}),
                  total, name="adamw_small")
    for dst, o in zip((grad, delta, new_m, new_v), outs):
        dst.update(_unpack_small(o, sp))

    update_layer(0, _send_wait(pending.pop("exchange"), outs[0], per_peer=True, name="exchange_wait_l0"))
    for n in BIG:
        grad[n], delta[n], new_m[n], new_v[n] = updated[n]

    return (loss, dh0[BLOCK:][None], *[grad[n] for n in order], *[delta[n] for n in order],
            *[new_m[n] for n in order], *[new_v[n] for n in order])
```

```python
import functools
import math

import jax
import jax.numpy as jnp
from jax import lax
from jax.experimental import pallas as pl
from jax.experimental.pallas import tpu as pltpu

F32 = jnp.float32
BF16 = jnp.bfloat16

N_DEV = 8
D_MODEL = 2048
SEQ = 4096
DEPTH = 4
N_META = 16
HEAD_DIM = 64
ATTN_WIDTH = D_MODEL // 2
N_HEADS = ATTN_WIDTH // HEAD_DIM
N_KV_HEADS = N_HEADS // 4
KV_GROUP = N_HEADS // N_KV_HEADS
KV_WIDTH = N_KV_HEADS * HEAD_DIM
SSM_WIDTH = D_MODEL - ATTN_WIDTH
SSM_GROUP_CH = 16
SSM_GROUPS = SSM_WIDTH // SSM_GROUP_CH
SSM_STATE = 64
WINDOW = 128
BLOCK = 128
PAD = BLOCK - N_META
D_FF = 4 * D_MODEL
IN_WIDTH = ATTN_WIDTH + 2 * KV_WIDTH + SSM_WIDTH
NORM_EPS = 1e-6
NEG_INF = -1e30
ADAM_LR = 0.001
ADAM_B1 = 0.9
ADAM_B2 = 0.999
ADAM_EPS = 1e-08
ADAM_WD = 0.01
ADAM_STEP = 10

VMEM_LIMIT = 56 * 1024 * 1024
_MESH = pl.DeviceIdType.MESH
_ANY = pl.BlockSpec(memory_space=pl.ANY)


def _cparams(sem=None):
    return pltpu.CompilerParams(dimension_semantics=sem, vmem_limit_bytes=VMEM_LIMIT)


def _matmul(a, b, *, mode, tm, tn, tk, outs, epilogue, tiles=(), rows=(), deps=(), blocked=False, name):
    if mode == "nn":
        m, k = a.shape
        if blocked:
            n = b.shape[0] * b.shape[2]
            assert tn == b.shape[2]
            b_spec = pl.BlockSpec((None, tk, tn), lambda i, j, kk: (j, kk, 0))
        else:
            n = b.shape[1]
            b_spec = pl.BlockSpec((tk, tn), lambda i, j, kk: (kk, j))
        a_spec = pl.BlockSpec((tm, tk), lambda i, j, kk: (i, kk))
        dims = (((1,), (0,)), ((), ()))
    elif mode == "nt":
        m, k = a.shape
        if blocked:
            n = b.shape[1]
            assert tk == b.shape[2] and k == b.shape[0] * b.shape[2]
            b_spec = pl.BlockSpec((None, tn, tk), lambda i, j, kk: (kk, j, 0))
        else:
            n = b.shape[0]
            b_spec = pl.BlockSpec((tn, tk), lambda i, j, kk: (j, kk))
        a_spec = pl.BlockSpec((tm, tk), lambda i, j, kk: (i, kk))
        dims = (((1,), (1,)), ((), ()))
    else:
        (k, m), n = a.shape, b.shape[1]
        a_spec = pl.BlockSpec((tk, tm), lambda i, j, kk: (kk, i))
        b_spec = pl.BlockSpec((tk, tn), lambda i, j, kk: (kk, j))
        dims = (((0,), (0,)), ((), ()))
    assert m % tm == 0 and n % tn == 0 and k % tk == 0, (name, m, n, k, tm, tn, tk)
    nk = k // tk
    n_tiles, n_rows, n_outs, n_deps = len(tiles), len(rows), len(outs), len(deps)

    def body(a_ref, b_ref, *rest):
        tile_refs = rest[:n_tiles]
        row_refs = rest[n_tiles:n_tiles + n_rows]
        out_refs = rest[n_tiles + n_rows + n_deps:n_tiles + n_rows + n_deps + n_outs]
        def product():
            return lax.dot_general(a_ref[...].astype(BF16), b_ref[...].astype(BF16), dims, preferred_element_type=F32)

        def finish(acc):
            res = epilogue(acc, *[r[...] for r in tile_refs], *[r[...] for r in row_refs])
            for o_ref, o in zip(out_refs, res):
                o_ref[...] = o.astype(o_ref.dtype)

        if nk == 1:
            finish(product())
            return
        acc_ref = rest[-1]
        kk = pl.program_id(2)

        @pl.when(kk == 0)
        def _():
            acc_ref[...] = jnp.zeros_like(acc_ref)

        acc_ref[...] += product()

        @pl.when(kk == nk - 1)
        def _():
            finish(acc_ref[...])

    tile_spec = pl.BlockSpec((tm, tn), lambda i, j, kk: (i, j))
    row_spec = pl.BlockSpec((1, tn), lambda i, j, kk: (0, j))
    if mode == "tn" and blocked:
        out_specs = [pl.BlockSpec((None, tm, tn), lambda i, j, kk: (j, i, 0))] * n_outs
        out_shape = [jax.ShapeDtypeStruct((n // tn, m, tn), dt) for dt in outs]
    else:
        out_specs = [tile_spec] * n_outs
        out_shape = [jax.ShapeDtypeStruct((m, n), dt) for dt in outs]
    return pl.pallas_call(
        body, name=name, grid=(m // tm, n // tn, nk),
        in_specs=[a_spec, b_spec] + [tile_spec] * n_tiles + [row_spec] * n_rows + [_ANY] * n_deps,
        out_specs=out_specs, out_shape=out_shape,
        scratch_shapes=[pltpu.VMEM((tm, tn), F32)] if nk > 1 else [],
        compiler_params=_cparams(("parallel", "parallel", "arbitrary")),
    )(a, b, *tiles, *rows, *deps)


def _ident(acc):
    return (acc,)


def _row_tile(n_rows, cap):
    best = BLOCK
    for t in range(BLOCK, cap + 1, BLOCK):
        if n_rows % t == 0:
            best = t
    return best


def _rmsnorm_fwd(xs, gs, *, name, deps=()):
    n_rows, width = xs[0].shape
    n = len(xs)
    tr = _row_tile(n_rows, 384)

    def body(*refs):
        o_ref = refs[-1]
        parts = []
        for x_ref, g_ref in zip(refs[:n], refs[n:2 * n]):
            x = x_ref[...]
            r = lax.rsqrt(jnp.mean(x * x, axis=-1, keepdims=True) + NORM_EPS)
            parts.append(x * r * g_ref[...])
        o_ref[...] = (parts[0] if n == 1 else jnp.concatenate(parts, axis=1)).astype(BF16)

    return pl.pallas_call(
        body, name=name, grid=(n_rows // tr,),
        in_specs=[pl.BlockSpec((tr, width), lambda i: (i, 0))] * n + [pl.BlockSpec((1, width), lambda i: (0, 0))] * n
        + [_ANY] * len(deps),
        out_specs=pl.BlockSpec((tr, n * width), lambda i: (i, 0)),
        out_shape=jax.ShapeDtypeStruct((n_rows, n * width), BF16),
        compiler_params=_cparams(("parallel",)),
    )(*xs, *gs, *deps)


def _rmsnorm_bwd(xs, gs, dy, res, *, name):
    n_rows, width = xs[0].shape
    n = len(xs)
    tr = _row_tile(n_rows, 384)
    has_res = res is not None

    def body(*refs):
        x_refs, g_refs, dy_ref = refs[:n], refs[n:2 * n], refs[2 * n]
        res_ref = refs[2 * n + 1] if has_res else None
        outs = refs[2 * n + 1 + int(has_res):]
        dx_refs, dg_refs = outs[:n], outs[n:2 * n]
        i = pl.program_id(0)
        for c in range(n):
            x = x_refs[c][...]
            d = dy_ref[:, c * width:(c + 1) * width]
            r = lax.rsqrt(jnp.mean(x * x, axis=-1, keepdims=True) + NORM_EPS)
            xh = x * r
            gd = d * g_refs[c][...]
            dx = r * (gd - xh * jnp.mean(gd * xh, axis=-1, keepdims=True))
            if has_res:
                dx = dx + res_ref[...]
                outs[2 * n][...] = dx.astype(BF16)
            dx_refs[c][...] = dx
            part = jnp.sum(d * xh, axis=0, keepdims=True)

            @pl.when(i == 0)
            def _():
                dg_refs[c][...] = part

            @pl.when(i > 0)
            def _():
                dg_refs[c][...] += part

    row_spec = pl.BlockSpec((tr, width), lambda i: (i, 0))
    vec_spec = pl.BlockSpec((1, width), lambda i: (0, 0))
    outs = pl.pallas_call(
        body, name=name, grid=(n_rows // tr,),
        in_specs=[row_spec] * n + [vec_spec] * n + [pl.BlockSpec((tr, n * width), lambda i: (i, 0))] + [row_spec] * int(has_res),
        out_specs=[row_spec] * n + [vec_spec] * n + [row_spec] * int(has_res),
        out_shape=[jax.ShapeDtypeStruct((n_rows, width), F32)] * n + [jax.ShapeDtypeStruct((1, width), F32)] * n
        + [jax.ShapeDtypeStruct((n_rows, width), BF16)] * int(has_res),
        compiler_params=_cparams(("arbitrary",)),
    )(*xs, *gs, dy, *([res] if has_res else []))
    if has_res:
        return outs[:n], outs[n:2 * n], outs[2 * n]
    return outs[:n], outs[n:]


_SCALE = 1.0 / math.sqrt(HEAD_DIM)
_DN_NT = (((1,), (1,)), ((), ()))
_DN_TN = (((0,), (0,)), ((), ()))


def _head_norm(x, g):
    r = lax.rsqrt(jnp.mean(x * x, axis=-1, keepdims=True) + NORM_EPS)
    return x * r * g, r


def _attn_geometry(n):
    rows = KV_GROUP * BLOCK
    i = lax.broadcasted_iota(jnp.int32, (rows, 3 * BLOCK), 0) % BLOCK
    j = lax.broadcasted_iota(jnp.int32, (rows, 3 * BLOCK), 1)
    t_pos = n * BLOCK + i - PAD
    is_meta = j < BLOCK
    m_pos = j - PAD
    s_pos = (n - 1) * BLOCK + (j - BLOCK) - PAD
    meta_ok = (j >= PAD) & (m_pos <= t_pos)
    band_ok = (s_pos >= N_META) & (s_pos <= t_pos) & (t_pos - s_pos < WINDOW)
    valid = (is_meta & meta_ok) | (jnp.logical_not(is_meta) & band_ok)
    dist = jnp.abs(t_pos - jnp.where(is_meta, m_pos, s_pos)).astype(F32)
    return valid, dist


def _slope_col(kv):
    g = lax.broadcasted_iota(jnp.int32, (KV_GROUP * BLOCK, 1), 0) // BLOCK
    col = jnp.zeros((KV_GROUP * BLOCK, 1), F32)
    for gi in range(KV_GROUP):
        col = jnp.where(g == gi, 2.0 ** (-8.0 * (kv * KV_GROUP + gi + 1) / N_HEADS), col)
    return col


def _sink_col(sink_ref, kv):
    g = lax.broadcasted_iota(jnp.int32, (KV_GROUP * BLOCK, 1), 0) // BLOCK
    col = jnp.zeros((KV_GROUP * BLOCK, 1), F32)
    for gi in range(KV_GROUP):
        h = kv * KV_GROUP + gi
        col = jnp.where(g == gi, sink_ref[0:1, h:h + 1], col)
    return col


def _stack_heads(x, kv):
    return jnp.concatenate([x[:, (kv * KV_GROUP + g) * HEAD_DIM:(kv * KV_GROUP + g + 1) * HEAD_DIM]
                            for g in range(KV_GROUP)], axis=0)


def _attn_scores(q_ref, k_refs, gq_ref, gk_ref, sink_ref, kv, valid, dist):
    qs = _stack_heads(q_ref[...], kv)
    kcat = jnp.concatenate([r[:, kv * HEAD_DIM:(kv + 1) * HEAD_DIM] for r in k_refs], axis=0)
    qn, rq = _head_norm(qs, gq_ref[...])
    kn, rk = _head_norm(kcat, gk_ref[...])
    s = lax.dot_general(qn.astype(BF16), kn.astype(BF16), _DN_NT, preferred_element_type=F32) * _SCALE
    s = jnp.where(valid, s - _slope_col(kv) * dist, NEG_INF)
    return qs, kcat, qn, kn, rq, rk, s, _sink_col(sink_ref, kv)


def _attn_specs():
    kq = ATTN_WIDTH // KV_WIDTH
    q_spec = pl.BlockSpec((BLOCK, ATTN_WIDTH), lambda n: (n, 0))
    kv_specs = []
    for col in (kq, kq + 1):
        kv_specs += [pl.BlockSpec((BLOCK, KV_WIDTH), lambda n, col=col: (0, col)),
                     pl.BlockSpec((BLOCK, KV_WIDTH), lambda n, col=col: (jnp.maximum(n - 1, 0), col)),
                     pl.BlockSpec((BLOCK, KV_WIDTH), lambda n, col=col: (n, col))]
    small = [pl.BlockSpec((1, HEAD_DIM), lambda n: (0, 0)), pl.BlockSpec((1, HEAD_DIM), lambda n: (0, 0)),
             pl.BlockSpec((1, N_HEADS), lambda n: (0, 0))]
    return q_spec, kv_specs, small


def _attn_fwd(proj, gq, gk, sinks, *, name):
    n_rows = proj.shape[0]
    q_spec, kv_specs, small = _attn_specs()

    def body(q_ref, k0, k1, k2, v0, v1, v2, gq_ref, gk_ref, sink_ref, o_ref, lse_ref):
        valid, dist = _attn_geometry(pl.program_id(0))
        o_parts, lse_parts = [], []
        for kv in range(N_KV_HEADS):
            _, _, _, _, _, _, s, sink = _attn_scores(q_ref, (k0, k1, k2), gq_ref, gk_ref, sink_ref, kv, valid, dist)
            vcat = jnp.concatenate([r[:, kv * HEAD_DIM:(kv + 1) * HEAD_DIM] for r in (v0, v1, v2)], axis=0)
            m = jnp.maximum(jnp.max(s, axis=-1, keepdims=True), sink)
            p = jnp.exp(s - m)
            l = jnp.sum(p, axis=-1, keepdims=True) + jnp.exp(sink - m)
            o = jnp.dot(p.astype(BF16), vcat.astype(BF16), preferred_element_type=F32) / l
            lse = m + jnp.log(l)
            o_parts += [o[g * BLOCK:(g + 1) * BLOCK] for g in range(KV_GROUP)]
            lse_parts += [lse[g * BLOCK:(g + 1) * BLOCK] for g in range(KV_GROUP)]
        o_ref[...] = jnp.concatenate(o_parts, axis=1)
        lse_ref[...] = jnp.concatenate(lse_parts, axis=1)

    return pl.pallas_call(
        body, name=name, grid=(n_rows // BLOCK,),
        in_specs=[q_spec] + kv_specs + small,
        out_specs=[pl.BlockSpec((BLOCK, ATTN_WIDTH), lambda n: (n, 0)), pl.BlockSpec((BLOCK, N_HEADS), lambda n: (n, 0))],
        out_shape=[jax.ShapeDtypeStruct((n_rows, ATTN_WIDTH), F32), jax.ShapeDtypeStruct((n_rows, N_HEADS), F32)],
        compiler_params=_cparams(("parallel",)),
    )(proj, proj, proj, proj, proj, proj, proj, gq, gk, sinks)


def _attn_bwd(proj, gq, gk, sinks, o, lse, do, *, name):
    n_rows = proj.shape[0]
    q_spec, kv_specs, small = _attn_specs()

    def body(q_ref, k0, k1, k2, v0, v1, v2, gq_ref, gk_ref, sink_ref, o_ref, lse_ref, do_ref,
             dq_ref, dkb_ref, dvb_ref, dgq_ref, dgk_ref, dsink_ref, dk_ref, dv_ref):
        n = pl.program_id(0)

        @pl.when(n == 0)
        def _():
            dk_ref[...] = jnp.zeros_like(dk_ref)
            dv_ref[...] = jnp.zeros_like(dv_ref)
            dgq_ref[...] = jnp.zeros_like(dgq_ref)
            dgk_ref[...] = jnp.zeros_like(dgk_ref)
            dsink_ref[...] = jnp.zeros_like(dsink_ref)

        valid, dist = _attn_geometry(n)
        dq_parts, dk_parts, dv_parts, dsink_parts = [], [], [], []
        dgq = jnp.zeros((1, HEAD_DIM), F32)
        dgk = jnp.zeros((1, HEAD_DIM), F32)
        for kv in range(N_KV_HEADS):
            qs, kcat, qn, kn, rq, rk, s, sink = _attn_scores(q_ref, (k0, k1, k2), gq_ref, gk_ref, sink_ref, kv, valid, dist)
            vcat = jnp.concatenate([r[:, kv * HEAD_DIM:(kv + 1) * HEAD_DIM] for r in (v0, v1, v2)], axis=0)
            os_ = _stack_heads(o_ref[...], kv)
            dos = _stack_heads(do_ref[...], kv)
            lse = jnp.concatenate([lse_ref[:, kv * KV_GROUP + g:kv * KV_GROUP + g + 1] for g in range(KV_GROUP)], axis=0)
            p = jnp.exp(s - lse)
            delta = jnp.sum(dos * os_, axis=-1, keepdims=True)
            dp = lax.dot_general(dos.astype(BF16), vcat.astype(BF16), _DN_NT, preferred_element_type=F32)
            ds = (p * (dp - delta)) * _SCALE
            dsink_rows = -jnp.exp(sink - lse) * delta
            dsink_parts += [jnp.sum(dsink_rows[g * BLOCK:(g + 1) * BLOCK], axis=0, keepdims=True) for g in range(KV_GROUP)]
            dv_parts.append(lax.dot_general(p.astype(BF16), dos.astype(BF16), _DN_TN, preferred_element_type=F32))
            dsb = ds.astype(BF16)
            dqn = jnp.dot(dsb, kn.astype(BF16), preferred_element_type=F32)
            dkn = lax.dot_general(dsb, qn.astype(BF16), _DN_TN, preferred_element_type=F32)
            qh = qs * rq
            gd = dqn * gq_ref[...]
            dqs = rq * (gd - qh * jnp.mean(gd * qh, axis=-1, keepdims=True))
            dgq = dgq + jnp.sum(dqn * qh, axis=0, keepdims=True)
            kh = kcat * rk
            gdk = dkn * gk_ref[...]
            dk_parts.append(rk * (gdk - kh * jnp.mean(gdk * kh, axis=-1, keepdims=True)))
            dgk = dgk + jnp.sum(dkn * kh, axis=0, keepdims=True)
            dq_parts += [dqs[g * BLOCK:(g + 1) * BLOCK] for g in range(KV_GROUP)]
        dq_ref[...] = jnp.concatenate(dq_parts, axis=1).astype(BF16)
        dkc = jnp.concatenate(dk_parts, axis=1)
        dvc = jnp.concatenate(dv_parts, axis=1)
        prev = pl.multiple_of(jnp.maximum(n - 1, 0) * BLOCK, BLOCK)
        cur = pl.multiple_of(n * BLOCK, BLOCK)
        for acc_ref, val in ((dk_ref, dkc), (dv_ref, dvc)):
            acc_ref[0:BLOCK, :] += val[0:BLOCK]
            acc_ref[pl.ds(prev, BLOCK), :] += val[BLOCK:2 * BLOCK]
            acc_ref[pl.ds(cur, BLOCK), :] += val[2 * BLOCK:3 * BLOCK]
        dgq_ref[...] += dgq
        dgk_ref[...] += dgk
        dsink_ref[...] += jnp.concatenate(dsink_parts, axis=1)

        @pl.when(n == pl.num_programs(0) - 1)
        def _():
            dkb_ref[...] = dk_ref[...].astype(BF16)
            dvb_ref[...] = dv_ref[...].astype(BF16)

    blk = lambda w: pl.BlockSpec((BLOCK, w), lambda n: (n, 0))
    full = lambda r, w: pl.BlockSpec((r, w), lambda n: (0, 0))
    return pl.pallas_call(
        body, name=name, grid=(n_rows // BLOCK,),
        in_specs=[q_spec] + kv_specs + small + [blk(ATTN_WIDTH), blk(N_HEADS), blk(ATTN_WIDTH)],
        out_specs=[blk(ATTN_WIDTH), full(n_rows, KV_WIDTH), full(n_rows, KV_WIDTH),
                   full(1, HEAD_DIM), full(1, HEAD_DIM), full(1, N_HEADS)],
        out_shape=[jax.ShapeDtypeStruct((n_rows, ATTN_WIDTH), BF16), jax.ShapeDtypeStruct((n_rows, KV_WIDTH), BF16),
                   jax.ShapeDtypeStruct((n_rows, KV_WIDTH), BF16), jax.ShapeDtypeStruct((1, HEAD_DIM), F32),
                   jax.ShapeDtypeStruct((1, HEAD_DIM), F32), jax.ShapeDtypeStruct((1, N_HEADS), F32)],
        scratch_shapes=[pltpu.VMEM((n_rows, KV_WIDTH), F32), pltpu.VMEM((n_rows, KV_WIDTH), F32)],
        compiler_params=_cparams(("arbitrary",)),
    )(proj, proj, proj, proj, proj, proj, proj, gq, gk, sinks, o, lse, do)


SSM_LAGS = 8
SLAB_G = 128 // SSM_GROUP_CH
N_SLABS = SSM_GROUPS // SLAB_G
SLAB_STATE = SLAB_G * SSM_STATE
U_COL = (ATTN_WIDTH + 2 * KV_WIDTH) // 128


def _ssm_prep(lam_re, lam_im, log_step, b_re, b_im, c_re, c_im):
    lam = lax.complex(lam_re, lam_im)
    delta = jnp.exp(log_step)[:, None]
    lam_bar = jnp.exp(lam * delta)
    b_bar = ((lam_bar - 1.0) / lam)[..., None] * lax.complex(b_re, b_im)
    pw = [jnp.ones_like(lam_bar)]
    for _ in range(SSM_LAGS):
        pw.append(pw[-1] * lam_bar)
    w = jnp.stack(pw[:SSM_LAGS])[..., None] * b_bar[None]
    wri = jnp.stack([jnp.real(w), jnp.imag(w)]).reshape(2, SSM_LAGS, N_SLABS, SLAB_G, SSM_STATE, SSM_GROUP_CH)
    eye = jnp.eye(SLAB_G, dtype=F32)
    wc = jnp.transpose(wri, (2, 1, 3, 5, 0, 4)).reshape(N_SLABS, SSM_LAGS * 128, 2 * SSM_STATE)
    cri = jnp.stack([c_re, -c_im]).reshape(2, N_SLABS, SLAB_G, SSM_GROUP_CH, SSM_STATE)
    ct = jnp.transpose(cri, (1, 0, 2, 4, 3))
    cmat = ct[:, :, :, :, None, :] * eye[None, None, :, None, :, None]
    cmat = cmat.reshape(N_SLABS, 2 * SLAB_STATE, 128)
    l8 = pw[SSM_LAGS]
    lam8 = jnp.concatenate([jnp.real(l8).reshape(N_SLABS, 1, SLAB_STATE), jnp.imag(l8).reshape(N_SLABS, 1, SLAB_STATE)], axis=2)
    return wc, cmat, lam8


def _row_group():
    return (lax.broadcasted_iota(jnp.int32, (SSM_LAGS * 128, 1), 0) // SSM_GROUP_CH) % SLAB_G


def _spread_groups(wc):
    g_of_row = _row_group()
    return jnp.concatenate([jnp.where(g_of_row == g, wc[:, r * SSM_STATE:(r + 1) * SSM_STATE], 0.0)
                            for r in range(2) for g in range(SLAB_G)], axis=1)


def _gather_groups(dw):
    g_of_row = _row_group()
    parts = []
    for r in range(2):
        acc = jnp.zeros((SSM_LAGS * 128, SSM_STATE), F32)
        for g in range(SLAB_G):
            c0 = r * SLAB_STATE + g * SSM_STATE
            acc = acc + jnp.where(g_of_row == g, dw[:, c0:c0 + SSM_STATE], 0.0)
        parts.append(acc)
    return jnp.concatenate(parts, axis=1)


def _lagged(u, up, t_rows):
    ue = jnp.concatenate([up, u], axis=0)
    return jnp.concatenate([ue[SSM_LAGS - tau:SSM_LAGS - tau + t_rows] for tau in range(SSM_LAGS)], axis=1).astype(BF16)


def _ssm_fwd(proj, wc, cmat, lam8, dvec, *, name):
    n_rows = proj.shape[0]
    tt = _row_tile(n_rows, 1408)
    n_t = n_rows // tt
    sw = 2 * SLAB_STATE
    hs = SLAB_STATE

    def body(u_ref, up_ref, wc_ref, c_ref, l_ref, d_ref, y_ref, x_ref, carry_ref, w_ref):
        t = pl.program_id(1)

        @pl.when(t == 0)
        def _():
            carry_ref[...] = jnp.zeros_like(carry_ref)
            w_ref[...] = _spread_groups(wc_ref[...]).astype(BF16)

        u = u_ref[...]
        up = jnp.where(t > 0, up_ref[...], 0.0)
        x_ref[...] = jnp.dot(_lagged(u, up, tt), w_ref[...], preferred_element_type=F32)
        ar = jnp.broadcast_to(l_ref[:, :hs], (8, hs))
        ai = jnp.broadcast_to(l_ref[:, hs:], (8, hs))

        def step(b, c):
            xr, xi = c
            r0 = pl.multiple_of(b * 8, 8)
            w = x_ref[pl.ds(r0, 8), :]
            nr = w[:, :hs] + ar * xr - ai * xi
            ni = w[:, hs:] + ar * xi + ai * xr
            x_ref[pl.ds(r0, 8), :] = jnp.concatenate([nr, ni], axis=1)
            return nr, ni

        xr, xi = lax.fori_loop(0, tt // 8, step, (carry_ref[:, :hs], carry_ref[:, hs:]), unroll=8)
        carry_ref[...] = jnp.concatenate([xr, xi], axis=1)
        y_ref[...] = jnp.dot(x_ref[...].astype(BF16), c_ref[...], preferred_element_type=F32) + d_ref[...] * u

    return pl.pallas_call(
        body, name=name, grid=(N_SLABS, n_t),
        in_specs=[pl.BlockSpec((tt, 128), lambda j, t: (t, U_COL + j)),
                  pl.BlockSpec((8, 128), lambda j, t: (jnp.maximum(t * (tt // 8) - 1, 0), U_COL + j)),
                  pl.BlockSpec((None, SSM_LAGS * 128, 2 * SSM_STATE), lambda j, t: (j, 0, 0)),
                  pl.BlockSpec((None, sw, 128), lambda j, t: (j, 0, 0)),
                  pl.BlockSpec((None, 1, sw), lambda j, t: (j, 0, 0)),
                  pl.BlockSpec((1, 128), lambda j, t: (0, j))],
        out_specs=[pl.BlockSpec((tt, 128), lambda j, t: (t, j)), pl.BlockSpec((tt, sw), lambda j, t: (t, j))],
        out_shape=[jax.ShapeDtypeStruct((n_rows, SSM_WIDTH), F32), jax.ShapeDtypeStruct((n_rows, N_SLABS * sw), F32)],
        scratch_shapes=[pltpu.VMEM((8, sw), F32), pltpu.VMEM((SSM_LAGS * 128, sw), BF16)],
        compiler_params=_cparams(("parallel", "arbitrary")),
    )(proj, proj, wc, cmat, lam8, dvec)


def _ssm_bwd(proj, xs, dy, wc, cmat, lam8, dvec, *, name):
    n_rows = proj.shape[0]
    tt = _row_tile(n_rows, 704)
    n_t = n_rows // tt
    sw = 2 * SLAB_STATE
    hs = SLAB_STATE

    def body(u_ref, up_ref, x_ref, xp_ref, dy_ref, wc_ref, c_ref, l_ref, d_ref,
             du_ref, dwc_ref, dc_ref, dl_ref, dd_ref, a_ref, carry_ref, head_ref, w_ref, dw_ref):
        t = pl.program_id(1)
        ti = n_t - 1 - t

        @pl.when(t == 0)
        def _():
            w_ref[...] = _spread_groups(wc_ref[...]).astype(BF16)
            carry_ref[...] = jnp.zeros_like(carry_ref)
            head_ref[...] = jnp.zeros_like(head_ref)
            dw_ref[...] = jnp.zeros_like(dw_ref)
            dc_ref[...] = jnp.zeros_like(dc_ref)
            dl_ref[...] = jnp.zeros_like(dl_ref)
            dd_ref[...] = jnp.zeros_like(dd_ref)

        u = u_ref[...]
        up = jnp.where(ti > 0, up_ref[...], 0.0)
        ucat = _lagged(u, up, tt)
        dyv = dy_ref[...]
        dyb = dyv.astype(BF16)
        a_ref[...] = lax.dot_general(dyb, c_ref[...], _DN_NT, preferred_element_type=F32)
        lr = jnp.broadcast_to(l_ref[:, :hs], (8, hs))
        li = jnp.broadcast_to(l_ref[:, hs:], (8, hs))

        def step(i, c):
            cr, ci = c
            r0 = pl.multiple_of((tt // 8 - 1 - i) * 8, 8)
            g = a_ref[pl.ds(r0, 8), :]
            nr = g[:, :hs] + lr * cr + li * ci
            ni = g[:, hs:] + lr * ci - li * cr
            a_ref[pl.ds(r0, 8), :] = jnp.concatenate([nr, ni], axis=1)
            return nr, ni

        cr, ci = lax.fori_loop(0, tt // 8, step, (carry_ref[:, :hs], carry_ref[:, hs:]), unroll=8)
        carry_ref[...] = jnp.concatenate([cr, ci], axis=1)

        a = a_ref[...]
        xv = x_ref[...]
        xprev = jnp.where(ti > 0, xp_ref[...], 0.0)
        xsh = jnp.concatenate([xprev, xv[:tt - SSM_LAGS]], axis=0)
        a_re, a_im, x_re, x_im = a[:, :hs], a[:, hs:], xsh[:, :hs], xsh[:, hs:]
        dl_ref[...] += jnp.concatenate([jnp.sum(a_re * x_re + a_im * x_im, axis=0, keepdims=True),
                                        jnp.sum(a_im * x_re - a_re * x_im, axis=0, keepdims=True)], axis=1)
        ab = a.astype(BF16)
        dw_ref[...] += lax.dot_general(ucat, ab, _DN_TN, preferred_element_type=F32)
        duc = lax.dot_general(ab, w_ref[...], _DN_NT, preferred_element_type=F32)
        ext = jnp.concatenate([duc, head_ref[...]], axis=0)
        du = d_ref[...] * dyv
        for tau in range(SSM_LAGS):
            du = du + ext[tau:tau + tt, tau * 128:(tau + 1) * 128]
        head_ref[...] = duc[0:8]
        row = ti * tt + lax.broadcasted_iota(jnp.int32, (tt, 128), 0)
        du_ref[...] = jnp.where(row >= PAD, du, 0.0).astype(BF16)
        dd_ref[...] += jnp.sum(dyv * u, axis=0, keepdims=True)
        dc_ref[...] += lax.dot_general(xv.astype(BF16), dyb, _DN_TN, preferred_element_type=F32)

        @pl.when(t == n_t - 1)
        def _():
            dwc_ref[...] = _gather_groups(dw_ref[...])

    rt = lambda t: n_t - 1 - t
    prev8 = lambda t: jnp.maximum(rt(t) * (tt // 8) - 1, 0)
    return pl.pallas_call(
        body, name=name, grid=(N_SLABS, n_t),
        in_specs=[pl.BlockSpec((tt, 128), lambda j, t: (rt(t), U_COL + j)),
                  pl.BlockSpec((8, 128), lambda j, t: (prev8(t), U_COL + j)),
                  pl.BlockSpec((tt, sw), lambda j, t: (rt(t), j)),
                  pl.BlockSpec((8, sw), lambda j, t: (prev8(t), j)),
                  pl.BlockSpec((tt, 128), lambda j, t: (rt(t), j)),
                  pl.BlockSpec((None, SSM_LAGS * 128, 2 * SSM_STATE), lambda j, t: (j, 0, 0)),
                  pl.BlockSpec((None, sw, 128), lambda j, t: (j, 0, 0)),
                  pl.BlockSpec((None, 1, sw), lambda j, t: (j, 0, 0)),
                  pl.BlockSpec((1, 128), lambda j, t: (0, j))],
        out_specs=[pl.BlockSpec((tt, 128), lambda j, t: (rt(t), j)),
                   pl.BlockSpec((None, SSM_LAGS * 128, 2 * SSM_STATE), lambda j, t: (j, 0, 0)),
                   pl.BlockSpec((None, sw, 128), lambda j, t: (j, 0, 0)),
                   pl.BlockSpec((None, 1, sw), lambda j, t: (j, 0, 0)),
                   pl.BlockSpec((1, 128), lambda j, t: (0, j))],
        out_shape=[jax.ShapeDtypeStruct((n_rows, SSM_WIDTH), BF16),
                   jax.ShapeDtypeStruct((N_SLABS, SSM_LAGS * 128, 2 * SSM_STATE), F32),
                   jax.ShapeDtypeStruct((N_SLABS, sw, 128), F32),
                   jax.ShapeDtypeStruct((N_SLABS, 1, sw), F32),
                   jax.ShapeDtypeStruct((1, SSM_WIDTH), F32)],
        scratch_shapes=[pltpu.VMEM((tt, sw), F32), pltpu.VMEM((8, sw), F32), pltpu.VMEM((8, sw), F32),
                        pltpu.VMEM((SSM_LAGS * 128, sw), BF16), pltpu.VMEM((SSM_LAGS * 128, sw), F32)],
        compiler_params=_cparams(("parallel", "arbitrary")),
    )(proj, proj, xs, xs, dy, wc, cmat, lam8, dvec)


_GELU_C = math.sqrt(2.0 / math.pi)
_GELU_A = 0.044715


def _gelu(y):
    th = jnp.tanh(_GELU_C * (y + _GELU_A * y * y * y))
    return 0.5 * y * (1.0 + th), th


def _glu_fwd(y, w, b, *, name):
    n_rows, width = y.shape
    tr = _row_tile(n_rows, 384)

    def body(y_ref, w_ref, b_ref, o_ref):
        g, _ = _gelu(y_ref[...])
        z = jnp.dot(g.astype(BF16), w_ref[...], preferred_element_type=F32) + b_ref[...]
        o_ref[...] = g * jax.nn.sigmoid(z)

    return pl.pallas_call(
        body, name=name, grid=(n_rows // tr,),
        in_specs=[pl.BlockSpec((tr, width), lambda i: (i, 0)), pl.BlockSpec((width, width), lambda i: (0, 0)),
                  pl.BlockSpec((1, width), lambda i: (0, 0))],
        out_specs=pl.BlockSpec((tr, width), lambda i: (i, 0)),
        out_shape=jax.ShapeDtypeStruct((n_rows, width), F32),
        compiler_params=_cparams(("parallel",)),
    )(y, w, b)


def _glu_bwd(y, w, b, dout, *, name):
    n_rows, width = y.shape
    tr = _row_tile(n_rows, 384)

    def body(y_ref, w_ref, b_ref, do_ref, dy_ref, g_ref, dz_ref, db_ref):
        i = pl.program_id(0)
        yv = y_ref[...]
        g, th = _gelu(yv)
        gb = g.astype(BF16)
        z = jnp.dot(gb, w_ref[...], preferred_element_type=F32) + b_ref[...]
        sg = jax.nn.sigmoid(z)
        do = do_ref[...]
        dz = do * g * sg * (1.0 - sg)
        dzb = dz.astype(BF16)
        dg = do * sg + lax.dot_general(dzb, w_ref[...], _DN_NT, preferred_element_type=F32)
        dgelu = 0.5 * (1.0 + th) + 0.5 * yv * (1.0 - th * th) * _GELU_C * (1.0 + 3.0 * _GELU_A * yv * yv)
        dy_ref[...] = dg * dgelu
        g_ref[...] = gb
        dz_ref[...] = dzb
        part = jnp.sum(dz, axis=0, keepdims=True)

        @pl.when(i == 0)
        def _():
            db_ref[...] = part

        @pl.when(i > 0)
        def _():
            db_ref[...] += part

    row = pl.BlockSpec((tr, width), lambda i: (i, 0))
    vec = pl.BlockSpec((1, width), lambda i: (0, 0))
    return pl.pallas_call(
        body, name=name, grid=(n_rows // tr,),
        in_specs=[row, pl.BlockSpec((width, width), lambda i: (0, 0)), vec, row],
        out_specs=[row, row, row, vec],
        out_shape=[jax.ShapeDtypeStruct((n_rows, width), F32), jax.ShapeDtypeStruct((n_rows, width), BF16),
                   jax.ShapeDtypeStruct((n_rows, width), BF16), jax.ShapeDtypeStruct((1, width), F32)],
        compiler_params=_cparams(("arbitrary",)),
    )(y, w, b, dout)


def _loss_head(h, target, *, name):
    n_rows, width = h.shape

    def body(h_ref, t_ref, dh_ref, dhb_ref, loss_ref):
        i = pl.program_id(0)

        @pl.when(i == 0)
        def _():
            dh_ref[...] = jnp.zeros_like(dh_ref)
            dhb_ref[...] = jnp.zeros_like(dhb_ref)
            loss_ref[...] = jnp.zeros_like(loss_ref)

        @pl.when(i > 0)
        def _():
            err = h_ref[...] - t_ref[...]
            dh = err * (1.0 / width)
            dh_ref[...] = dh
            dhb_ref[...] = dh.astype(BF16)
            loss_ref[...] += (0.5 / width) * jnp.sum(err * err, keepdims=True)

    return pl.pallas_call(
        body, name=name, grid=(n_rows // BLOCK,),
        in_specs=[pl.BlockSpec((BLOCK, width), lambda i: (i, 0)),
                  pl.BlockSpec((BLOCK, width), lambda i: (jnp.maximum(i - 1, 0), 0))],
        out_specs=[pl.BlockSpec((BLOCK, width), lambda i: (i, 0)), pl.BlockSpec((BLOCK, width), lambda i: (i, 0)),
                   pl.BlockSpec((1, 1), lambda i: (0, 0))],
        out_shape=[jax.ShapeDtypeStruct((n_rows, width), F32), jax.ShapeDtypeStruct((n_rows, width), BF16),
                   jax.ShapeDtypeStruct((1, 1), F32)],
        compiler_params=_cparams(("arbitrary",)),
    )(h, target)


def _elem_rows(n_rows, n_cols, bytes_per_row_elem):
    cap = max(16, (4 * 1024 * 1024) // (n_cols * bytes_per_row_elem))
    best = None
    for t in range(16, min(n_rows, cap) + 1, 16):
        if n_rows % t == 0:
            best = t
    return best or n_rows


def _cast_bf16(x, *, name):
    n_rows, n_cols = x.shape
    tr = _elem_rows(n_rows, n_cols, 4)

    def body(x_ref, o_ref):
        o_ref[...] = x_ref[...].astype(BF16)

    spec = pl.BlockSpec((tr, n_cols), lambda i: (i, 0))
    return pl.pallas_call(body, name=name, grid=(n_rows // tr,), in_specs=[spec], out_specs=spec,
                          out_shape=jax.ShapeDtypeStruct(x.shape, BF16), compiler_params=_cparams(("parallel",)))(x)


def _adamw(w, m, v, parts, *, name):
    n_rows, n_cols = w.shape
    n_parts = parts.shape[0]
    tr = _elem_rows(n_rows, n_cols, 4 * (8 + n_parts))
    c1 = 1.0 / (1.0 - ADAM_B1 ** ADAM_STEP)
    c2 = 1.0 / (1.0 - ADAM_B2 ** ADAM_STEP)

    def body(w_ref, m_ref, v_ref, p_ref, g_ref, d_ref, nm_ref, nv_ref):
        g = p_ref[0].astype(F32)
        for k in range(1, n_parts):
            g = g + p_ref[k].astype(F32)
        nm = ADAM_B1 * m_ref[...] + (1.0 - ADAM_B1) * g
        nv = ADAM_B2 * v_ref[...] + (1.0 - ADAM_B2) * (g * g)
        g_ref[...] = g
        nm_ref[...] = nm
        nv_ref[...] = nv
        d_ref[...] = -ADAM_LR * ((nm * c1) / (jnp.sqrt(nv * c2) + ADAM_EPS) + ADAM_WD * w_ref[...])

    spec = pl.BlockSpec((tr, n_cols), lambda i: (i, 0))
    return pl.pallas_call(
        body, name=name, grid=(n_rows // tr,),
        in_specs=[spec, spec, spec, pl.BlockSpec((n_parts, tr, n_cols), lambda i: (0, i, 0))],
        out_specs=[spec] * 4, out_shape=[jax.ShapeDtypeStruct(w.shape, F32)] * 4,
        compiler_params=_cparams(("parallel",)),
    )(w, m, v, parts)


def _sum_parts(parts, *, name):
    n_parts, n_rows, n_cols = parts.shape
    tr = _elem_rows(n_rows, n_cols, 4 * (1 + n_parts))

    def body(p_ref, o_ref):
        g = p_ref[0].astype(F32)
        for k in range(1, n_parts):
            g = g + p_ref[k].astype(F32)
        o_ref[...] = g

    return pl.pallas_call(
        body, name=name, grid=(n_rows // tr,),
        in_specs=[pl.BlockSpec((n_parts, tr, n_cols), lambda i: (0, i, 0))],
        out_specs=pl.BlockSpec((tr, n_cols), lambda i: (i, 0)),
        out_shape=jax.ShapeDtypeStruct((n_rows, n_cols), F32), compiler_params=_cparams(("parallel",)),
    )(parts)


BIG = ("w_in", "w_glu", "w_out", "w_up", "w_down")
SMALL = ("norm_mix_g", "q_norm_g", "k_norm_g", "attn_sinks", "ssm_lambda_re", "ssm_lambda_im", "ssm_log_step",
         "ssm_b_re", "ssm_b_im", "ssm_c_re", "ssm_c_im", "ssm_d", "b_glu", "attn_out_g", "ssm_out_g", "norm_mlp_g")
_SSM_NAMES = ("ssm_lambda_re", "ssm_lambda_im", "ssm_log_step", "ssm_b_re", "ssm_b_im", "ssm_c_re", "ssm_c_im")


def _divisor(n, cands):
    for c in cands:
        if n % c == 0:
            return c
    return n


def _mm(a, b, mode, name, outs=(F32,), epilogue=_ident, tiles=(), deps=(), blocked=False):
    if mode == "nn":
        m, k = a.shape
        n = b.shape[0] * b.shape[2] if blocked else b.shape[1]
    elif mode == "nt":
        m, k = a.shape
        n = b.shape[1] if blocked else b.shape[0]
    else:
        (k, m), n = a.shape, b.shape[1]
    if mode == "tn":
        tm, tk = _divisor(m, (1024,)), _row_tile(k, 1408)
    else:
        tm, tk = _row_tile(m, 1408), (k if k <= 2560 else _divisor(k, (1024, 512)))
    tn = _divisor(n, (1024, 1280, 512))
    return _matmul(a, b, mode=mode, tm=tm, tn=tn, tk=tk, outs=list(outs), epilogue=epilogue, tiles=tiles, deps=deps,
                   blocked=blocked, name=name)


def _add_tile(acc, res):
    return (acc + res,)


def _relu_sq(acc):
    r = jnp.maximum(acc, 0.0)
    return r, r * r


def _relu_sq_bwd(acc, r):
    return (acc * (2.0 * r.astype(F32)),)


def _row(v):
    return v.reshape(1, -1)


def _layer_fwd(hres, fetch, sp, l, deps=()):
    tag = f"_l{l}"
    wts = {}
    hb = _rmsnorm_fwd([hres], [_row(sp["norm_mix_g"])], name="norm_mix" + tag, deps=deps)
    wts["w_in"] = fetch("w_in", hb)
    proj, = _mm(hb, wts["w_in"], "nn", "proj" + tag)
    gq, gk, sinks = _row(sp["q_norm_g"]), _row(sp["k_norm_g"]), _row(sp["attn_sinks"])
    o, lse = _attn_fwd(proj, gq, gk, sinks, name="attn_fwd" + tag)
    (wc, cmat, lam8), prep_vjp = jax.vjp(_ssm_prep, *[sp[n] for n in _SSM_NAMES])
    cmat = cmat.astype(BF16)
    y, xs = _ssm_fwd(proj, wc, cmat, lam8, _row(sp["ssm_d"]), name="ssm_fwd" + tag)
    wts["w_glu"] = fetch("w_glu", y)
    s = _glu_fwd(y, wts["w_glu"], _row(sp["b_glu"]), name="glu_fwd" + tag)
    mix = _rmsnorm_fwd([o, s], [_row(sp["attn_out_g"]), _row(sp["ssm_out_g"])], name="norm_out" + tag)
    wts["w_out"] = fetch("w_out", mix)
    hres2, = _mm(mix, wts["w_out"], "nn", "out_proj" + tag, epilogue=_add_tile, tiles=(hres,))
    h2 = _rmsnorm_fwd([hres2], [_row(sp["norm_mlp_g"])], name="norm_mlp" + tag)
    wts["w_up"] = fetch("w_up", h2)
    r, act = _mm(h2, wts["w_up"], "nn", "mlp_up" + tag, outs=(BF16, BF16), epilogue=_relu_sq, blocked=True)
    wts["w_down"] = fetch("w_down", act)
    hres3, = _mm(act, wts["w_down"], "nn", "mlp_down" + tag, epilogue=_add_tile, tiles=(hres2,))
    saved = dict(wts=wts, hres=hres, hb=hb, proj=proj, o=o, lse=lse, wc=wc, cmat=cmat, lam8=lam8, prep_vjp=prep_vjp,
                 y=y, xs=xs, s=s, mix=mix, hres2=hres2, h2=h2, r=r, act=act)
    return hres3, saved


def _layer_bwd(dres, dres_b, sp, sv, l, early_grads, deps=()):
    tag = f"_l{l}"
    wts = sv["wts"]
    gb, gs = {}, {}
    d_up, = _mm(dres_b, wts["w_down"], "nt", "mlp_down_dx" + tag, outs=(BF16,), epilogue=_relu_sq_bwd, tiles=(sv["r"],),
                deps=deps)
    gb["w_down"], = _mm(sv["act"], dres_b, "tn", "mlp_down_dw" + tag, outs=(BF16,))
    gb["w_up"], = _mm(sv["h2"], d_up, "tn", "mlp_up_dw" + tag, outs=(BF16,), blocked=True)
    deps = early_grads(l, {n: gb.pop(n) for n in ("w_up", "w_down")})
    dh2, = _mm(d_up, wts["w_up"], "nt", "mlp_up_dx" + tag, blocked=True, deps=deps)
    (dres2,), (dg,), dres2_b = _rmsnorm_bwd([sv["hres2"]], [_row(sp["norm_mlp_g"])], dh2, dres, name="norm_mlp_bwd" + tag)
    gs["norm_mlp_g"] = dg
    dmix, = _mm(dres2_b, wts["w_out"], "nt", "out_proj_dx" + tag)
    gb["w_out"], = _mm(sv["mix"], dres2_b, "tn", "out_proj_dw" + tag, outs=(BF16,))
    (do, ds), (dga, dgs) = _rmsnorm_bwd([sv["o"], sv["s"]], [_row(sp["attn_out_g"]), _row(sp["ssm_out_g"])], dmix, None,
                                        name="norm_out_bwd" + tag)
    gs["attn_out_g"], gs["ssm_out_g"] = dga, dgs
    dy, g_b, dz_b, db = _glu_bwd(sv["y"], wts["w_glu"], _row(sp["b_glu"]), ds, name="glu_bwd" + tag)
    gs["b_glu"] = db
    gb["w_glu"], = _mm(g_b, dz_b, "tn", "glu_dw" + tag, outs=(BF16,))
    du, dwc, dcmat, dlam8, dd = _ssm_bwd(sv["proj"], sv["xs"], dy, sv["wc"], sv["cmat"], sv["lam8"], _row(sp["ssm_d"]),
                                           name="ssm_bwd" + tag)
    gs["ssm_d"] = dd
    for n, g in zip(_SSM_NAMES, sv["prep_vjp"]((dwc, dcmat, dlam8))):
        gs[n] = g
    dq, dk, dv, dgq, dgk, dsinks = _attn_bwd(sv["proj"], _row(sp["q_norm_g"]), _row(sp["k_norm_g"]), _row(sp["attn_sinks"]),
                                             sv["o"], sv["lse"], do, name="attn_bwd" + tag)
    gs["q_norm_g"], gs["k_norm_g"], gs["attn_sinks"] = dgq, dgk, dsinks
    dproj = jnp.concatenate([dq, dk, dv, du], axis=1)
    gb["w_in"], = _mm(sv["hb"], dproj, "tn", "proj_dw" + tag, outs=(BF16,))
    dh, = _mm(dproj, wts["w_in"], "nt", "proj_dx" + tag)
    (dres_in,), (dg,), dres_in_b = _rmsnorm_bwd([sv["hres"]], [_row(sp["norm_mix_g"])], dh, dres2, name="norm_mix_bwd" + tag)
    gs["norm_mix_g"] = dg
    return dres_in, dres_in_b, gb, gs


def _local_step(x, target, meta, sp, weights_for_layer, early_grads, grads_of_layer):
    h = jnp.concatenate([jnp.zeros((PAD, x.shape[1]), F32), meta, x], axis=0)
    saved = []
    for l in range(DEPTH):
        fetch, deps = weights_for_layer(l, h)
        h, sv = _layer_fwd(h, fetch, {n: sp[n][l] for n in SMALL}, l, deps)
        saved.append(sv)
    dh, dh_b, loss = _loss_head(h, target, name="loss_head")
    gsmall = {n: [None] * DEPTH for n in SMALL}
    deps = ()
    for l in reversed(range(DEPTH)):
        dh, dh_b, gb, gs = _layer_bwd(dh, dh_b, {n: sp[n][l] for n in SMALL}, saved[l], l, early_grads, deps)
        deps = grads_of_layer(l, gb, dh)
        for n in SMALL:
            gsmall[n][l] = gs[n].reshape(sp[n][l].shape)
    return loss, dh, gsmall


def _all_gather(x, *, name):
    def body(x_ref, out_ref, send_sems, recv_sems, local_sem):
        x, y, c = lax.axis_index("x"), lax.axis_index("y"), lax.axis_index("c")
        me, sibling = (x, y, c), (x, y, 1 - c)
        chips = [(1 - x, y), (x, 1 - y), (1 - x, 1 - y)]

        def slot(px, py, pc):
            return out_ref.at[4 * px + 2 * py + pc]

        def copy(k, block, to, src=None):
            return pltpu.make_async_remote_copy(
                src_ref=slot(*block) if src is None else src, dst_ref=slot(*block),
                send_sem=send_sems.at[k], recv_sem=recv_sems.at[k], device_id=to, device_id_type=_MESH)

        mine = pltpu.make_async_copy(x_ref, slot(*me), local_sem)
        mine.start()
        first = [copy(0, me, sibling, src=x_ref)]
        first += [copy(1 + j, me, (*chip, c), src=x_ref) for j, chip in enumerate(chips)]
        for cp in first:
            cp.start()
        passed = [copy(4 + j, (*chip, c), sibling) for j, chip in enumerate(chips)]
        for j, chip in enumerate(chips):
            copy(1 + j, (*chip, c), me).wait_recv()
            passed[j].start()
        copy(0, sibling, me).wait_recv()
        for j, chip in enumerate(chips):
            copy(4 + j, (*chip, 1 - c), me).wait_recv()
        for cp in first + passed:
            cp.wait_send()
        mine.wait()

    return pl.pallas_call(
        body, name=name, out_shape=jax.ShapeDtypeStruct((N_DEV,) + x.shape, x.dtype),
        in_specs=[_ANY], out_specs=_ANY,
        scratch_shapes=[pltpu.SemaphoreType.DMA((7,)), pltpu.SemaphoreType.DMA((7,)), pltpu.SemaphoreType.DMA],
    )(x)


def _exchange(g, *, name):
    def body(g_ref, r_ref, send_sems, recv_sems, local_sem):
        x, y, c = lax.axis_index("x"), lax.axis_index("y"), lax.axis_index("c")
        me = 4 * x + 2 * y + c
        mine = pltpu.make_async_copy(g_ref.at[me], r_ref.at[me], local_sem)
        mine.start()

        def peer(k):
            px, py, pc = (x + (k >> 2)) % 2, (y + ((k >> 1) & 1)) % 2, (c + (k & 1)) % 2
            return (px, py, pc), 4 * px + 2 * py + pc

        def copy(k, src_block, dst_block):
            to, _ = peer(k)
            return pltpu.make_async_remote_copy(
                src_ref=g_ref.at[src_block], dst_ref=r_ref.at[dst_block],
                send_sem=send_sems.at[k - 1], recv_sem=recv_sems.at[k - 1], device_id=to, device_id_type=_MESH)

        sends = [copy(k, peer(k)[1], me) for k in range(1, N_DEV)]
        for cp in sends:
            cp.start()
        for k in range(1, N_DEV):
            copy(k, me, peer(k)[1]).wait_recv()
        for cp in sends:
            cp.wait_send()
        mine.wait()

    return pl.pallas_call(
        body, name=name, out_shape=jax.ShapeDtypeStruct(g.shape, g.dtype),
        in_specs=[_ANY], out_specs=_ANY,
        scratch_shapes=[pltpu.SemaphoreType.DMA((7,)), pltpu.SemaphoreType.DMA((7,)), pltpu.SemaphoreType.DMA],
    )(g)


_HBM = pl.BlockSpec(memory_space=pltpu.HBM)
_SEM = pl.BlockSpec(memory_space=pltpu.SEMAPHORE)
_EFFECT = pltpu.SideEffectType.DATAFLOW_SIDE_EFFECTING
N_PEERS = N_DEV - 1


def _me_and_peers():
    x, y, c = lax.axis_index("x"), lax.axis_index("y"), lax.axis_index("c")
    peers = []
    for k in range(1, N_DEV):
        px, py, pc = (x + (k >> 2)) % 2, (y + ((k >> 1) & 1)) % 2, (c + (k & 1)) % 2
        peers.append(((px, py, pc), 4 * px + 2 * py + pc))
    return 4 * x + 2 * y + c, peers


def _send_start(srcs, after, *, per_peer, name):
    n_t = len(srcs)
    blks = [s.shape[1:] if per_peer else s.shape for s in srcs]
    lands = [lax.empty((N_DEV,) + b, s.dtype) for b, s in zip(blks, srcs)]

    def body(*refs):
        src_refs, land_refs = refs[:n_t], refs[n_t:2 * n_t]
        send_sems, recv_sems = refs[2 * n_t + 1], refs[2 * n_t + 2]
        token = refs[-1]
        me, peers = _me_and_peers()
        for t in range(n_t):
            for k, (to, idx) in enumerate(peers):
                pltpu.make_async_remote_copy(
                    src_ref=src_refs[t].at[idx] if per_peer else src_refs[t], dst_ref=land_refs[t].at[me],
                    send_sem=send_sems.at[t * N_PEERS + k], recv_sem=recv_sems.at[t * N_PEERS + k],
                    device_id=to, device_id_type=_MESH).start()
        token[...] = jnp.zeros_like(token)

    sems = pltpu.SemaphoreType.DMA((n_t * N_PEERS,))
    outs = pl.pallas_call(
        body, name=name,
        out_shape=(sems, sems, *[pltpu.HBM(s.shape, s.dtype) for s in srcs], *[pltpu.HBM(z.shape, z.dtype) for z in lands],
                   jax.ShapeDtypeStruct((8, 128), F32)),
        in_specs=[_HBM] * (2 * n_t) + [_ANY],
        out_specs=(_SEM, _SEM, *[_HBM] * (2 * n_t), pl.BlockSpec(memory_space=pltpu.VMEM)),
        input_output_aliases={i: 2 + i for i in range(2 * n_t)},
        compiler_params=pltpu.CompilerParams(has_side_effects=_EFFECT),
    )(*[pltpu.with_memory_space_constraint(s, pltpu.HBM) for s in srcs],
      *[pltpu.with_memory_space_constraint(z, pltpu.HBM) for z in lands], after)
    return outs[0], outs[1], list(outs[2:2 + n_t]), list(outs[2 + n_t:2 + 2 * n_t]), outs[-1]


def _send_wait(handles, after, *, per_peer, name):
    send_sems, recv_sems, srcs, lands = handles
    n_t = len(srcs)

    def body(*refs):
        src_refs, land_refs = refs[:n_t], refs[n_t:2 * n_t]
        send_sems, recv_sems = refs[2 * n_t], refs[2 * n_t + 1]
        _, peers = _me_and_peers()
        for t in range(n_t):
            for k, (to, idx) in enumerate(peers):
                cp = pltpu.make_async_remote_copy(
                    src_ref=src_refs[t].at[idx] if per_peer else src_refs[t], dst_ref=land_refs[t].at[idx],
                    send_sem=send_sems.at[t * N_PEERS + k], recv_sem=recv_sems.at[t * N_PEERS + k],
                    device_id=to, device_id_type=_MESH)
                cp.wait_send()
                cp.wait_recv()

    outs = pl.pallas_call(
        body, name=name,
        out_shape=(*[pltpu.HBM(s.shape, s.dtype) for s in srcs], *[pltpu.HBM(z.shape, z.dtype) for z in lands]),
        in_specs=[_HBM] * (2 * n_t) + [_SEM, _SEM, _ANY], out_specs=tuple([_HBM] * (2 * n_t)),
        input_output_aliases={i: i for i in range(2 * n_t)},
        compiler_params=pltpu.CompilerParams(has_side_effects=_EFFECT),
    )(*srcs, *lands, send_sems, recv_sems, after)
    me = 4 * lax.axis_index("x") + 2 * lax.axis_index("y") + lax.axis_index("c")
    filled = []
    for src, land in zip(outs[:n_t], outs[n_t:]):
        own = lax.dynamic_index_in_dim(src, me, 0, keepdims=False) if per_peer else src
        filled.append(lax.dynamic_update_index_in_dim(land, own, me, 0))
    return filled


def _adamw_layer(w, m, v, parts, l, prev, *, name):
    depth, n_rows, n_cols = w.shape
    n_parts = parts.shape[0]
    tr = _elem_rows(n_rows, n_cols, 4 * (8 + n_parts))
    c1 = 1.0 / (1.0 - ADAM_B1 ** ADAM_STEP)
    c2 = 1.0 / (1.0 - ADAM_B2 ** ADAM_STEP)
    n_prev = 0 if prev is None else 4

    def body(w_ref, m_ref, v_ref, p_ref, *rest):
        g_ref, d_ref, nm_ref, nv_ref = rest[n_prev:]
        g = p_ref[0].astype(F32)
        for k in range(1, n_parts):
            g = g + p_ref[k].astype(F32)
        nm = ADAM_B1 * m_ref[...] + (1.0 - ADAM_B1) * g
        nv = ADAM_B2 * v_ref[...] + (1.0 - ADAM_B2) * (g * g)
        g_ref[...] = g
        nm_ref[...] = nm
        nv_ref[...] = nv
        d_ref[...] = -ADAM_LR * ((nm * c1) / (jnp.sqrt(nv * c2) + ADAM_EPS) + ADAM_WD * w_ref[...])

    spec = pl.BlockSpec((None, tr, n_cols), lambda i: (l, i, 0))
    return pl.pallas_call(
        body, name=name, grid=(n_rows // tr,),
        in_specs=[spec, spec, spec, pl.BlockSpec((n_parts, tr, n_cols), lambda i: (0, i, 0))] + [_ANY] * n_prev,
        out_specs=[spec] * 4, out_shape=[jax.ShapeDtypeStruct(w.shape, F32)] * 4,
        input_output_aliases={4 + i: i for i in range(n_prev)},
        compiler_params=_cparams(("parallel",)),
    )(w, m, v, parts, *(prev or ()))


def _full_weights(g):
    out = {}
    for n in g:
        _, r, c = g[n].shape
        if n == "w_in":
            out[n] = jnp.transpose(g[n], (1, 0, 2)).reshape(r, N_DEV * c)
        elif n == "w_up":
            out[n] = g[n]
        else:
            out[n] = g[n].reshape(N_DEV * r, c)
    return out


def _grad_blocks(gb):
    out = []
    for n in gb:
        g = gb[n]
        if n == "w_in":
            rows, cols = g.shape
            out.append(jnp.transpose(g.reshape(rows, N_DEV, cols // N_DEV), (1, 0, 2)))
        elif n == "w_up":
            out.append(g)
        else:
            out.append(g.reshape(N_DEV, g.shape[0] // N_DEV, g.shape[1]))
    return out


_SMALL_ROWS = 1096


def _pack_small(d):
    flat = jnp.concatenate([d[n].reshape(-1) for n in SMALL])
    total = N_DEV * _SMALL_ROWS * 128
    assert flat.shape[0] <= total
    return jnp.pad(flat, (0, total - flat.shape[0])).reshape(N_DEV * _SMALL_ROWS, 128)


def _unpack_small(packed, like):
    flat = packed.reshape(-1)
    out, off = {}, 0
    for n in SMALL:
        size = like[n].size
        out[n] = flat[off:off + size].reshape(like[n].shape)
        off += size
    return out


def kernel(x, meta_tokens, norm_mix_g, w_in, q_norm_g, k_norm_g, attn_sinks, ssm_lambda_re, ssm_lambda_im, ssm_log_step, ssm_b_re, ssm_b_im, ssm_c_re, ssm_c_im, ssm_d, w_glu, b_glu, attn_out_g, ssm_out_g, w_out, norm_mlp_g, w_up, w_down, loss_target, m_meta_tokens, m_norm_mix_g, m_w_in, m_q_norm_g, m_k_norm_g, m_attn_sinks, m_ssm_lambda_re, m_ssm_lambda_im, m_ssm_log_step, m_ssm_b_re, m_ssm_b_im, m_ssm_c_re, m_ssm_c_im, m_ssm_d, m_w_glu, m_b_glu, m_attn_out_g, m_ssm_out_g, m_w_out, m_norm_mlp_g, m_w_up, m_w_down, v_meta_tokens, v_norm_mix_g, v_w_in, v_q_norm_g, v_k_norm_g, v_attn_sinks, v_ssm_lambda_re, v_ssm_lambda_im, v_ssm_log_step, v_ssm_b_re, v_ssm_b_im, v_ssm_c_re, v_ssm_c_im, v_ssm_d, v_w_glu, v_b_glu, v_attn_out_g, v_ssm_out_g, v_w_out, v_norm_mlp_g, v_w_up, v_w_down):
    a = dict(locals())
    order = ("meta_tokens", "norm_mix_g", "w_in", "q_norm_g", "k_norm_g", "attn_sinks", "ssm_lambda_re", "ssm_lambda_im",
             "ssm_log_step", "ssm_b_re", "ssm_b_im", "ssm_c_re", "ssm_c_im", "ssm_d", "w_glu", "b_glu", "attn_out_g",
             "ssm_out_g", "w_out", "norm_mlp_g", "w_up", "w_down")

    sp = {n: a[n] for n in SMALL}
    wb = {}
    for n in BIG:
        depth, r, c = a[n].shape
        wb[n] = _cast_bf16(a[n].reshape(depth * r, c), name="cast_" + n).reshape(depth, r, c)
    meta_all = _all_gather(meta_tokens, name="gather_meta")
    meta = jnp.transpose(meta_all, (1, 0, 2)).reshape(N_META, D_MODEL)

    gathers, exchanges = {}, {}
    updated = {n: None for n in BIG}
    groups = (("w_in",), ("w_glu", "w_out"), ("w_up", "w_down"))

    def start_gather(l, after):
        for gi, names in enumerate(groups):
            *handles, after = _send_start([wb[n][l] for n in names], after, per_peer=False, name=f"gather_start_l{l}_g{gi}")
            gathers[l, gi] = handles
        return after

    def weights_for_layer(l, h):
        token = start_gather(0, wb["w_down"]) if l == 0 else h
        if l + 1 < DEPTH:
            token = start_gather(l + 1, token)
        got = {}

        def fetch(name, after):
            if name not in got:
                gi = [name in names for names in groups].index(True)
                lands = _send_wait(gathers.pop((l, gi)), after, per_peer=False, name=f"gather_wait_l{l}_g{gi}")
                got.update(_full_weights(dict(zip(groups[gi], lands))))
            return got[name]

        return fetch, (token,)

    def update_layer(l):
        for part in ("a", "b"):
            names, handles, after = exchanges.pop((l, part))
            recv = _send_wait(handles, after, per_peer=True, name=f"exchange_wait_l{l}_{part}")
            for n, parts in zip(names, recv):
                updated[n] = _adamw_layer(a[n], a["m_" + n], a["v_" + n], parts, l, updated[n], name=f"adamw_{n}_l{l}")

    def early_grads(l, gb):
        *handles, token = _send_start(_grad_blocks(gb), gb["w_up"], per_peer=True, name=f"exchange_start_l{l}_a")
        exchanges[l, "a"] = [tuple(gb), handles, token]
        return (token,)

    def grads_of_layer(l, gb, dh):
        *handles, token = _send_start(_grad_blocks(gb), dh, per_peer=True, name=f"exchange_start_l{l}_b")
        exchanges[l, "b"] = [tuple(gb), handles, token]
        if l + 1 < DEPTH:
            for part in ("a", "b"):
                exchanges[l + 1, part][2] = token
            update_layer(l + 1)
        return (token,)

    loss, dh0, gsmall = _local_step(x[0], loss_target[0], meta, sp, weights_for_layer, early_grads, grads_of_layer)
    loss = lax.psum(loss[0, 0], ("x", "y", "c"))
    grad, delta, new_m, new_v = {}, {}, {}, {}

    dmeta = jnp.transpose(dh0[PAD:BLOCK].reshape(N_META, N_DEV, D_MODEL // N_DEV), (1, 0, 2))
    outs = _adamw(meta_tokens, m_meta_tokens, v_meta_tokens, _exchange(dmeta, name="exchange_meta"), name="adamw_meta_tokens")
    grad["meta_tokens"], delta["meta_tokens"], new_m["meta_tokens"], new_v["meta_tokens"] = outs

    packed = _pack_small({n: jnp.stack(gsmall[n]) for n in SMALL}).reshape(N_DEV, _SMALL_ROWS, 128)
    share = _sum_parts(_exchange(packed, name="exchange_small"), name="sum_small")
    total = _all_gather(share, name="gather_small").reshape(1, N_DEV * _SMALL_ROWS, 128)
    outs = _adamw(_pack_small(sp), _pack_small({n: a["m_" + n] for n in SMALL}), _pack_small({n: a["v_" + n] for n in SMALL}),
                  total, name="adamw_small")
    for dst, o in zip((grad, delta, new_m, new_v), outs):
        dst.update(_unpack_small(o, sp))

    for part in ("a", "b"):
        exchanges[0, part][2] = outs[0]
    update_layer(0)
    for n in BIG:
        grad[n], delta[n], new_m[n], new_v[n] = updated[n]

    return (loss, dh0[BLOCK:][None], *[grad[n] for n in order], *[delta[n] for n in order],
            *[new_m[n] for n in order], *[new_v[n] for n in order])
```

```python
import functools
import math

import jax
import jax.numpy as jnp
from jax import lax
from jax.experimental import pallas as pl
from jax.experimental.pallas import tpu as pltpu

F32 = jnp.float32
BF16 = jnp.bfloat16

N_DEV = 8
D_MODEL = 2048
SEQ = 4096
DEPTH = 4
N_META = 16
HEAD_DIM = 64
ATTN_WIDTH = D_MODEL // 2
N_HEADS = ATTN_WIDTH // HEAD_DIM
N_KV_HEADS = N_HEADS // 4
KV_GROUP = N_HEADS // N_KV_HEADS
KV_WIDTH = N_KV_HEADS * HEAD_DIM
SSM_WIDTH = D_MODEL - ATTN_WIDTH
SSM_GROUP_CH = 16
SSM_GROUPS = SSM_WIDTH // SSM_GROUP_CH
SSM_STATE = 64
WINDOW = 128
BLOCK = 128
PAD = BLOCK - N_META
D_FF = 4 * D_MODEL
IN_WIDTH = ATTN_WIDTH + 2 * KV_WIDTH + SSM_WIDTH
NORM_EPS = 1e-6
NEG_INF = -1e30
ADAM_LR = 0.001
ADAM_B1 = 0.9
ADAM_B2 = 0.999
ADAM_EPS = 1e-08
ADAM_WD = 0.01
ADAM_STEP = 10

VMEM_LIMIT = 56 * 1024 * 1024
_MESH = pl.DeviceIdType.MESH
_ANY = pl.BlockSpec(memory_space=pl.ANY)


def _cparams(sem=None):
    return pltpu.CompilerParams(dimension_semantics=sem, vmem_limit_bytes=VMEM_LIMIT)


def _matmul(a, b, *, mode, tm, tn, tk, outs, epilogue, tiles=(), rows=(), deps=(), blocked=False, name):
    if mode == "nn":
        m, k = a.shape
        if blocked:
            n = b.shape[0] * b.shape[2]
            assert tn == b.shape[2]
            b_spec = pl.BlockSpec((None, tk, tn), lambda i, j, kk: (j, kk, 0))
        else:
            n = b.shape[1]
            b_spec = pl.BlockSpec((tk, tn), lambda i, j, kk: (kk, j))
        a_spec = pl.BlockSpec((tm, tk), lambda i, j, kk: (i, kk))
        dims = (((1,), (0,)), ((), ()))
    elif mode == "nt":
        m, k = a.shape
        if blocked:
            n = b.shape[1]
            assert tk == b.shape[2] and k == b.shape[0] * b.shape[2]
            b_spec = pl.BlockSpec((None, tn, tk), lambda i, j, kk: (kk, j, 0))
        else:
            n = b.shape[0]
            b_spec = pl.BlockSpec((tn, tk), lambda i, j, kk: (j, kk))
        a_spec = pl.BlockSpec((tm, tk), lambda i, j, kk: (i, kk))
        dims = (((1,), (1,)), ((), ()))
    else:
        (k, m), n = a.shape, b.shape[1]
        a_spec = pl.BlockSpec((tk, tm), lambda i, j, kk: (kk, i))
        b_spec = pl.BlockSpec((tk, tn), lambda i, j, kk: (kk, j))
        dims = (((0,), (0,)), ((), ()))
    assert m % tm == 0 and n % tn == 0 and k % tk == 0, (name, m, n, k, tm, tn, tk)
    nk = k // tk
    n_tiles, n_rows, n_outs, n_deps = len(tiles), len(rows), len(outs), len(deps)

    def body(a_ref, b_ref, *rest):
        tile_refs = rest[:n_tiles]
        row_refs = rest[n_tiles:n_tiles + n_rows]
        out_refs = rest[n_tiles + n_rows + n_deps:n_tiles + n_rows + n_deps + n_outs]
        def product():
            return lax.dot_general(a_ref[...].astype(BF16), b_ref[...].astype(BF16), dims, preferred_element_type=F32)

        def finish(acc):
            res = epilogue(acc, *[r[...] for r in tile_refs], *[r[...] for r in row_refs])
            for o_ref, o in zip(out_refs, res):
                o_ref[...] = o.astype(o_ref.dtype)

        if nk == 1:
            finish(product())
            return
        acc_ref = rest[-1]
        kk = pl.program_id(2)

        @pl.when(kk == 0)
        def _():
            acc_ref[...] = jnp.zeros_like(acc_ref)

        acc_ref[...] += product()

        @pl.when(kk == nk - 1)
        def _():
            finish(acc_ref[...])

    tile_spec = pl.BlockSpec((tm, tn), lambda i, j, kk: (i, j))
    row_spec = pl.BlockSpec((1, tn), lambda i, j, kk: (0, j))
    if mode == "tn" and blocked:
        out_specs = [pl.BlockSpec((None, tm, tn), lambda i, j, kk: (j, i, 0))] * n_outs
        out_shape = [jax.ShapeDtypeStruct((n // tn, m, tn), dt) for dt in outs]
    else:
        out_specs = [tile_spec] * n_outs
        out_shape = [jax.ShapeDtypeStruct((m, n), dt) for dt in outs]
    return pl.pallas_call(
        body, name=name, grid=(m // tm, n // tn, nk),
        in_specs=[a_spec, b_spec] + [tile_spec] * n_tiles + [row_spec] * n_rows + [_ANY] * n_deps,
        out_specs=out_specs, out_shape=out_shape,
        scratch_shapes=[pltpu.VMEM((tm, tn), F32)] if nk > 1 else [],
        compiler_params=_cparams(("parallel", "parallel", "arbitrary")),
    )(a, b, *tiles, *rows, *deps)


def _ident(acc):
    return (acc,)


def _row_tile(n_rows, cap):
    best = BLOCK
    for t in range(BLOCK, cap + 1, BLOCK):
        if n_rows % t == 0:
            best = t
    return best


def _rmsnorm_fwd(xs, gs, *, name, deps=()):
    n_rows, width = xs[0].shape
    n = len(xs)
    tr = _row_tile(n_rows, 384)

    def body(*refs):
        o_ref = refs[-1]
        parts = []
        for x_ref, g_ref in zip(refs[:n], refs[n:2 * n]):
            x = x_ref[...]
            r = lax.rsqrt(jnp.mean(x * x, axis=-1, keepdims=True) + NORM_EPS)
            parts.append(x * r * g_ref[...])
        o_ref[...] = (parts[0] if n == 1 else jnp.concatenate(parts, axis=1)).astype(BF16)

    return pl.pallas_call(
        body, name=name, grid=(n_rows // tr,),
        in_specs=[pl.BlockSpec((tr, width), lambda i: (i, 0))] * n + [pl.BlockSpec((1, width), lambda i: (0, 0))] * n
        + [_ANY] * len(deps),
        out_specs=pl.BlockSpec((tr, n * width), lambda i: (i, 0)),
        out_shape=jax.ShapeDtypeStruct((n_rows, n * width), BF16),
        compiler_params=_cparams(("parallel",)),
    )(*xs, *gs, *deps)


def _rmsnorm_bwd(xs, gs, dy, res, *, name):
    n_rows, width = xs[0].shape
    n = len(xs)
    tr = _row_tile(n_rows, 384)
    has_res = res is not None

    def body(*refs):
        x_refs, g_refs, dy_ref = refs[:n], refs[n:2 * n], refs[2 * n]
        res_ref = refs[2 * n + 1] if has_res else None
        outs = refs[2 * n + 1 + int(has_res):]
        dx_refs, dg_refs = outs[:n], outs[n:2 * n]
        i = pl.program_id(0)
        for c in range(n):
            x = x_refs[c][...]
            d = dy_ref[:, c * width:(c + 1) * width]
            r = lax.rsqrt(jnp.mean(x * x, axis=-1, keepdims=True) + NORM_EPS)
            xh = x * r
            gd = d * g_refs[c][...]
            dx = r * (gd - xh * jnp.mean(gd * xh, axis=-1, keepdims=True))
            if has_res:
                dx = dx + res_ref[...]
                outs[2 * n][...] = dx.astype(BF16)
            dx_refs[c][...] = dx
            part = jnp.sum(d * xh, axis=0, keepdims=True)

            @pl.when(i == 0)
            def _():
                dg_refs[c][...] = part

            @pl.when(i > 0)
            def _():
                dg_refs[c][...] += part

    row_spec = pl.BlockSpec((tr, width), lambda i: (i, 0))
    vec_spec = pl.BlockSpec((1, width), lambda i: (0, 0))
    outs = pl.pallas_call(
        body, name=name, grid=(n_rows // tr,),
        in_specs=[row_spec] * n + [vec_spec] * n + [pl.BlockSpec((tr, n * width), lambda i: (i, 0))] + [row_spec] * int(has_res),
        out_specs=[row_spec] * n + [vec_spec] * n + [row_spec] * int(has_res),
        out_shape=[jax.ShapeDtypeStruct((n_rows, width), F32)] * n + [jax.ShapeDtypeStruct((1, width), F32)] * n
        + [jax.ShapeDtypeStruct((n_rows, width), BF16)] * int(has_res),
        compiler_params=_cparams(("arbitrary",)),
    )(*xs, *gs, dy, *([res] if has_res else []))
    if has_res:
        return outs[:n], outs[n:2 * n], outs[2 * n]
    return outs[:n], outs[n:]


_SCALE = 1.0 / math.sqrt(HEAD_DIM)
_DN_NT = (((1,), (1,)), ((), ()))
_DN_TN = (((0,), (0,)), ((), ()))


def _head_norm(x, g):
    r = lax.rsqrt(jnp.mean(x * x, axis=-1, keepdims=True) + NORM_EPS)
    return x * r * g, r


def _attn_geometry(n):
    rows = KV_GROUP * BLOCK
    i = lax.broadcasted_iota(jnp.int32, (rows, 3 * BLOCK), 0) % BLOCK
    j = lax.broadcasted_iota(jnp.int32, (rows, 3 * BLOCK), 1)
    t_pos = n * BLOCK + i - PAD
    is_meta = j < BLOCK
    m_pos = j - PAD
    s_pos = (n - 1) * BLOCK + (j - BLOCK) - PAD
    meta_ok = (j >= PAD) & (m_pos <= t_pos)
    band_ok = (s_pos >= N_META) & (s_pos <= t_pos) & (t_pos - s_pos < WINDOW)
    valid = (is_meta & meta_ok) | (jnp.logical_not(is_meta) & band_ok)
    dist = jnp.abs(t_pos - jnp.where(is_meta, m_pos, s_pos)).astype(F32)
    return valid, dist


def _slope_col(kv):
    g = lax.broadcasted_iota(jnp.int32, (KV_GROUP * BLOCK, 1), 0) // BLOCK
    col = jnp.zeros((KV_GROUP * BLOCK, 1), F32)
    for gi in range(KV_GROUP):
        col = jnp.where(g == gi, 2.0 ** (-8.0 * (kv * KV_GROUP + gi + 1) / N_HEADS), col)
    return col


def _sink_col(sink_ref, kv):
    g = lax.broadcasted_iota(jnp.int32, (KV_GROUP * BLOCK, 1), 0) // BLOCK
    col = jnp.zeros((KV_GROUP * BLOCK, 1), F32)
    for gi in range(KV_GROUP):
        h = kv * KV_GROUP + gi
        col = jnp.where(g == gi, sink_ref[0:1, h:h + 1], col)
    return col


def _stack_heads(x, kv):
    return jnp.concatenate([x[:, (kv * KV_GROUP + g) * HEAD_DIM:(kv * KV_GROUP + g + 1) * HEAD_DIM]
                            for g in range(KV_GROUP)], axis=0)


def _attn_scores(q_ref, k_refs, gq_ref, gk_ref, sink_ref, kv, valid, dist):
    qs = _stack_heads(q_ref[...], kv)
    kcat = jnp.concatenate([r[:, kv * HEAD_DIM:(kv + 1) * HEAD_DIM] for r in k_refs], axis=0)
    qn, rq = _head_norm(qs, gq_ref[...])
    kn, rk = _head_norm(kcat, gk_ref[...])
    s = lax.dot_general(qn.astype(BF16), kn.astype(BF16), _DN_NT, preferred_element_type=F32) * _SCALE
    s = jnp.where(valid, s - _slope_col(kv) * dist, NEG_INF)
    return qs, kcat, qn, kn, rq, rk, s, _sink_col(sink_ref, kv)


def _attn_specs():
    kq = ATTN_WIDTH // KV_WIDTH
    q_spec = pl.BlockSpec((BLOCK, ATTN_WIDTH), lambda n: (n, 0))
    kv_specs = []
    for col in (kq, kq + 1):
        kv_specs += [pl.BlockSpec((BLOCK, KV_WIDTH), lambda n, col=col: (0, col)),
                     pl.BlockSpec((BLOCK, KV_WIDTH), lambda n, col=col: (jnp.maximum(n - 1, 0), col)),
                     pl.BlockSpec((BLOCK, KV_WIDTH), lambda n, col=col: (n, col))]
    small = [pl.BlockSpec((1, HEAD_DIM), lambda n: (0, 0)), pl.BlockSpec((1, HEAD_DIM), lambda n: (0, 0)),
             pl.BlockSpec((1, N_HEADS), lambda n: (0, 0))]
    return q_spec, kv_specs, small


def _attn_fwd(proj, gq, gk, sinks, *, name):
    n_rows = proj.shape[0]
    q_spec, kv_specs, small = _attn_specs()

    def body(q_ref, k0, k1, k2, v0, v1, v2, gq_ref, gk_ref, sink_ref, o_ref, lse_ref):
        valid, dist = _attn_geometry(pl.program_id(0))
        o_parts, lse_parts = [], []
        for kv in range(N_KV_HEADS):
            _, _, _, _, _, _, s, sink = _attn_scores(q_ref, (k0, k1, k2), gq_ref, gk_ref, sink_ref, kv, valid, dist)
            vcat = jnp.concatenate([r[:, kv * HEAD_DIM:(kv + 1) * HEAD_DIM] for r in (v0, v1, v2)], axis=0)
            m = jnp.maximum(jnp.max(s, axis=-1, keepdims=True), sink)
            p = jnp.exp(s - m)
            l = jnp.sum(p, axis=-1, keepdims=True) + jnp.exp(sink - m)
            o = jnp.dot(p.astype(BF16), vcat.astype(BF16), preferred_element_type=F32) / l
            lse = m + jnp.log(l)
            o_parts += [o[g * BLOCK:(g + 1) * BLOCK] for g in range(KV_GROUP)]
            lse_parts += [lse[g * BLOCK:(g + 1) * BLOCK] for g in range(KV_GROUP)]
        o_ref[...] = jnp.concatenate(o_parts, axis=1)
        lse_ref[...] = jnp.concatenate(lse_parts, axis=1)

    return pl.pallas_call(
        body, name=name, grid=(n_rows // BLOCK,),
        in_specs=[q_spec] + kv_specs + small,
        out_specs=[pl.BlockSpec((BLOCK, ATTN_WIDTH), lambda n: (n, 0)), pl.BlockSpec((BLOCK, N_HEADS), lambda n: (n, 0))],
        out_shape=[jax.ShapeDtypeStruct((n_rows, ATTN_WIDTH), F32), jax.ShapeDtypeStruct((n_rows, N_HEADS), F32)],
        compiler_params=_cparams(("parallel",)),
    )(proj, proj, proj, proj, proj, proj, proj, gq, gk, sinks)


def _attn_bwd(proj, gq, gk, sinks, o, lse, do, *, name):
    n_rows = proj.shape[0]
    q_spec, kv_specs, small = _attn_specs()

    def body(q_ref, k0, k1, k2, v0, v1, v2, gq_ref, gk_ref, sink_ref, o_ref, lse_ref, do_ref,
             dq_ref, dkb_ref, dvb_ref, dgq_ref, dgk_ref, dsink_ref, dk_ref, dv_ref):
        n = pl.program_id(0)

        @pl.when(n == 0)
        def _():
            dk_ref[...] = jnp.zeros_like(dk_ref)
            dv_ref[...] = jnp.zeros_like(dv_ref)
            dgq_ref[...] = jnp.zeros_like(dgq_ref)
            dgk_ref[...] = jnp.zeros_like(dgk_ref)
            dsink_ref[...] = jnp.zeros_like(dsink_ref)

        valid, dist = _attn_geometry(n)
        dq_parts, dk_parts, dv_parts, dsink_parts = [], [], [], []
        dgq = jnp.zeros((1, HEAD_DIM), F32)
        dgk = jnp.zeros((1, HEAD_DIM), F32)
        for kv in range(N_KV_HEADS):
            qs, kcat, qn, kn, rq, rk, s, sink = _attn_scores(q_ref, (k0, k1, k2), gq_ref, gk_ref, sink_ref, kv, valid, dist)
            vcat = jnp.concatenate([r[:, kv * HEAD_DIM:(kv + 1) * HEAD_DIM] for r in (v0, v1, v2)], axis=0)
            os_ = _stack_heads(o_ref[...], kv)
            dos = _stack_heads(do_ref[...], kv)
            lse = jnp.concatenate([lse_ref[:, kv * KV_GROUP + g:kv * KV_GROUP + g + 1] for g in range(KV_GROUP)], axis=0)
            p = jnp.exp(s - lse)
            delta = jnp.sum(dos * os_, axis=-1, keepdims=True)
            dp = lax.dot_general(dos.astype(BF16), vcat.astype(BF16), _DN_NT, preferred_element_type=F32)
            ds = (p * (dp - delta)) * _SCALE
            dsink_rows = -jnp.exp(sink - lse) * delta
            dsink_parts += [jnp.sum(dsink_rows[g * BLOCK:(g + 1) * BLOCK], axis=0, keepdims=True) for g in range(KV_GROUP)]
            dv_parts.append(lax.dot_general(p.astype(BF16), dos.astype(BF16), _DN_TN, preferred_element_type=F32))
            dsb = ds.astype(BF16)
            dqn = jnp.dot(dsb, kn.astype(BF16), preferred_element_type=F32)
            dkn = lax.dot_general(dsb, qn.astype(BF16), _DN_TN, preferred_element_type=F32)
            qh = qs * rq
            gd = dqn * gq_ref[...]
            dqs = rq * (gd - qh * jnp.mean(gd * qh, axis=-1, keepdims=True))
            dgq = dgq + jnp.sum(dqn * qh, axis=0, keepdims=True)
            kh = kcat * rk
            gdk = dkn * gk_ref[...]
            dk_parts.append(rk * (gdk - kh * jnp.mean(gdk * kh, axis=-1, keepdims=True)))
            dgk = dgk + jnp.sum(dkn * kh, axis=0, keepdims=True)
            dq_parts += [dqs[g * BLOCK:(g + 1) * BLOCK] for g in range(KV_GROUP)]
        dq_ref[...] = jnp.concatenate(dq_parts, axis=1).astype(BF16)
        dkc = jnp.concatenate(dk_parts, axis=1)
        dvc = jnp.concatenate(dv_parts, axis=1)
        prev = pl.multiple_of(jnp.maximum(n - 1, 0) * BLOCK, BLOCK)
        cur = pl.multiple_of(n * BLOCK, BLOCK)
        for acc_ref, val in ((dk_ref, dkc), (dv_ref, dvc)):
            acc_ref[0:BLOCK, :] += val[0:BLOCK]
            acc_ref[pl.ds(prev, BLOCK), :] += val[BLOCK:2 * BLOCK]
            acc_ref[pl.ds(cur, BLOCK), :] += val[2 * BLOCK:3 * BLOCK]
        dgq_ref[...] += dgq
        dgk_ref[...] += dgk
        dsink_ref[...] += jnp.concatenate(dsink_parts, axis=1)

        @pl.when(n == pl.num_programs(0) - 1)
        def _():
            dkb_ref[...] = dk_ref[...].astype(BF16)
            dvb_ref[...] = dv_ref[...].astype(BF16)

    blk = lambda w: pl.BlockSpec((BLOCK, w), lambda n: (n, 0))
    full = lambda r, w: pl.BlockSpec((r, w), lambda n: (0, 0))
    return pl.pallas_call(
        body, name=name, grid=(n_rows // BLOCK,),
        in_specs=[q_spec] + kv_specs + small + [blk(ATTN_WIDTH), blk(N_HEADS), blk(ATTN_WIDTH)],
        out_specs=[blk(ATTN_WIDTH), full(n_rows, KV_WIDTH), full(n_rows, KV_WIDTH),
                   full(1, HEAD_DIM), full(1, HEAD_DIM), full(1, N_HEADS)],
        out_shape=[jax.ShapeDtypeStruct((n_rows, ATTN_WIDTH), BF16), jax.ShapeDtypeStruct((n_rows, KV_WIDTH), BF16),
                   jax.ShapeDtypeStruct((n_rows, KV_WIDTH), BF16), jax.ShapeDtypeStruct((1, HEAD_DIM), F32),
                   jax.ShapeDtypeStruct((1, HEAD_DIM), F32), jax.ShapeDtypeStruct((1, N_HEADS), F32)],
        scratch_shapes=[pltpu.VMEM((n_rows, KV_WIDTH), F32), pltpu.VMEM((n_rows, KV_WIDTH), F32)],
        compiler_params=_cparams(("arbitrary",)),
    )(proj, proj, proj, proj, proj, proj, proj, gq, gk, sinks, o, lse, do)


SSM_LAGS = 8
SLAB_G = 128 // SSM_GROUP_CH
N_SLABS = SSM_GROUPS // SLAB_G
SLAB_STATE = SLAB_G * SSM_STATE
U_COL = (ATTN_WIDTH + 2 * KV_WIDTH) // 128


def _ssm_prep(lam_re, lam_im, log_step, b_re, b_im, c_re, c_im):
    lam = lax.complex(lam_re, lam_im)
    delta = jnp.exp(log_step)[:, None]
    lam_bar = jnp.exp(lam * delta)
    b_bar = ((lam_bar - 1.0) / lam)[..., None] * lax.complex(b_re, b_im)
    pw = [jnp.ones_like(lam_bar)]
    for _ in range(SSM_LAGS):
        pw.append(pw[-1] * lam_bar)
    w = jnp.stack(pw[:SSM_LAGS])[..., None] * b_bar[None]
    wri = jnp.stack([jnp.real(w), jnp.imag(w)]).reshape(2, SSM_LAGS, N_SLABS, SLAB_G, SSM_STATE, SSM_GROUP_CH)
    eye = jnp.eye(SLAB_G, dtype=F32)
    wc = jnp.transpose(wri, (2, 1, 3, 5, 0, 4)).reshape(N_SLABS, SSM_LAGS * 128, 2 * SSM_STATE)
    cri = jnp.stack([c_re, -c_im]).reshape(2, N_SLABS, SLAB_G, SSM_GROUP_CH, SSM_STATE)
    ct = jnp.transpose(cri, (1, 0, 2, 4, 3))
    cmat = ct[:, :, :, :, None, :] * eye[None, None, :, None, :, None]
    cmat = cmat.reshape(N_SLABS, 2 * SLAB_STATE, 128)
    l8 = pw[SSM_LAGS]
    lam8 = jnp.concatenate([jnp.real(l8).reshape(N_SLABS, 1, SLAB_STATE), jnp.imag(l8).reshape(N_SLABS, 1, SLAB_STATE)], axis=2)
    return wc, cmat, lam8


def _row_group():
    return (lax.broadcasted_iota(jnp.int32, (SSM_LAGS * 128, 1), 0) // SSM_GROUP_CH) % SLAB_G


def _spread_groups(wc):
    g_of_row = _row_group()
    return jnp.concatenate([jnp.where(g_of_row == g, wc[:, r * SSM_STATE:(r + 1) * SSM_STATE], 0.0)
                            for r in range(2) for g in range(SLAB_G)], axis=1)


def _gather_groups(dw):
    g_of_row = _row_group()
    parts = []
    for r in range(2):
        acc = jnp.zeros((SSM_LAGS * 128, SSM_STATE), F32)
        for g in range(SLAB_G):
            c0 = r * SLAB_STATE + g * SSM_STATE
            acc = acc + jnp.where(g_of_row == g, dw[:, c0:c0 + SSM_STATE], 0.0)
        parts.append(acc)
    return jnp.concatenate(parts, axis=1)


def _lagged(u, up, t_rows):
    ue = jnp.concatenate([up, u], axis=0)
    return jnp.concatenate([ue[SSM_LAGS - tau:SSM_LAGS - tau + t_rows] for tau in range(SSM_LAGS)], axis=1).astype(BF16)


def _ssm_fwd(proj, wc, cmat, lam8, dvec, *, name):
    n_rows = proj.shape[0]
    tt = _row_tile(n_rows, 1408)
    n_t = n_rows // tt
    sw = 2 * SLAB_STATE
    hs = SLAB_STATE

    def body(u_ref, up_ref, wc_ref, c_ref, l_ref, d_ref, y_ref, x_ref, carry_ref, w_ref):
        t = pl.program_id(1)

        @pl.when(t == 0)
        def _():
            carry_ref[...] = jnp.zeros_like(carry_ref)
            w_ref[...] = _spread_groups(wc_ref[...]).astype(BF16)

        u = u_ref[...]
        up = jnp.where(t > 0, up_ref[...], 0.0)
        x_ref[...] = jnp.dot(_lagged(u, up, tt), w_ref[...], preferred_element_type=F32)
        ar = jnp.broadcast_to(l_ref[:, :hs], (8, hs))
        ai = jnp.broadcast_to(l_ref[:, hs:], (8, hs))

        def step(b, c):
            xr, xi = c
            r0 = pl.multiple_of(b * 8, 8)
            w = x_ref[pl.ds(r0, 8), :]
            nr = w[:, :hs] + ar * xr - ai * xi
            ni = w[:, hs:] + ar * xi + ai * xr
            x_ref[pl.ds(r0, 8), :] = jnp.concatenate([nr, ni], axis=1)
            return nr, ni

        xr, xi = lax.fori_loop(0, tt // 8, step, (carry_ref[:, :hs], carry_ref[:, hs:]), unroll=8)
        carry_ref[...] = jnp.concatenate([xr, xi], axis=1)
        y_ref[...] = jnp.dot(x_ref[...].astype(BF16), c_ref[...], preferred_element_type=F32) + d_ref[...] * u

    return pl.pallas_call(
        body, name=name, grid=(N_SLABS, n_t),
        in_specs=[pl.BlockSpec((tt, 128), lambda j, t: (t, U_COL + j)),
                  pl.BlockSpec((8, 128), lambda j, t: (jnp.maximum(t * (tt // 8) - 1, 0), U_COL + j)),
                  pl.BlockSpec((None, SSM_LAGS * 128, 2 * SSM_STATE), lambda j, t: (j, 0, 0)),
                  pl.BlockSpec((None, sw, 128), lambda j, t: (j, 0, 0)),
                  pl.BlockSpec((None, 1, sw), lambda j, t: (j, 0, 0)),
                  pl.BlockSpec((1, 128), lambda j, t: (0, j))],
        out_specs=[pl.BlockSpec((tt, 128), lambda j, t: (t, j)), pl.BlockSpec((tt, sw), lambda j, t: (t, j))],
        out_shape=[jax.ShapeDtypeStruct((n_rows, SSM_WIDTH), F32), jax.ShapeDtypeStruct((n_rows, N_SLABS * sw), F32)],
        scratch_shapes=[pltpu.VMEM((8, sw), F32), pltpu.VMEM((SSM_LAGS * 128, sw), BF16)],
        compiler_params=_cparams(("parallel", "arbitrary")),
    )(proj, proj, wc, cmat, lam8, dvec)


def _ssm_bwd(proj, xs, dy, wc, cmat, lam8, dvec, *, name, deps=()):
    n_rows = proj.shape[0]
    tt = _row_tile(n_rows, 704)
    n_t = n_rows // tt
    sw = 2 * SLAB_STATE
    hs = SLAB_STATE

    def body(u_ref, up_ref, x_ref, xp_ref, dy_ref, wc_ref, c_ref, l_ref, d_ref,
             *rest):
        du_ref, dwc_ref, dc_ref, dl_ref, dd_ref, a_ref, carry_ref, head_ref, w_ref, dw_ref, ext_ref = rest[len(deps):]
        t = pl.program_id(1)
        ti = n_t - 1 - t

        @pl.when(t == 0)
        def _():
            w_ref[...] = _spread_groups(wc_ref[...]).astype(BF16)
            carry_ref[...] = jnp.zeros_like(carry_ref)
            head_ref[...] = jnp.zeros_like(head_ref)
            dw_ref[...] = jnp.zeros_like(dw_ref)
            dc_ref[...] = jnp.zeros_like(dc_ref)
            dl_ref[...] = jnp.zeros_like(dl_ref)
            dd_ref[...] = jnp.zeros_like(dd_ref)

        u = u_ref[...]
        up = jnp.where(ti > 0, up_ref[...], 0.0)
        ucat = _lagged(u, up, tt)
        dyv = dy_ref[...]
        dyb = dyv.astype(BF16)
        a_ref[...] = lax.dot_general(dyb, c_ref[...], _DN_NT, preferred_element_type=F32)
        lr = jnp.broadcast_to(l_ref[:, :hs], (8, hs))
        li = jnp.broadcast_to(l_ref[:, hs:], (8, hs))

        def step(i, c):
            cr, ci = c
            r0 = pl.multiple_of((tt // 8 - 1 - i) * 8, 8)
            g = a_ref[pl.ds(r0, 8), :]
            nr = g[:, :hs] + lr * cr + li * ci
            ni = g[:, hs:] + lr * ci - li * cr
            a_ref[pl.ds(r0, 8), :] = jnp.concatenate([nr, ni], axis=1)
            return nr, ni

        cr, ci = lax.fori_loop(0, tt // 8, step, (carry_ref[:, :hs], carry_ref[:, hs:]), unroll=8)
        carry_ref[...] = jnp.concatenate([cr, ci], axis=1)

        def conj_products(av, xv_):
            a_re, a_im, x_re, x_im = av[:, :hs], av[:, hs:], xv_[:, :hs], xv_[:, hs:]
            return jnp.concatenate([jnp.sum(a_re * x_re + a_im * x_im, axis=0, keepdims=True),
                                    jnp.sum(a_im * x_re - a_re * x_im, axis=0, keepdims=True)], axis=1)

        xprev = jnp.where(ti > 0, xp_ref[...], 0.0)
        dl_ref[...] += (conj_products(a_ref[SSM_LAGS:tt, :], x_ref[0:tt - SSM_LAGS, :])
                        + conj_products(a_ref[0:SSM_LAGS, :], xprev))
        ab = a_ref[...].astype(BF16)
        dw_ref[...] += lax.dot_general(ucat, ab, _DN_TN, preferred_element_type=F32)
        ext_ref[0:tt, :] = lax.dot_general(ab, w_ref[...], _DN_NT, preferred_element_type=F32)
        ext_ref[tt:tt + SSM_LAGS, :] = head_ref[...]
        du = d_ref[...] * dyv
        for tau in range(SSM_LAGS):
            du = du + ext_ref[tau:tau + tt, tau * 128:(tau + 1) * 128]
        head_ref[...] = ext_ref[0:SSM_LAGS, :]
        xv = x_ref[...]
        row = ti * tt + lax.broadcasted_iota(jnp.int32, (tt, 128), 0)
        du_ref[...] = jnp.where(row >= PAD, du, 0.0).astype(BF16)
        dd_ref[...] += jnp.sum(dyv * u, axis=0, keepdims=True)
        dc_ref[...] += lax.dot_general(xv.astype(BF16), dyb, _DN_TN, preferred_element_type=F32)

        @pl.when(t == n_t - 1)
        def _():
            dwc_ref[...] = _gather_groups(dw_ref[...])

    rt = lambda t: n_t - 1 - t
    prev8 = lambda t: jnp.maximum(rt(t) * (tt // 8) - 1, 0)
    return pl.pallas_call(
        body, name=name, grid=(N_SLABS, n_t),
        in_specs=[pl.BlockSpec((tt, 128), lambda j, t: (rt(t), U_COL + j)),
                  pl.BlockSpec((8, 128), lambda j, t: (prev8(t), U_COL + j)),
                  pl.BlockSpec((tt, sw), lambda j, t: (rt(t), j)),
                  pl.BlockSpec((8, sw), lambda j, t: (prev8(t), j)),
                  pl.BlockSpec((tt, 128), lambda j, t: (rt(t), j)),
                  pl.BlockSpec((None, SSM_LAGS * 128, 2 * SSM_STATE), lambda j, t: (j, 0, 0)),
                  pl.BlockSpec((None, sw, 128), lambda j, t: (j, 0, 0)),
                  pl.BlockSpec((None, 1, sw), lambda j, t: (j, 0, 0)),
                  pl.BlockSpec((1, 128), lambda j, t: (0, j))] + [_ANY] * len(deps),
        out_specs=[pl.BlockSpec((tt, 128), lambda j, t: (rt(t), j)),
                   pl.BlockSpec((None, SSM_LAGS * 128, 2 * SSM_STATE), lambda j, t: (j, 0, 0)),
                   pl.BlockSpec((None, sw, 128), lambda j, t: (j, 0, 0)),
                   pl.BlockSpec((None, 1, sw), lambda j, t: (j, 0, 0)),
                   pl.BlockSpec((1, 128), lambda j, t: (0, j))],
        out_shape=[jax.ShapeDtypeStruct((n_rows, SSM_WIDTH), BF16),
                   jax.ShapeDtypeStruct((N_SLABS, SSM_LAGS * 128, 2 * SSM_STATE), F32),
                   jax.ShapeDtypeStruct((N_SLABS, sw, 128), F32),
                   jax.ShapeDtypeStruct((N_SLABS, 1, sw), F32),
                   jax.ShapeDtypeStruct((1, SSM_WIDTH), F32)],
        scratch_shapes=[pltpu.VMEM((tt, sw), F32), pltpu.VMEM((8, sw), F32), pltpu.VMEM((8, sw), F32),
                        pltpu.VMEM((SSM_LAGS * 128, sw), BF16), pltpu.VMEM((SSM_LAGS * 128, sw), F32),
                        pltpu.VMEM((tt + SSM_LAGS, sw), F32)],
        compiler_params=_cparams(("parallel", "arbitrary")),
    )(proj, proj, xs, xs, dy, wc, cmat, lam8, dvec, *deps)


_GELU_C = math.sqrt(2.0 / math.pi)
_GELU_A = 0.044715


def _gelu(y):
    th = jnp.tanh(_GELU_C * (y + _GELU_A * y * y * y))
    return 0.5 * y * (1.0 + th), th


def _glu_fwd(y, w, b, *, name):
    n_rows, width = y.shape
    tr = _row_tile(n_rows, 384)

    def body(y_ref, w_ref, b_ref, o_ref):
        g, _ = _gelu(y_ref[...])
        z = jnp.dot(g.astype(BF16), w_ref[...], preferred_element_type=F32) + b_ref[...]
        o_ref[...] = g * jax.nn.sigmoid(z)

    return pl.pallas_call(
        body, name=name, grid=(n_rows // tr,),
        in_specs=[pl.BlockSpec((tr, width), lambda i: (i, 0)), pl.BlockSpec((width, width), lambda i: (0, 0)),
                  pl.BlockSpec((1, width), lambda i: (0, 0))],
        out_specs=pl.BlockSpec((tr, width), lambda i: (i, 0)),
        out_shape=jax.ShapeDtypeStruct((n_rows, width), F32),
        compiler_params=_cparams(("parallel",)),
    )(y, w, b)


def _glu_bwd(y, w, b, dout, *, name):
    n_rows, width = y.shape
    tr = _row_tile(n_rows, 384)

    def body(y_ref, w_ref, b_ref, do_ref, dy_ref, g_ref, dz_ref, db_ref):
        i = pl.program_id(0)
        yv = y_ref[...]
        g, th = _gelu(yv)
        gb = g.astype(BF16)
        z = jnp.dot(gb, w_ref[...], preferred_element_type=F32) + b_ref[...]
        sg = jax.nn.sigmoid(z)
        do = do_ref[...]
        dz = do * g * sg * (1.0 - sg)
        dzb = dz.astype(BF16)
        dg = do * sg + lax.dot_general(dzb, w_ref[...], _DN_NT, preferred_element_type=F32)
        dgelu = 0.5 * (1.0 + th) + 0.5 * yv * (1.0 - th * th) * _GELU_C * (1.0 + 3.0 * _GELU_A * yv * yv)
        dy_ref[...] = dg * dgelu
        g_ref[...] = gb
        dz_ref[...] = dzb
        part = jnp.sum(dz, axis=0, keepdims=True)

        @pl.when(i == 0)
        def _():
            db_ref[...] = part

        @pl.when(i > 0)
        def _():
            db_ref[...] += part

    row = pl.BlockSpec((tr, width), lambda i: (i, 0))
    vec = pl.BlockSpec((1, width), lambda i: (0, 0))
    return pl.pallas_call(
        body, name=name, grid=(n_rows // tr,),
        in_specs=[row, pl.BlockSpec((width, width), lambda i: (0, 0)), vec, row],
        out_specs=[row, row, row, vec],
        out_shape=[jax.ShapeDtypeStruct((n_rows, width), F32), jax.ShapeDtypeStruct((n_rows, width), BF16),
                   jax.ShapeDtypeStruct((n_rows, width), BF16), jax.ShapeDtypeStruct((1, width), F32)],
        compiler_params=_cparams(("arbitrary",)),
    )(y, w, b, dout)


def _loss_head(h, target, *, name):
    n_rows, width = h.shape

    def body(h_ref, t_ref, dh_ref, dhb_ref, loss_ref):
        i = pl.program_id(0)

        @pl.when(i == 0)
        def _():
            dh_ref[...] = jnp.zeros_like(dh_ref)
            dhb_ref[...] = jnp.zeros_like(dhb_ref)
            loss_ref[...] = jnp.zeros_like(loss_ref)

        @pl.when(i > 0)
        def _():
            err = h_ref[...] - t_ref[...]
            dh = err * (1.0 / width)
            dh_ref[...] = dh
            dhb_ref[...] = dh.astype(BF16)
            loss_ref[...] += (0.5 / width) * jnp.sum(err * err, keepdims=True)

    return pl.pallas_call(
        body, name=name, grid=(n_rows // BLOCK,),
        in_specs=[pl.BlockSpec((BLOCK, width), lambda i: (i, 0)),
                  pl.BlockSpec((BLOCK, width), lambda i: (jnp.maximum(i - 1, 0), 0))],
        out_specs=[pl.BlockSpec((BLOCK, width), lambda i: (i, 0)), pl.BlockSpec((BLOCK, width), lambda i: (i, 0)),
                   pl.BlockSpec((1, 1), lambda i: (0, 0))],
        out_shape=[jax.ShapeDtypeStruct((n_rows, width), F32), jax.ShapeDtypeStruct((n_rows, width), BF16),
                   jax.ShapeDtypeStruct((1, 1), F32)],
        compiler_params=_cparams(("arbitrary",)),
    )(h, target)


def _elem_rows(n_rows, n_cols, bytes_per_row_elem):
    cap = max(16, (12 * 1024 * 1024) // (n_cols * bytes_per_row_elem))
    best = None
    for t in range(16, min(n_rows, cap) + 1, 16):
        if n_rows % t == 0:
            best = t
    return best or n_rows


def _cast_bf16(x, *, name):
    n_rows, n_cols = x.shape
    tr = _elem_rows(n_rows, n_cols, 4)

    def body(x_ref, o_ref):
        o_ref[...] = x_ref[...].astype(BF16)

    spec = pl.BlockSpec((tr, n_cols), lambda i: (i, 0))
    return pl.pallas_call(body, name=name, grid=(n_rows // tr,), in_specs=[spec], out_specs=spec,
                          out_shape=jax.ShapeDtypeStruct(x.shape, BF16), compiler_params=_cparams(("parallel",)))(x)


def _adamw(w, m, v, parts, *, name):
    n_rows, n_cols = w.shape
    n_parts = parts.shape[0]
    tr = _elem_rows(n_rows, n_cols, 4 * (8 + n_parts))
    c1 = 1.0 / (1.0 - ADAM_B1 ** ADAM_STEP)
    c2 = 1.0 / (1.0 - ADAM_B2 ** ADAM_STEP)

    def body(w_ref, m_ref, v_ref, p_ref, g_ref, d_ref, nm_ref, nv_ref):
        g = p_ref[0].astype(F32)
        for k in range(1, n_parts):
            g = g + p_ref[k].astype(F32)
        nm = ADAM_B1 * m_ref[...] + (1.0 - ADAM_B1) * g
        nv = ADAM_B2 * v_ref[...] + (1.0 - ADAM_B2) * (g * g)
        g_ref[...] = g
        nm_ref[...] = nm
        nv_ref[...] = nv
        d_ref[...] = -ADAM_LR * ((nm * c1) / (jnp.sqrt(nv * c2) + ADAM_EPS) + ADAM_WD * w_ref[...])

    spec = pl.BlockSpec((tr, n_cols), lambda i: (i, 0))
    return pl.pallas_call(
        body, name=name, grid=(n_rows // tr,),
        in_specs=[spec, spec, spec, pl.BlockSpec((n_parts, tr, n_cols), lambda i: (0, i, 0))],
        out_specs=[spec] * 4, out_shape=[jax.ShapeDtypeStruct(w.shape, F32)] * 4,
        compiler_params=_cparams(("parallel",)),
    )(w, m, v, parts)


def _sum_parts(parts, *, name):
    n_parts, n_rows, n_cols = parts.shape
    tr = _elem_rows(n_rows, n_cols, 4 * (1 + n_parts))

    def body(p_ref, o_ref):
        g = p_ref[0].astype(F32)
        for k in range(1, n_parts):
            g = g + p_ref[k].astype(F32)
        o_ref[...] = g

    return pl.pallas_call(
        body, name=name, grid=(n_rows // tr,),
        in_specs=[pl.BlockSpec((n_parts, tr, n_cols), lambda i: (0, i, 0))],
        out_specs=pl.BlockSpec((tr, n_cols), lambda i: (i, 0)),
        out_shape=jax.ShapeDtypeStruct((n_rows, n_cols), F32), compiler_params=_cparams(("parallel",)),
    )(parts)


BIG = ("w_in", "w_glu", "w_out", "w_up", "w_down")
SMALL = ("norm_mix_g", "q_norm_g", "k_norm_g", "attn_sinks", "ssm_lambda_re", "ssm_lambda_im", "ssm_log_step",
         "ssm_b_re", "ssm_b_im", "ssm_c_re", "ssm_c_im", "ssm_d", "b_glu", "attn_out_g", "ssm_out_g", "norm_mlp_g")
_SSM_NAMES = ("ssm_lambda_re", "ssm_lambda_im", "ssm_log_step", "ssm_b_re", "ssm_b_im", "ssm_c_re", "ssm_c_im")


def _divisor(n, cands):
    for c in cands:
        if n % c == 0:
            return c
    return n


def _mm(a, b, mode, name, outs=(F32,), epilogue=_ident, tiles=(), deps=(), blocked=False):
    if mode == "nn":
        m, k = a.shape
        n = b.shape[0] * b.shape[2] if blocked else b.shape[1]
    elif mode == "nt":
        m, k = a.shape
        n = b.shape[1] if blocked else b.shape[0]
    else:
        (k, m), n = a.shape, b.shape[1]
    if mode == "tn":
        tm, tn, tk = _divisor(m, (1024,)), _divisor(n, (1024, 512)), k
    elif k <= 2560:
        tm, tn, tk = _row_tile(m, 1408), _divisor(n, (1024, 1280, 512)), k
    elif blocked:
        tm, tn, tk = _row_tile(m, 1408), _divisor(n, (1024, 512)), b.shape[2]
    else:
        tm, tn, tk = _row_tile(m, 1408), _divisor(n, (1024, 512)), _divisor(k, (1024, 512))
    return _matmul(a, b, mode=mode, tm=tm, tn=tn, tk=tk, outs=list(outs), epilogue=epilogue, tiles=tiles, deps=deps,
                   blocked=blocked, name=name)


def _add_tile(acc, res):
    return (acc + res,)


def _relu_sq(acc):
    r = jnp.maximum(acc, 0.0)
    return r, r * r


def _relu_sq_bwd(acc, r):
    return (acc * (2.0 * r.astype(F32)),)


def _row(v):
    return v.reshape(1, -1)


def _layer_fwd(hres, fetch, sp, l, deps=()):
    tag = f"_l{l}"
    wts = {}
    hb = _rmsnorm_fwd([hres], [_row(sp["norm_mix_g"])], name="norm_mix" + tag, deps=deps)
    wts["w_in"] = fetch("w_in", hb)
    proj, = _mm(hb, wts["w_in"], "nn", "proj" + tag)
    gq, gk, sinks = _row(sp["q_norm_g"]), _row(sp["k_norm_g"]), _row(sp["attn_sinks"])
    o, lse = _attn_fwd(proj, gq, gk, sinks, name="attn_fwd" + tag)
    (wc, cmat, lam8), prep_vjp = jax.vjp(_ssm_prep, *[sp[n] for n in _SSM_NAMES])
    cmat = cmat.astype(BF16)
    y, xs = _ssm_fwd(proj, wc, cmat, lam8, _row(sp["ssm_d"]), name="ssm_fwd" + tag)
    wts["w_glu"] = fetch("w_glu", y)
    s = _glu_fwd(y, wts["w_glu"], _row(sp["b_glu"]), name="glu_fwd" + tag)
    mix = _rmsnorm_fwd([o, s], [_row(sp["attn_out_g"]), _row(sp["ssm_out_g"])], name="norm_out" + tag)
    wts["w_out"] = fetch("w_out", mix)
    hres2, = _mm(mix, wts["w_out"], "nn", "out_proj" + tag, epilogue=_add_tile, tiles=(hres,))
    h2 = _rmsnorm_fwd([hres2], [_row(sp["norm_mlp_g"])], name="norm_mlp" + tag)
    wts["w_up"] = fetch("w_up", h2)
    r, act = _mm(h2, wts["w_up"], "nn", "mlp_up" + tag, outs=(BF16, BF16), epilogue=_relu_sq, blocked=True)
    wts["w_down"] = fetch("w_down", act)
    hres3, = _mm(act, wts["w_down"], "nn", "mlp_down" + tag, epilogue=_add_tile, tiles=(hres2,))
    saved = dict(wts=wts, hres=hres, hb=hb, proj=proj, o=o, lse=lse, wc=wc, cmat=cmat, lam8=lam8, prep_vjp=prep_vjp,
                 y=y, xs=xs, s=s, mix=mix, hres2=hres2, h2=h2, r=r, act=act)
    return hres3, saved


def _layer_bwd(dres, dres_b, sp, sv, l, early_grads, deps=()):
    tag = f"_l{l}"
    wts = sv["wts"]
    gb, gs = {}, {}
    d_up, = _mm(dres_b, wts["w_down"], "nt", "mlp_down_dx" + tag, outs=(BF16,), epilogue=_relu_sq_bwd, tiles=(sv["r"],),
                deps=deps)
    gb["w_down"], = _mm(sv["act"], dres_b, "tn", "mlp_down_dw" + tag, outs=(BF16,))
    gb["w_up"], = _mm(sv["h2"], d_up, "tn", "mlp_up_dw" + tag, outs=(BF16,), blocked=True)
    deps = early_grads(l, "a", {n: gb.pop(n) for n in ("w_up", "w_down")})
    dh2, = _mm(d_up, wts["w_up"], "nt", "mlp_up_dx" + tag, blocked=True, deps=deps)
    (dres2,), (dg,), dres2_b = _rmsnorm_bwd([sv["hres2"]], [_row(sp["norm_mlp_g"])], dh2, dres, name="norm_mlp_bwd" + tag)
    gs["norm_mlp_g"] = dg
    dmix, = _mm(dres2_b, wts["w_out"], "nt", "out_proj_dx" + tag)
    gb["w_out"], = _mm(sv["mix"], dres2_b, "tn", "out_proj_dw" + tag, outs=(BF16,))
    (do, ds), (dga, dgs) = _rmsnorm_bwd([sv["o"], sv["s"]], [_row(sp["attn_out_g"]), _row(sp["ssm_out_g"])], dmix, None,
                                        name="norm_out_bwd" + tag)
    gs["attn_out_g"], gs["ssm_out_g"] = dga, dgs
    dy, g_b, dz_b, db = _glu_bwd(sv["y"], wts["w_glu"], _row(sp["b_glu"]), ds, name="glu_bwd" + tag)
    gs["b_glu"] = db
    gb["w_glu"], = _mm(g_b, dz_b, "tn", "glu_dw" + tag, outs=(BF16,))
    deps = early_grads(l, "b", {n: gb.pop(n) for n in ("w_out", "w_glu")})
    du, dwc, dcmat, dlam8, dd = _ssm_bwd(sv["proj"], sv["xs"], dy, sv["wc"], sv["cmat"], sv["lam8"], _row(sp["ssm_d"]),
                                         name="ssm_bwd" + tag, deps=deps)
    gs["ssm_d"] = dd
    for n, g in zip(_SSM_NAMES, sv["prep_vjp"]((dwc, dcmat, dlam8))):
        gs[n] = g
    dq, dk, dv, dgq, dgk, dsinks = _attn_bwd(sv["proj"], _row(sp["q_norm_g"]), _row(sp["k_norm_g"]), _row(sp["attn_sinks"]),
                                             sv["o"], sv["lse"], do, name="attn_bwd" + tag)
    gs["q_norm_g"], gs["k_norm_g"], gs["attn_sinks"] = dgq, dgk, dsinks
    dproj = _concat_cols([dq, dk, dv, du], name="dproj" + tag)
    gb["w_in"], = _mm(sv["hb"], dproj, "tn", "proj_dw" + tag, outs=(BF16,))
    deps = early_grads(l, "c", {"w_in": gb.pop("w_in")})
    dh, = _mm(dproj, wts["w_in"], "nt", "proj_dx" + tag, deps=deps)
    (dres_in,), (dg,), dres_in_b = _rmsnorm_bwd([sv["hres"]], [_row(sp["norm_mix_g"])], dh, dres2, name="norm_mix_bwd" + tag)
    gs["norm_mix_g"] = dg
    return dres_in, dres_in_b, gs


def _local_step(x, target, meta, sp, weights_for_layer, early_grads, grads_of_layer):
    h = jnp.concatenate([jnp.zeros((PAD, x.shape[1]), F32), meta, x], axis=0)
    saved = []
    for l in range(DEPTH):
        fetch, deps = weights_for_layer(l, h)
        h, sv = _layer_fwd(h, fetch, {n: sp[n][l] for n in SMALL}, l, deps)
        saved.append(sv)
    dh, dh_b, loss = _loss_head(h, target, name="loss_head")
    gsmall = {n: [None] * DEPTH for n in SMALL}
    deps = ()
    for l in reversed(range(DEPTH)):
        dh, dh_b, gs = _layer_bwd(dh, dh_b, {n: sp[n][l] for n in SMALL}, saved[l], l, early_grads, deps)
        deps = grads_of_layer(l, dh)
        for n in SMALL:
            gsmall[n][l] = gs[n].reshape(sp[n][l].shape)
    return loss, dh, gsmall


def _all_gather(x, *, name):
    def body(x_ref, out_ref, send_sems, recv_sems, local_sem):
        x, y, c = lax.axis_index("x"), lax.axis_index("y"), lax.axis_index("c")
        me, sibling = (x, y, c), (x, y, 1 - c)
        chips = [(1 - x, y), (x, 1 - y), (1 - x, 1 - y)]

        def slot(px, py, pc):
            return out_ref.at[4 * px + 2 * py + pc]

        def copy(k, block, to, src=None):
            return pltpu.make_async_remote_copy(
                src_ref=slot(*block) if src is None else src, dst_ref=slot(*block),
                send_sem=send_sems.at[k], recv_sem=recv_sems.at[k], device_id=to, device_id_type=_MESH)

        mine = pltpu.make_async_copy(x_ref, slot(*me), local_sem)
        mine.start()
        first = [copy(0, me, sibling, src=x_ref)]
        first += [copy(1 + j, me, (*chip, c), src=x_ref) for j, chip in enumerate(chips)]
        for cp in first:
            cp.start()
        passed = [copy(4 + j, (*chip, c), sibling) for j, chip in enumerate(chips)]
        for j, chip in enumerate(chips):
            copy(1 + j, (*chip, c), me).wait_recv()
            passed[j].start()
        copy(0, sibling, me).wait_recv()
        for j, chip in enumerate(chips):
            copy(4 + j, (*chip, 1 - c), me).wait_recv()
        for cp in first + passed:
            cp.wait_send()
        mine.wait()

    return pl.pallas_call(
        body, name=name, out_shape=jax.ShapeDtypeStruct((N_DEV,) + x.shape, x.dtype),
        in_specs=[_ANY], out_specs=_ANY,
        scratch_shapes=[pltpu.SemaphoreType.DMA((7,)), pltpu.SemaphoreType.DMA((7,)), pltpu.SemaphoreType.DMA],
    )(x)


def _exchange(g, *, name):
    def body(g_ref, r_ref, send_sems, recv_sems, local_sem):
        x, y, c = lax.axis_index("x"), lax.axis_index("y"), lax.axis_index("c")
        me = 4 * x + 2 * y + c
        mine = pltpu.make_async_copy(g_ref.at[me], r_ref.at[me], local_sem)
        mine.start()

        def peer(k):
            px, py, pc = (x + (k >> 2)) % 2, (y + ((k >> 1) & 1)) % 2, (c + (k & 1)) % 2
            return (px, py, pc), 4 * px + 2 * py + pc

        def copy(k, src_block, dst_block):
            to, _ = peer(k)
            return pltpu.make_async_remote_copy(
                src_ref=g_ref.at[src_block], dst_ref=r_ref.at[dst_block],
                send_sem=send_sems.at[k - 1], recv_sem=recv_sems.at[k - 1], device_id=to, device_id_type=_MESH)

        sends = [copy(k, peer(k)[1], me) for k in range(1, N_DEV)]
        for cp in sends:
            cp.start()
        for k in range(1, N_DEV):
            copy(k, me, peer(k)[1]).wait_recv()
        for cp in sends:
            cp.wait_send()
        mine.wait()

    return pl.pallas_call(
        body, name=name, out_shape=jax.ShapeDtypeStruct(g.shape, g.dtype),
        in_specs=[_ANY], out_specs=_ANY,
        scratch_shapes=[pltpu.SemaphoreType.DMA((7,)), pltpu.SemaphoreType.DMA((7,)), pltpu.SemaphoreType.DMA],
    )(g)


_HBM = pl.BlockSpec(memory_space=pltpu.HBM)
_SEM = pl.BlockSpec(memory_space=pltpu.SEMAPHORE)
_EFFECT = pltpu.SideEffectType.DATAFLOW_SIDE_EFFECTING
N_PEERS = N_DEV - 1


def _me_and_peers():
    x, y, c = lax.axis_index("x"), lax.axis_index("y"), lax.axis_index("c")
    peers = []
    for k in range(1, N_DEV):
        px, py, pc = (x + (k >> 2)) % 2, (y + ((k >> 1) & 1)) % 2, (c + (k & 1)) % 2
        peers.append(((px, py, pc), 4 * px + 2 * py + pc))
    return 4 * x + 2 * y + c, peers


def _send_start(srcs, after, *, per_peer, name):
    n_t = len(srcs)
    blks = [s.shape[1:] if per_peer else s.shape for s in srcs]
    lands = [lax.empty((N_DEV,) + b, s.dtype) for b, s in zip(blks, srcs)]

    def body(*refs):
        src_refs, land_refs = refs[:n_t], refs[n_t:2 * n_t]
        send_sems, recv_sems = refs[2 * n_t + 1], refs[2 * n_t + 2]
        token = refs[-1]
        me, peers = _me_and_peers()
        for t in range(n_t):
            for k, (to, idx) in enumerate(peers):
                pltpu.make_async_remote_copy(
                    src_ref=src_refs[t].at[idx] if per_peer else src_refs[t], dst_ref=land_refs[t].at[me],
                    send_sem=send_sems.at[t * N_PEERS + k], recv_sem=recv_sems.at[t * N_PEERS + k],
                    device_id=to, device_id_type=_MESH).start()
        token[...] = jnp.zeros_like(token)

    sems = pltpu.SemaphoreType.DMA((n_t * N_PEERS,))
    outs = pl.pallas_call(
        body, name=name,
        out_shape=(sems, sems, *[pltpu.HBM(s.shape, s.dtype) for s in srcs], *[pltpu.HBM(z.shape, z.dtype) for z in lands],
                   jax.ShapeDtypeStruct((8, 128), F32)),
        in_specs=[_HBM] * (2 * n_t) + [_ANY],
        out_specs=(_SEM, _SEM, *[_HBM] * (2 * n_t), pl.BlockSpec(memory_space=pltpu.VMEM)),
        input_output_aliases={i: 2 + i for i in range(2 * n_t)},
        compiler_params=pltpu.CompilerParams(has_side_effects=_EFFECT),
    )(*[pltpu.with_memory_space_constraint(s, pltpu.HBM) for s in srcs],
      *[pltpu.with_memory_space_constraint(z, pltpu.HBM) for z in lands], after)
    return outs[0], outs[1], list(outs[2:2 + n_t]), list(outs[2 + n_t:2 + 2 * n_t]), outs[-1]


def _send_wait(handles, after, *, per_peer, name):
    send_sems, recv_sems, srcs, lands = handles
    n_t = len(srcs)

    def body(*refs):
        src_refs, land_refs = refs[:n_t], refs[n_t:2 * n_t]
        send_sems, recv_sems = refs[2 * n_t], refs[2 * n_t + 1]
        _, peers = _me_and_peers()
        for t in range(n_t):
            for k, (to, idx) in enumerate(peers):
                cp = pltpu.make_async_remote_copy(
                    src_ref=src_refs[t].at[idx] if per_peer else src_refs[t], dst_ref=land_refs[t].at[idx],
                    send_sem=send_sems.at[t * N_PEERS + k], recv_sem=recv_sems.at[t * N_PEERS + k],
                    device_id=to, device_id_type=_MESH)
                cp.wait_send()
                cp.wait_recv()

    outs = pl.pallas_call(
        body, name=name,
        out_shape=(*[pltpu.HBM(s.shape, s.dtype) for s in srcs], *[pltpu.HBM(z.shape, z.dtype) for z in lands]),
        in_specs=[_HBM] * (2 * n_t) + [_SEM, _SEM, _ANY], out_specs=tuple([_HBM] * (2 * n_t)),
        input_output_aliases={i: i for i in range(2 * n_t)},
        compiler_params=pltpu.CompilerParams(has_side_effects=_EFFECT),
    )(*srcs, *lands, send_sems, recv_sems, after)
    me = 4 * lax.axis_index("x") + 2 * lax.axis_index("y") + lax.axis_index("c")
    filled = []
    for src, land in zip(outs[:n_t], outs[n_t:]):
        own = lax.dynamic_index_in_dim(src, me, 0, keepdims=False) if per_peer else src
        filled.append(lax.dynamic_update_index_in_dim(land, own, me, 0))
    return filled


def _adamw_layer(w, m, v, parts, l, prev, *, name):
    depth, n_rows, n_cols = w.shape
    n_parts = parts.shape[0]
    tr = _elem_rows(n_rows, n_cols, 4 * (8 + n_parts))
    c1 = 1.0 / (1.0 - ADAM_B1 ** ADAM_STEP)
    c2 = 1.0 / (1.0 - ADAM_B2 ** ADAM_STEP)
    n_prev = 0 if prev is None else 4

    def body(w_ref, m_ref, v_ref, p_ref, *rest):
        g_ref, d_ref, nm_ref, nv_ref = rest[n_prev:]
        g = p_ref[0].astype(F32)
        for k in range(1, n_parts):
            g = g + p_ref[k].astype(F32)
        nm = ADAM_B1 * m_ref[...] + (1.0 - ADAM_B1) * g
        nv = ADAM_B2 * v_ref[...] + (1.0 - ADAM_B2) * (g * g)
        g_ref[...] = g
        nm_ref[...] = nm
        nv_ref[...] = nv
        d_ref[...] = -ADAM_LR * ((nm * c1) / (jnp.sqrt(nv * c2) + ADAM_EPS) + ADAM_WD * w_ref[...])

    spec = pl.BlockSpec((None, tr, n_cols), lambda i: (l, i, 0))
    return pl.pallas_call(
        body, name=name, grid=(n_rows // tr,),
        in_specs=[spec, spec, spec, pl.BlockSpec((n_parts, tr, n_cols), lambda i: (0, i, 0))] + [_ANY] * n_prev,
        out_specs=[spec] * 4, out_shape=[jax.ShapeDtypeStruct(w.shape, F32)] * 4,
        input_output_aliases={4 + i: i for i in range(n_prev)},
        compiler_params=_cparams(("parallel",)),
    )(w, m, v, parts, *(prev or ()))


def _join_cols(blocks, *, name):
    nb, n_rows, c = blocks.shape
    tr = _divisor(n_rows, (256,))

    def body(b_ref, o_ref):
        o_ref[...] = jnp.concatenate([b_ref[j] for j in range(nb)], axis=1)

    return pl.pallas_call(
        body, name=name, grid=(n_rows // tr,), in_specs=[pl.BlockSpec((nb, tr, c), lambda i: (0, i, 0))],
        out_specs=pl.BlockSpec((tr, nb * c), lambda i: (i, 0)), out_shape=jax.ShapeDtypeStruct((n_rows, nb * c), blocks.dtype),
        compiler_params=_cparams(("parallel",)))(blocks)


def _split_cols(g, nb, *, name):
    n_rows, n_cols = g.shape
    c = n_cols // nb
    tr = _divisor(n_rows, (256,))

    def body(g_ref, o_ref):
        for j in range(nb):
            o_ref[j] = g_ref[:, j * c:(j + 1) * c]

    return pl.pallas_call(
        body, name=name, grid=(n_rows // tr,), in_specs=[pl.BlockSpec((tr, n_cols), lambda i: (i, 0))],
        out_specs=pl.BlockSpec((nb, tr, c), lambda i: (0, i, 0)), out_shape=jax.ShapeDtypeStruct((nb, n_rows, c), g.dtype),
        compiler_params=_cparams(("parallel",)))(g)


def _concat_cols(parts, *, name):
    n_rows = parts[0].shape[0]
    widths = [p.shape[1] for p in parts]
    tr = _row_tile(n_rows, 1408)

    def body(*refs):
        o_ref, off = refs[-1], 0
        for p_ref, w in zip(refs[:-1], widths):
            o_ref[:, off:off + w] = p_ref[...]
            off += w

    return pl.pallas_call(
        body, name=name, grid=(n_rows // tr,), in_specs=[pl.BlockSpec((tr, w), lambda i: (i, 0)) for w in widths],
        out_specs=pl.BlockSpec((tr, sum(widths)), lambda i: (i, 0)),
        out_shape=jax.ShapeDtypeStruct((n_rows, sum(widths)), parts[0].dtype), compiler_params=_cparams(("parallel",)))(*parts)


def _full_weights(g, tag):
    out = {}
    for n in g:
        _, r, c = g[n].shape
        if n == "w_in":
            out[n] = _join_cols(g[n], name="join_w_in" + tag)
        elif n == "w_up":
            out[n] = g[n]
        else:
            out[n] = g[n].reshape(N_DEV * r, c)
    return out


def _grad_blocks(gb, tag):
    out = []
    for n in gb:
        g = gb[n]
        if n == "w_in":
            out.append(_split_cols(g, N_DEV, name="split_w_in" + tag))
        elif n == "w_up":
            out.append(g)
        else:
            out.append(g.reshape(N_DEV, g.shape[0] // N_DEV, g.shape[1]))
    return out


_SMALL_ROWS = 1096


def _pack_small(d):
    flat = jnp.concatenate([d[n].reshape(-1) for n in SMALL])
    total = N_DEV * _SMALL_ROWS * 128
    assert flat.shape[0] <= total
    return jnp.pad(flat, (0, total - flat.shape[0])).reshape(N_DEV * _SMALL_ROWS, 128)


def _unpack_small(packed, like):
    flat = packed.reshape(-1)
    out, off = {}, 0
    for n in SMALL:
        size = like[n].size
        out[n] = flat[off:off + size].reshape(like[n].shape)
        off += size
    return out


def kernel(x, meta_tokens, norm_mix_g, w_in, q_norm_g, k_norm_g, attn_sinks, ssm_lambda_re, ssm_lambda_im, ssm_log_step, ssm_b_re, ssm_b_im, ssm_c_re, ssm_c_im, ssm_d, w_glu, b_glu, attn_out_g, ssm_out_g, w_out, norm_mlp_g, w_up, w_down, loss_target, m_meta_tokens, m_norm_mix_g, m_w_in, m_q_norm_g, m_k_norm_g, m_attn_sinks, m_ssm_lambda_re, m_ssm_lambda_im, m_ssm_log_step, m_ssm_b_re, m_ssm_b_im, m_ssm_c_re, m_ssm_c_im, m_ssm_d, m_w_glu, m_b_glu, m_attn_out_g, m_ssm_out_g, m_w_out, m_norm_mlp_g, m_w_up, m_w_down, v_meta_tokens, v_norm_mix_g, v_w_in, v_q_norm_g, v_k_norm_g, v_attn_sinks, v_ssm_lambda_re, v_ssm_lambda_im, v_ssm_log_step, v_ssm_b_re, v_ssm_b_im, v_ssm_c_re, v_ssm_c_im, v_ssm_d, v_w_glu, v_b_glu, v_attn_out_g, v_ssm_out_g, v_w_out, v_norm_mlp_g, v_w_up, v_w_down):
    a = dict(locals())
    order = ("meta_tokens", "norm_mix_g", "w_in", "q_norm_g", "k_norm_g", "attn_sinks", "ssm_lambda_re", "ssm_lambda_im",
             "ssm_log_step", "ssm_b_re", "ssm_b_im", "ssm_c_re", "ssm_c_im", "ssm_d", "w_glu", "b_glu", "attn_out_g",
             "ssm_out_g", "w_out", "norm_mlp_g", "w_up", "w_down")

    sp = {n: a[n] for n in SMALL}
    wb = {}
    for n in BIG:
        depth, r, c = a[n].shape
        wb[n] = _cast_bf16(a[n].reshape(depth * r, c), name="cast_" + n).reshape(depth, r, c)
    meta_all = _all_gather(meta_tokens, name="gather_meta")
    meta = jnp.transpose(meta_all, (1, 0, 2)).reshape(N_META, D_MODEL)

    gathers, exchanges = {}, {}
    updated = {n: None for n in BIG}
    groups = (("w_in",), ("w_glu", "w_out"), ("w_up",), ("w_down",))

    def start_gather(l, after):
        for gi, names in enumerate(groups):
            *handles, after = _send_start([wb[n][l] for n in names], after, per_peer=False, name=f"gather_start_l{l}_g{gi}")
            gathers[l, gi] = handles
        return after

    def weights_for_layer(l, h):
        token = start_gather(0, wb["w_in"]) if l == 0 else h
        if l + 1 < DEPTH:
            token = start_gather(l + 1, token)
        got = {}

        def fetch(name, after):
            if name not in got:
                gi = [name in names for names in groups].index(True)
                lands = _send_wait(gathers.pop((l, gi)), after, per_peer=False, name=f"gather_wait_l{l}_g{gi}")
                got.update(_full_weights(dict(zip(groups[gi], lands)), f"_l{l}"))
            return got[name]

        return fetch, (token,)

    def update_layer(l, after):
        for part in ("a", "b", "c"):
            names, handles = exchanges.pop((l, part))
            recv = _send_wait(handles, after, per_peer=True, name=f"exchange_wait_l{l}_{part}")
            for n, parts in zip(names, recv):
                updated[n] = _adamw_layer(a[n], a["m_" + n], a["v_" + n], parts, l, updated[n], name=f"adamw_{n}_l{l}")

    def early_grads(l, part, gb):
        *handles, token = _send_start(_grad_blocks(gb, f"_l{l}"), next(iter(gb.values())), per_peer=True,
                                      name=f"exchange_start_l{l}_{part}")
        exchanges[l, part] = (tuple(gb), handles)
        return (token,)

    def grads_of_layer(l, dh):
        if l + 1 < DEPTH:
            update_layer(l + 1, dh)
        return ()

    loss, dh0, gsmall = _local_step(x[0], loss_target[0], meta, sp, weights_for_layer, early_grads, grads_of_layer)
    loss = lax.psum(loss[0, 0], ("x", "y", "c"))
    grad, delta, new_m, new_v = {}, {}, {}, {}

    dmeta = jnp.transpose(dh0[PAD:BLOCK].reshape(N_META, N_DEV, D_MODEL // N_DEV), (1, 0, 2))
    outs = _adamw(meta_tokens, m_meta_tokens, v_meta_tokens, _exchange(dmeta, name="exchange_meta"), name="adamw_meta_tokens")
    grad["meta_tokens"], delta["meta_tokens"], new_m["meta_tokens"], new_v["meta_tokens"] = outs

    packed = _pack_small({n: jnp.stack(gsmall[n]) for n in SMALL}).reshape(N_DEV, _SMALL_ROWS, 128)
    share = _sum_parts(_exchange(packed, name="exchange_small"), name="sum_small")
    total = _all_gather(share, name="gather_small").reshape(1, N_DEV * _SMALL_ROWS, 128)
    outs = _adamw(_pack_small(sp), _pack_small({n: a["m_" + n] for n in SMALL}), _pack_small({n: a["v_" + n] for n in SMALL}),
                  total, name="adamw_small")
    for dst, o in zip((grad, delta, new_m, new_v), outs):
        dst.update(_unpack_small(o, sp))

    update_layer(0, outs[0])
    for n in BIG:
        grad[n], delta[n], new_m[n], new_v[n] = updated[n]

    return (loss, dh0[BLOCK:][None], *[grad[n] for n in order], *[delta[n] for n in order],
            *[new_m[n] for n in order], *[new_v[n] for n in order])
```

```python
import functools
import math

import jax
import jax.numpy as jnp
from jax import lax
from jax.experimental import pallas as pl
from jax.experimental.pallas import tpu as pltpu

F32 = jnp.float32
BF16 = jnp.bfloat16

N_DEV = 8
D_MODEL = 2048
SEQ = 4096
DEPTH = 4
N_META = 16
HEAD_DIM = 64
ATTN_WIDTH = D_MODEL // 2
N_HEADS = ATTN_WIDTH // HEAD_DIM
N_KV_HEADS = N_HEADS // 4
KV_GROUP = N_HEADS // N_KV_HEADS
KV_WIDTH = N_KV_HEADS * HEAD_DIM
SSM_WIDTH = D_MODEL - ATTN_WIDTH
SSM_GROUP_CH = 16
SSM_GROUPS = SSM_WIDTH // SSM_GROUP_CH
SSM_STATE = 64
WINDOW = 128
BLOCK = 128
PAD = BLOCK - N_META
D_FF = 4 * D_MODEL
IN_WIDTH = ATTN_WIDTH + 2 * KV_WIDTH + SSM_WIDTH
NORM_EPS = 1e-6
NEG_INF = -1e30
ADAM_LR = 0.001
ADAM_B1 = 0.9
ADAM_B2 = 0.999
ADAM_EPS = 1e-08
ADAM_WD = 0.01
ADAM_STEP = 10

VMEM_LIMIT = 56 * 1024 * 1024
_MESH = pl.DeviceIdType.MESH
_ANY = pl.BlockSpec(memory_space=pl.ANY)


def _cparams(sem=None):
    return pltpu.CompilerParams(dimension_semantics=sem, vmem_limit_bytes=VMEM_LIMIT)


def _matmul(a, b, *, mode, tm, tn, tk, outs, epilogue, tiles=(), rows=(), deps=(), blocked=False, name):
    if mode == "nn":
        m, k = a.shape
        if blocked:
            n = b.shape[0] * b.shape[2]
            assert tn == b.shape[2]
            b_spec = pl.BlockSpec((None, tk, tn), lambda i, j, kk: (j, kk, 0))
        else:
            n = b.shape[1]
            b_spec = pl.BlockSpec((tk, tn), lambda i, j, kk: (kk, j))
        a_spec = pl.BlockSpec((tm, tk), lambda i, j, kk: (i, kk))
        dims = (((1,), (0,)), ((), ()))
    elif mode == "nt":
        m, k = a.shape
        if blocked:
            n = b.shape[1]
            assert tk == b.shape[2] and k == b.shape[0] * b.shape[2]
            b_spec = pl.BlockSpec((None, tn, tk), lambda i, j, kk: (kk, j, 0))
        else:
            n = b.shape[0]
            b_spec = pl.BlockSpec((tn, tk), lambda i, j, kk: (j, kk))
        a_spec = pl.BlockSpec((tm, tk), lambda i, j, kk: (i, kk))
        dims = (((1,), (1,)), ((), ()))
    else:
        (k, m), n = a.shape, b.shape[1]
        a_spec = pl.BlockSpec((tk, tm), lambda i, j, kk: (kk, i))
        b_spec = pl.BlockSpec((tk, tn), lambda i, j, kk: (kk, j))
        dims = (((0,), (0,)), ((), ()))
    assert m % tm == 0 and n % tn == 0 and k % tk == 0, (name, m, n, k, tm, tn, tk)
    nk = k // tk
    n_tiles, n_rows, n_outs, n_deps = len(tiles), len(rows), len(outs), len(deps)

    def body(a_ref, b_ref, *rest):
        tile_refs = rest[:n_tiles]
        row_refs = rest[n_tiles:n_tiles + n_rows]
        out_refs = rest[n_tiles + n_rows + n_deps:n_tiles + n_rows + n_deps + n_outs]
        def product():
            return lax.dot_general(a_ref[...].astype(BF16), b_ref[...].astype(BF16), dims, preferred_element_type=F32)

        def finish(acc):
            res = epilogue(acc, *[r[...] for r in tile_refs], *[r[...] for r in row_refs])
            for o_ref, o in zip(out_refs, res):
                o_ref[...] = o.astype(o_ref.dtype)

        if nk == 1:
            finish(product())
            return
        acc_ref = rest[-1]
        kk = pl.program_id(2)

        @pl.when(kk == 0)
        def _():
            acc_ref[...] = jnp.zeros_like(acc_ref)

        acc_ref[...] += product()

        @pl.when(kk == nk - 1)
        def _():
            finish(acc_ref[...])

    tile_spec = pl.BlockSpec((tm, tn), lambda i, j, kk: (i, j))
    row_spec = pl.BlockSpec((1, tn), lambda i, j, kk: (0, j))
    if mode == "tn" and blocked:
        out_specs = [pl.BlockSpec((None, tm, tn), lambda i, j, kk: (j, i, 0))] * n_outs
        out_shape = [jax.ShapeDtypeStruct((n // tn, m, tn), dt) for dt in outs]
    else:
        out_specs = [tile_spec] * n_outs
        out_shape = [jax.ShapeDtypeStruct((m, n), dt) for dt in outs]
    return pl.pallas_call(
        body, name=name, grid=(m // tm, n // tn, nk),
        in_specs=[a_spec, b_spec] + [tile_spec] * n_tiles + [row_spec] * n_rows + [_ANY] * n_deps,
        out_specs=out_specs, out_shape=out_shape,
        scratch_shapes=[pltpu.VMEM((tm, tn), F32)] if nk > 1 else [],
        compiler_params=_cparams(("parallel", "parallel", "arbitrary")),
    )(a, b, *tiles, *rows, *deps)


def _ident(acc):
    return (acc,)


def _row_tile(n_rows, cap):
    best = BLOCK
    for t in range(BLOCK, cap + 1, BLOCK):
        if n_rows % t == 0:
            best = t
    return best


def _rmsnorm_fwd(xs, gs, *, name, deps=()):
    n_rows, width = xs[0].shape
    n = len(xs)
    tr = _row_tile(n_rows, 384)

    def body(*refs):
        o_ref = refs[-1]
        parts = []
        for x_ref, g_ref in zip(refs[:n], refs[n:2 * n]):
            x = x_ref[...]
            r = lax.rsqrt(jnp.mean(x * x, axis=-1, keepdims=True) + NORM_EPS)
            parts.append(x * r * g_ref[...])
        o_ref[...] = (parts[0] if n == 1 else jnp.concatenate(parts, axis=1)).astype(BF16)

    return pl.pallas_call(
        body, name=name, grid=(n_rows // tr,),
        in_specs=[pl.BlockSpec((tr, width), lambda i: (i, 0))] * n + [pl.BlockSpec((1, width), lambda i: (0, 0))] * n
        + [_ANY] * len(deps),
        out_specs=pl.BlockSpec((tr, n * width), lambda i: (i, 0)),
        out_shape=jax.ShapeDtypeStruct((n_rows, n * width), BF16),
        compiler_params=_cparams(("parallel",)),
    )(*xs, *gs, *deps)


def _rmsnorm_bwd(xs, gs, dy, res, *, name):
    n_rows, width = xs[0].shape
    n = len(xs)
    tr = _row_tile(n_rows, 384)
    has_res = res is not None

    def body(*refs):
        x_refs, g_refs, dy_ref = refs[:n], refs[n:2 * n], refs[2 * n]
        res_ref = refs[2 * n + 1] if has_res else None
        outs = refs[2 * n + 1 + int(has_res):]
        dx_refs, dg_refs = outs[:n], outs[n:2 * n]
        i = pl.program_id(0)
        for c in range(n):
            x = x_refs[c][...]
            d = dy_ref[:, c * width:(c + 1) * width]
            r = lax.rsqrt(jnp.mean(x * x, axis=-1, keepdims=True) + NORM_EPS)
            xh = x * r
            gd = d * g_refs[c][...]
            dx = r * (gd - xh * jnp.mean(gd * xh, axis=-1, keepdims=True))
            if has_res:
                dx = dx + res_ref[...]
                outs[2 * n][...] = dx.astype(BF16)
            dx_refs[c][...] = dx
            part = jnp.sum(d * xh, axis=0, keepdims=True)

            @pl.when(i == 0)
            def _():
                dg_refs[c][...] = part

            @pl.when(i > 0)
            def _():
                dg_refs[c][...] += part

    row_spec = pl.BlockSpec((tr, width), lambda i: (i, 0))
    vec_spec = pl.BlockSpec((1, width), lambda i: (0, 0))
    outs = pl.pallas_call(
        body, name=name, grid=(n_rows // tr,),
        in_specs=[row_spec] * n + [vec_spec] * n + [pl.BlockSpec((tr, n * width), lambda i: (i, 0))] + [row_spec] * int(has_res),
        out_specs=[row_spec] * n + [vec_spec] * n + [row_spec] * int(has_res),
        out_shape=[jax.ShapeDtypeStruct((n_rows, width), F32)] * n + [jax.ShapeDtypeStruct((1, width), F32)] * n
        + [jax.ShapeDtypeStruct((n_rows, width), BF16)] * int(has_res),
        compiler_params=_cparams(("arbitrary",)),
    )(*xs, *gs, dy, *([res] if has_res else []))
    if has_res:
        return outs[:n], outs[n:2 * n], outs[2 * n]
    return outs[:n], outs[n:]


_SCALE = 1.0 / math.sqrt(HEAD_DIM)
_DN_NT = (((1,), (1,)), ((), ()))
_DN_TN = (((0,), (0,)), ((), ()))


def _head_norm(x, g):
    r = lax.rsqrt(jnp.mean(x * x, axis=-1, keepdims=True) + NORM_EPS)
    return x * r * g, r


def _attn_geometry(n):
    rows = KV_GROUP * BLOCK
    i = lax.broadcasted_iota(jnp.int32, (rows, 3 * BLOCK), 0) % BLOCK
    j = lax.broadcasted_iota(jnp.int32, (rows, 3 * BLOCK), 1)
    t_pos = n * BLOCK + i - PAD
    is_meta = j < BLOCK
    m_pos = j - PAD
    s_pos = (n - 1) * BLOCK + (j - BLOCK) - PAD
    meta_ok = (j >= PAD) & (m_pos <= t_pos)
    band_ok = (s_pos >= N_META) & (s_pos <= t_pos) & (t_pos - s_pos < WINDOW)
    valid = (is_meta & meta_ok) | (jnp.logical_not(is_meta) & band_ok)
    dist = jnp.abs(t_pos - jnp.where(is_meta, m_pos, s_pos)).astype(F32)
    return valid, dist


def _slope_col(kv):
    g = lax.broadcasted_iota(jnp.int32, (KV_GROUP * BLOCK, 1), 0) // BLOCK
    col = jnp.zeros((KV_GROUP * BLOCK, 1), F32)
    for gi in range(KV_GROUP):
        col = jnp.where(g == gi, 2.0 ** (-8.0 * (kv * KV_GROUP + gi + 1) / N_HEADS), col)
    return col


def _sink_col(sink_ref, kv):
    g = lax.broadcasted_iota(jnp.int32, (KV_GROUP * BLOCK, 1), 0) // BLOCK
    col = jnp.zeros((KV_GROUP * BLOCK, 1), F32)
    for gi in range(KV_GROUP):
        h = kv * KV_GROUP + gi
        col = jnp.where(g == gi, sink_ref[0:1, h:h + 1], col)
    return col


def _stack_heads(x, kv):
    return jnp.concatenate([x[:, (kv * KV_GROUP + g) * HEAD_DIM:(kv * KV_GROUP + g + 1) * HEAD_DIM]
                            for g in range(KV_GROUP)], axis=0)


def _attn_scores(q_ref, k_refs, gq_ref, gk_ref, sink_ref, kv, valid, dist):
    qs = _stack_heads(q_ref[...], kv)
    kcat = jnp.concatenate([r[:, kv * HEAD_DIM:(kv + 1) * HEAD_DIM] for r in k_refs], axis=0)
    qn, rq = _head_norm(qs, gq_ref[...])
    kn, rk = _head_norm(kcat, gk_ref[...])
    s = lax.dot_general(qn.astype(BF16), kn.astype(BF16), _DN_NT, preferred_element_type=F32) * _SCALE
    s = jnp.where(valid, s - _slope_col(kv) * dist, NEG_INF)
    return qs, kcat, qn, kn, rq, rk, s, _sink_col(sink_ref, kv)


def _attn_specs():
    kq = ATTN_WIDTH // KV_WIDTH
    q_spec = pl.BlockSpec((BLOCK, ATTN_WIDTH), lambda n: (n, 0))
    kv_specs = []
    for col in (kq, kq + 1):
        kv_specs += [pl.BlockSpec((BLOCK, KV_WIDTH), lambda n, col=col: (0, col)),
                     pl.BlockSpec((BLOCK, KV_WIDTH), lambda n, col=col: (jnp.maximum(n - 1, 0), col)),
                     pl.BlockSpec((BLOCK, KV_WIDTH), lambda n, col=col: (n, col))]
    small = [pl.BlockSpec((1, HEAD_DIM), lambda n: (0, 0)), pl.BlockSpec((1, HEAD_DIM), lambda n: (0, 0)),
             pl.BlockSpec((1, N_HEADS), lambda n: (0, 0))]
    return q_spec, kv_specs, small


def _attn_fwd(proj, gq, gk, sinks, *, name):
    n_rows = proj.shape[0]
    q_spec, kv_specs, small = _attn_specs()

    def body(q_ref, k0, k1, k2, v0, v1, v2, gq_ref, gk_ref, sink_ref, o_ref, lse_ref):
        valid, dist = _attn_geometry(pl.program_id(0))
        o_parts, lse_parts = [], []
        for kv in range(N_KV_HEADS):
            _, _, _, _, _, _, s, sink = _attn_scores(q_ref, (k0, k1, k2), gq_ref, gk_ref, sink_ref, kv, valid, dist)
            vcat = jnp.concatenate([r[:, kv * HEAD_DIM:(kv + 1) * HEAD_DIM] for r in (v0, v1, v2)], axis=0)
            m = jnp.maximum(jnp.max(s, axis=-1, keepdims=True), sink)
            p = jnp.exp(s - m)
            l = jnp.sum(p, axis=-1, keepdims=True) + jnp.exp(sink - m)
            o = jnp.dot(p.astype(BF16), vcat.astype(BF16), preferred_element_type=F32) / l
            lse = m + jnp.log(l)
            o_parts += [o[g * BLOCK:(g + 1) * BLOCK] for g in range(KV_GROUP)]
            lse_parts += [lse[g * BLOCK:(g + 1) * BLOCK] for g in range(KV_GROUP)]
        o_ref[...] = jnp.concatenate(o_parts, axis=1)
        lse_ref[...] = jnp.concatenate(lse_parts, axis=1)

    return pl.pallas_call(
        body, name=name, grid=(n_rows // BLOCK,),
        in_specs=[q_spec] + kv_specs + small,
        out_specs=[pl.BlockSpec((BLOCK, ATTN_WIDTH), lambda n: (n, 0)), pl.BlockSpec((BLOCK, N_HEADS), lambda n: (n, 0))],
        out_shape=[jax.ShapeDtypeStruct((n_rows, ATTN_WIDTH), F32), jax.ShapeDtypeStruct((n_rows, N_HEADS), F32)],
        compiler_params=_cparams(("parallel",)),
    )(proj, proj, proj, proj, proj, proj, proj, gq, gk, sinks)


def _attn_bwd(proj, gq, gk, sinks, o, lse, do, *, name):
    n_rows = proj.shape[0]
    q_spec, kv_specs, small = _attn_specs()

    def body(q_ref, k0, k1, k2, v0, v1, v2, gq_ref, gk_ref, sink_ref, o_ref, lse_ref, do_ref,
             dq_ref, dkb_ref, dvb_ref, dgq_ref, dgk_ref, dsink_ref, dk_ref, dv_ref):
        n = pl.program_id(0)

        @pl.when(n == 0)
        def _():
            dk_ref[...] = jnp.zeros_like(dk_ref)
            dv_ref[...] = jnp.zeros_like(dv_ref)
            dgq_ref[...] = jnp.zeros_like(dgq_ref)
            dgk_ref[...] = jnp.zeros_like(dgk_ref)
            dsink_ref[...] = jnp.zeros_like(dsink_ref)

        valid, dist = _attn_geometry(n)
        dq_parts, dk_parts, dv_parts, dsink_parts = [], [], [], []
        dgq = jnp.zeros((1, HEAD_DIM), F32)
        dgk = jnp.zeros((1, HEAD_DIM), F32)
        for kv in range(N_KV_HEADS):
            qs, kcat, qn, kn, rq, rk, s, sink = _attn_scores(q_ref, (k0, k1, k2), gq_ref, gk_ref, sink_ref, kv, valid, dist)
            vcat = jnp.concatenate([r[:, kv * HEAD_DIM:(kv + 1) * HEAD_DIM] for r in (v0, v1, v2)], axis=0)
            os_ = _stack_heads(o_ref[...], kv)
            dos = _stack_heads(do_ref[...], kv)
            lse = jnp.concatenate([lse_ref[:, kv * KV_GROUP + g:kv * KV_GROUP + g + 1] for g in range(KV_GROUP)], axis=0)
            p = jnp.exp(s - lse)
            delta = jnp.sum(dos * os_, axis=-1, keepdims=True)
            dp = lax.dot_general(dos.astype(BF16), vcat.astype(BF16), _DN_NT, preferred_element_type=F32)
            ds = (p * (dp - delta)) * _SCALE
            dsink_rows = -jnp.exp(sink - lse) * delta
            dsink_parts += [jnp.sum(dsink_rows[g * BLOCK:(g + 1) * BLOCK], axis=0, keepdims=True) for g in range(KV_GROUP)]
            dv_parts.append(lax.dot_general(p.astype(BF16), dos.astype(BF16), _DN_TN, preferred_element_type=F32))
            dsb = ds.astype(BF16)
            dqn = jnp.dot(dsb, kn.astype(BF16), preferred_element_type=F32)
            dkn = lax.dot_general(dsb, qn.astype(BF16), _DN_TN, preferred_element_type=F32)
            qh = qs * rq
            gd = dqn * gq_ref[...]
            dqs = rq * (gd - qh * jnp.mean(gd * qh, axis=-1, keepdims=True))
            dgq = dgq + jnp.sum(dqn * qh, axis=0, keepdims=True)
            kh = kcat * rk
            gdk = dkn * gk_ref[...]
            dk_parts.append(rk * (gdk - kh * jnp.mean(gdk * kh, axis=-1, keepdims=True)))
            dgk = dgk + jnp.sum(dkn * kh, axis=0, keepdims=True)
            dq_parts += [dqs[g * BLOCK:(g + 1) * BLOCK] for g in range(KV_GROUP)]
        dq_ref[...] = jnp.concatenate(dq_parts, axis=1).astype(BF16)
        dkc = jnp.concatenate(dk_parts, axis=1)
        dvc = jnp.concatenate(dv_parts, axis=1)
        prev = pl.multiple_of(jnp.maximum(n - 1, 0) * BLOCK, BLOCK)
        cur = pl.multiple_of(n * BLOCK, BLOCK)
        for acc_ref, val in ((dk_ref, dkc), (dv_ref, dvc)):
            acc_ref[0:BLOCK, :] += val[0:BLOCK]
            acc_ref[pl.ds(prev, BLOCK), :] += val[BLOCK:2 * BLOCK]
            acc_ref[pl.ds(cur, BLOCK), :] += val[2 * BLOCK:3 * BLOCK]
        dgq_ref[...] += dgq
        dgk_ref[...] += dgk
        dsink_ref[...] += jnp.concatenate(dsink_parts, axis=1)

        @pl.when(n == pl.num_programs(0) - 1)
        def _():
            dkb_ref[...] = dk_ref[...].astype(BF16)
            dvb_ref[...] = dv_ref[...].astype(BF16)

    blk = lambda w: pl.BlockSpec((BLOCK, w), lambda n: (n, 0))
    full = lambda r, w: pl.BlockSpec((r, w), lambda n: (0, 0))
    return pl.pallas_call(
        body, name=name, grid=(n_rows // BLOCK,),
        in_specs=[q_spec] + kv_specs + small + [blk(ATTN_WIDTH), blk(N_HEADS), blk(ATTN_WIDTH)],
        out_specs=[blk(ATTN_WIDTH), full(n_rows, KV_WIDTH), full(n_rows, KV_WIDTH),
                   full(1, HEAD_DIM), full(1, HEAD_DIM), full(1, N_HEADS)],
        out_shape=[jax.ShapeDtypeStruct((n_rows, ATTN_WIDTH), BF16), jax.ShapeDtypeStruct((n_rows, KV_WIDTH), BF16),
                   jax.ShapeDtypeStruct((n_rows, KV_WIDTH), BF16), jax.ShapeDtypeStruct((1, HEAD_DIM), F32),
                   jax.ShapeDtypeStruct((1, HEAD_DIM), F32), jax.ShapeDtypeStruct((1, N_HEADS), F32)],
        scratch_shapes=[pltpu.VMEM((n_rows, KV_WIDTH), F32), pltpu.VMEM((n_rows, KV_WIDTH), F32)],
        compiler_params=_cparams(("arbitrary",)),
    )(proj, proj, proj, proj, proj, proj, proj, gq, gk, sinks, o, lse, do)


SSM_LAGS = 8
SLAB_G = 128 // SSM_GROUP_CH
N_SLABS = SSM_GROUPS // SLAB_G
SLAB_STATE = SLAB_G * SSM_STATE
U_COL = (ATTN_WIDTH + 2 * KV_WIDTH) // 128


def _ssm_prep(lam_re, lam_im, log_step, b_re, b_im, c_re, c_im):
    lam = lax.complex(lam_re, lam_im)
    delta = jnp.exp(log_step)[:, None]
    lam_bar = jnp.exp(lam * delta)
    b_t = lax.complex(jnp.swapaxes(b_re, 1, 2), jnp.swapaxes(b_im, 1, 2))
    b_bar = ((lam_bar - 1.0) / lam)[:, None, :] * b_t
    pw = [jnp.ones_like(lam_bar)]
    for _ in range(SSM_LAGS):
        pw.append(pw[-1] * lam_bar)
    w = jnp.stack(pw[:SSM_LAGS])[:, :, None, :] * b_bar[None]
    wri = jnp.stack([jnp.real(w), jnp.imag(w)], axis=3)
    wc = wri.reshape(SSM_LAGS, N_SLABS, SLAB_G * SSM_GROUP_CH, 2 * SSM_STATE)
    wc = jnp.swapaxes(wc, 0, 1).reshape(N_SLABS, SSM_LAGS * 128, 2 * SSM_STATE)
    cri = jnp.stack([c_re, -c_im], axis=2).reshape(N_SLABS, SLAB_G, SSM_GROUP_CH, 2, 1, SSM_STATE)
    eye = jnp.eye(SLAB_G, dtype=F32).reshape(1, SLAB_G, 1, 1, SLAB_G, 1)
    ct = (cri * eye).reshape(N_SLABS, 128, 2 * SLAB_STATE)
    l8 = pw[SSM_LAGS]
    lam8 = jnp.concatenate([jnp.real(l8).reshape(N_SLABS, 1, SLAB_STATE), jnp.imag(l8).reshape(N_SLABS, 1, SLAB_STATE)], axis=2)
    return wc, ct, lam8


def _row_group():
    return (lax.broadcasted_iota(jnp.int32, (SSM_LAGS * 128, 1), 0) // SSM_GROUP_CH) % SLAB_G


def _spread_groups(wc):
    g_of_row = _row_group()
    return jnp.concatenate([jnp.where(g_of_row == g, wc[:, r * SSM_STATE:(r + 1) * SSM_STATE], 0.0)
                            for r in range(2) for g in range(SLAB_G)], axis=1)


def _gather_groups(dw):
    g_of_row = _row_group()
    parts = []
    for r in range(2):
        acc = jnp.zeros((SSM_LAGS * 128, SSM_STATE), F32)
        for g in range(SLAB_G):
            c0 = r * SLAB_STATE + g * SSM_STATE
            acc = acc + jnp.where(g_of_row == g, dw[:, c0:c0 + SSM_STATE], 0.0)
        parts.append(acc)
    return jnp.concatenate(parts, axis=1)


def _lagged(u, up, t_rows):
    ue = jnp.concatenate([up, u], axis=0)
    return jnp.concatenate([ue[SSM_LAGS - tau:SSM_LAGS - tau + t_rows] for tau in range(SSM_LAGS)], axis=1).astype(BF16)


def _ssm_fwd(proj, wc, cmat, lam8, dvec, *, name):
    n_rows = proj.shape[0]
    tt = _row_tile(n_rows, 1408)
    n_t = n_rows // tt
    sw = 2 * SLAB_STATE
    hs = SLAB_STATE

    def body(u_ref, up_ref, wc_ref, c_ref, l_ref, d_ref, y_ref, x_ref, carry_ref, w_ref):
        t = pl.program_id(1)

        @pl.when(t == 0)
        def _():
            carry_ref[...] = jnp.zeros_like(carry_ref)
            w_ref[...] = _spread_groups(wc_ref[...]).astype(BF16)

        u = u_ref[...]
        up = jnp.where(t > 0, up_ref[...], 0.0)
        x_ref[...] = jnp.dot(_lagged(u, up, tt), w_ref[...], preferred_element_type=F32)
        ar = jnp.broadcast_to(l_ref[:, :hs], (8, hs))
        ai = jnp.broadcast_to(l_ref[:, hs:], (8, hs))

        def step(b, c):
            xr, xi = c
            r0 = pl.multiple_of(b * 8, 8)
            w = x_ref[pl.ds(r0, 8), :]
            nr = w[:, :hs] + ar * xr - ai * xi
            ni = w[:, hs:] + ar * xi + ai * xr
            x_ref[pl.ds(r0, 8), :] = jnp.concatenate([nr, ni], axis=1)
            return nr, ni

        xr, xi = lax.fori_loop(0, tt // 8, step, (carry_ref[:, :hs], carry_ref[:, hs:]), unroll=8)
        carry_ref[...] = jnp.concatenate([xr, xi], axis=1)
        y_ref[...] = lax.dot_general(x_ref[...].astype(BF16), c_ref[...], _DN_NT, preferred_element_type=F32) + d_ref[...] * u

    return pl.pallas_call(
        body, name=name, grid=(N_SLABS, n_t),
        in_specs=[pl.BlockSpec((tt, 128), lambda j, t: (t, U_COL + j)),
                  pl.BlockSpec((8, 128), lambda j, t: (jnp.maximum(t * (tt // 8) - 1, 0), U_COL + j)),
                  pl.BlockSpec((None, SSM_LAGS * 128, 2 * SSM_STATE), lambda j, t: (j, 0, 0)),
                  pl.BlockSpec((None, 128, sw), lambda j, t: (j, 0, 0)),
                  pl.BlockSpec((None, 1, sw), lambda j, t: (j, 0, 0)),
                  pl.BlockSpec((1, 128), lambda j, t: (0, j))],
        out_specs=[pl.BlockSpec((tt, 128), lambda j, t: (t, j)), pl.BlockSpec((tt, sw), lambda j, t: (t, j))],
        out_shape=[jax.ShapeDtypeStruct((n_rows, SSM_WIDTH), F32), jax.ShapeDtypeStruct((n_rows, N_SLABS * sw), F32)],
        scratch_shapes=[pltpu.VMEM((8, sw), F32), pltpu.VMEM((SSM_LAGS * 128, sw), BF16)],
        compiler_params=_cparams(("parallel", "arbitrary")),
    )(proj, proj, wc, cmat, lam8, dvec)


def _ssm_bwd(proj, xs, dy, wc, cmat, lam8, dvec, *, name, deps=()):
    n_rows = proj.shape[0]
    tt = _row_tile(n_rows, 704)
    n_t = n_rows // tt
    sw = 2 * SLAB_STATE
    hs = SLAB_STATE

    def body(u_ref, up_ref, x_ref, xp_ref, dy_ref, wc_ref, c_ref, l_ref, d_ref,
             *rest):
        du_ref, dwc_ref, dc_ref, dl_ref, dd_ref, a_ref, carry_ref, head_ref, w_ref, dw_ref = rest[len(deps):]
        t = pl.program_id(1)
        ti = n_t - 1 - t

        @pl.when(t == 0)
        def _():
            w_ref[...] = _spread_groups(wc_ref[...]).astype(BF16)
            carry_ref[...] = jnp.zeros_like(carry_ref)
            head_ref[...] = jnp.zeros_like(head_ref)
            dw_ref[...] = jnp.zeros_like(dw_ref)
            dc_ref[...] = jnp.zeros_like(dc_ref)
            dl_ref[...] = jnp.zeros_like(dl_ref)
            dd_ref[...] = jnp.zeros_like(dd_ref)

        u = u_ref[...]
        up = jnp.where(ti > 0, up_ref[...], 0.0)
        ucat = _lagged(u, up, tt)
        dyv = dy_ref[...]
        dyb = dyv.astype(BF16)
        a_ref[...] = jnp.dot(dyb, c_ref[...], preferred_element_type=F32)
        lr = jnp.broadcast_to(l_ref[:, :hs], (8, hs))
        li = jnp.broadcast_to(l_ref[:, hs:], (8, hs))

        def step(i, c):
            cr, ci = c
            r0 = pl.multiple_of((tt // 8 - 1 - i) * 8, 8)
            g = a_ref[pl.ds(r0, 8), :]
            nr = g[:, :hs] + lr * cr + li * ci
            ni = g[:, hs:] + lr * ci - li * cr
            a_ref[pl.ds(r0, 8), :] = jnp.concatenate([nr, ni], axis=1)
            return nr, ni

        cr, ci = lax.fori_loop(0, tt // 8, step, (carry_ref[:, :hs], carry_ref[:, hs:]), unroll=8)
        carry_ref[...] = jnp.concatenate([cr, ci], axis=1)

        a = a_ref[...]
        xv = x_ref[...]
        xprev = jnp.where(ti > 0, xp_ref[...], 0.0)
        xsh = jnp.concatenate([xprev, xv[:tt - SSM_LAGS]], axis=0)
        a_re, a_im, x_re, x_im = a[:, :hs], a[:, hs:], xsh[:, :hs], xsh[:, hs:]
        dl_ref[...] += jnp.concatenate([jnp.sum(a_re * x_re + a_im * x_im, axis=0, keepdims=True),
                                        jnp.sum(a_im * x_re - a_re * x_im, axis=0, keepdims=True)], axis=1)
        ab = a.astype(BF16)
        dw_ref[...] += lax.dot_general(ucat, ab, _DN_TN, preferred_element_type=F32)
        duc = lax.dot_general(ab, w_ref[...], _DN_NT, preferred_element_type=F32)
        ext = jnp.concatenate([duc, head_ref[...]], axis=0)
        du = d_ref[...] * dyv
        for tau in range(SSM_LAGS):
            du = du + ext[tau:tau + tt, tau * 128:(tau + 1) * 128]
        head_ref[...] = duc[0:8]
        row = ti * tt + lax.broadcasted_iota(jnp.int32, (tt, 128), 0)
        du_ref[...] = jnp.where(row >= PAD, du, 0.0).astype(BF16)
        dd_ref[...] += jnp.sum(dyv * u, axis=0, keepdims=True)
        dc_ref[...] += lax.dot_general(dyb, xv.astype(BF16), _DN_TN, preferred_element_type=F32)

        @pl.when(t == n_t - 1)
        def _():
            dwc_ref[...] = _gather_groups(dw_ref[...])

    rt = lambda t: n_t - 1 - t
    prev8 = lambda t: jnp.maximum(rt(t) * (tt // 8) - 1, 0)
    return pl.pallas_call(
        body, name=name, grid=(N_SLABS, n_t),
        in_specs=[pl.BlockSpec((tt, 128), lambda j, t: (rt(t), U_COL + j)),
                  pl.BlockSpec((8, 128), lambda j, t: (prev8(t), U_COL + j)),
                  pl.BlockSpec((tt, sw), lambda j, t: (rt(t), j)),
                  pl.BlockSpec((8, sw), lambda j, t: (prev8(t), j)),
                  pl.BlockSpec((tt, 128), lambda j, t: (rt(t), j)),
                  pl.BlockSpec((None, SSM_LAGS * 128, 2 * SSM_STATE), lambda j, t: (j, 0, 0)),
                  pl.BlockSpec((None, 128, sw), lambda j, t: (j, 0, 0)),
                  pl.BlockSpec((None, 1, sw), lambda j, t: (j, 0, 0)),
                  pl.BlockSpec((1, 128), lambda j, t: (0, j))] + [_ANY] * len(deps),
        out_specs=[pl.BlockSpec((tt, 128), lambda j, t: (rt(t), j)),
                   pl.BlockSpec((None, SSM_LAGS * 128, 2 * SSM_STATE), lambda j, t: (j, 0, 0)),
                   pl.BlockSpec((None, 128, sw), lambda j, t: (j, 0, 0)),
                   pl.BlockSpec((None, 1, sw), lambda j, t: (j, 0, 0)),
                   pl.BlockSpec((1, 128), lambda j, t: (0, j))],
        out_shape=[jax.ShapeDtypeStruct((n_rows, SSM_WIDTH), BF16),
                   jax.ShapeDtypeStruct((N_SLABS, SSM_LAGS * 128, 2 * SSM_STATE), F32),
                   jax.ShapeDtypeStruct((N_SLABS, 128, sw), F32),
                   jax.ShapeDtypeStruct((N_SLABS, 1, sw), F32),
                   jax.ShapeDtypeStruct((1, SSM_WIDTH), F32)],
        scratch_shapes=[pltpu.VMEM((tt, sw), F32), pltpu.VMEM((8, sw), F32), pltpu.VMEM((8, sw), F32),
                        pltpu.VMEM((SSM_LAGS * 128, sw), BF16), pltpu.VMEM((SSM_LAGS * 128, sw), F32)],
        compiler_params=_cparams(("parallel", "arbitrary")),
    )(proj, proj, xs, xs, dy, wc, cmat, lam8, dvec, *deps)


_GELU_C = math.sqrt(2.0 / math.pi)
_GELU_A = 0.044715


def _gelu(y):
    th = jnp.tanh(_GELU_C * (y + _GELU_A * y * y * y))
    return 0.5 * y * (1.0 + th), th


def _glu_fwd(y, w, b, *, name):
    n_rows, width = y.shape
    tr = _row_tile(n_rows, 384)

    def body(y_ref, w_ref, b_ref, o_ref):
        g, _ = _gelu(y_ref[...])
        z = jnp.dot(g.astype(BF16), w_ref[...], preferred_element_type=F32) + b_ref[...]
        o_ref[...] = g * jax.nn.sigmoid(z)

    return pl.pallas_call(
        body, name=name, grid=(n_rows // tr,),
        in_specs=[pl.BlockSpec((tr, width), lambda i: (i, 0)), pl.BlockSpec((width, width), lambda i: (0, 0)),
                  pl.BlockSpec((1, width), lambda i: (0, 0))],
        out_specs=pl.BlockSpec((tr, width), lambda i: (i, 0)),
        out_shape=jax.ShapeDtypeStruct((n_rows, width), F32),
        compiler_params=_cparams(("parallel",)),
    )(y, w, b)


def _glu_bwd(y, w, b, dout, *, name):
    n_rows, width = y.shape
    tr = _row_tile(n_rows, 384)

    def body(y_ref, w_ref, b_ref, do_ref, dy_ref, g_ref, dz_ref, db_ref):
        i = pl.program_id(0)
        yv = y_ref[...]
        g, th = _gelu(yv)
        gb = g.astype(BF16)
        z = jnp.dot(gb, w_ref[...], preferred_element_type=F32) + b_ref[...]
        sg = jax.nn.sigmoid(z)
        do = do_ref[...]
        dz = do * g * sg * (1.0 - sg)
        dzb = dz.astype(BF16)
        dg = do * sg + lax.dot_general(dzb, w_ref[...], _DN_NT, preferred_element_type=F32)
        dgelu = 0.5 * (1.0 + th) + 0.5 * yv * (1.0 - th * th) * _GELU_C * (1.0 + 3.0 * _GELU_A * yv * yv)
        dy_ref[...] = dg * dgelu
        g_ref[...] = gb
        dz_ref[...] = dzb
        part = jnp.sum(dz, axis=0, keepdims=True)

        @pl.when(i == 0)
        def _():
            db_ref[...] = part

        @pl.when(i > 0)
        def _():
            db_ref[...] += part

    row = pl.BlockSpec((tr, width), lambda i: (i, 0))
    vec = pl.BlockSpec((1, width), lambda i: (0, 0))
    return pl.pallas_call(
        body, name=name, grid=(n_rows // tr,),
        in_specs=[row, pl.BlockSpec((width, width), lambda i: (0, 0)), vec, row],
        out_specs=[row, row, row, vec],
        out_shape=[jax.ShapeDtypeStruct((n_rows, width), F32), jax.ShapeDtypeStruct((n_rows, width), BF16),
                   jax.ShapeDtypeStruct((n_rows, width), BF16), jax.ShapeDtypeStruct((1, width), F32)],
        compiler_params=_cparams(("arbitrary",)),
    )(y, w, b, dout)


def _loss_head(h, target, *, name):
    n_rows, width = h.shape

    def body(h_ref, t_ref, dh_ref, dhb_ref, loss_ref):
        i = pl.program_id(0)

        @pl.when(i == 0)
        def _():
            dh_ref[...] = jnp.zeros_like(dh_ref)
            dhb_ref[...] = jnp.zeros_like(dhb_ref)
            loss_ref[...] = jnp.zeros_like(loss_ref)

        @pl.when(i > 0)
        def _():
            err = h_ref[...] - t_ref[...]
            dh = err * (1.0 / width)
            dh_ref[...] = dh
            dhb_ref[...] = dh.astype(BF16)
            loss_ref[...] += (0.5 / width) * jnp.sum(err * err, keepdims=True)

    return pl.pallas_call(
        body, name=name, grid=(n_rows // BLOCK,),
        in_specs=[pl.BlockSpec((BLOCK, width), lambda i: (i, 0)),
                  pl.BlockSpec((BLOCK, width), lambda i: (jnp.maximum(i - 1, 0), 0))],
        out_specs=[pl.BlockSpec((BLOCK, width), lambda i: (i, 0)), pl.BlockSpec((BLOCK, width), lambda i: (i, 0)),
                   pl.BlockSpec((1, 1), lambda i: (0, 0))],
        out_shape=[jax.ShapeDtypeStruct((n_rows, width), F32), jax.ShapeDtypeStruct((n_rows, width), BF16),
                   jax.ShapeDtypeStruct((1, 1), F32)],
        compiler_params=_cparams(("arbitrary",)),
    )(h, target)


def _elem_rows(n_rows, n_cols, bytes_per_row_elem):
    lanes = -(-n_cols // 128) * 128
    cap = max(16, (12 * 1024 * 1024) // (lanes * bytes_per_row_elem))
    best = None
    for t in range(16, min(n_rows, cap) + 1, 16):
        if n_rows % t == 0:
            best = t
    return best or n_rows


def _cast_bf16(x, *, name):
    n_rows, n_cols = x.shape
    tr = _elem_rows(n_rows, n_cols, 4)

    def body(x_ref, o_ref):
        o_ref[...] = x_ref[...].astype(BF16)

    spec = pl.BlockSpec((tr, n_cols), lambda i: (i, 0))
    return pl.pallas_call(body, name=name, grid=(n_rows // tr,), in_specs=[spec], out_specs=spec,
                          out_shape=jax.ShapeDtypeStruct(x.shape, BF16), compiler_params=_cparams(("parallel",)))(x)


def _adamw(w, m, v, parts, *, name):
    n_rows, n_cols = w.shape
    n_parts = parts.shape[0]
    tr = _elem_rows(n_rows, n_cols, 4 * (8 + n_parts))
    c1 = 1.0 / (1.0 - ADAM_B1 ** ADAM_STEP)
    c2 = 1.0 / (1.0 - ADAM_B2 ** ADAM_STEP)

    def body(w_ref, m_ref, v_ref, p_ref, g_ref, d_ref, nm_ref, nv_ref):
        g = p_ref[0].astype(F32)
        for k in range(1, n_parts):
            g = g + p_ref[k].astype(F32)
        nm = ADAM_B1 * m_ref[...] + (1.0 - ADAM_B1) * g
        nv = ADAM_B2 * v_ref[...] + (1.0 - ADAM_B2) * (g * g)
        g_ref[...] = g
        nm_ref[...] = nm
        nv_ref[...] = nv
        d_ref[...] = -ADAM_LR * ((nm * c1) / (jnp.sqrt(nv * c2) + ADAM_EPS) + ADAM_WD * w_ref[...])

    spec = pl.BlockSpec((tr, n_cols), lambda i: (i, 0))
    return pl.pallas_call(
        body, name=name, grid=(n_rows // tr,),
        in_specs=[spec, spec, spec, pl.BlockSpec((n_parts, tr, n_cols), lambda i: (0, i, 0))],
        out_specs=[spec] * 4, out_shape=[jax.ShapeDtypeStruct(w.shape, F32)] * 4,
        compiler_params=_cparams(("parallel",)),
    )(w, m, v, parts)


def _sum_parts(parts, *, name):
    n_parts, n_rows, n_cols = parts.shape
    tr = _elem_rows(n_rows, n_cols, 4 * (1 + n_parts))

    def body(p_ref, o_ref):
        g = p_ref[0].astype(F32)
        for k in range(1, n_parts):
            g = g + p_ref[k].astype(F32)
        o_ref[...] = g

    return pl.pallas_call(
        body, name=name, grid=(n_rows // tr,),
        in_specs=[pl.BlockSpec((n_parts, tr, n_cols), lambda i: (0, i, 0))],
        out_specs=pl.BlockSpec((tr, n_cols), lambda i: (i, 0)),
        out_shape=jax.ShapeDtypeStruct((n_rows, n_cols), F32), compiler_params=_cparams(("parallel",)),
    )(parts)


BIG = ("w_in", "w_glu", "w_out", "w_up", "w_down")
SMALL = ("norm_mix_g", "q_norm_g", "k_norm_g", "attn_sinks", "ssm_lambda_re", "ssm_lambda_im", "ssm_log_step",
         "ssm_b_re", "ssm_b_im", "ssm_c_re", "ssm_c_im", "ssm_d", "b_glu", "attn_out_g", "ssm_out_g", "norm_mlp_g")
_SSM_NAMES = ("ssm_lambda_re", "ssm_lambda_im", "ssm_log_step", "ssm_b_re", "ssm_b_im", "ssm_c_re", "ssm_c_im")


def _divisor(n, cands):
    for c in cands:
        if n % c == 0:
            return c
    return n


def _mm(a, b, mode, name, outs=(F32,), epilogue=_ident, tiles=(), deps=(), blocked=False):
    if mode == "nn":
        m, k = a.shape
        n = b.shape[0] * b.shape[2] if blocked else b.shape[1]
    elif mode == "nt":
        m, k = a.shape
        n = b.shape[1] if blocked else b.shape[0]
    else:
        (k, m), n = a.shape, b.shape[1]
    if mode == "tn":
        tm, tn, tk = _divisor(m, (1024, 512)), _divisor(n, (1024, 512)), k
    elif k <= 2560:
        tm, tn, tk = _row_tile(m, 1408), _divisor(n, (1024, 1280, 512)), k
    elif blocked:
        tm, tn, tk = _row_tile(m, 1408), _divisor(n, (1024, 512)), b.shape[2]
    else:
        tm, tn, tk = _row_tile(m, 1408), _divisor(n, (1024, 512)), _divisor(k, (1024, 512))
    return _matmul(a, b, mode=mode, tm=tm, tn=tn, tk=tk, outs=list(outs), epilogue=epilogue, tiles=tiles, deps=deps,
                   blocked=blocked, name=name)


def _add_tile(acc, res):
    return (acc + res,)


def _relu_sq(acc):
    r = jnp.maximum(acc, 0.0)
    return r, r * r


def _relu_sq_bwd(acc, r):
    return (acc * (2.0 * r.astype(F32)),)


def _row(v):
    return v.reshape(1, -1)


def _layer_fwd(hres, fetch, sp, l, deps=()):
    tag = f"_l{l}"
    wts = {}
    hb = _rmsnorm_fwd([hres], [_row(sp["norm_mix_g"])], name="norm_mix" + tag, deps=deps)
    wts["w_in"] = fetch("w_in", hb)
    proj, = _mm(hb, wts["w_in"], "nt", "proj" + tag)
    gq, gk, sinks = _row(sp["q_norm_g"]), _row(sp["k_norm_g"]), _row(sp["attn_sinks"])
    o, lse = _attn_fwd(proj, gq, gk, sinks, name="attn_fwd" + tag)
    (wc, cmat, lam8), prep_vjp = jax.vjp(_ssm_prep, *[sp[n] for n in _SSM_NAMES])
    cmat = cmat.astype(BF16)
    y, xs = _ssm_fwd(proj, wc, cmat, lam8, _row(sp["ssm_d"]), name="ssm_fwd" + tag)
    wts["w_glu"] = fetch("w_glu", y)
    s = _glu_fwd(y, wts["w_glu"], _row(sp["b_glu"]), name="glu_fwd" + tag)
    mix = _rmsnorm_fwd([o, s], [_row(sp["attn_out_g"]), _row(sp["ssm_out_g"])], name="norm_out" + tag)
    wts["w_out"] = fetch("w_out", mix)
    hres2, = _mm(mix, wts["w_out"], "nn", "out_proj" + tag, epilogue=_add_tile, tiles=(hres,))
    h2 = _rmsnorm_fwd([hres2], [_row(sp["norm_mlp_g"])], name="norm_mlp" + tag)
    wts["w_up"] = fetch("w_up", h2)
    r, act = _mm(h2, wts["w_up"], "nn", "mlp_up" + tag, outs=(BF16, BF16), epilogue=_relu_sq, blocked=True)
    wts["w_down"] = fetch("w_down", act)
    hres3, = _mm(act, wts["w_down"], "nn", "mlp_down" + tag, epilogue=_add_tile, tiles=(hres2,))
    saved = dict(wts=wts, hres=hres, hb=hb, proj=proj, o=o, lse=lse, wc=wc, cmat=cmat, lam8=lam8, prep_vjp=prep_vjp,
                 y=y, xs=xs, s=s, mix=mix, hres2=hres2, h2=h2, r=r, act=act)
    return hres3, saved


def _layer_bwd(dres, dres_b, sp, sv, l, early_grads, deps=()):
    tag = f"_l{l}"
    wts = sv["wts"]
    gb, gs = {}, {}
    d_up, = _mm(dres_b, wts["w_down"], "nt", "mlp_down_dx" + tag, outs=(BF16,), epilogue=_relu_sq_bwd, tiles=(sv["r"],),
                deps=deps)
    gb["w_down"], = _mm(sv["act"], dres_b, "tn", "mlp_down_dw" + tag, outs=(BF16,))
    gb["w_up"], = _mm(sv["h2"], d_up, "tn", "mlp_up_dw" + tag, outs=(BF16,), blocked=True)
    deps = early_grads(l, "a", {n: gb.pop(n) for n in ("w_up", "w_down")})
    dh2, = _mm(d_up, wts["w_up"], "nt", "mlp_up_dx" + tag, blocked=True, deps=deps)
    (dres2,), (dg,), dres2_b = _rmsnorm_bwd([sv["hres2"]], [_row(sp["norm_mlp_g"])], dh2, dres, name="norm_mlp_bwd" + tag)
    gs["norm_mlp_g"] = dg
    dmix, = _mm(dres2_b, wts["w_out"], "nt", "out_proj_dx" + tag)
    gb["w_out"], = _mm(sv["mix"], dres2_b, "tn", "out_proj_dw" + tag, outs=(BF16,))
    (do, ds), (dga, dgs) = _rmsnorm_bwd([sv["o"], sv["s"]], [_row(sp["attn_out_g"]), _row(sp["ssm_out_g"])], dmix, None,
                                        name="norm_out_bwd" + tag)
    gs["attn_out_g"], gs["ssm_out_g"] = dga, dgs
    dy, g_b, dz_b, db = _glu_bwd(sv["y"], wts["w_glu"], _row(sp["b_glu"]), ds, name="glu_bwd" + tag)
    gs["b_glu"] = db
    gb["w_glu"], = _mm(g_b, dz_b, "tn", "glu_dw" + tag, outs=(BF16,))
    deps = early_grads(l, "b", {n: gb.pop(n) for n in ("w_out", "w_glu")})
    du, dwc, dcmat, dlam8, dd = _ssm_bwd(sv["proj"], sv["xs"], dy, sv["wc"], sv["cmat"], sv["lam8"], _row(sp["ssm_d"]),
                                         name="ssm_bwd" + tag, deps=deps)
    gs["ssm_d"] = dd
    for n, g in zip(_SSM_NAMES, sv["prep_vjp"]((dwc, dcmat, dlam8))):
        gs[n] = g
    dq, dk, dv, dgq, dgk, dsinks = _attn_bwd(sv["proj"], _row(sp["q_norm_g"]), _row(sp["k_norm_g"]), _row(sp["attn_sinks"]),
                                             sv["o"], sv["lse"], do, name="attn_bwd" + tag)
    gs["q_norm_g"], gs["k_norm_g"], gs["attn_sinks"] = dgq, dgk, dsinks
    dproj = _concat_cols([dq, dk, dv, du], name="dproj" + tag)
    gb["w_in"], = _mm(dproj, sv["hb"], "tn", "proj_dw" + tag, outs=(BF16,))
    deps = early_grads(l, "c", {"w_in": gb.pop("w_in")})
    dh, = _mm(dproj, wts["w_in"], "nn", "proj_dx" + tag, deps=deps)
    (dres_in,), (dg,), dres_in_b = _rmsnorm_bwd([sv["hres"]], [_row(sp["norm_mix_g"])], dh, dres2, name="norm_mix_bwd" + tag)
    gs["norm_mix_g"] = dg
    return dres_in, dres_in_b, gs


def _local_step(x, target, meta, sp, weights_for_layer, early_grads, grads_of_layer):
    h = jnp.concatenate([jnp.zeros((PAD, x.shape[1]), F32), meta, x], axis=0)
    saved = []
    for l in range(DEPTH):
        fetch, deps = weights_for_layer(l, h)
        h, sv = _layer_fwd(h, fetch, {n: sp[n][l] for n in SMALL}, l, deps)
        saved.append(sv)
    dh, dh_b, loss = _loss_head(h, target, name="loss_head")
    gsmall = {n: [None] * DEPTH for n in SMALL}
    deps = ()
    for l in reversed(range(DEPTH)):
        dh, dh_b, gs = _layer_bwd(dh, dh_b, {n: sp[n][l] for n in SMALL}, saved[l], l, early_grads, deps)
        deps = grads_of_layer(l, dh)
        for n in SMALL:
            gsmall[n][l] = gs[n].reshape(sp[n][l].shape)
    return loss, dh, gsmall


def _all_gather(x, *, name):
    def body(x_ref, out_ref, send_sems, recv_sems, local_sem):
        x, y, c = lax.axis_index("x"), lax.axis_index("y"), lax.axis_index("c")
        me, sibling = (x, y, c), (x, y, 1 - c)
        chips = [(1 - x, y), (x, 1 - y), (1 - x, 1 - y)]

        def slot(px, py, pc):
            return out_ref.at[4 * px + 2 * py + pc]

        def copy(k, block, to, src=None):
            return pltpu.make_async_remote_copy(
                src_ref=slot(*block) if src is None else src, dst_ref=slot(*block),
                send_sem=send_sems.at[k], recv_sem=recv_sems.at[k], device_id=to, device_id_type=_MESH)

        mine = pltpu.make_async_copy(x_ref, slot(*me), local_sem)
        mine.start()
        first = [copy(0, me, sibling, src=x_ref)]
        first += [copy(1 + j, me, (*chip, c), src=x_ref) for j, chip in enumerate(chips)]
        for cp in first:
            cp.start()
        passed = [copy(4 + j, (*chip, c), sibling) for j, chip in enumerate(chips)]
        for j, chip in enumerate(chips):
            copy(1 + j, (*chip, c), me).wait_recv()
            passed[j].start()
        copy(0, sibling, me).wait_recv()
        for j, chip in enumerate(chips):
            copy(4 + j, (*chip, 1 - c), me).wait_recv()
        for cp in first + passed:
            cp.wait_send()
        mine.wait()

    return pl.pallas_call(
        body, name=name, out_shape=jax.ShapeDtypeStruct((N_DEV,) + x.shape, x.dtype),
        in_specs=[_ANY], out_specs=_ANY,
        scratch_shapes=[pltpu.SemaphoreType.DMA((7,)), pltpu.SemaphoreType.DMA((7,)), pltpu.SemaphoreType.DMA],
    )(x)


def _exchange(g, *, name):
    def body(g_ref, r_ref, send_sems, recv_sems, local_sem):
        x, y, c = lax.axis_index("x"), lax.axis_index("y"), lax.axis_index("c")
        me = 4 * x + 2 * y + c
        mine = pltpu.make_async_copy(g_ref.at[me], r_ref.at[me], local_sem)
        mine.start()

        def peer(k):
            px, py, pc = (x + (k >> 2)) % 2, (y + ((k >> 1) & 1)) % 2, (c + (k & 1)) % 2
            return (px, py, pc), 4 * px + 2 * py + pc

        def copy(k, src_block, dst_block):
            to, _ = peer(k)
            return pltpu.make_async_remote_copy(
                src_ref=g_ref.at[src_block], dst_ref=r_ref.at[dst_block],
                send_sem=send_sems.at[k - 1], recv_sem=recv_sems.at[k - 1], device_id=to, device_id_type=_MESH)

        sends = [copy(k, peer(k)[1], me) for k in range(1, N_DEV)]
        for cp in sends:
            cp.start()
        for k in range(1, N_DEV):
            copy(k, me, peer(k)[1]).wait_recv()
        for cp in sends:
            cp.wait_send()
        mine.wait()

    return pl.pallas_call(
        body, name=name, out_shape=jax.ShapeDtypeStruct(g.shape, g.dtype),
        in_specs=[_ANY], out_specs=_ANY,
        scratch_shapes=[pltpu.SemaphoreType.DMA((7,)), pltpu.SemaphoreType.DMA((7,)), pltpu.SemaphoreType.DMA],
    )(g)


_HBM = pl.BlockSpec(memory_space=pltpu.HBM)
_SEM = pl.BlockSpec(memory_space=pltpu.SEMAPHORE)
_EFFECT = pltpu.SideEffectType.DATAFLOW_SIDE_EFFECTING
N_PEERS = N_DEV - 1


def _me_and_peers():
    x, y, c = lax.axis_index("x"), lax.axis_index("y"), lax.axis_index("c")
    peers = []
    for k in range(1, N_DEV):
        px, py, pc = (x + (k >> 2)) % 2, (y + ((k >> 1) & 1)) % 2, (c + (k & 1)) % 2
        peers.append(((px, py, pc), 4 * px + 2 * py + pc))
    return 4 * x + 2 * y + c, peers


def _send_start(srcs, after, *, per_peer, name):
    n_t = len(srcs)
    blks = [s.shape[1:] if per_peer else s.shape for s in srcs]
    lands = [lax.empty((N_DEV,) + b, s.dtype) for b, s in zip(blks, srcs)]

    def body(*refs):
        src_refs, land_refs = refs[:n_t], refs[n_t:2 * n_t]
        send_sems, recv_sems = refs[2 * n_t + 1], refs[2 * n_t + 2]
        token = refs[-1]
        me, peers = _me_and_peers()
        for t in range(n_t):
            for k, (to, idx) in enumerate(peers):
                pltpu.make_async_remote_copy(
                    src_ref=src_refs[t].at[idx] if per_peer else src_refs[t], dst_ref=land_refs[t].at[me],
                    send_sem=send_sems.at[t * N_PEERS + k], recv_sem=recv_sems.at[t * N_PEERS + k],
                    device_id=to, device_id_type=_MESH).start()
        token[...] = jnp.zeros_like(token)

    sems = pltpu.SemaphoreType.DMA((n_t * N_PEERS,))
    outs = pl.pallas_call(
        body, name=name,
        out_shape=(sems, sems, *[pltpu.HBM(s.shape, s.dtype) for s in srcs], *[pltpu.HBM(z.shape, z.dtype) for z in lands],
                   jax.ShapeDtypeStruct((8, 128), F32)),
        in_specs=[_HBM] * (2 * n_t) + [_ANY],
        out_specs=(_SEM, _SEM, *[_HBM] * (2 * n_t), pl.BlockSpec(memory_space=pltpu.VMEM)),
        input_output_aliases={i: 2 + i for i in range(2 * n_t)},
        compiler_params=pltpu.CompilerParams(has_side_effects=_EFFECT),
    )(*[pltpu.with_memory_space_constraint(s, pltpu.HBM) for s in srcs],
      *[pltpu.with_memory_space_constraint(z, pltpu.HBM) for z in lands], after)
    return outs[0], outs[1], list(outs[2:2 + n_t]), list(outs[2 + n_t:2 + 2 * n_t]), outs[-1]


def _send_wait(handles, after, *, per_peer, name):
    send_sems, recv_sems, srcs, lands = handles
    n_t = len(srcs)

    def body(*refs):
        src_refs, land_refs = refs[:n_t], refs[n_t:2 * n_t]
        send_sems, recv_sems = refs[2 * n_t], refs[2 * n_t + 1]
        _, peers = _me_and_peers()
        for t in range(n_t):
            for k, (to, idx) in enumerate(peers):
                cp = pltpu.make_async_remote_copy(
                    src_ref=src_refs[t].at[idx] if per_peer else src_refs[t], dst_ref=land_refs[t].at[idx],
                    send_sem=send_sems.at[t * N_PEERS + k], recv_sem=recv_sems.at[t * N_PEERS + k],
                    device_id=to, device_id_type=_MESH)
                cp.wait_send()
                cp.wait_recv()

    outs = pl.pallas_call(
        body, name=name,
        out_shape=(*[pltpu.HBM(s.shape, s.dtype) for s in srcs], *[pltpu.HBM(z.shape, z.dtype) for z in lands]),
        in_specs=[_HBM] * (2 * n_t) + [_SEM, _SEM, _ANY], out_specs=tuple([_HBM] * (2 * n_t)),
        input_output_aliases={i: i for i in range(2 * n_t)},
        compiler_params=pltpu.CompilerParams(has_side_effects=_EFFECT),
    )(*srcs, *lands, send_sems, recv_sems, after)
    me = 4 * lax.axis_index("x") + 2 * lax.axis_index("y") + lax.axis_index("c")
    filled = []
    for src, land in zip(outs[:n_t], outs[n_t:]):
        own = lax.dynamic_index_in_dim(src, me, 0, keepdims=False) if per_peer else src
        filled.append(lax.dynamic_update_index_in_dim(land, own, me, 0))
    return filled


def _adamw_layer(w, m, v, parts, l, prev, *, name):
    depth, n_rows, n_cols = w.shape
    n_parts = parts.shape[0]
    tr = _elem_rows(n_rows, n_cols, 4 * (8 + n_parts))
    c1 = 1.0 / (1.0 - ADAM_B1 ** ADAM_STEP)
    c2 = 1.0 / (1.0 - ADAM_B2 ** ADAM_STEP)
    n_prev = 0 if prev is None else 4

    def body(w_ref, m_ref, v_ref, p_ref, *rest):
        g_ref, d_ref, nm_ref, nv_ref = rest[n_prev:]
        g = p_ref[0].astype(F32)
        for k in range(1, n_parts):
            g = g + p_ref[k].astype(F32)
        nm = ADAM_B1 * m_ref[...] + (1.0 - ADAM_B1) * g
        nv = ADAM_B2 * v_ref[...] + (1.0 - ADAM_B2) * (g * g)
        g_ref[...] = g
        nm_ref[...] = nm
        nv_ref[...] = nv
        d_ref[...] = -ADAM_LR * ((nm * c1) / (jnp.sqrt(nv * c2) + ADAM_EPS) + ADAM_WD * w_ref[...])

    spec = pl.BlockSpec((None, tr, n_cols), lambda i: (l, i, 0))
    return pl.pallas_call(
        body, name=name, grid=(n_rows // tr,),
        in_specs=[spec, spec, spec, pl.BlockSpec((n_parts, tr, n_cols), lambda i: (0, i, 0))] + [_ANY] * n_prev,
        out_specs=[spec] * 4, out_shape=[jax.ShapeDtypeStruct(w.shape, F32)] * 4,
        input_output_aliases={4 + i: i for i in range(n_prev)},
        compiler_params=_cparams(("parallel",)),
    )(w, m, v, parts, *(prev or ()))


def _concat_cols(parts, *, name):
    n_rows = parts[0].shape[0]
    widths = [p.shape[1] for p in parts]
    tr = _row_tile(n_rows, 1408)

    def body(*refs):
        o_ref, off = refs[-1], 0
        for p_ref, w in zip(refs[:-1], widths):
            o_ref[:, off:off + w] = p_ref[...]
            off += w

    return pl.pallas_call(
        body, name=name, grid=(n_rows // tr,), in_specs=[pl.BlockSpec((tr, w), lambda i: (i, 0)) for w in widths],
        out_specs=pl.BlockSpec((tr, sum(widths)), lambda i: (i, 0)),
        out_shape=jax.ShapeDtypeStruct((n_rows, sum(widths)), parts[0].dtype), compiler_params=_cparams(("parallel",)))(*parts)


def _full_weights(g):
    return {n: v if n == "w_up" else v.reshape(N_DEV * v.shape[1], v.shape[2]) for n, v in g.items()}


def _grad_blocks(gb):
    return [g if n == "w_up" else g.reshape(N_DEV, g.shape[0] // N_DEV, g.shape[1]) for n, g in gb.items()]


_SMALL_ROWS = 1096


def _pack_small(d):
    flat = jnp.concatenate([d[n].reshape(-1) for n in SMALL])
    total = N_DEV * _SMALL_ROWS * 128
    assert flat.shape[0] <= total
    return jnp.pad(flat, (0, total - flat.shape[0])).reshape(N_DEV * _SMALL_ROWS, 128)


def _unpack_small(packed, like):
    flat = packed.reshape(-1)
    out, off = {}, 0
    for n in SMALL:
        size = like[n].size
        out[n] = flat[off:off + size].reshape(like[n].shape)
        off += size
    return out


def kernel(x, meta_tokens, norm_mix_g, w_in, q_norm_g, k_norm_g, attn_sinks, ssm_lambda_re, ssm_lambda_im, ssm_log_step, ssm_b_re, ssm_b_im, ssm_c_re, ssm_c_im, ssm_d, w_glu, b_glu, attn_out_g, ssm_out_g, w_out, norm_mlp_g, w_up, w_down, loss_target, m_meta_tokens, m_norm_mix_g, m_w_in, m_q_norm_g, m_k_norm_g, m_attn_sinks, m_ssm_lambda_re, m_ssm_lambda_im, m_ssm_log_step, m_ssm_b_re, m_ssm_b_im, m_ssm_c_re, m_ssm_c_im, m_ssm_d, m_w_glu, m_b_glu, m_attn_out_g, m_ssm_out_g, m_w_out, m_norm_mlp_g, m_w_up, m_w_down, v_meta_tokens, v_norm_mix_g, v_w_in, v_q_norm_g, v_k_norm_g, v_attn_sinks, v_ssm_lambda_re, v_ssm_lambda_im, v_ssm_log_step, v_ssm_b_re, v_ssm_b_im, v_ssm_c_re, v_ssm_c_im, v_ssm_d, v_w_glu, v_b_glu, v_attn_out_g, v_ssm_out_g, v_w_out, v_norm_mlp_g, v_w_up, v_w_down):
    a = dict(locals())
    order = ("meta_tokens", "norm_mix_g", "w_in", "q_norm_g", "k_norm_g", "attn_sinks", "ssm_lambda_re", "ssm_lambda_im",
             "ssm_log_step", "ssm_b_re", "ssm_b_im", "ssm_c_re", "ssm_c_im", "ssm_d", "w_glu", "b_glu", "attn_out_g",
             "ssm_out_g", "w_out", "norm_mlp_g", "w_up", "w_down")

    for n in ("w_in", "m_w_in", "v_w_in"):
        a[n] = jnp.swapaxes(a[n], 1, 2)
    no_dep = jnp.zeros((8, 128), F32)
    sp = {n: a[n] for n in SMALL}
    wb = {}
    for n in BIG:
        depth, r, c = a[n].shape
        wb[n] = _cast_bf16(a[n].reshape(depth * r, c), name="cast_" + n).reshape(depth, r, c)
    meta_all = _all_gather(meta_tokens, name="gather_meta")
    meta = jnp.transpose(meta_all, (1, 0, 2)).reshape(N_META, D_MODEL)

    gathers, exchanges = {}, {}
    updated = {n: None for n in BIG}
    groups = (("w_in",), ("w_glu", "w_out"), ("w_up",), ("w_down",))

    def start_gather(l, after):
        for gi, names in enumerate(groups):
            *handles, after = _send_start([wb[n][l] for n in names], after, per_peer=False, name=f"gather_start_l{l}_g{gi}")
            gathers[l, gi] = handles
        return after

    def weights_for_layer(l, h):
        token = start_gather(0, wb["w_in"]) if l == 0 else h
        if l + 1 < DEPTH:
            token = start_gather(l + 1, token)
        got = {}

        def fetch(name, after):
            if name not in got:
                gi = [name in names for names in groups].index(True)
                lands = _send_wait(gathers.pop((l, gi)), after, per_peer=False, name=f"gather_wait_l{l}_g{gi}")
                got.update(_full_weights(dict(zip(groups[gi], lands))))
            return got[name]

        return fetch, (token,)

    def update_layer(l, after):
        for part in ("a", "b", "c"):
            names, handles = exchanges.pop((l, part))
            recv = _send_wait(handles, after, per_peer=True, name=f"exchange_wait_l{l}_{part}")
            for n, parts in zip(names, recv):
                updated[n] = _adamw_layer(a[n], a["m_" + n], a["v_" + n], parts, l, updated[n], name=f"adamw_{n}_l{l}")

    def early_grads(l, part, gb):
        *handles, token = _send_start(_grad_blocks(gb), no_dep, per_peer=True, name=f"exchange_start_l{l}_{part}")
        exchanges[l, part] = (tuple(gb), handles)
        return (token,)

    def grads_of_layer(l, dh):
        if l + 1 < DEPTH:
            update_layer(l + 1, dh)
        return ()

    loss, dh0, gsmall = _local_step(x[0], loss_target[0], meta, sp, weights_for_layer, early_grads, grads_of_layer)
    loss = lax.psum(loss[0, 0], ("x", "y", "c"))
    grad, delta, new_m, new_v = {}, {}, {}, {}

    dmeta = jnp.transpose(dh0[PAD:BLOCK].reshape(N_META, N_DEV, D_MODEL // N_DEV), (1, 0, 2))
    outs = _adamw(meta_tokens, m_meta_tokens, v_meta_tokens, _exchange(dmeta, name="exchange_meta"), name="adamw_meta_tokens")
    grad["meta_tokens"], delta["meta_tokens"], new_m["meta_tokens"], new_v["meta_tokens"] = outs

    packed = _pack_small({n: jnp.stack(gsmall[n]) for n in SMALL}).reshape(N_DEV, _SMALL_ROWS, 128)
    share = _sum_parts(_exchange(packed, name="exchange_small"), name="sum_small")
    total = _all_gather(share, name="gather_small").reshape(1, N_DEV * _SMALL_ROWS, 128)
    gsum = _unpack_small(total, sp)
    for n in SMALL:
        as2d = lambda v: v.reshape(-1, v.shape[-1])
        outs = _adamw(as2d(a[n]), as2d(a["m_" + n]), as2d(a["v_" + n]), as2d(gsum[n])[None], name="adamw_" + n)
        grad[n], delta[n], new_m[n], new_v[n] = [o.reshape(a[n].shape) for o in outs]

    update_layer(0, outs[0])
    for n in BIG:
        grad[n], delta[n], new_m[n], new_v[n] = [jnp.swapaxes(o, 1, 2) if n == "w_in" else o for o in updated[n]]

    return (loss, dh0[BLOCK:][None], *[grad[n] for n in order], *[delta[n] for n in order],
            *[new_m[n] for n in order], *[new_v[n] for n in order])
```

```python
import functools
import math

import jax
import jax.numpy as jnp
from jax import lax
from jax.experimental import pallas as pl
from jax.experimental.pallas import tpu as pltpu

F32 = jnp.float32
BF16 = jnp.bfloat16

N_DEV = 8
D_MODEL = 2048
SEQ = 4096
DEPTH = 4
N_META = 16
HEAD_DIM = 64
ATTN_WIDTH = D_MODEL // 2
N_HEADS = ATTN_WIDTH // HEAD_DIM
N_KV_HEADS = N_HEADS // 4
KV_GROUP = N_HEADS // N_KV_HEADS
KV_WIDTH = N_KV_HEADS * HEAD_DIM
SSM_WIDTH = D_MODEL - ATTN_WIDTH
SSM_GROUP_CH = 16
SSM_GROUPS = SSM_WIDTH // SSM_GROUP_CH
SSM_STATE = 64
WINDOW = 128
BLOCK = 128
PAD = BLOCK - N_META
D_FF = 4 * D_MODEL
IN_WIDTH = ATTN_WIDTH + 2 * KV_WIDTH + SSM_WIDTH
NORM_EPS = 1e-6
NEG_INF = -1e30
ADAM_LR = 0.001
ADAM_B1 = 0.9
ADAM_B2 = 0.999
ADAM_EPS = 1e-08
ADAM_WD = 0.01
ADAM_STEP = 10

VMEM_LIMIT = 56 * 1024 * 1024
_MESH = pl.DeviceIdType.MESH
_ANY = pl.BlockSpec(memory_space=pl.ANY)


def _cparams(sem=None):
    return pltpu.CompilerParams(dimension_semantics=sem, vmem_limit_bytes=VMEM_LIMIT)


def _matmul(a, b, *, mode, tm, tn, tk, outs, epilogue, tiles=(), rows=(), deps=(), blocked=False, name):
    if mode == "nn":
        m, k = a.shape
        if blocked:
            n = b.shape[0] * b.shape[2]
            assert tn == b.shape[2]
            b_spec = pl.BlockSpec((None, tk, tn), lambda i, j, kk: (j, kk, 0))
        else:
            n = b.shape[1]
            b_spec = pl.BlockSpec((tk, tn), lambda i, j, kk: (kk, j))
        a_spec = pl.BlockSpec((tm, tk), lambda i, j, kk: (i, kk))
        dims = (((1,), (0,)), ((), ()))
    elif mode == "nt":
        m, k = a.shape
        if blocked:
            n = b.shape[1]
            assert tk == b.shape[2] and k == b.shape[0] * b.shape[2]
            b_spec = pl.BlockSpec((None, tn, tk), lambda i, j, kk: (kk, j, 0))
        else:
            n = b.shape[0]
            b_spec = pl.BlockSpec((tn, tk), lambda i, j, kk: (j, kk))
        a_spec = pl.BlockSpec((tm, tk), lambda i, j, kk: (i, kk))
        dims = (((1,), (1,)), ((), ()))
    else:
        (k, m), n = a.shape, b.shape[1]
        a_spec = pl.BlockSpec((tk, tm), lambda i, j, kk: (kk, i))
        b_spec = pl.BlockSpec((tk, tn), lambda i, j, kk: (kk, j))
        dims = (((0,), (0,)), ((), ()))
    assert m % tm == 0 and n % tn == 0 and k % tk == 0, (name, m, n, k, tm, tn, tk)
    nk = k // tk
    n_tiles, n_rows, n_outs, n_deps = len(tiles), len(rows), len(outs), len(deps)

    def body(a_ref, b_ref, *rest):
        tile_refs = rest[:n_tiles]
        row_refs = rest[n_tiles:n_tiles + n_rows]
        out_refs = rest[n_tiles + n_rows + n_deps:n_tiles + n_rows + n_deps + n_outs]
        def product():
            return lax.dot_general(a_ref[...].astype(BF16), b_ref[...].astype(BF16), dims, preferred_element_type=F32)

        def finish(acc):
            res = epilogue(acc, *[r[...] for r in tile_refs], *[r[...] for r in row_refs])
            for o_ref, o in zip(out_refs, res):
                o_ref[...] = o.astype(o_ref.dtype)

        if nk == 1:
            finish(product())
            return
        acc_ref = rest[-1]
        kk = pl.program_id(2)

        @pl.when(kk == 0)
        def _():
            acc_ref[...] = jnp.zeros_like(acc_ref)

        acc_ref[...] += product()

        @pl.when(kk == nk - 1)
        def _():
            finish(acc_ref[...])

    tile_spec = pl.BlockSpec((tm, tn), lambda i, j, kk: (i, j))
    row_spec = pl.BlockSpec((1, tn), lambda i, j, kk: (0, j))
    if mode == "tn" and blocked:
        out_specs = [pl.BlockSpec((None, tm, tn), lambda i, j, kk: (j, i, 0))] * n_outs
        out_shape = [jax.ShapeDtypeStruct((n // tn, m, tn), dt) for dt in outs]
    else:
        out_specs = [tile_spec] * n_outs
        out_shape = [jax.ShapeDtypeStruct((m, n), dt) for dt in outs]
    return pl.pallas_call(
        body, name=name, grid=(m // tm, n // tn, nk),
        in_specs=[a_spec, b_spec] + [tile_spec] * n_tiles + [row_spec] * n_rows + [_ANY] * n_deps,
        out_specs=out_specs, out_shape=out_shape,
        scratch_shapes=[pltpu.VMEM((tm, tn), F32)] if nk > 1 else [],
        compiler_params=_cparams(("parallel", "parallel", "arbitrary")),
    )(a, b, *tiles, *rows, *deps)


def _ident(acc):
    return (acc,)


def _row_tile(n_rows, cap):
    best = BLOCK
    for t in range(BLOCK, cap + 1, BLOCK):
        if n_rows % t == 0:
            best = t
    return best


def _rmsnorm_fwd(xs, gs, *, name, deps=()):
    n_rows, width = xs[0].shape
    n = len(xs)
    tr = _row_tile(n_rows, 384)

    def body(*refs):
        o_ref = refs[-1]
        parts = []
        for x_ref, g_ref in zip(refs[:n], refs[n:2 * n]):
            x = x_ref[...]
            r = lax.rsqrt(jnp.mean(x * x, axis=-1, keepdims=True) + NORM_EPS)
            parts.append(x * r * g_ref[...])
        o_ref[...] = (parts[0] if n == 1 else jnp.concatenate(parts, axis=1)).astype(BF16)

    return pl.pallas_call(
        body, name=name, grid=(n_rows // tr,),
        in_specs=[pl.BlockSpec((tr, width), lambda i: (i, 0))] * n + [pl.BlockSpec((1, width), lambda i: (0, 0))] * n
        + [_ANY] * len(deps),
        out_specs=pl.BlockSpec((tr, n * width), lambda i: (i, 0)),
        out_shape=jax.ShapeDtypeStruct((n_rows, n * width), BF16),
        compiler_params=_cparams(("parallel",)),
    )(*xs, *gs, *deps)


def _norm_matmul(xs, gs, w, *, mode, res=None, deps=(), name):
    n_rows, width = xs[0].shape
    ng = len(xs)
    k = ng * width
    n = w.shape[1] if mode == "nn" else w.shape[0]
    tm, tn = _row_tile(n_rows, 704), _divisor(n, (1280, 1024, 512))
    dims = (((1,), (0,)), ((), ())) if mode == "nn" else _DN_NT
    has_res = res is not None

    def body(*refs):
        x_refs, g_refs, w_ref = refs[:ng], refs[ng:2 * ng], refs[2 * ng]
        y_ref, xb_ref = refs[-2], refs[-1]

        @pl.when(pl.program_id(1) == 0)
        def _():
            parts = []
            for x_ref, g_ref in zip(x_refs, g_refs):
                x = x_ref[...]
                parts.append(x * lax.rsqrt(jnp.mean(x * x, axis=-1, keepdims=True) + NORM_EPS) * g_ref[...])
            xb_ref[...] = (parts[0] if ng == 1 else jnp.concatenate(parts, axis=1)).astype(BF16)

        acc = lax.dot_general(xb_ref[...], w_ref[...], dims, preferred_element_type=F32)
        y_ref[...] = acc + refs[2 * ng + 1][...] if has_res else acc

    w_spec = pl.BlockSpec((k, tn), lambda i, j: (0, j)) if mode == "nn" else pl.BlockSpec((tn, k), lambda i, j: (j, 0))
    tile_spec = pl.BlockSpec((tm, tn), lambda i, j: (i, j))
    return pl.pallas_call(
        body, name=name, grid=(n_rows // tm, n // tn),
        in_specs=[pl.BlockSpec((tm, width), lambda i, j: (i, 0))] * ng + [pl.BlockSpec((1, width), lambda i, j: (0, 0))] * ng
        + [w_spec] + [tile_spec] * int(has_res) + [_ANY] * len(deps),
        out_specs=[tile_spec, pl.BlockSpec((tm, k), lambda i, j: (i, 0))],
        out_shape=[jax.ShapeDtypeStruct((n_rows, n), F32), jax.ShapeDtypeStruct((n_rows, k), BF16)],
        compiler_params=_cparams(("parallel", "arbitrary")),
    )(*xs, *gs, w, *([res] if has_res else []), *deps)


def _rmsnorm_bwd(xs, gs, dy, res, *, name):
    n_rows, width = xs[0].shape
    n = len(xs)
    tr = _row_tile(n_rows, 384)
    has_res = res is not None

    def body(*refs):
        x_refs, g_refs, dy_ref = refs[:n], refs[n:2 * n], refs[2 * n]
        res_ref = refs[2 * n + 1] if has_res else None
        outs = refs[2 * n + 1 + int(has_res):]
        dx_refs, dg_refs = outs[:n], outs[n:2 * n]
        i = pl.program_id(0)
        for c in range(n):
            x = x_refs[c][...]
            d = dy_ref[:, c * width:(c + 1) * width]
            r = lax.rsqrt(jnp.mean(x * x, axis=-1, keepdims=True) + NORM_EPS)
            xh = x * r
            gd = d * g_refs[c][...]
            dx = r * (gd - xh * jnp.mean(gd * xh, axis=-1, keepdims=True))
            if has_res:
                dx = dx + res_ref[...]
                outs[2 * n][...] = dx.astype(BF16)
            dx_refs[c][...] = dx
            part = jnp.sum(d * xh, axis=0, keepdims=True)

            @pl.when(i == 0)
            def _():
                dg_refs[c][...] = part

            @pl.when(i > 0)
            def _():
                dg_refs[c][...] += part

    row_spec = pl.BlockSpec((tr, width), lambda i: (i, 0))
    vec_spec = pl.BlockSpec((1, width), lambda i: (0, 0))
    outs = pl.pallas_call(
        body, name=name, grid=(n_rows // tr,),
        in_specs=[row_spec] * n + [vec_spec] * n + [pl.BlockSpec((tr, n * width), lambda i: (i, 0))] + [row_spec] * int(has_res),
        out_specs=[row_spec] * n + [vec_spec] * n + [row_spec] * int(has_res),
        out_shape=[jax.ShapeDtypeStruct((n_rows, width), F32)] * n + [jax.ShapeDtypeStruct((1, width), F32)] * n
        + [jax.ShapeDtypeStruct((n_rows, width), BF16)] * int(has_res),
        compiler_params=_cparams(("arbitrary",)),
    )(*xs, *gs, dy, *([res] if has_res else []))
    if has_res:
        return outs[:n], outs[n:2 * n], outs[2 * n]
    return outs[:n], outs[n:]


_SCALE = 1.0 / math.sqrt(HEAD_DIM)
_DN_NT = (((1,), (1,)), ((), ()))
_DN_TN = (((0,), (0,)), ((), ()))


def _head_norm(x, g):
    r = lax.rsqrt(jnp.mean(x * x, axis=-1, keepdims=True) + NORM_EPS)
    return x * r * g, r


def _attn_bias():
    rows = KV_GROUP * BLOCK
    r = lax.broadcasted_iota(jnp.int32, (rows, 3 * BLOCK), 0)
    j = lax.broadcasted_iota(jnp.int32, (rows, 3 * BLOCK), 1)
    i = r % BLOCK
    is_meta = j < BLOCK
    dist_band = 2 * BLOCK + i - j
    ok = jnp.where(is_meta, j >= PAD, (dist_band >= 0) & (dist_band < WINDOW))
    dist = jnp.where(is_meta, i - j, dist_band).astype(F32)
    slopes = jnp.asarray([2.0 ** (-8.0 * (h + 1) / N_HEADS) for h in range(N_HEADS)], F32).reshape(N_KV_HEADS, KV_GROUP)
    slope_rows = jnp.repeat(slopes, BLOCK, axis=1)[:, :, None]
    bias = jnp.where(ok[None], -slope_rows * dist[None], NEG_INF)
    neg0 = jnp.where(j[:, :BLOCK] <= i[:, :BLOCK], 0.0, NEG_INF)
    return bias.astype(F32), neg0.astype(F32)


def _slope_col(kv):
    g = lax.broadcasted_iota(jnp.int32, (KV_GROUP * BLOCK, 1), 0) // BLOCK
    col = jnp.zeros((KV_GROUP * BLOCK, 1), F32)
    for gi in range(KV_GROUP):
        col = jnp.where(g == gi, 2.0 ** (-8.0 * (kv * KV_GROUP + gi + 1) / N_HEADS), col)
    return col


def _sink_col(sink_ref, kv):
    g = lax.broadcasted_iota(jnp.int32, (KV_GROUP * BLOCK, 1), 0) // BLOCK
    col = jnp.zeros((KV_GROUP * BLOCK, 1), F32)
    for gi in range(KV_GROUP):
        h = kv * KV_GROUP + gi
        col = jnp.where(g == gi, sink_ref[0:1, h:h + 1], col)
    return col


def _stack_heads(x, kv):
    return jnp.concatenate([x[:, (kv * KV_GROUP + g) * HEAD_DIM:(kv * KV_GROUP + g + 1) * HEAD_DIM]
                            for g in range(KV_GROUP)], axis=0)


def _attn_scores(q_ref, k_refs, gq_ref, gk_ref, sink_ref, bias_ref, neg0_ref, kv, n):
    qs = _stack_heads(q_ref[...], kv)
    kcat = jnp.concatenate([r[:, kv * HEAD_DIM:(kv + 1) * HEAD_DIM] for r in k_refs], axis=0)
    qn, rq = _head_norm(qs, gq_ref[...])
    kn, rk = _head_norm(kcat, gk_ref[...])
    s = lax.dot_general((qn * _SCALE).astype(BF16), kn.astype(BF16), _DN_NT, preferred_element_type=F32)
    first = jnp.where(n == 0, 1.0, 0.0)
    second = jnp.where(n == 1, 1.0, 0.0)
    meta = (s[:, :BLOCK] + bias_ref[kv, :, :BLOCK] + _slope_col(kv) * (-float(BLOCK) * n.astype(F32))
            + first * neg0_ref[...])
    in_prev = jnp.where(lax.broadcasted_iota(jnp.int32, (1, 2 * BLOCK), 1) < BLOCK, 1.0, 0.0)
    band = s[:, BLOCK:] + bias_ref[kv, :, BLOCK:] + NEG_INF * (first + second * in_prev)
    s = jnp.concatenate([meta, band], axis=1)
    return qs, kcat, qn, kn, rq, rk, s, _sink_col(sink_ref, kv)


def _attn_specs():
    kq = ATTN_WIDTH // KV_WIDTH
    q_spec = pl.BlockSpec((BLOCK, ATTN_WIDTH), lambda n: (n, 0))
    kv_specs = []
    for col in (kq, kq + 1):
        kv_specs += [pl.BlockSpec((BLOCK, KV_WIDTH), lambda n, col=col: (0, col)),
                     pl.BlockSpec((BLOCK, KV_WIDTH), lambda n, col=col: (jnp.maximum(n - 1, 0), col)),
                     pl.BlockSpec((BLOCK, KV_WIDTH), lambda n, col=col: (n, col))]
    small = [pl.BlockSpec((1, HEAD_DIM), lambda n: (0, 0)), pl.BlockSpec((1, HEAD_DIM), lambda n: (0, 0)),
             pl.BlockSpec((1, N_HEADS), lambda n: (0, 0)),
             pl.BlockSpec((N_KV_HEADS, KV_GROUP * BLOCK, 3 * BLOCK), lambda n: (0, 0, 0)),
             pl.BlockSpec((KV_GROUP * BLOCK, BLOCK), lambda n: (0, 0))]
    return q_spec, kv_specs, small


def _attn_fwd(proj, gq, gk, sinks, *, name):
    n_rows = proj.shape[0]
    q_spec, kv_specs, small = _attn_specs()

    def body(q_ref, k0, k1, k2, v0, v1, v2, gq_ref, gk_ref, sink_ref, bias_ref, neg0_ref, o_ref, lse_ref):
        n = pl.program_id(0)
        o_parts, lse_parts = [], []
        for kv in range(N_KV_HEADS):
            _, _, _, _, _, _, s, sink = _attn_scores(q_ref, (k0, k1, k2), gq_ref, gk_ref, sink_ref, bias_ref, neg0_ref, kv, n)
            vcat = jnp.concatenate([r[:, kv * HEAD_DIM:(kv + 1) * HEAD_DIM] for r in (v0, v1, v2)], axis=0)
            m = jnp.maximum(jnp.max(s, axis=-1, keepdims=True), sink)
            p = jnp.exp(s - m)
            l = jnp.sum(p, axis=-1, keepdims=True) + jnp.exp(sink - m)
            o = jnp.dot(p.astype(BF16), vcat.astype(BF16), preferred_element_type=F32) / l
            lse = m + jnp.log(l)
            o_parts += [o[g * BLOCK:(g + 1) * BLOCK] for g in range(KV_GROUP)]
            lse_parts += [lse[g * BLOCK:(g + 1) * BLOCK] for g in range(KV_GROUP)]
        o_ref[...] = jnp.concatenate(o_parts, axis=1)
        lse_ref[...] = jnp.concatenate(lse_parts, axis=1)

    return pl.pallas_call(
        body, name=name, grid=(n_rows // BLOCK,),
        in_specs=[q_spec] + kv_specs + small,
        out_specs=[pl.BlockSpec((BLOCK, ATTN_WIDTH), lambda n: (n, 0)), pl.BlockSpec((BLOCK, N_HEADS), lambda n: (n, 0))],
        out_shape=[jax.ShapeDtypeStruct((n_rows, ATTN_WIDTH), F32), jax.ShapeDtypeStruct((n_rows, N_HEADS), F32)],
        compiler_params=_cparams(("parallel",)),
    )(proj, proj, proj, proj, proj, proj, proj, gq, gk, sinks, *_attn_bias())


def _attn_bwd(proj, gq, gk, sinks, o, lse, do, *, name):
    n_rows = proj.shape[0]
    q_spec, kv_specs, small = _attn_specs()

    def body(q_ref, k0, k1, k2, v0, v1, v2, gq_ref, gk_ref, sink_ref, bias_ref, neg0_ref, o_ref, lse_ref, do_ref,
             dq_ref, dkb_ref, dvb_ref, dgq_ref, dgk_ref, dsink_ref, dk_ref, dv_ref):
        n = pl.program_id(0)

        @pl.when(n == 0)
        def _():
            dk_ref[...] = jnp.zeros_like(dk_ref)
            dv_ref[...] = jnp.zeros_like(dv_ref)
            dgq_ref[...] = jnp.zeros_like(dgq_ref)
            dgk_ref[...] = jnp.zeros_like(dgk_ref)
            dsink_ref[...] = jnp.zeros_like(dsink_ref)

        dq_parts, dk_parts, dv_parts, dsink_parts = [], [], [], []
        dgq = jnp.zeros((1, HEAD_DIM), F32)
        dgk = jnp.zeros((1, HEAD_DIM), F32)
        for kv in range(N_KV_HEADS):
            qs, kcat, qn, kn, rq, rk, s, sink = _attn_scores(q_ref, (k0, k1, k2), gq_ref, gk_ref, sink_ref, bias_ref, neg0_ref, kv, n)
            vcat = jnp.concatenate([r[:, kv * HEAD_DIM:(kv + 1) * HEAD_DIM] for r in (v0, v1, v2)], axis=0)
            os_ = _stack_heads(o_ref[...], kv)
            dos = _stack_heads(do_ref[...], kv)
            lse = jnp.concatenate([lse_ref[:, kv * KV_GROUP + g:kv * KV_GROUP + g + 1] for g in range(KV_GROUP)], axis=0)
            p = jnp.exp(s - lse)
            delta = jnp.sum(dos * os_, axis=-1, keepdims=True)
            dp = lax.dot_general(dos.astype(BF16), vcat.astype(BF16), _DN_NT, preferred_element_type=F32)
            ds = (p * (dp - delta)) * _SCALE
            dsink_rows = -jnp.exp(sink - lse) * delta
            dsink_parts += [jnp.sum(dsink_rows[g * BLOCK:(g + 1) * BLOCK], axis=0, keepdims=True) for g in range(KV_GROUP)]
            dv_parts.append(lax.dot_general(p.astype(BF16), dos.astype(BF16), _DN_TN, preferred_element_type=F32))
            dsb = ds.astype(BF16)
            dqn = jnp.dot(dsb, kn.astype(BF16), preferred_element_type=F32)
            dkn = lax.dot_general(dsb, qn.astype(BF16), _DN_TN, preferred_element_type=F32)
            qh = qs * rq
            gd = dqn * gq_ref[...]
            dqs = rq * (gd - qh * jnp.mean(gd * qh, axis=-1, keepdims=True))
            dgq = dgq + jnp.sum(dqn * qh, axis=0, keepdims=True)
            kh = kcat * rk
            gdk = dkn * gk_ref[...]
            dk_parts.append(rk * (gdk - kh * jnp.mean(gdk * kh, axis=-1, keepdims=True)))
            dgk = dgk + jnp.sum(dkn * kh, axis=0, keepdims=True)
            dq_parts += [dqs[g * BLOCK:(g + 1) * BLOCK] for g in range(KV_GROUP)]
        dq_ref[...] = jnp.concatenate(dq_parts, axis=1).astype(BF16)
        dkc = jnp.concatenate(dk_parts, axis=1)
        dvc = jnp.concatenate(dv_parts, axis=1)
        prev = pl.multiple_of(jnp.maximum(n - 1, 0) * BLOCK, BLOCK)
        cur = pl.multiple_of(n * BLOCK, BLOCK)
        for acc_ref, val in ((dk_ref, dkc), (dv_ref, dvc)):
            acc_ref[0:BLOCK, :] += val[0:BLOCK]
            acc_ref[pl.ds(prev, BLOCK), :] += val[BLOCK:2 * BLOCK]
            acc_ref[pl.ds(cur, BLOCK), :] += val[2 * BLOCK:3 * BLOCK]
        dgq_ref[...] += dgq
        dgk_ref[...] += dgk
        dsink_ref[...] += jnp.concatenate(dsink_parts, axis=1)

        @pl.when(n == pl.num_programs(0) - 1)
        def _():
            dkb_ref[...] = dk_ref[...].astype(BF16)
            dvb_ref[...] = dv_ref[...].astype(BF16)

    blk = lambda w: pl.BlockSpec((BLOCK, w), lambda n: (n, 0))
    full = lambda r, w: pl.BlockSpec((r, w), lambda n: (0, 0))
    return pl.pallas_call(
        body, name=name, grid=(n_rows // BLOCK,),
        in_specs=[q_spec] + kv_specs + small + [blk(ATTN_WIDTH), blk(N_HEADS), blk(ATTN_WIDTH)],
        out_specs=[blk(ATTN_WIDTH), full(n_rows, KV_WIDTH), full(n_rows, KV_WIDTH),
                   full(1, HEAD_DIM), full(1, HEAD_DIM), full(1, N_HEADS)],
        out_shape=[jax.ShapeDtypeStruct((n_rows, ATTN_WIDTH), BF16), jax.ShapeDtypeStruct((n_rows, KV_WIDTH), BF16),
                   jax.ShapeDtypeStruct((n_rows, KV_WIDTH), BF16), jax.ShapeDtypeStruct((1, HEAD_DIM), F32),
                   jax.ShapeDtypeStruct((1, HEAD_DIM), F32), jax.ShapeDtypeStruct((1, N_HEADS), F32)],
        scratch_shapes=[pltpu.VMEM((n_rows, KV_WIDTH), F32), pltpu.VMEM((n_rows, KV_WIDTH), F32)],
        compiler_params=_cparams(("arbitrary",)),
    )(proj, proj, proj, proj, proj, proj, proj, gq, gk, sinks, *_attn_bias(), o, lse, do)


SSM_LAGS = 8
SLAB_G = 128 // SSM_GROUP_CH
N_SLABS = SSM_GROUPS // SLAB_G
SLAB_STATE = SLAB_G * SSM_STATE
U_COL = (ATTN_WIDTH + 2 * KV_WIDTH) // 128


def _ssm_prep(lam_re, lam_im, log_step, b_re, b_im, c_re, c_im):
    lam = lax.complex(lam_re, lam_im)
    delta = jnp.exp(log_step)[:, None]
    lam_bar = jnp.exp(lam * delta)
    b_t = lax.complex(jnp.swapaxes(b_re, 1, 2), jnp.swapaxes(b_im, 1, 2))
    b_bar = ((lam_bar - 1.0) / lam)[:, None, :] * b_t
    pw = [jnp.ones_like(lam_bar)]
    for _ in range(SSM_LAGS):
        pw.append(pw[-1] * lam_bar)
    w = jnp.stack(pw[:SSM_LAGS])[:, :, None, :] * b_bar[None]
    wri = jnp.stack([jnp.real(w), jnp.imag(w)], axis=3)
    wc = wri.reshape(SSM_LAGS, N_SLABS, SLAB_G * SSM_GROUP_CH, 2 * SSM_STATE)
    wc = jnp.swapaxes(wc, 0, 1).reshape(N_SLABS, SSM_LAGS * 128, 2 * SSM_STATE)
    cri = jnp.stack([c_re, -c_im], axis=2).reshape(N_SLABS, SLAB_G, SSM_GROUP_CH, 2, 1, SSM_STATE)
    eye = jnp.eye(SLAB_G, dtype=F32).reshape(1, SLAB_G, 1, 1, SLAB_G, 1)
    ct = (cri * eye).reshape(N_SLABS, 128, 2 * SLAB_STATE)
    l8 = pw[SSM_LAGS]
    lam8 = jnp.concatenate([jnp.real(l8).reshape(N_SLABS, 1, SLAB_STATE), jnp.imag(l8).reshape(N_SLABS, 1, SLAB_STATE)], axis=2)
    return wc, ct, lam8


def _row_group():
    return (lax.broadcasted_iota(jnp.int32, (SSM_LAGS * 128, 1), 0) // SSM_GROUP_CH) % SLAB_G


def _spread_groups(wc):
    g_of_row = _row_group()
    return jnp.concatenate([jnp.where(g_of_row == g, wc[:, r * SSM_STATE:(r + 1) * SSM_STATE], 0.0)
                            for r in range(2) for g in range(SLAB_G)], axis=1)


def _gather_groups(dw):
    g_of_row = _row_group()
    parts = []
    for r in range(2):
        acc = jnp.zeros((SSM_LAGS * 128, SSM_STATE), F32)
        for g in range(SLAB_G):
            c0 = r * SLAB_STATE + g * SSM_STATE
            acc = acc + jnp.where(g_of_row == g, dw[:, c0:c0 + SSM_STATE], 0.0)
        parts.append(acc)
    return jnp.concatenate(parts, axis=1)


def _lagged(u, up, t_rows):
    ue = jnp.concatenate([up, u], axis=0)
    return jnp.concatenate([ue[SSM_LAGS - tau:SSM_LAGS - tau + t_rows] for tau in range(SSM_LAGS)], axis=1).astype(BF16)


def _ssm_fwd(proj, wc, cmat, lam8, dvec, *, name):
    n_rows = proj.shape[0]
    tt = _row_tile(n_rows, 1408)
    n_t = n_rows // tt
    sw = 2 * SLAB_STATE
    hs = SLAB_STATE

    def body(u_ref, up_ref, wc_ref, c_ref, l_ref, d_ref, y_ref, x_ref, carry_ref, w_ref):
        t = pl.program_id(1)

        @pl.when(t == 0)
        def _():
            carry_ref[...] = jnp.zeros_like(carry_ref)
            w_ref[...] = _spread_groups(wc_ref[...]).astype(BF16)

        u = u_ref[...]
        up = jnp.where(t > 0, up_ref[...], 0.0)
        x_ref[...] = jnp.dot(_lagged(u, up, tt), w_ref[...], preferred_element_type=F32)
        ar = jnp.broadcast_to(l_ref[:, :hs], (8, hs))
        ai = jnp.broadcast_to(l_ref[:, hs:], (8, hs))

        def step(b, c):
            xr, xi = c
            r0 = pl.multiple_of(b * 8, 8)
            w = x_ref[pl.ds(r0, 8), :]
            nr = w[:, :hs] + ar * xr - ai * xi
            ni = w[:, hs:] + ar * xi + ai * xr
            x_ref[pl.ds(r0, 8), :] = jnp.concatenate([nr, ni], axis=1)
            return nr, ni

        xr, xi = lax.fori_loop(0, tt // 8, step, (carry_ref[:, :hs], carry_ref[:, hs:]), unroll=8)
        carry_ref[...] = jnp.concatenate([xr, xi], axis=1)
        y_ref[...] = lax.dot_general(x_ref[...].astype(BF16), c_ref[...], _DN_NT, preferred_element_type=F32) + d_ref[...] * u

    return pl.pallas_call(
        body, name=name, grid=(N_SLABS, n_t),
        in_specs=[pl.BlockSpec((tt, 128), lambda j, t: (t, U_COL + j)),
                  pl.BlockSpec((8, 128), lambda j, t: (jnp.maximum(t * (tt // 8) - 1, 0), U_COL + j)),
                  pl.BlockSpec((None, SSM_LAGS * 128, 2 * SSM_STATE), lambda j, t: (j, 0, 0)),
                  pl.BlockSpec((None, 128, sw), lambda j, t: (j, 0, 0)),
                  pl.BlockSpec((None, 1, sw), lambda j, t: (j, 0, 0)),
                  pl.BlockSpec((1, 128), lambda j, t: (0, j))],
        out_specs=[pl.BlockSpec((tt, 128), lambda j, t: (t, j)), pl.BlockSpec((tt, sw), lambda j, t: (t, j))],
        out_shape=[jax.ShapeDtypeStruct((n_rows, SSM_WIDTH), F32), jax.ShapeDtypeStruct((n_rows, N_SLABS * sw), F32)],
        scratch_shapes=[pltpu.VMEM((8, sw), F32), pltpu.VMEM((SSM_LAGS * 128, sw), BF16)],
        compiler_params=_cparams(("parallel", "arbitrary")),
    )(proj, proj, wc, cmat, lam8, dvec)


def _ssm_bwd(proj, xs, dy, wc, cmat, lam8, dvec, *, name, deps=()):
    n_rows = proj.shape[0]
    tt = _row_tile(n_rows, 704)
    n_t = n_rows // tt
    sw = 2 * SLAB_STATE
    hs = SLAB_STATE

    def body(u_ref, up_ref, x_ref, xp_ref, dy_ref, wc_ref, c_ref, l_ref, d_ref,
             *rest):
        du_ref, dwc_ref, dc_ref, dl_ref, dd_ref, a_ref, carry_ref, head_ref, w_ref, dw_ref = rest[len(deps):]
        t = pl.program_id(1)
        ti = n_t - 1 - t

        @pl.when(t == 0)
        def _():
            w_ref[...] = _spread_groups(wc_ref[...]).astype(BF16)
            carry_ref[...] = jnp.zeros_like(carry_ref)
            head_ref[...] = jnp.zeros_like(head_ref)
            dw_ref[...] = jnp.zeros_like(dw_ref)
            dc_ref[...] = jnp.zeros_like(dc_ref)
            dl_ref[...] = jnp.zeros_like(dl_ref)
            dd_ref[...] = jnp.zeros_like(dd_ref)

        u = u_ref[...]
        up = jnp.where(ti > 0, up_ref[...], 0.0)
        ucat = _lagged(u, up, tt)
        dyv = dy_ref[...]
        dyb = dyv.astype(BF16)
        a_ref[...] = jnp.dot(dyb, c_ref[...], preferred_element_type=F32)
        lr = jnp.broadcast_to(l_ref[:, :hs], (8, hs))
        li = jnp.broadcast_to(l_ref[:, hs:], (8, hs))

        def step(i, c):
            cr, ci = c
            r0 = pl.multiple_of((tt // 8 - 1 - i) * 8, 8)
            g = a_ref[pl.ds(r0, 8), :]
            nr = g[:, :hs] + lr * cr + li * ci
            ni = g[:, hs:] + lr * ci - li * cr
            a_ref[pl.ds(r0, 8), :] = jnp.concatenate([nr, ni], axis=1)
            return nr, ni

        cr, ci = lax.fori_loop(0, tt // 8, step, (carry_ref[:, :hs], carry_ref[:, hs:]), unroll=8)
        carry_ref[...] = jnp.concatenate([cr, ci], axis=1)

        a = a_ref[...]
        xv = x_ref[...]
        xprev = jnp.where(ti > 0, xp_ref[...], 0.0)
        xsh = jnp.concatenate([xprev, xv[:tt - SSM_LAGS]], axis=0)
        a_re, a_im, x_re, x_im = a[:, :hs], a[:, hs:], xsh[:, :hs], xsh[:, hs:]
        dl_ref[...] += jnp.concatenate([jnp.sum(a_re * x_re + a_im * x_im, axis=0, keepdims=True),
                                        jnp.sum(a_im * x_re - a_re * x_im, axis=0, keepdims=True)], axis=1)
        ab = a.astype(BF16)
        dw_ref[...] += lax.dot_general(ucat, ab, _DN_TN, preferred_element_type=F32)
        duc = lax.dot_general(ab, w_ref[...], _DN_NT, preferred_element_type=F32)
        ext = jnp.concatenate([duc, head_ref[...]], axis=0)
        du = d_ref[...] * dyv
        for tau in range(SSM_LAGS):
            du = du + ext[tau:tau + tt, tau * 128:(tau + 1) * 128]
        head_ref[...] = duc[0:8]
        row = ti * tt + lax.broadcasted_iota(jnp.int32, (tt, 128), 0)
        du_ref[...] = jnp.where(row >= PAD, du, 0.0).astype(BF16)
        dd_ref[...] += jnp.sum(dyv * u, axis=0, keepdims=True)
        dc_ref[...] += lax.dot_general(dyb, xv.astype(BF16), _DN_TN, preferred_element_type=F32)

        @pl.when(t == n_t - 1)
        def _():
            dwc_ref[...] = _gather_groups(dw_ref[...])

    rt = lambda t: n_t - 1 - t
    prev8 = lambda t: jnp.maximum(rt(t) * (tt // 8) - 1, 0)
    return pl.pallas_call(
        body, name=name, grid=(N_SLABS, n_t),
        in_specs=[pl.BlockSpec((tt, 128), lambda j, t: (rt(t), U_COL + j)),
                  pl.BlockSpec((8, 128), lambda j, t: (prev8(t), U_COL + j)),
                  pl.BlockSpec((tt, sw), lambda j, t: (rt(t), j)),
                  pl.BlockSpec((8, sw), lambda j, t: (prev8(t), j)),
                  pl.BlockSpec((tt, 128), lambda j, t: (rt(t), j)),
                  pl.BlockSpec((None, SSM_LAGS * 128, 2 * SSM_STATE), lambda j, t: (j, 0, 0)),
                  pl.BlockSpec((None, 128, sw), lambda j, t: (j, 0, 0)),
                  pl.BlockSpec((None, 1, sw), lambda j, t: (j, 0, 0)),
                  pl.BlockSpec((1, 128), lambda j, t: (0, j))] + [_ANY] * len(deps),
        out_specs=[pl.BlockSpec((tt, 128), lambda j, t: (rt(t), j)),
                   pl.BlockSpec((None, SSM_LAGS * 128, 2 * SSM_STATE), lambda j, t: (j, 0, 0)),
                   pl.BlockSpec((None, 128, sw), lambda j, t: (j, 0, 0)),
                   pl.BlockSpec((None, 1, sw), lambda j, t: (j, 0, 0)),
                   pl.BlockSpec((1, 128), lambda j, t: (0, j))],
        out_shape=[jax.ShapeDtypeStruct((n_rows, SSM_WIDTH), BF16),
                   jax.ShapeDtypeStruct((N_SLABS, SSM_LAGS * 128, 2 * SSM_STATE), F32),
                   jax.ShapeDtypeStruct((N_SLABS, 128, sw), F32),
                   jax.ShapeDtypeStruct((N_SLABS, 1, sw), F32),
                   jax.ShapeDtypeStruct((1, SSM_WIDTH), F32)],
        scratch_shapes=[pltpu.VMEM((tt, sw), F32), pltpu.VMEM((8, sw), F32), pltpu.VMEM((8, sw), F32),
                        pltpu.VMEM((SSM_LAGS * 128, sw), BF16), pltpu.VMEM((SSM_LAGS * 128, sw), F32)],
        compiler_params=_cparams(("parallel", "arbitrary")),
    )(proj, proj, xs, xs, dy, wc, cmat, lam8, dvec, *deps)


_GELU_C = math.sqrt(2.0 / math.pi)
_GELU_A = 0.044715


def _gelu(y):
    th = jnp.tanh(_GELU_C * (y + _GELU_A * y * y * y))
    return 0.5 * y * (1.0 + th), th


def _glu_fwd(y, w, b, *, name):
    n_rows, width = y.shape
    tr = _row_tile(n_rows, 384)

    def body(y_ref, w_ref, b_ref, o_ref):
        g, _ = _gelu(y_ref[...])
        z = jnp.dot(g.astype(BF16), w_ref[...], preferred_element_type=F32) + b_ref[...]
        o_ref[...] = g * jax.nn.sigmoid(z)

    return pl.pallas_call(
        body, name=name, grid=(n_rows // tr,),
        in_specs=[pl.BlockSpec((tr, width), lambda i: (i, 0)), pl.BlockSpec((width, width), lambda i: (0, 0)),
                  pl.BlockSpec((1, width), lambda i: (0, 0))],
        out_specs=pl.BlockSpec((tr, width), lambda i: (i, 0)),
        out_shape=jax.ShapeDtypeStruct((n_rows, width), F32),
        compiler_params=_cparams(("parallel",)),
    )(y, w, b)


def _glu_bwd(y, w, b, dout, *, name):
    n_rows, width = y.shape
    tr = _row_tile(n_rows, 384)

    def body(y_ref, w_ref, b_ref, do_ref, dy_ref, g_ref, dz_ref, db_ref):
        i = pl.program_id(0)
        yv = y_ref[...]
        g, th = _gelu(yv)
        gb = g.astype(BF16)
        z = jnp.dot(gb, w_ref[...], preferred_element_type=F32) + b_ref[...]
        sg = jax.nn.sigmoid(z)
        do = do_ref[...]
        dz = do * g * sg * (1.0 - sg)
        dzb = dz.astype(BF16)
        dg = do * sg + lax.dot_general(dzb, w_ref[...], _DN_NT, preferred_element_type=F32)
        dgelu = 0.5 * (1.0 + th) + 0.5 * yv * (1.0 - th * th) * _GELU_C * (1.0 + 3.0 * _GELU_A * yv * yv)
        dy_ref[...] = dg * dgelu
        g_ref[...] = gb
        dz_ref[...] = dzb
        part = jnp.sum(dz, axis=0, keepdims=True)

        @pl.when(i == 0)
        def _():
            db_ref[...] = part

        @pl.when(i > 0)
        def _():
            db_ref[...] += part

    row = pl.BlockSpec((tr, width), lambda i: (i, 0))
    vec = pl.BlockSpec((1, width), lambda i: (0, 0))
    return pl.pallas_call(
        body, name=name, grid=(n_rows // tr,),
        in_specs=[row, pl.BlockSpec((width, width), lambda i: (0, 0)), vec, row],
        out_specs=[row, row, row, vec],
        out_shape=[jax.ShapeDtypeStruct((n_rows, width), F32), jax.ShapeDtypeStruct((n_rows, width), BF16),
                   jax.ShapeDtypeStruct((n_rows, width), BF16), jax.ShapeDtypeStruct((1, width), F32)],
        compiler_params=_cparams(("arbitrary",)),
    )(y, w, b, dout)


def _loss_head(h, target, *, name):
    n_rows, width = h.shape

    def body(h_ref, t_ref, dh_ref, dhb_ref, loss_ref):
        i = pl.program_id(0)

        @pl.when(i == 0)
        def _():
            dh_ref[...] = jnp.zeros_like(dh_ref)
            dhb_ref[...] = jnp.zeros_like(dhb_ref)
            loss_ref[...] = jnp.zeros_like(loss_ref)

        @pl.when(i > 0)
        def _():
            err = h_ref[...] - t_ref[...]
            dh = err * (1.0 / width)
            dh_ref[...] = dh
            dhb_ref[...] = dh.astype(BF16)
            loss_ref[...] += (0.5 / width) * jnp.sum(err * err, keepdims=True)

    return pl.pallas_call(
        body, name=name, grid=(n_rows // BLOCK,),
        in_specs=[pl.BlockSpec((BLOCK, width), lambda i: (i, 0)),
                  pl.BlockSpec((BLOCK, width), lambda i: (jnp.maximum(i - 1, 0), 0))],
        out_specs=[pl.BlockSpec((BLOCK, width), lambda i: (i, 0)), pl.BlockSpec((BLOCK, width), lambda i: (i, 0)),
                   pl.BlockSpec((1, 1), lambda i: (0, 0))],
        out_shape=[jax.ShapeDtypeStruct((n_rows, width), F32), jax.ShapeDtypeStruct((n_rows, width), BF16),
                   jax.ShapeDtypeStruct((1, 1), F32)],
        compiler_params=_cparams(("arbitrary",)),
    )(h, target)


def _elem_rows(n_rows, n_cols, bytes_per_row_elem):
    lanes = -(-n_cols // 128) * 128
    cap = max(16, (12 * 1024 * 1024) // (lanes * bytes_per_row_elem))
    best = None
    for t in range(16, min(n_rows, cap) + 1, 16):
        if n_rows % t == 0:
            best = t
    return best or n_rows


def _cast_bf16(x, *, name):
    n_rows, n_cols = x.shape
    tr = _elem_rows(n_rows, n_cols, 4)

    def body(x_ref, o_ref):
        o_ref[...] = x_ref[...].astype(BF16)

    spec = pl.BlockSpec((tr, n_cols), lambda i: (i, 0))
    return pl.pallas_call(body, name=name, grid=(n_rows // tr,), in_specs=[spec], out_specs=spec,
                          out_shape=jax.ShapeDtypeStruct(x.shape, BF16), compiler_params=_cparams(("parallel",)))(x)


def _adamw(w, m, v, parts, *, name):
    n_rows, n_cols = w.shape
    n_parts = parts.shape[0]
    tr = _elem_rows(n_rows, n_cols, 4 * (8 + n_parts))
    c1 = 1.0 / (1.0 - ADAM_B1 ** ADAM_STEP)
    c2 = 1.0 / (1.0 - ADAM_B2 ** ADAM_STEP)

    def body(w_ref, m_ref, v_ref, p_ref, g_ref, d_ref, nm_ref, nv_ref):
        g = p_ref[0].astype(F32)
        for k in range(1, n_parts):
            g = g + p_ref[k].astype(F32)
        nm = ADAM_B1 * m_ref[...] + (1.0 - ADAM_B1) * g
        nv = ADAM_B2 * v_ref[...] + (1.0 - ADAM_B2) * (g * g)
        g_ref[...] = g
        nm_ref[...] = nm
        nv_ref[...] = nv
        d_ref[...] = -ADAM_LR * ((nm * c1) / (jnp.sqrt(nv * c2) + ADAM_EPS) + ADAM_WD * w_ref[...])

    spec = pl.BlockSpec((tr, n_cols), lambda i: (i, 0))
    return pl.pallas_call(
        body, name=name, grid=(n_rows // tr,),
        in_specs=[spec, spec, spec, pl.BlockSpec((n_parts, tr, n_cols), lambda i: (0, i, 0))],
        out_specs=[spec] * 4, out_shape=[jax.ShapeDtypeStruct(w.shape, F32)] * 4,
        compiler_params=_cparams(("parallel",)),
    )(w, m, v, parts)


def _sum_parts(parts, *, name):
    n_parts, n_rows, n_cols = parts.shape
    tr = _elem_rows(n_rows, n_cols, 4 * (1 + n_parts))

    def body(p_ref, o_ref):
        g = p_ref[0].astype(F32)
        for k in range(1, n_parts):
            g = g + p_ref[k].astype(F32)
        o_ref[...] = g

    return pl.pallas_call(
        body, name=name, grid=(n_rows // tr,),
        in_specs=[pl.BlockSpec((n_parts, tr, n_cols), lambda i: (0, i, 0))],
        out_specs=pl.BlockSpec((tr, n_cols), lambda i: (i, 0)),
        out_shape=jax.ShapeDtypeStruct((n_rows, n_cols), F32), compiler_params=_cparams(("parallel",)),
    )(parts)


BIG = ("w_in", "w_glu", "w_out", "w_up", "w_down")
SMALL = ("norm_mix_g", "q_norm_g", "k_norm_g", "attn_sinks", "ssm_lambda_re", "ssm_lambda_im", "ssm_log_step",
         "ssm_b_re", "ssm_b_im", "ssm_c_re", "ssm_c_im", "ssm_d", "b_glu", "attn_out_g", "ssm_out_g", "norm_mlp_g")
_SSM_NAMES = ("ssm_lambda_re", "ssm_lambda_im", "ssm_log_step", "ssm_b_re", "ssm_b_im", "ssm_c_re", "ssm_c_im")


def _divisor(n, cands):
    for c in cands:
        if n % c == 0:
            return c
    return n


def _mm(a, b, mode, name, outs=(F32,), epilogue=_ident, tiles=(), deps=(), blocked=False):
    if mode == "nn":
        m, k = a.shape
        n = b.shape[0] * b.shape[2] if blocked else b.shape[1]
    elif mode == "nt":
        m, k = a.shape
        n = b.shape[1] if blocked else b.shape[0]
    else:
        (k, m), n = a.shape, b.shape[1]
    if mode == "tn":
        tm, tn, tk = _divisor(m, (1024, 512)), _divisor(n, (1024, 512)), k
    elif k <= 2560:
        tm, tn, tk = _row_tile(m, 1408), _divisor(n, (1024, 1280, 512)), k
    elif blocked:
        tm, tn, tk = _row_tile(m, 1408), _divisor(n, (1024, 512)), b.shape[2]
    else:
        tm, tn, tk = _row_tile(m, 1408), _divisor(n, (1024, 512)), _divisor(k, (1024, 512))
    return _matmul(a, b, mode=mode, tm=tm, tn=tn, tk=tk, outs=list(outs), epilogue=epilogue, tiles=tiles, deps=deps,
                   blocked=blocked, name=name)


def _add_tile(acc, res):
    return (acc + res,)


def _relu_sq(acc):
    r = jnp.maximum(acc, 0.0)
    return r, r * r


def _relu_sq_bwd(acc, r):
    return (acc * (2.0 * r.astype(F32)),)


def _row(v):
    return v.reshape(1, -1)


def _layer_fwd(hres, fetch, sp, l, deps=()):
    tag = f"_l{l}"
    wts = {}
    wts["w_in"] = fetch("w_in", hres)
    proj, hb = _norm_matmul([hres], [_row(sp["norm_mix_g"])], wts["w_in"], mode="nt", deps=deps, name="proj" + tag)
    gq, gk, sinks = _row(sp["q_norm_g"]), _row(sp["k_norm_g"]), _row(sp["attn_sinks"])
    o, lse = _attn_fwd(proj, gq, gk, sinks, name="attn_fwd" + tag)
    (wc, cmat, lam8), prep_vjp = jax.vjp(_ssm_prep, *[sp[n] for n in _SSM_NAMES])
    cmat = cmat.astype(BF16)
    y, xs = _ssm_fwd(proj, wc, cmat, lam8, _row(sp["ssm_d"]), name="ssm_fwd" + tag)
    wts["w_glu"] = fetch("w_glu", y)
    s = _glu_fwd(y, wts["w_glu"], _row(sp["b_glu"]), name="glu_fwd" + tag)
    wts["w_out"] = fetch("w_out", s)
    hres2, mix = _norm_matmul([o, s], [_row(sp["attn_out_g"]), _row(sp["ssm_out_g"])], wts["w_out"], mode="nn", res=hres,
                              name="out_proj" + tag)
    h2 = _rmsnorm_fwd([hres2], [_row(sp["norm_mlp_g"])], name="norm_mlp" + tag)
    wts["w_up"] = fetch("w_up", h2)
    r, act = _mm(h2, wts["w_up"], "nn", "mlp_up" + tag, outs=(BF16, BF16), epilogue=_relu_sq, blocked=True)
    wts["w_down"] = fetch("w_down", act)
    hres3, = _mm(act, wts["w_down"], "nn", "mlp_down" + tag, epilogue=_add_tile, tiles=(hres2,))
    saved = dict(wts=wts, hres=hres, hb=hb, proj=proj, o=o, lse=lse, wc=wc, cmat=cmat, lam8=lam8, prep_vjp=prep_vjp,
                 y=y, xs=xs, s=s, mix=mix, hres2=hres2, h2=h2, r=r, act=act)
    return hres3, saved


def _layer_bwd(dres, dres_b, sp, sv, l, early_grads, deps=()):
    tag = f"_l{l}"
    wts = sv["wts"]
    gb, gs = {}, {}
    d_up, = _mm(dres_b, wts["w_down"], "nt", "mlp_down_dx" + tag, outs=(BF16,), epilogue=_relu_sq_bwd, tiles=(sv["r"],),
                deps=deps)
    gb["w_down"], = _mm(sv["act"], dres_b, "tn", "mlp_down_dw" + tag, outs=(BF16,))
    gb["w_up"], = _mm(sv["h2"], d_up, "tn", "mlp_up_dw" + tag, outs=(BF16,), blocked=True)
    deps = early_grads(l, "a", {n: gb.pop(n) for n in ("w_up", "w_down")})
    dh2, = _mm(d_up, wts["w_up"], "nt", "mlp_up_dx" + tag, blocked=True, deps=deps)
    (dres2,), (dg,), dres2_b = _rmsnorm_bwd([sv["hres2"]], [_row(sp["norm_mlp_g"])], dh2, dres, name="norm_mlp_bwd" + tag)
    gs["norm_mlp_g"] = dg
    dmix, = _mm(dres2_b, wts["w_out"], "nt", "out_proj_dx" + tag)
    gb["w_out"], = _mm(sv["mix"], dres2_b, "tn", "out_proj_dw" + tag, outs=(BF16,))
    (do, ds), (dga, dgs) = _rmsnorm_bwd([sv["o"], sv["s"]], [_row(sp["attn_out_g"]), _row(sp["ssm_out_g"])], dmix, None,
                                        name="norm_out_bwd" + tag)
    gs["attn_out_g"], gs["ssm_out_g"] = dga, dgs
    dy, g_b, dz_b, db = _glu_bwd(sv["y"], wts["w_glu"], _row(sp["b_glu"]), ds, name="glu_bwd" + tag)
    gs["b_glu"] = db
    gb["w_glu"], = _mm(g_b, dz_b, "tn", "glu_dw" + tag, outs=(BF16,))
    deps = early_grads(l, "b", {n: gb.pop(n) for n in ("w_out", "w_glu")})
    du, dwc, dcmat, dlam8, dd = _ssm_bwd(sv["proj"], sv["xs"], dy, sv["wc"], sv["cmat"], sv["lam8"], _row(sp["ssm_d"]),
                                         name="ssm_bwd" + tag, deps=deps)
    gs["ssm_d"] = dd
    for n, g in zip(_SSM_NAMES, sv["prep_vjp"]((dwc, dcmat, dlam8))):
        gs[n] = g
    dq, dk, dv, dgq, dgk, dsinks = _attn_bwd(sv["proj"], _row(sp["q_norm_g"]), _row(sp["k_norm_g"]), _row(sp["attn_sinks"]),
                                             sv["o"], sv["lse"], do, name="attn_bwd" + tag)
    gs["q_norm_g"], gs["k_norm_g"], gs["attn_sinks"] = dgq, dgk, dsinks
    dproj = _concat_cols([dq, dk, dv, du], name="dproj" + tag)
    gb["w_in"], = _mm(dproj, sv["hb"], "tn", "proj_dw" + tag, outs=(BF16,))
    deps = early_grads(l, "c", {"w_in": gb.pop("w_in")})
    dh, = _mm(dproj, wts["w_in"], "nn", "proj_dx" + tag, deps=deps)
    (dres_in,), (dg,), dres_in_b = _rmsnorm_bwd([sv["hres"]], [_row(sp["norm_mix_g"])], dh, dres2, name="norm_mix_bwd" + tag)
    gs["norm_mix_g"] = dg
    return dres_in, dres_in_b, gs


def _local_step(x, target, meta, sp, weights_for_layer, early_grads, grads_of_layer):
    h = jnp.concatenate([jnp.zeros((PAD, x.shape[1]), F32), meta, x], axis=0)
    saved = []
    for l in range(DEPTH):
        fetch, deps = weights_for_layer(l, h)
        h, sv = _layer_fwd(h, fetch, {n: sp[n][l] for n in SMALL}, l, deps)
        saved.append(sv)
    dh, dh_b, loss = _loss_head(h, target, name="loss_head")
    gsmall = {n: [None] * DEPTH for n in SMALL}
    deps = ()
    for l in reversed(range(DEPTH)):
        dh, dh_b, gs = _layer_bwd(dh, dh_b, {n: sp[n][l] for n in SMALL}, saved[l], l, early_grads, deps)
        deps = grads_of_layer(l, dh)
        for n in SMALL:
            gsmall[n][l] = gs[n].reshape(sp[n][l].shape)
    return loss, dh, gsmall


def _all_gather(x, *, name):
    def body(x_ref, out_ref, send_sems, recv_sems, local_sem):
        x, y, c = lax.axis_index("x"), lax.axis_index("y"), lax.axis_index("c")
        me, sibling = (x, y, c), (x, y, 1 - c)
        chips = [(1 - x, y), (x, 1 - y), (1 - x, 1 - y)]

        def slot(px, py, pc):
            return out_ref.at[4 * px + 2 * py + pc]

        def copy(k, block, to, src=None):
            return pltpu.make_async_remote_copy(
                src_ref=slot(*block) if src is None else src, dst_ref=slot(*block),
                send_sem=send_sems.at[k], recv_sem=recv_sems.at[k], device_id=to, device_id_type=_MESH)

        mine = pltpu.make_async_copy(x_ref, slot(*me), local_sem)
        mine.start()
        first = [copy(0, me, sibling, src=x_ref)]
        first += [copy(1 + j, me, (*chip, c), src=x_ref) for j, chip in enumerate(chips)]
        for cp in first:
            cp.start()
        passed = [copy(4 + j, (*chip, c), sibling) for j, chip in enumerate(chips)]
        for j, chip in enumerate(chips):
            copy(1 + j, (*chip, c), me).wait_recv()
            passed[j].start()
        copy(0, sibling, me).wait_recv()
        for j, chip in enumerate(chips):
            copy(4 + j, (*chip, 1 - c), me).wait_recv()
        for cp in first + passed:
            cp.wait_send()
        mine.wait()

    return pl.pallas_call(
        body, name=name, out_shape=jax.ShapeDtypeStruct((N_DEV,) + x.shape, x.dtype),
        in_specs=[_ANY], out_specs=_ANY,
        scratch_shapes=[pltpu.SemaphoreType.DMA((7,)), pltpu.SemaphoreType.DMA((7,)), pltpu.SemaphoreType.DMA],
    )(x)


def _exchange(g, *, name):
    def body(g_ref, r_ref, send_sems, recv_sems, local_sem):
        x, y, c = lax.axis_index("x"), lax.axis_index("y"), lax.axis_index("c")
        me = 4 * x + 2 * y + c
        mine = pltpu.make_async_copy(g_ref.at[me], r_ref.at[me], local_sem)
        mine.start()

        def peer(k):
            px, py, pc = (x + (k >> 2)) % 2, (y + ((k >> 1) & 1)) % 2, (c + (k & 1)) % 2
            return (px, py, pc), 4 * px + 2 * py + pc

        def copy(k, src_block, dst_block):
            to, _ = peer(k)
            return pltpu.make_async_remote_copy(
                src_ref=g_ref.at[src_block], dst_ref=r_ref.at[dst_block],
                send_sem=send_sems.at[k - 1], recv_sem=recv_sems.at[k - 1], device_id=to, device_id_type=_MESH)

        sends = [copy(k, peer(k)[1], me) for k in range(1, N_DEV)]
        for cp in sends:
            cp.start()
        for k in range(1, N_DEV):
            copy(k, me, peer(k)[1]).wait_recv()
        for cp in sends:
            cp.wait_send()
        mine.wait()

    return pl.pallas_call(
        body, name=name, out_shape=jax.ShapeDtypeStruct(g.shape, g.dtype),
        in_specs=[_ANY], out_specs=_ANY,
        scratch_shapes=[pltpu.SemaphoreType.DMA((7,)), pltpu.SemaphoreType.DMA((7,)), pltpu.SemaphoreType.DMA],
    )(g)


_HBM = pl.BlockSpec(memory_space=pltpu.HBM)
_SEM = pl.BlockSpec(memory_space=pltpu.SEMAPHORE)
_EFFECT = pltpu.SideEffectType.DATAFLOW_SIDE_EFFECTING
N_PEERS = N_DEV - 1


def _me_and_peers():
    x, y, c = lax.axis_index("x"), lax.axis_index("y"), lax.axis_index("c")
    peers = []
    for k in range(1, N_DEV):
        px, py, pc = (x + (k >> 2)) % 2, (y + ((k >> 1) & 1)) % 2, (c + (k & 1)) % 2
        peers.append(((px, py, pc), 4 * px + 2 * py + pc))
    return 4 * x + 2 * y + c, peers


def _send_start(srcs, after, *, per_peer, name):
    n_t = len(srcs)
    blks = [s.shape[1:] if per_peer else s.shape for s in srcs]
    lands = [lax.empty((N_DEV,) + b, s.dtype) for b, s in zip(blks, srcs)]

    def body(*refs):
        src_refs, land_refs = refs[:n_t], refs[n_t:2 * n_t]
        send_sems, recv_sems = refs[2 * n_t + 1], refs[2 * n_t + 2]
        token = refs[-1]
        me, peers = _me_and_peers()
        for t in range(n_t):
            for k, (to, idx) in enumerate(peers):
                pltpu.make_async_remote_copy(
                    src_ref=src_refs[t].at[idx] if per_peer else src_refs[t], dst_ref=land_refs[t].at[me],
                    send_sem=send_sems.at[t * N_PEERS + k], recv_sem=recv_sems.at[t * N_PEERS + k],
                    device_id=to, device_id_type=_MESH).start()
        token[...] = jnp.zeros_like(token)

    sems = pltpu.SemaphoreType.DMA((n_t * N_PEERS,))
    outs = pl.pallas_call(
        body, name=name,
        out_shape=(sems, sems, *[pltpu.HBM(s.shape, s.dtype) for s in srcs], *[pltpu.HBM(z.shape, z.dtype) for z in lands],
                   jax.ShapeDtypeStruct((8, 128), F32)),
        in_specs=[_HBM] * (2 * n_t) + [_ANY],
        out_specs=(_SEM, _SEM, *[_HBM] * (2 * n_t), pl.BlockSpec(memory_space=pltpu.VMEM)),
        input_output_aliases={i: 2 + i for i in range(2 * n_t)},
        compiler_params=pltpu.CompilerParams(has_side_effects=_EFFECT),
    )(*[pltpu.with_memory_space_constraint(s, pltpu.HBM) for s in srcs],
      *[pltpu.with_memory_space_constraint(z, pltpu.HBM) for z in lands], after)
    return outs[0], outs[1], list(outs[2:2 + n_t]), list(outs[2 + n_t:2 + 2 * n_t]), outs[-1]


def _send_wait(handles, after, *, per_peer, name):
    send_sems, recv_sems, srcs, lands = handles
    n_t = len(srcs)

    def body(*refs):
        src_refs, land_refs = refs[:n_t], refs[n_t:2 * n_t]
        send_sems, recv_sems = refs[2 * n_t], refs[2 * n_t + 1]
        _, peers = _me_and_peers()
        for t in range(n_t):
            for k, (to, idx) in enumerate(peers):
                cp = pltpu.make_async_remote_copy(
                    src_ref=src_refs[t].at[idx] if per_peer else src_refs[t], dst_ref=land_refs[t].at[idx],
                    send_sem=send_sems.at[t * N_PEERS + k], recv_sem=recv_sems.at[t * N_PEERS + k],
                    device_id=to, device_id_type=_MESH)
                cp.wait_send()
                cp.wait_recv()

    outs = pl.pallas_call(
        body, name=name,
        out_shape=(*[pltpu.HBM(s.shape, s.dtype) for s in srcs], *[pltpu.HBM(z.shape, z.dtype) for z in lands]),
        in_specs=[_HBM] * (2 * n_t) + [_SEM, _SEM, _ANY], out_specs=tuple([_HBM] * (2 * n_t)),
        input_output_aliases={i: i for i in range(2 * n_t)},
        compiler_params=pltpu.CompilerParams(has_side_effects=_EFFECT),
    )(*srcs, *lands, send_sems, recv_sems, after)
    me = 4 * lax.axis_index("x") + 2 * lax.axis_index("y") + lax.axis_index("c")
    filled = []
    for src, land in zip(outs[:n_t], outs[n_t:]):
        own = lax.dynamic_index_in_dim(src, me, 0, keepdims=False) if per_peer else src
        filled.append(lax.dynamic_update_index_in_dim(land, own, me, 0))
    return filled


def _adamw_layer(w, m, v, parts, l, prev, *, name):
    depth, n_rows, n_cols = w.shape
    n_parts = parts.shape[0]
    tr = _elem_rows(n_rows, n_cols, 4 * (8 + n_parts))
    c1 = 1.0 / (1.0 - ADAM_B1 ** ADAM_STEP)
    c2 = 1.0 / (1.0 - ADAM_B2 ** ADAM_STEP)
    n_prev = 0 if prev is None else 4

    def body(w_ref, m_ref, v_ref, p_ref, *rest):
        g_ref, d_ref, nm_ref, nv_ref = rest[n_prev:]
        g = p_ref[0].astype(F32)
        for k in range(1, n_parts):
            g = g + p_ref[k].astype(F32)
        nm = ADAM_B1 * m_ref[...] + (1.0 - ADAM_B1) * g
        nv = ADAM_B2 * v_ref[...] + (1.0 - ADAM_B2) * (g * g)
        g_ref[...] = g
        nm_ref[...] = nm
        nv_ref[...] = nv
        d_ref[...] = -ADAM_LR * ((nm * c1) / (jnp.sqrt(nv * c2) + ADAM_EPS) + ADAM_WD * w_ref[...])

    spec = pl.BlockSpec((None, tr, n_cols), lambda i: (l, i, 0))
    return pl.pallas_call(
        body, name=name, grid=(n_rows // tr,),
        in_specs=[spec, spec, spec, pl.BlockSpec((n_parts, tr, n_cols), lambda i: (0, i, 0))] + [_ANY] * n_prev,
        out_specs=[spec] * 4, out_shape=[jax.ShapeDtypeStruct(w.shape, F32)] * 4,
        input_output_aliases={4 + i: i for i in range(n_prev)},
        compiler_params=_cparams(("parallel",)),
    )(w, m, v, parts, *(prev or ()))


def _concat_cols(parts, *, name):
    n_rows = parts[0].shape[0]
    widths = [p.shape[1] for p in parts]
    tr = _row_tile(n_rows, 1408)

    def body(*refs):
        o_ref, off = refs[-1], 0
        for p_ref, w in zip(refs[:-1], widths):
            o_ref[:, off:off + w] = p_ref[...]
            off += w

    return pl.pallas_call(
        body, name=name, grid=(n_rows // tr,), in_specs=[pl.BlockSpec((tr, w), lambda i: (i, 0)) for w in widths],
        out_specs=pl.BlockSpec((tr, sum(widths)), lambda i: (i, 0)),
        out_shape=jax.ShapeDtypeStruct((n_rows, sum(widths)), parts[0].dtype), compiler_params=_cparams(("parallel",)))(*parts)


def _full_weights(g):
    return {n: v if n == "w_up" else v.reshape(N_DEV * v.shape[1], v.shape[2]) for n, v in g.items()}


def _grad_blocks(gb):
    return [g if n == "w_up" else g.reshape(N_DEV, g.shape[0] // N_DEV, g.shape[1]) for n, g in gb.items()]


_SMALL_ROWS = 1096


def _pack_small(d):
    flat = jnp.concatenate([d[n].reshape(-1) for n in SMALL])
    total = N_DEV * _SMALL_ROWS * 128
    assert flat.shape[0] <= total
    return jnp.pad(flat, (0, total - flat.shape[0])).reshape(N_DEV * _SMALL_ROWS, 128)


def _unpack_small(packed, like):
    flat = packed.reshape(-1)
    out, off = {}, 0
    for n in SMALL:
        size = like[n].size
        out[n] = flat[off:off + size].reshape(like[n].shape)
        off += size
    return out


def kernel(x, meta_tokens, norm_mix_g, w_in, q_norm_g, k_norm_g, attn_sinks, ssm_lambda_re, ssm_lambda_im, ssm_log_step, ssm_b_re, ssm_b_im, ssm_c_re, ssm_c_im, ssm_d, w_glu, b_glu, attn_out_g, ssm_out_g, w_out, norm_mlp_g, w_up, w_down, loss_target, m_meta_tokens, m_norm_mix_g, m_w_in, m_q_norm_g, m_k_norm_g, m_attn_sinks, m_ssm_lambda_re, m_ssm_lambda_im, m_ssm_log_step, m_ssm_b_re, m_ssm_b_im, m_ssm_c_re, m_ssm_c_im, m_ssm_d, m_w_glu, m_b_glu, m_attn_out_g, m_ssm_out_g, m_w_out, m_norm_mlp_g, m_w_up, m_w_down, v_meta_tokens, v_norm_mix_g, v_w_in, v_q_norm_g, v_k_norm_g, v_attn_sinks, v_ssm_lambda_re, v_ssm_lambda_im, v_ssm_log_step, v_ssm_b_re, v_ssm_b_im, v_ssm_c_re, v_ssm_c_im, v_ssm_d, v_w_glu, v_b_glu, v_attn_out_g, v_ssm_out_g, v_w_out, v_norm_mlp_g, v_w_up, v_w_down):
    a = dict(locals())
    order = ("meta_tokens", "norm_mix_g", "w_in", "q_norm_g", "k_norm_g", "attn_sinks", "ssm_lambda_re", "ssm_lambda_im",
             "ssm_log_step", "ssm_b_re", "ssm_b_im", "ssm_c_re", "ssm_c_im", "ssm_d", "w_glu", "b_glu", "attn_out_g",
             "ssm_out_g", "w_out", "norm_mlp_g", "w_up", "w_down")

    for n in ("w_in", "m_w_in", "v_w_in"):
        a[n] = jnp.swapaxes(a[n], 1, 2)
    no_dep = jnp.zeros((8, 128), F32)
    sp = {n: a[n] for n in SMALL}
    wb = {}
    for n in BIG:
        depth, r, c = a[n].shape
        wb[n] = _cast_bf16(a[n].reshape(depth * r, c), name="cast_" + n).reshape(depth, r, c)
    meta_all = _all_gather(meta_tokens, name="gather_meta")
    meta = jnp.transpose(meta_all, (1, 0, 2)).reshape(N_META, D_MODEL)

    gathers, exchanges = {}, {}
    updated = {n: None for n in BIG}
    groups = (("w_in",), ("w_glu", "w_out"), ("w_up",), ("w_down",))

    def start_gather(l, after):
        for gi, names in enumerate(groups):
            *handles, after = _send_start([wb[n][l] for n in names], after, per_peer=False, name=f"gather_start_l{l}_g{gi}")
            gathers[l, gi] = handles
        return after

    def weights_for_layer(l, h):
        token = start_gather(0, wb["w_in"]) if l == 0 else h
        if l + 1 < DEPTH:
            token = start_gather(l + 1, token)
        got = {}

        def fetch(name, after):
            if name not in got:
                gi = [name in names for names in groups].index(True)
                lands = _send_wait(gathers.pop((l, gi)), after, per_peer=False, name=f"gather_wait_l{l}_g{gi}")
                got.update(_full_weights(dict(zip(groups[gi], lands))))
            return got[name]

        return fetch, (token,)

    def update_layer(l, after):
        for part in ("a", "b", "c"):
            names, handles = exchanges.pop((l, part))
            recv = _send_wait(handles, after, per_peer=True, name=f"exchange_wait_l{l}_{part}")
            for n, parts in zip(names, recv):
                updated[n] = _adamw_layer(a[n], a["m_" + n], a["v_" + n], parts, l, updated[n], name=f"adamw_{n}_l{l}")

    def early_grads(l, part, gb):
        *handles, token = _send_start(_grad_blocks(gb), no_dep, per_peer=True, name=f"exchange_start_l{l}_{part}")
        exchanges[l, part] = (tuple(gb), handles)
        return (token,)

    def grads_of_layer(l, dh):
        if l + 1 < DEPTH:
            update_layer(l + 1, dh)
        return ()

    loss, dh0, gsmall = _local_step(x[0], loss_target[0], meta, sp, weights_for_layer, early_grads, grads_of_layer)
    loss = lax.psum(loss[0, 0], ("x", "y", "c"))
    grad, delta, new_m, new_v = {}, {}, {}, {}

    dmeta = jnp.transpose(dh0[PAD:BLOCK].reshape(N_META, N_DEV, D_MODEL // N_DEV), (1, 0, 2))
    outs = _adamw(meta_tokens, m_meta_tokens, v_meta_tokens, _exchange(dmeta, name="exchange_meta"), name="adamw_meta_tokens")
    grad["meta_tokens"], delta["meta_tokens"], new_m["meta_tokens"], new_v["meta_tokens"] = outs

    packed = _pack_small({n: jnp.stack(gsmall[n]) for n in SMALL}).reshape(N_DEV, _SMALL_ROWS, 128)
    share = _sum_parts(_exchange(packed, name="exchange_small"), name="sum_small")
    total = _all_gather(share, name="gather_small").reshape(1, N_DEV * _SMALL_ROWS, 128)
    gsum = _unpack_small(total, sp)
    for n in SMALL:
        as2d = lambda v: v.reshape(-1, v.shape[-1])
        outs = _adamw(as2d(a[n]), as2d(a["m_" + n]), as2d(a["v_" + n]), as2d(gsum[n])[None], name="adamw_" + n)
        grad[n], delta[n], new_m[n], new_v[n] = [o.reshape(a[n].shape) for o in outs]

    update_layer(0, outs[0])
    for n in BIG:
        grad[n], delta[n], new_m[n], new_v[n] = [jnp.swapaxes(o, 1, 2) if n == "w_in" else o for o in updated[n]]

    return (loss, dh0[BLOCK:][None], *[grad[n] for n in order], *[delta[n] for n in order],
            *[new_m[n] for n in order], *[new_v[n] for n in order])
```

```python
import math

import jax
import jax.numpy as jnp
from jax import lax
from jax.experimental import pallas as pl
from jax.experimental.pallas import tpu as pltpu

F32 = jnp.float32
BF16 = jnp.bfloat16

N_DEV = 8
D_MODEL = 2048
SEQ = 4096
DEPTH = 4
N_META = 16
HEAD_DIM = 64
ATTN_WIDTH = D_MODEL // 2
N_HEADS = ATTN_WIDTH // HEAD_DIM
N_KV_HEADS = N_HEADS // 4
KV_GROUP = N_HEADS // N_KV_HEADS
KV_WIDTH = N_KV_HEADS * HEAD_DIM
SSM_WIDTH = D_MODEL - ATTN_WIDTH
SSM_GROUP_CH = 16
SSM_GROUPS = SSM_WIDTH // SSM_GROUP_CH
SSM_STATE = 64
WINDOW = 128
BLOCK = 128
PAD = BLOCK - N_META
D_FF = 4 * D_MODEL
IN_WIDTH = ATTN_WIDTH + 2 * KV_WIDTH + SSM_WIDTH
NORM_EPS = 1e-6
NEG_INF = -1e30
ADAM_LR = 0.001
ADAM_B1 = 0.9
ADAM_B2 = 0.999
ADAM_EPS = 1e-08
ADAM_WD = 0.01
ADAM_STEP = 10

VMEM_LIMIT = 56 * 1024 * 1024
_MESH = pl.DeviceIdType.MESH
_ANY = pl.BlockSpec(memory_space=pl.ANY)


def _cparams(sem=None):
    return pltpu.CompilerParams(dimension_semantics=sem, vmem_limit_bytes=VMEM_LIMIT)


def _matmul(a, b, *, mode, tm, tn, tk, outs, epilogue, tiles=(), deps=(), blocked=False, name):
    if mode == "nn":
        m, k = a.shape
        if blocked:
            n = b.shape[0] * b.shape[2]
            assert tn == b.shape[2]
            b_spec = pl.BlockSpec((None, tk, tn), lambda i, j, kk: (j, kk, 0))
        else:
            n = b.shape[1]
            b_spec = pl.BlockSpec((tk, tn), lambda i, j, kk: (kk, j))
        a_spec = pl.BlockSpec((tm, tk), lambda i, j, kk: (i, kk))
        dims = (((1,), (0,)), ((), ()))
    elif mode == "nt":
        m, k = a.shape
        if blocked:
            n = b.shape[1]
            assert tk == b.shape[2] and k == b.shape[0] * b.shape[2]
            b_spec = pl.BlockSpec((None, tn, tk), lambda i, j, kk: (kk, j, 0))
        else:
            n = b.shape[0]
            b_spec = pl.BlockSpec((tn, tk), lambda i, j, kk: (j, kk))
        a_spec = pl.BlockSpec((tm, tk), lambda i, j, kk: (i, kk))
        dims = (((1,), (1,)), ((), ()))
    else:
        (k, m), n = a.shape, b.shape[1]
        a_spec = pl.BlockSpec((tk, tm), lambda i, j, kk: (kk, i))
        b_spec = pl.BlockSpec((tk, tn), lambda i, j, kk: (kk, j))
        dims = (((0,), (0,)), ((), ()))
    assert m % tm == 0 and n % tn == 0 and k % tk == 0, (name, m, n, k, tm, tn, tk)
    nk = k // tk
    n_tiles, n_outs, n_deps = len(tiles), len(outs), len(deps)

    def body(a_ref, b_ref, *rest):
        tile_refs = rest[:n_tiles]
        out_refs = rest[n_tiles + n_deps:n_tiles + n_deps + n_outs]

        def product():
            return lax.dot_general(a_ref[...].astype(BF16), b_ref[...].astype(BF16), dims, preferred_element_type=F32)

        def finish(acc):
            res = epilogue(acc, *[r[...] for r in tile_refs])
            for o_ref, o in zip(out_refs, res):
                o_ref[...] = o.astype(o_ref.dtype)

        if nk == 1:
            finish(product())
            return
        acc_ref = rest[-1]
        kk = pl.program_id(2)

        @pl.when(kk == 0)
        def _():
            acc_ref[...] = jnp.zeros_like(acc_ref)

        acc_ref[...] += product()

        @pl.when(kk == nk - 1)
        def _():
            finish(acc_ref[...])

    tile_spec = pl.BlockSpec((tm, tn), lambda i, j, kk: (i, j))
    if mode == "tn" and blocked:
        out_specs = [pl.BlockSpec((None, tm, tn), lambda i, j, kk: (j, i, 0))] * n_outs
        out_shape = [jax.ShapeDtypeStruct((n // tn, m, tn), dt) for dt in outs]
    else:
        out_specs = [tile_spec] * n_outs
        out_shape = [jax.ShapeDtypeStruct((m, n), dt) for dt in outs]
    return pl.pallas_call(
        body, name=name, grid=(m // tm, n // tn, nk),
        in_specs=[a_spec, b_spec] + [tile_spec] * n_tiles + [_ANY] * n_deps,
        out_specs=out_specs, out_shape=out_shape,
        scratch_shapes=[pltpu.VMEM((tm, tn), F32)] if nk > 1 else [],
        compiler_params=_cparams(("parallel", "parallel", "arbitrary")),
    )(a, b, *tiles, *deps)


def _ident(acc):
    return (acc,)


def _row_tile(n_rows, cap):
    best = BLOCK
    for t in range(BLOCK, cap + 1, BLOCK):
        if n_rows % t == 0:
            best = t
    return best


def _rmsnorm_fwd(xs, gs, *, name, deps=()):
    n_rows, width = xs[0].shape
    n = len(xs)
    tr = _row_tile(n_rows, 384)

    def body(*refs):
        o_ref = refs[-1]
        parts = []
        for x_ref, g_ref in zip(refs[:n], refs[n:2 * n]):
            x = x_ref[...]
            r = lax.rsqrt(jnp.mean(x * x, axis=-1, keepdims=True) + NORM_EPS)
            parts.append(x * r * g_ref[...])
        o_ref[...] = (parts[0] if n == 1 else jnp.concatenate(parts, axis=1)).astype(BF16)

    return pl.pallas_call(
        body, name=name, grid=(n_rows // tr,),
        in_specs=[pl.BlockSpec((tr, width), lambda i: (i, 0))] * n + [pl.BlockSpec((1, width), lambda i: (0, 0))] * n
        + [_ANY] * len(deps),
        out_specs=pl.BlockSpec((tr, n * width), lambda i: (i, 0)),
        out_shape=jax.ShapeDtypeStruct((n_rows, n * width), BF16),
        compiler_params=_cparams(("parallel",)),
    )(*xs, *gs, *deps)


def _rmsnorm_bwd(xs, gs, dy, res, *, name):
    n_rows, width = xs[0].shape
    n = len(xs)
    tr = _row_tile(n_rows, 384)
    has_res = res is not None

    def body(*refs):
        x_refs, g_refs, dy_ref = refs[:n], refs[n:2 * n], refs[2 * n]
        res_ref = refs[2 * n + 1] if has_res else None
        outs = refs[2 * n + 1 + int(has_res):]
        dx_refs, dg_refs = outs[:n], outs[n:2 * n]
        i = pl.program_id(0)
        for c in range(n):
            x = x_refs[c][...]
            d = dy_ref[:, c * width:(c + 1) * width]
            r = lax.rsqrt(jnp.mean(x * x, axis=-1, keepdims=True) + NORM_EPS)
            xh = x * r
            gd = d * g_refs[c][...]
            dx = r * (gd - xh * jnp.mean(gd * xh, axis=-1, keepdims=True))
            if has_res:
                dx = dx + res_ref[...]
                outs[2 * n][...] = dx.astype(BF16)
            dx_refs[c][...] = dx
            part = jnp.sum(d * xh, axis=0, keepdims=True)

            @pl.when(i == 0)
            def _():
                dg_refs[c][...] = part

            @pl.when(i > 0)
            def _():
                dg_refs[c][...] += part

    row_spec = pl.BlockSpec((tr, width), lambda i: (i, 0))
    vec_spec = pl.BlockSpec((1, width), lambda i: (0, 0))
    outs = pl.pallas_call(
        body, name=name, grid=(n_rows // tr,),
        in_specs=[row_spec] * n + [vec_spec] * n + [pl.BlockSpec((tr, n * width), lambda i: (i, 0))] + [row_spec] * int(has_res),
        out_specs=[row_spec] * n + [vec_spec] * n + [row_spec] * int(has_res),
        out_shape=[jax.ShapeDtypeStruct((n_rows, width), F32)] * n + [jax.ShapeDtypeStruct((1, width), F32)] * n
        + [jax.ShapeDtypeStruct((n_rows, width), BF16)] * int(has_res),
        compiler_params=_cparams(("arbitrary",)),
    )(*xs, *gs, dy, *([res] if has_res else []))
    if has_res:
        return outs[:n], outs[n:2 * n], outs[2 * n]
    return outs[:n], outs[n:]


_SCALE = 1.0 / math.sqrt(HEAD_DIM)
_DN_NT = (((1,), (1,)), ((), ()))
_DN_TN = (((0,), (0,)), ((), ()))


def _head_norm(x, g):
    r = lax.rsqrt(jnp.mean(x * x, axis=-1, keepdims=True) + NORM_EPS)
    return x * r * g, r


def _attn_bias():
    rows = KV_GROUP * BLOCK
    r = lax.broadcasted_iota(jnp.int32, (rows, 3 * BLOCK), 0)
    j = lax.broadcasted_iota(jnp.int32, (rows, 3 * BLOCK), 1)
    i = r % BLOCK
    is_meta = j < BLOCK
    dist_band = 2 * BLOCK + i - j
    ok = jnp.where(is_meta, j >= PAD, (dist_band >= 0) & (dist_band < WINDOW))
    dist = jnp.where(is_meta, i - j, dist_band).astype(F32)
    slopes = jnp.asarray([2.0 ** (-8.0 * (h + 1) / N_HEADS) for h in range(N_HEADS)], F32).reshape(N_KV_HEADS, KV_GROUP)
    slope_rows = jnp.repeat(slopes, BLOCK, axis=1)[:, :, None]
    bias = jnp.where(ok[None], -slope_rows * dist[None], NEG_INF)
    neg0 = jnp.where(j[:, :BLOCK] <= i[:, :BLOCK], 0.0, NEG_INF)
    return bias.astype(F32), neg0.astype(F32)


def _slope_col(kv):
    g = lax.broadcasted_iota(jnp.int32, (KV_GROUP * BLOCK, 1), 0) // BLOCK
    col = jnp.zeros((KV_GROUP * BLOCK, 1), F32)
    for gi in range(KV_GROUP):
        col = jnp.where(g == gi, 2.0 ** (-8.0 * (kv * KV_GROUP + gi + 1) / N_HEADS), col)
    return col


def _sink_col(sink_ref, kv):
    g = lax.broadcasted_iota(jnp.int32, (KV_GROUP * BLOCK, 1), 0) // BLOCK
    col = jnp.zeros((KV_GROUP * BLOCK, 1), F32)
    for gi in range(KV_GROUP):
        h = kv * KV_GROUP + gi
        col = jnp.where(g == gi, sink_ref[0:1, h:h + 1], col)
    return col


def _stack_heads(x, kv):
    return jnp.concatenate([x[:, (kv * KV_GROUP + g) * HEAD_DIM:(kv * KV_GROUP + g + 1) * HEAD_DIM]
                            for g in range(KV_GROUP)], axis=0)


def _attn_scores(q_ref, k_refs, gq_ref, gk_ref, sink_ref, bias_ref, neg0_ref, kv, n):
    qs = _stack_heads(q_ref[...], kv)
    kcat = jnp.concatenate([r[:, kv * HEAD_DIM:(kv + 1) * HEAD_DIM] for r in k_refs], axis=0)
    qn, rq = _head_norm(qs, gq_ref[...])
    kn, rk = _head_norm(kcat, gk_ref[...])
    s = lax.dot_general((qn * _SCALE).astype(BF16), kn.astype(BF16), _DN_NT, preferred_element_type=F32)
    first = jnp.where(n == 0, 1.0, 0.0)
    second = jnp.where(n == 1, 1.0, 0.0)
    meta = (s[:, :BLOCK] + bias_ref[kv, :, :BLOCK] + _slope_col(kv) * (-float(BLOCK) * n.astype(F32))
            + first * neg0_ref[...])
    in_prev = jnp.where(lax.broadcasted_iota(jnp.int32, (1, 2 * BLOCK), 1) < BLOCK, 1.0, 0.0)
    band = s[:, BLOCK:] + bias_ref[kv, :, BLOCK:] + NEG_INF * (first + second * in_prev)
    s = jnp.concatenate([meta, band], axis=1)
    return qs, kcat, qn, kn, rq, rk, s, _sink_col(sink_ref, kv)


def _attn_specs():
    kq = ATTN_WIDTH // KV_WIDTH
    q_spec = pl.BlockSpec((BLOCK, ATTN_WIDTH), lambda n: (n, 0))
    kv_specs = []
    for col in (kq, kq + 1):
        kv_specs += [pl.BlockSpec((BLOCK, KV_WIDTH), lambda n, col=col: (0, col)),
                     pl.BlockSpec((BLOCK, KV_WIDTH), lambda n, col=col: (jnp.maximum(n - 1, 0), col)),
                     pl.BlockSpec((BLOCK, KV_WIDTH), lambda n, col=col: (n, col))]
    small = [pl.BlockSpec((1, HEAD_DIM), lambda n: (0, 0)), pl.BlockSpec((1, HEAD_DIM), lambda n: (0, 0)),
             pl.BlockSpec((1, N_HEADS), lambda n: (0, 0)),
             pl.BlockSpec((N_KV_HEADS, KV_GROUP * BLOCK, 3 * BLOCK), lambda n: (0, 0, 0)),
             pl.BlockSpec((KV_GROUP * BLOCK, BLOCK), lambda n: (0, 0))]
    return q_spec, kv_specs, small


def _attn_fwd(proj, gq, gk, sinks, *, name):
    n_rows = proj.shape[0]
    q_spec, kv_specs, small = _attn_specs()

    def body(q_ref, k0, k1, k2, v0, v1, v2, gq_ref, gk_ref, sink_ref, bias_ref, neg0_ref, o_ref, lse_ref):
        n = pl.program_id(0)
        o_parts, lse_parts = [], []
        for kv in range(N_KV_HEADS):
            _, _, _, _, _, _, s, sink = _attn_scores(q_ref, (k0, k1, k2), gq_ref, gk_ref, sink_ref, bias_ref, neg0_ref, kv, n)
            vcat = jnp.concatenate([r[:, kv * HEAD_DIM:(kv + 1) * HEAD_DIM] for r in (v0, v1, v2)], axis=0)
            m = jnp.maximum(jnp.max(s, axis=-1, keepdims=True), sink)
            p = jnp.exp(s - m)
            v_ones = jnp.concatenate([vcat, jnp.ones_like(vcat)], axis=1)
            ov = jnp.dot(p.astype(BF16), v_ones.astype(BF16), preferred_element_type=F32)
            l = ov[:, HEAD_DIM:HEAD_DIM + 1] + jnp.exp(sink - m)
            o = ov[:, :HEAD_DIM] / l
            lse = m + jnp.log(l)
            o_parts += [o[g * BLOCK:(g + 1) * BLOCK] for g in range(KV_GROUP)]
            lse_parts += [lse[g * BLOCK:(g + 1) * BLOCK] for g in range(KV_GROUP)]
        o_ref[...] = jnp.concatenate(o_parts, axis=1)
        lse_ref[...] = jnp.concatenate(lse_parts, axis=1)

    return pl.pallas_call(
        body, name=name, grid=(n_rows // BLOCK,),
        in_specs=[q_spec] + kv_specs + small,
        out_specs=[pl.BlockSpec((BLOCK, ATTN_WIDTH), lambda n: (n, 0)), pl.BlockSpec((BLOCK, N_HEADS), lambda n: (n, 0))],
        out_shape=[jax.ShapeDtypeStruct((n_rows, ATTN_WIDTH), F32), jax.ShapeDtypeStruct((n_rows, N_HEADS), F32)],
        compiler_params=_cparams(("parallel",)),
    )(proj, proj, proj, proj, proj, proj, proj, gq, gk, sinks, *_attn_bias())


def _attn_bwd(proj, gq, gk, sinks, o, lse, do, *, name):
    n_rows = proj.shape[0]
    q_spec, kv_specs, small = _attn_specs()

    def body(q_ref, k0, k1, k2, v0, v1, v2, gq_ref, gk_ref, sink_ref, bias_ref, neg0_ref, o_ref, lse_ref, do_ref,
             dq_ref, dkb_ref, dvb_ref, dgq_ref, dgk_ref, dsink_ref, dk_ref, dv_ref):
        n = pl.program_id(0)

        @pl.when(n == 0)
        def _():
            dk_ref[...] = jnp.zeros_like(dk_ref)
            dv_ref[...] = jnp.zeros_like(dv_ref)
            dgq_ref[...] = jnp.zeros_like(dgq_ref)
            dgk_ref[...] = jnp.zeros_like(dgk_ref)
            dsink_ref[...] = jnp.zeros_like(dsink_ref)

        dq_parts, dk_parts, dv_parts, dsink_parts = [], [], [], []
        dgq = jnp.zeros((1, HEAD_DIM), F32)
        dgk = jnp.zeros((1, HEAD_DIM), F32)
        for kv in range(N_KV_HEADS):
            qs, kcat, qn, kn, rq, rk, s, sink = _attn_scores(q_ref, (k0, k1, k2), gq_ref, gk_ref, sink_ref, bias_ref, neg0_ref, kv, n)
            vcat = jnp.concatenate([r[:, kv * HEAD_DIM:(kv + 1) * HEAD_DIM] for r in (v0, v1, v2)], axis=0)
            os_ = _stack_heads(o_ref[...], kv)
            dos = _stack_heads(do_ref[...], kv)
            lse = jnp.concatenate([lse_ref[:, kv * KV_GROUP + g:kv * KV_GROUP + g + 1] for g in range(KV_GROUP)], axis=0)
            p = jnp.exp(s - lse)
            delta = jnp.sum(dos * os_, axis=-1, keepdims=True)
            dp = lax.dot_general(dos.astype(BF16), vcat.astype(BF16), _DN_NT, preferred_element_type=F32)
            ds = (p * (dp - delta)) * _SCALE
            dsink_rows = -jnp.exp(sink - lse) * delta
            dsink_parts += [jnp.sum(dsink_rows[g * BLOCK:(g + 1) * BLOCK], axis=0, keepdims=True) for g in range(KV_GROUP)]
            dv_parts.append(lax.dot_general(p.astype(BF16), dos.astype(BF16), _DN_TN, preferred_element_type=F32))
            dsb = ds.astype(BF16)
            dqn = jnp.dot(dsb, kn.astype(BF16), preferred_element_type=F32)
            dkn = lax.dot_general(dsb, qn.astype(BF16), _DN_TN, preferred_element_type=F32)
            qh = qs * rq
            gd = dqn * gq_ref[...]
            dqs = rq * (gd - qh * jnp.mean(gd * qh, axis=-1, keepdims=True))
            dgq = dgq + jnp.sum(dqn * qh, axis=0, keepdims=True)
            kh = kcat * rk
            gdk = dkn * gk_ref[...]
            dk_parts.append(rk * (gdk - kh * jnp.mean(gdk * kh, axis=-1, keepdims=True)))
            dgk = dgk + jnp.sum(dkn * kh, axis=0, keepdims=True)
            dq_parts += [dqs[g * BLOCK:(g + 1) * BLOCK] for g in range(KV_GROUP)]
        dq_ref[...] = jnp.concatenate(dq_parts, axis=1).astype(BF16)
        dkc = jnp.concatenate(dk_parts, axis=1)
        dvc = jnp.concatenate(dv_parts, axis=1)
        prev = pl.multiple_of(jnp.maximum(n - 1, 0) * BLOCK, BLOCK)
        cur = pl.multiple_of(n * BLOCK, BLOCK)
        for acc_ref, val in ((dk_ref, dkc), (dv_ref, dvc)):
            acc_ref[0:BLOCK, :] += val[0:BLOCK]
            acc_ref[pl.ds(prev, BLOCK), :] += val[BLOCK:2 * BLOCK]
            acc_ref[pl.ds(cur, BLOCK), :] += val[2 * BLOCK:3 * BLOCK]
        dgq_ref[...] += dgq
        dgk_ref[...] += dgk
        dsink_ref[...] += jnp.concatenate(dsink_parts, axis=1)

        @pl.when(n == pl.num_programs(0) - 1)
        def _():
            dkb_ref[...] = dk_ref[...].astype(BF16)
            dvb_ref[...] = dv_ref[...].astype(BF16)

    blk = lambda w: pl.BlockSpec((BLOCK, w), lambda n: (n, 0))
    full = lambda r, w: pl.BlockSpec((r, w), lambda n: (0, 0))
    return pl.pallas_call(
        body, name=name, grid=(n_rows // BLOCK,),
        in_specs=[q_spec] + kv_specs + small + [blk(ATTN_WIDTH), blk(N_HEADS), blk(ATTN_WIDTH)],
        out_specs=[blk(ATTN_WIDTH), full(n_rows, KV_WIDTH), full(n_rows, KV_WIDTH),
                   full(1, HEAD_DIM), full(1, HEAD_DIM), full(1, N_HEADS)],
        out_shape=[jax.ShapeDtypeStruct((n_rows, ATTN_WIDTH), BF16), jax.ShapeDtypeStruct((n_rows, KV_WIDTH), BF16),
                   jax.ShapeDtypeStruct((n_rows, KV_WIDTH), BF16), jax.ShapeDtypeStruct((1, HEAD_DIM), F32),
                   jax.ShapeDtypeStruct((1, HEAD_DIM), F32), jax.ShapeDtypeStruct((1, N_HEADS), F32)],
        scratch_shapes=[pltpu.VMEM((n_rows, KV_WIDTH), F32), pltpu.VMEM((n_rows, KV_WIDTH), F32)],
        compiler_params=_cparams(("arbitrary",)),
    )(proj, proj, proj, proj, proj, proj, proj, gq, gk, sinks, *_attn_bias(), o, lse, do)


SSM_LAGS = 8
SLAB_G = 128 // SSM_GROUP_CH
N_SLABS = SSM_GROUPS // SLAB_G
SLAB_STATE = SLAB_G * SSM_STATE
U_COL = (ATTN_WIDTH + 2 * KV_WIDTH) // 128


def _ssm_prep(lam_re, lam_im, log_step, b_re, b_im, c_re, c_im):
    lam = lax.complex(lam_re, lam_im)
    delta = jnp.exp(log_step)[:, None]
    lam_bar = jnp.exp(lam * delta)
    b_t = lax.complex(jnp.swapaxes(b_re, 1, 2), jnp.swapaxes(b_im, 1, 2))
    b_bar = ((lam_bar - 1.0) / lam)[:, None, :] * b_t
    pw = [jnp.ones_like(lam_bar)]
    for _ in range(SSM_LAGS):
        pw.append(pw[-1] * lam_bar)
    w = jnp.stack(pw[:SSM_LAGS])[:, :, None, :] * b_bar[None]
    wri = jnp.stack([jnp.real(w), jnp.imag(w)], axis=3)
    wc = wri.reshape(SSM_LAGS, N_SLABS, SLAB_G * SSM_GROUP_CH, 2 * SSM_STATE)
    wc = jnp.swapaxes(wc, 0, 1).reshape(N_SLABS, SSM_LAGS * 128, 2 * SSM_STATE)
    cri = jnp.stack([c_re, -c_im], axis=2).reshape(N_SLABS, SLAB_G, SSM_GROUP_CH, 2, 1, SSM_STATE)
    eye = jnp.eye(SLAB_G, dtype=F32).reshape(1, SLAB_G, 1, 1, SLAB_G, 1)
    ct = (cri * eye).reshape(N_SLABS, 128, 2 * SLAB_STATE)
    l8 = pw[SSM_LAGS]
    lam8 = jnp.concatenate([jnp.real(l8).reshape(N_SLABS, 1, SLAB_STATE), jnp.imag(l8).reshape(N_SLABS, 1, SLAB_STATE)], axis=2)
    return wc, ct, lam8


def _row_group():
    return (lax.broadcasted_iota(jnp.int32, (SSM_LAGS * 128, 1), 0) // SSM_GROUP_CH) % SLAB_G


def _spread_groups(wc):
    g_of_row = _row_group()
    return jnp.concatenate([jnp.where(g_of_row == g, wc[:, r * SSM_STATE:(r + 1) * SSM_STATE], 0.0)
                            for r in range(2) for g in range(SLAB_G)], axis=1)


def _gather_groups(dw):
    g_of_row = _row_group()
    parts = []
    for r in range(2):
        acc = jnp.zeros((SSM_LAGS * 128, SSM_STATE), F32)
        for g in range(SLAB_G):
            c0 = r * SLAB_STATE + g * SSM_STATE
            acc = acc + jnp.where(g_of_row == g, dw[:, c0:c0 + SSM_STATE], 0.0)
        parts.append(acc)
    return jnp.concatenate(parts, axis=1)


def _lagged(u, up, t_rows):
    ue = jnp.concatenate([up, u], axis=0)
    return jnp.concatenate([ue[SSM_LAGS - tau:SSM_LAGS - tau + t_rows] for tau in range(SSM_LAGS)], axis=1).astype(BF16)


def _ssm_fwd(proj, wc, cmat, lam8, dvec, *, name):
    n_rows = proj.shape[0]
    tt = _row_tile(n_rows, 1408)
    n_t = n_rows // tt
    sw = 2 * SLAB_STATE
    hs = SLAB_STATE

    def body(u_ref, up_ref, wc_ref, c_ref, l_ref, d_ref, y_ref, x_ref, carry_ref, w_ref):
        t = pl.program_id(1)

        @pl.when(t == 0)
        def _():
            carry_ref[...] = jnp.zeros_like(carry_ref)
            w_ref[...] = _spread_groups(wc_ref[...]).astype(BF16)

        u = u_ref[...]
        up = jnp.where(t > 0, up_ref[...], 0.0)
        x_ref[...] = jnp.dot(_lagged(u, up, tt), w_ref[...], preferred_element_type=F32)
        ar = jnp.broadcast_to(l_ref[:, :hs], (8, hs))
        ai = jnp.broadcast_to(l_ref[:, hs:], (8, hs))

        def step(b, c):
            xr, xi = c
            r0 = pl.multiple_of(b * 8, 8)
            w = x_ref[pl.ds(r0, 8), :]
            nr = w[:, :hs] + ar * xr - ai * xi
            ni = w[:, hs:] + ar * xi + ai * xr
            x_ref[pl.ds(r0, 8), :] = jnp.concatenate([nr, ni], axis=1)
            return nr, ni

        xr, xi = lax.fori_loop(0, tt // 8, step, (carry_ref[:, :hs], carry_ref[:, hs:]), unroll=8)
        carry_ref[...] = jnp.concatenate([xr, xi], axis=1)
        y_ref[...] = lax.dot_general(x_ref[...].astype(BF16), c_ref[...], _DN_NT, preferred_element_type=F32) + d_ref[...] * u

    return pl.pallas_call(
        body, name=name, grid=(N_SLABS, n_t),
        in_specs=[pl.BlockSpec((tt, 128), lambda j, t: (t, U_COL + j)),
                  pl.BlockSpec((8, 128), lambda j, t: (jnp.maximum(t * (tt // 8) - 1, 0), U_COL + j)),
                  pl.BlockSpec((None, SSM_LAGS * 128, 2 * SSM_STATE), lambda j, t: (j, 0, 0)),
                  pl.BlockSpec((None, 128, sw), lambda j, t: (j, 0, 0)),
                  pl.BlockSpec((None, 1, sw), lambda j, t: (j, 0, 0)),
                  pl.BlockSpec((1, 128), lambda j, t: (0, j))],
        out_specs=[pl.BlockSpec((tt, 128), lambda j, t: (t, j)), pl.BlockSpec((tt, sw), lambda j, t: (t, j))],
        out_shape=[jax.ShapeDtypeStruct((n_rows, SSM_WIDTH), F32), jax.ShapeDtypeStruct((n_rows, N_SLABS * sw), F32)],
        scratch_shapes=[pltpu.VMEM((8, sw), F32), pltpu.VMEM((SSM_LAGS * 128, sw), BF16)],
        compiler_params=_cparams(("parallel", "arbitrary")),
    )(proj, proj, wc, cmat, lam8, dvec)


def _ssm_bwd(proj, xs, dy, wc, cmat, lam8, dvec, *, name, deps=()):
    n_rows = proj.shape[0]
    tt = _row_tile(n_rows, 704)
    n_t = n_rows // tt
    sw = 2 * SLAB_STATE
    hs = SLAB_STATE

    def body(u_ref, up_ref, x_ref, xp_ref, dy_ref, wc_ref, c_ref, l_ref, d_ref,
             *rest):
        du_ref, dwc_ref, dc_ref, dl_ref, dd_ref, a_ref, carry_ref, head_ref, w_ref, dw_ref = rest[len(deps):]
        t = pl.program_id(1)
        ti = n_t - 1 - t

        @pl.when(t == 0)
        def _():
            w_ref[...] = _spread_groups(wc_ref[...]).astype(BF16)
            carry_ref[...] = jnp.zeros_like(carry_ref)
            head_ref[...] = jnp.zeros_like(head_ref)
            dw_ref[...] = jnp.zeros_like(dw_ref)
            dc_ref[...] = jnp.zeros_like(dc_ref)
            dl_ref[...] = jnp.zeros_like(dl_ref)
            dd_ref[...] = jnp.zeros_like(dd_ref)

        u = u_ref[...]
        up = jnp.where(ti > 0, up_ref[...], 0.0)
        ucat = _lagged(u, up, tt)
        dyv = dy_ref[...]
        dyb = dyv.astype(BF16)
        a_ref[...] = jnp.dot(dyb, c_ref[...], preferred_element_type=F32)
        lr = jnp.broadcast_to(l_ref[:, :hs], (8, hs))
        li = jnp.broadcast_to(l_ref[:, hs:], (8, hs))

        def step(i, c):
            cr, ci = c
            r0 = pl.multiple_of((tt // 8 - 1 - i) * 8, 8)
            g = a_ref[pl.ds(r0, 8), :]
            nr = g[:, :hs] + lr * cr + li * ci
            ni = g[:, hs:] + lr * ci - li * cr
            a_ref[pl.ds(r0, 8), :] = jnp.concatenate([nr, ni], axis=1)
            return nr, ni

        cr, ci = lax.fori_loop(0, tt // 8, step, (carry_ref[:, :hs], carry_ref[:, hs:]), unroll=8)
        carry_ref[...] = jnp.concatenate([cr, ci], axis=1)

        a = a_ref[...]
        xv = x_ref[...]
        xprev = jnp.where(ti > 0, xp_ref[...], 0.0)
        xsh = jnp.concatenate([xprev, xv[:tt - SSM_LAGS]], axis=0)
        a_re, a_im, x_re, x_im = a[:, :hs], a[:, hs:], xsh[:, :hs], xsh[:, hs:]
        dl_ref[...] += jnp.concatenate([jnp.sum(a_re * x_re + a_im * x_im, axis=0, keepdims=True),
                                        jnp.sum(a_im * x_re - a_re * x_im, axis=0, keepdims=True)], axis=1)
        ab = a.astype(BF16)
        dw_ref[...] += lax.dot_general(ucat, ab, _DN_TN, preferred_element_type=F32)
        duc = lax.dot_general(ab, w_ref[...], _DN_NT, preferred_element_type=F32)
        ext = jnp.concatenate([duc, head_ref[...]], axis=0)
        du = d_ref[...] * dyv
        for tau in range(SSM_LAGS):
            du = du + ext[tau:tau + tt, tau * 128:(tau + 1) * 128]
        head_ref[...] = duc[0:8]
        row = ti * tt + lax.broadcasted_iota(jnp.int32, (tt, 128), 0)
        du_ref[...] = jnp.where(row >= PAD, du, 0.0).astype(BF16)
        dd_ref[...] += jnp.sum(dyv * u, axis=0, keepdims=True)
        dc_ref[...] += lax.dot_general(dyb, xv.astype(BF16), _DN_TN, preferred_element_type=F32)

        @pl.when(t == n_t - 1)
        def _():
            dwc_ref[...] = _gather_groups(dw_ref[...])

    rt = lambda t: n_t - 1 - t
    prev8 = lambda t: jnp.maximum(rt(t) * (tt // 8) - 1, 0)
    return pl.pallas_call(
        body, name=name, grid=(N_SLABS, n_t),
        in_specs=[pl.BlockSpec((tt, 128), lambda j, t: (rt(t), U_COL + j)),
                  pl.BlockSpec((8, 128), lambda j, t: (prev8(t), U_COL + j)),
                  pl.BlockSpec((tt, sw), lambda j, t: (rt(t), j)),
                  pl.BlockSpec((8, sw), lambda j, t: (prev8(t), j)),
                  pl.BlockSpec((tt, 128), lambda j, t: (rt(t), j)),
                  pl.BlockSpec((None, SSM_LAGS * 128, 2 * SSM_STATE), lambda j, t: (j, 0, 0)),
                  pl.BlockSpec((None, 128, sw), lambda j, t: (j, 0, 0)),
                  pl.BlockSpec((None, 1, sw), lambda j, t: (j, 0, 0)),
                  pl.BlockSpec((1, 128), lambda j, t: (0, j))] + [_ANY] * len(deps),
        out_specs=[pl.BlockSpec((tt, 128), lambda j, t: (rt(t), j)),
                   pl.BlockSpec((None, SSM_LAGS * 128, 2 * SSM_STATE), lambda j, t: (j, 0, 0)),
                   pl.BlockSpec((None, 128, sw), lambda j, t: (j, 0, 0)),
                   pl.BlockSpec((None, 1, sw), lambda j, t: (j, 0, 0)),
                   pl.BlockSpec((1, 128), lambda j, t: (0, j))],
        out_shape=[jax.ShapeDtypeStruct((n_rows, SSM_WIDTH), BF16),
                   jax.ShapeDtypeStruct((N_SLABS, SSM_LAGS * 128, 2 * SSM_STATE), F32),
                   jax.ShapeDtypeStruct((N_SLABS, 128, sw), F32),
                   jax.ShapeDtypeStruct((N_SLABS, 1, sw), F32),
                   jax.ShapeDtypeStruct((1, SSM_WIDTH), F32)],
        scratch_shapes=[pltpu.VMEM((tt, sw), F32), pltpu.VMEM((8, sw), F32), pltpu.VMEM((8, sw), F32),
                        pltpu.VMEM((SSM_LAGS * 128, sw), BF16), pltpu.VMEM((SSM_LAGS * 128, sw), F32)],
        compiler_params=_cparams(("parallel", "arbitrary")),
    )(proj, proj, xs, xs, dy, wc, cmat, lam8, dvec, *deps)


_GELU_C = math.sqrt(2.0 / math.pi)
_GELU_A = 0.044715


def _gelu(y):
    th = jnp.tanh(_GELU_C * (y + _GELU_A * y * y * y))
    return 0.5 * y * (1.0 + th), th


def _glu_fwd(y, w, b, *, name):
    n_rows, width = y.shape
    tr = _row_tile(n_rows, 384)

    def body(y_ref, w_ref, b_ref, o_ref):
        g, _ = _gelu(y_ref[...])
        z = jnp.dot(g.astype(BF16), w_ref[...], preferred_element_type=F32) + b_ref[...]
        o_ref[...] = g * jax.nn.sigmoid(z)

    return pl.pallas_call(
        body, name=name, grid=(n_rows // tr,),
        in_specs=[pl.BlockSpec((tr, width), lambda i: (i, 0)), pl.BlockSpec((width, width), lambda i: (0, 0)),
                  pl.BlockSpec((1, width), lambda i: (0, 0))],
        out_specs=pl.BlockSpec((tr, width), lambda i: (i, 0)),
        out_shape=jax.ShapeDtypeStruct((n_rows, width), F32),
        compiler_params=_cparams(("parallel",)),
    )(y, w, b)


def _glu_bwd(y, w, b, dout, *, name):
    n_rows, width = y.shape
    tr = _row_tile(n_rows, 384)

    def body(y_ref, w_ref, b_ref, do_ref, dy_ref, g_ref, dz_ref, db_ref):
        i = pl.program_id(0)
        yv = y_ref[...]
        g, th = _gelu(yv)
        gb = g.astype(BF16)
        z = jnp.dot(gb, w_ref[...], preferred_element_type=F32) + b_ref[...]
        sg = jax.nn.sigmoid(z)
        do = do_ref[...]
        dz = do * g * sg * (1.0 - sg)
        dzb = dz.astype(BF16)
        dg = do * sg + lax.dot_general(dzb, w_ref[...], _DN_NT, preferred_element_type=F32)
        dgelu = 0.5 * (1.0 + th) + 0.5 * yv * (1.0 - th * th) * _GELU_C * (1.0 + 3.0 * _GELU_A * yv * yv)
        dy_ref[...] = dg * dgelu
        g_ref[...] = gb
        dz_ref[...] = dzb
        part = jnp.sum(dz, axis=0, keepdims=True)

        @pl.when(i == 0)
        def _():
            db_ref[...] = part

        @pl.when(i > 0)
        def _():
            db_ref[...] += part

    row = pl.BlockSpec((tr, width), lambda i: (i, 0))
    vec = pl.BlockSpec((1, width), lambda i: (0, 0))
    return pl.pallas_call(
        body, name=name, grid=(n_rows // tr,),
        in_specs=[row, pl.BlockSpec((width, width), lambda i: (0, 0)), vec, row],
        out_specs=[row, row, row, vec],
        out_shape=[jax.ShapeDtypeStruct((n_rows, width), F32), jax.ShapeDtypeStruct((n_rows, width), BF16),
                   jax.ShapeDtypeStruct((n_rows, width), BF16), jax.ShapeDtypeStruct((1, width), F32)],
        compiler_params=_cparams(("arbitrary",)),
    )(y, w, b, dout)


def _loss_head(h, target, *, name):
    n_rows, width = h.shape

    def body(h_ref, t_ref, dh_ref, dhb_ref, loss_ref):
        i = pl.program_id(0)

        @pl.when(i == 0)
        def _():
            dh_ref[...] = jnp.zeros_like(dh_ref)
            dhb_ref[...] = jnp.zeros_like(dhb_ref)
            loss_ref[...] = jnp.zeros_like(loss_ref)

        @pl.when(i > 0)
        def _():
            err = h_ref[...] - t_ref[...]
            dh = err * (1.0 / width)
            dh_ref[...] = dh
            dhb_ref[...] = dh.astype(BF16)
            loss_ref[...] += (0.5 / width) * jnp.sum(err * err, keepdims=True)

    return pl.pallas_call(
        body, name=name, grid=(n_rows // BLOCK,),
        in_specs=[pl.BlockSpec((BLOCK, width), lambda i: (i, 0)),
                  pl.BlockSpec((BLOCK, width), lambda i: (jnp.maximum(i - 1, 0), 0))],
        out_specs=[pl.BlockSpec((BLOCK, width), lambda i: (i, 0)), pl.BlockSpec((BLOCK, width), lambda i: (i, 0)),
                   pl.BlockSpec((1, 1), lambda i: (0, 0))],
        out_shape=[jax.ShapeDtypeStruct((n_rows, width), F32), jax.ShapeDtypeStruct((n_rows, width), BF16),
                   jax.ShapeDtypeStruct((1, 1), F32)],
        compiler_params=_cparams(("arbitrary",)),
    )(h, target)


def _elem_rows(n_rows, n_cols, bytes_per_row_elem):
    lanes = -(-n_cols // 128) * 128
    cap = max(16, (12 * 1024 * 1024) // (lanes * bytes_per_row_elem))
    best = None
    for t in range(16, min(n_rows, cap) + 1, 16):
        if n_rows % t == 0:
            best = t
    return best or n_rows


def _cast_bf16(x, *, name):
    n_rows, n_cols = x.shape
    tr = _elem_rows(n_rows, n_cols, 4)

    def body(x_ref, o_ref):
        o_ref[...] = x_ref[...].astype(BF16)

    spec = pl.BlockSpec((tr, n_cols), lambda i: (i, 0))
    return pl.pallas_call(body, name=name, grid=(n_rows // tr,), in_specs=[spec], out_specs=spec,
                          out_shape=jax.ShapeDtypeStruct(x.shape, BF16), compiler_params=_cparams(("parallel",)))(x)


def _adamw(w, m, v, parts, *, name):
    n_rows, n_cols = w.shape
    n_parts = parts.shape[0]
    tr = _elem_rows(n_rows, n_cols, 4 * (8 + n_parts))
    c1 = 1.0 / (1.0 - ADAM_B1 ** ADAM_STEP)
    c2 = 1.0 / (1.0 - ADAM_B2 ** ADAM_STEP)

    def body(w_ref, m_ref, v_ref, p_ref, g_ref, d_ref, nm_ref, nv_ref):
        g = p_ref[0].astype(F32)
        for k in range(1, n_parts):
            g = g + p_ref[k].astype(F32)
        nm = ADAM_B1 * m_ref[...] + (1.0 - ADAM_B1) * g
        nv = ADAM_B2 * v_ref[...] + (1.0 - ADAM_B2) * (g * g)
        g_ref[...] = g
        nm_ref[...] = nm
        nv_ref[...] = nv
        d_ref[...] = -ADAM_LR * ((nm * c1) / (jnp.sqrt(nv * c2) + ADAM_EPS) + ADAM_WD * w_ref[...])

    spec = pl.BlockSpec((tr, n_cols), lambda i: (i, 0))
    return pl.pallas_call(
        body, name=name, grid=(n_rows // tr,),
        in_specs=[spec, spec, spec, pl.BlockSpec((n_parts, tr, n_cols), lambda i: (0, i, 0))],
        out_specs=[spec] * 4, out_shape=[jax.ShapeDtypeStruct(w.shape, F32)] * 4,
        compiler_params=_cparams(("parallel",)),
    )(w, m, v, parts)


def _sum_parts(parts, *, name):
    n_parts, n_rows, n_cols = parts.shape
    tr = _elem_rows(n_rows, n_cols, 4 * (1 + n_parts))

    def body(p_ref, o_ref):
        g = p_ref[0].astype(F32)
        for k in range(1, n_parts):
            g = g + p_ref[k].astype(F32)
        o_ref[...] = g

    return pl.pallas_call(
        body, name=name, grid=(n_rows // tr,),
        in_specs=[pl.BlockSpec((n_parts, tr, n_cols), lambda i: (0, i, 0))],
        out_specs=pl.BlockSpec((tr, n_cols), lambda i: (i, 0)),
        out_shape=jax.ShapeDtypeStruct((n_rows, n_cols), F32), compiler_params=_cparams(("parallel",)),
    )(parts)


BIG = ("w_in", "w_glu", "w_out", "w_up", "w_down")
SMALL = ("norm_mix_g", "q_norm_g", "k_norm_g", "attn_sinks", "ssm_lambda_re", "ssm_lambda_im", "ssm_log_step",
         "ssm_b_re", "ssm_b_im", "ssm_c_re", "ssm_c_im", "ssm_d", "b_glu", "attn_out_g", "ssm_out_g", "norm_mlp_g")
_SSM_NAMES = ("ssm_lambda_re", "ssm_lambda_im", "ssm_log_step", "ssm_b_re", "ssm_b_im", "ssm_c_re", "ssm_c_im")


def _divisor(n, cands):
    for c in cands:
        if n % c == 0:
            return c
    return n


def _mm(a, b, mode, name, outs=(F32,), epilogue=_ident, tiles=(), deps=(), blocked=False):
    if mode == "nn":
        m, k = a.shape
        n = b.shape[0] * b.shape[2] if blocked else b.shape[1]
    elif mode == "nt":
        m, k = a.shape
        n = b.shape[1] if blocked else b.shape[0]
    else:
        (k, m), n = a.shape, b.shape[1]
    if mode == "tn":
        tm, tn, tk = _divisor(m, (1024, 512)), _divisor(n, (1024, 512)), k
    elif k <= 2560:
        tm, tn, tk = _row_tile(m, 1408), _divisor(n, (1024, 1280, 512)), k
    elif blocked:
        tm, tn, tk = _divisor(m, (2112,)), _divisor(n, (1024, 512)), b.shape[2]
    else:
        tm, tn, tk = _row_tile(m, 1408), _divisor(n, (1024, 512)), _divisor(k, (1024, 512))
    return _matmul(a, b, mode=mode, tm=tm, tn=tn, tk=tk, outs=list(outs), epilogue=epilogue, tiles=tiles, deps=deps,
                   blocked=blocked, name=name)


def _add_tile(acc, res):
    return (acc + res,)


def _relu_sq(acc):
    r = jnp.maximum(acc, 0.0)
    return r, r * r


def _relu_sq_bwd(acc, r):
    return (acc * (2.0 * r.astype(F32)),)


def _row(v):
    return v.reshape(1, -1)


def _layer_fwd(hres, fetch, sp, l, deps=()):
    tag = f"_l{l}"
    wts = {}
    hb = _rmsnorm_fwd([hres], [_row(sp["norm_mix_g"])], name="norm_mix" + tag, deps=deps)
    wts["w_in"] = fetch("w_in", hb)
    proj, = _mm(hb, wts["w_in"], "nt", "proj" + tag)
    gq, gk, sinks = _row(sp["q_norm_g"]), _row(sp["k_norm_g"]), _row(sp["attn_sinks"])
    o, lse = _attn_fwd(proj, gq, gk, sinks, name="attn_fwd" + tag)
    (wc, cmat, lam8), prep_vjp = jax.vjp(_ssm_prep, *[sp[n] for n in _SSM_NAMES])
    cmat = cmat.astype(BF16)
    y, xs = _ssm_fwd(proj, wc, cmat, lam8, _row(sp["ssm_d"]), name="ssm_fwd" + tag)
    wts["w_glu"] = fetch("w_glu", y)
    s = _glu_fwd(y, wts["w_glu"], _row(sp["b_glu"]), name="glu_fwd" + tag)
    mix = _rmsnorm_fwd([o, s], [_row(sp["attn_out_g"]), _row(sp["ssm_out_g"])], name="norm_out" + tag)
    wts["w_out"] = fetch("w_out", mix)
    hres2, = _mm(mix, wts["w_out"], "nn", "out_proj" + tag, epilogue=_add_tile, tiles=(hres,))
    h2 = _rmsnorm_fwd([hres2], [_row(sp["norm_mlp_g"])], name="norm_mlp" + tag)
    wts["w_up"] = fetch("w_up", h2)
    r, act = _mm(h2, wts["w_up"], "nn", "mlp_up" + tag, outs=(BF16, BF16), epilogue=_relu_sq, blocked=True)
    wts["w_down"] = fetch("w_down", act)
    hres3, = _mm(act, wts["w_down"], "nn", "mlp_down" + tag, epilogue=_add_tile, tiles=(hres2,))
    saved = dict(wts=wts, hres=hres, hb=hb, proj=proj, o=o, lse=lse, wc=wc, cmat=cmat, lam8=lam8, prep_vjp=prep_vjp,
                 y=y, xs=xs, s=s, mix=mix, hres2=hres2, h2=h2, r=r, act=act)
    return hres3, saved


def _layer_bwd(dres, dres_b, sp, sv, l, early_grads, deps=()):
    tag = f"_l{l}"
    wts = sv["wts"]
    gb, gs = {}, {}
    d_up, = _mm(dres_b, wts["w_down"], "nt", "mlp_down_dx" + tag, outs=(BF16,), epilogue=_relu_sq_bwd, tiles=(sv["r"],),
                deps=deps)
    gb["w_down"], = _mm(sv["act"], dres_b, "tn", "mlp_down_dw" + tag, outs=(BF16,))
    gb["w_up"], = _mm(sv["h2"], d_up, "tn", "mlp_up_dw" + tag, outs=(BF16,), blocked=True)
    deps = early_grads(l, "a", {n: gb.pop(n) for n in ("w_up", "w_down")})
    dh2, = _mm(d_up, wts["w_up"], "nt", "mlp_up_dx" + tag, blocked=True, deps=deps)
    (dres2,), (dg,), dres2_b = _rmsnorm_bwd([sv["hres2"]], [_row(sp["norm_mlp_g"])], dh2, dres, name="norm_mlp_bwd" + tag)
    gs["norm_mlp_g"] = dg
    dmix, = _mm(dres2_b, wts["w_out"], "nt", "out_proj_dx" + tag)
    gb["w_out"], = _mm(sv["mix"], dres2_b, "tn", "out_proj_dw" + tag, outs=(BF16,))
    (do, ds), (dga, dgs) = _rmsnorm_bwd([sv["o"], sv["s"]], [_row(sp["attn_out_g"]), _row(sp["ssm_out_g"])], dmix, None,
                                        name="norm_out_bwd" + tag)
    gs["attn_out_g"], gs["ssm_out_g"] = dga, dgs
    dy, g_b, dz_b, db = _glu_bwd(sv["y"], wts["w_glu"], _row(sp["b_glu"]), ds, name="glu_bwd" + tag)
    gs["b_glu"] = db
    gb["w_glu"], = _mm(g_b, dz_b, "tn", "glu_dw" + tag, outs=(BF16,))
    deps = early_grads(l, "b", {n: gb.pop(n) for n in ("w_out", "w_glu")})
    du, dwc, dcmat, dlam8, dd = _ssm_bwd(sv["proj"], sv["xs"], dy, sv["wc"], sv["cmat"], sv["lam8"], _row(sp["ssm_d"]),
                                         name="ssm_bwd" + tag, deps=deps)
    gs["ssm_d"] = dd
    for n, g in zip(_SSM_NAMES, sv["prep_vjp"]((dwc, dcmat, dlam8))):
        gs[n] = g
    dq, dk, dv, dgq, dgk, dsinks = _attn_bwd(sv["proj"], _row(sp["q_norm_g"]), _row(sp["k_norm_g"]), _row(sp["attn_sinks"]),
                                             sv["o"], sv["lse"], do, name="attn_bwd" + tag)
    gs["q_norm_g"], gs["k_norm_g"], gs["attn_sinks"] = dgq, dgk, dsinks
    dproj = _concat_cols([dq, dk, dv, du], name="dproj" + tag)
    gb["w_in"], = _mm(dproj, sv["hb"], "tn", "proj_dw" + tag, outs=(BF16,))
    deps = early_grads(l, "c", {"w_in": gb.pop("w_in")})
    dh, = _mm(dproj, wts["w_in"], "nn", "proj_dx" + tag, deps=deps)
    (dres_in,), (dg,), dres_in_b = _rmsnorm_bwd([sv["hres"]], [_row(sp["norm_mix_g"])], dh, dres2, name="norm_mix_bwd" + tag)
    gs["norm_mix_g"] = dg
    return dres_in, dres_in_b, gs


def _local_step(x, target, meta, sp, weights_for_layer, early_grads, grads_of_layer):
    h = jnp.concatenate([jnp.zeros((PAD, x.shape[1]), F32), meta, x], axis=0)
    saved = []
    for l in range(DEPTH):
        fetch, deps = weights_for_layer(l, h)
        h, sv = _layer_fwd(h, fetch, {n: sp[n][l] for n in SMALL}, l, deps)
        saved.append(sv)
    dh, dh_b, loss = _loss_head(h, target, name="loss_head")
    gsmall = {n: [None] * DEPTH for n in SMALL}
    deps = ()
    for l in reversed(range(DEPTH)):
        dh, dh_b, gs = _layer_bwd(dh, dh_b, {n: sp[n][l] for n in SMALL}, saved[l], l, early_grads, deps)
        deps = grads_of_layer(l, dh)
        for n in SMALL:
            gsmall[n][l] = gs[n].reshape(sp[n][l].shape)
    return loss, dh, gsmall


def _all_gather(x, *, name):
    def body(x_ref, out_ref, send_sems, recv_sems, local_sem):
        x, y, c = lax.axis_index("x"), lax.axis_index("y"), lax.axis_index("c")
        me, sibling = (x, y, c), (x, y, 1 - c)
        chips = [(1 - x, y), (x, 1 - y), (1 - x, 1 - y)]

        def slot(px, py, pc):
            return out_ref.at[4 * px + 2 * py + pc]

        def copy(k, block, to, src=None):
            return pltpu.make_async_remote_copy(
                src_ref=slot(*block) if src is None else src, dst_ref=slot(*block),
                send_sem=send_sems.at[k], recv_sem=recv_sems.at[k], device_id=to, device_id_type=_MESH)

        mine = pltpu.make_async_copy(x_ref, slot(*me), local_sem)
        mine.start()
        first = [copy(0, me, sibling, src=x_ref)]
        first += [copy(1 + j, me, (*chip, c), src=x_ref) for j, chip in enumerate(chips)]
        for cp in first:
            cp.start()
        passed = [copy(4 + j, (*chip, c), sibling) for j, chip in enumerate(chips)]
        for j, chip in enumerate(chips):
            copy(1 + j, (*chip, c), me).wait_recv()
            passed[j].start()
        copy(0, sibling, me).wait_recv()
        for j, chip in enumerate(chips):
            copy(4 + j, (*chip, 1 - c), me).wait_recv()
        for cp in first + passed:
            cp.wait_send()
        mine.wait()

    return pl.pallas_call(
        body, name=name, out_shape=jax.ShapeDtypeStruct((N_DEV,) + x.shape, x.dtype),
        in_specs=[_ANY], out_specs=_ANY,
        scratch_shapes=[pltpu.SemaphoreType.DMA((7,)), pltpu.SemaphoreType.DMA((7,)), pltpu.SemaphoreType.DMA],
    )(x)


def _exchange(g, *, name):
    def body(g_ref, r_ref, send_sems, recv_sems, local_sem):
        x, y, c = lax.axis_index("x"), lax.axis_index("y"), lax.axis_index("c")
        me = 4 * x + 2 * y + c
        mine = pltpu.make_async_copy(g_ref.at[me], r_ref.at[me], local_sem)
        mine.start()

        def peer(k):
            px, py, pc = (x + (k >> 2)) % 2, (y + ((k >> 1) & 1)) % 2, (c + (k & 1)) % 2
            return (px, py, pc), 4 * px + 2 * py + pc

        def copy(k, src_block, dst_block):
            to, _ = peer(k)
            return pltpu.make_async_remote_copy(
                src_ref=g_ref.at[src_block], dst_ref=r_ref.at[dst_block],
                send_sem=send_sems.at[k - 1], recv_sem=recv_sems.at[k - 1], device_id=to, device_id_type=_MESH)

        sends = [copy(k, peer(k)[1], me) for k in range(1, N_DEV)]
        for cp in sends:
            cp.start()
        for k in range(1, N_DEV):
            copy(k, me, peer(k)[1]).wait_recv()
        for cp in sends:
            cp.wait_send()
        mine.wait()

    return pl.pallas_call(
        body, name=name, out_shape=jax.ShapeDtypeStruct(g.shape, g.dtype),
        in_specs=[_ANY], out_specs=_ANY,
        scratch_shapes=[pltpu.SemaphoreType.DMA((7,)), pltpu.SemaphoreType.DMA((7,)), pltpu.SemaphoreType.DMA],
    )(g)


_HBM = pl.BlockSpec(memory_space=pltpu.HBM)
_SEM = pl.BlockSpec(memory_space=pltpu.SEMAPHORE)
_EFFECT = pltpu.SideEffectType.DATAFLOW_SIDE_EFFECTING
N_PEERS = N_DEV - 1


def _me_and_peers():
    x, y, c = lax.axis_index("x"), lax.axis_index("y"), lax.axis_index("c")
    peers = []
    for k in range(1, N_DEV):
        px, py, pc = (x + (k >> 2)) % 2, (y + ((k >> 1) & 1)) % 2, (c + (k & 1)) % 2
        peers.append(((px, py, pc), 4 * px + 2 * py + pc))
    return 4 * x + 2 * y + c, peers


def _send_start(srcs, after, *, per_peer, name):
    n_t = len(srcs)
    blks = [s.shape[1:] if per_peer else s.shape for s in srcs]
    lands = [lax.empty((N_DEV,) + b, s.dtype) for b, s in zip(blks, srcs)]

    def body(*refs):
        src_refs, land_refs = refs[:n_t], refs[n_t:2 * n_t]
        send_sems, recv_sems = refs[2 * n_t + 1], refs[2 * n_t + 2]
        token = refs[-1]
        me, peers = _me_and_peers()
        for t in range(n_t):
            for k, (to, idx) in enumerate(peers):
                pltpu.make_async_remote_copy(
                    src_ref=src_refs[t].at[idx] if per_peer else src_refs[t], dst_ref=land_refs[t].at[me],
                    send_sem=send_sems.at[t * N_PEERS + k], recv_sem=recv_sems.at[t * N_PEERS + k],
                    device_id=to, device_id_type=_MESH).start()
        token[...] = jnp.zeros_like(token)

    sems = pltpu.SemaphoreType.DMA((n_t * N_PEERS,))
    outs = pl.pallas_call(
        body, name=name,
        out_shape=(sems, sems, *[pltpu.HBM(s.shape, s.dtype) for s in srcs], *[pltpu.HBM(z.shape, z.dtype) for z in lands],
                   jax.ShapeDtypeStruct((8, 128), F32)),
        in_specs=[_HBM] * (2 * n_t) + [_ANY],
        out_specs=(_SEM, _SEM, *[_HBM] * (2 * n_t), pl.BlockSpec(memory_space=pltpu.VMEM)),
        input_output_aliases={i: 2 + i for i in range(2 * n_t)},
        compiler_params=pltpu.CompilerParams(has_side_effects=_EFFECT),
    )(*[pltpu.with_memory_space_constraint(s, pltpu.HBM) for s in srcs],
      *[pltpu.with_memory_space_constraint(z, pltpu.HBM) for z in lands], after)
    return outs[0], outs[1], list(outs[2:2 + n_t]), list(outs[2 + n_t:2 + 2 * n_t]), outs[-1]


def _send_wait(handles, after, *, per_peer, name):
    send_sems, recv_sems, srcs, lands = handles
    n_t = len(srcs)

    def body(*refs):
        src_refs, land_refs = refs[:n_t], refs[n_t:2 * n_t]
        send_sems, recv_sems = refs[2 * n_t], refs[2 * n_t + 1]
        _, peers = _me_and_peers()
        for t in range(n_t):
            for k, (to, idx) in enumerate(peers):
                cp = pltpu.make_async_remote_copy(
                    src_ref=src_refs[t].at[idx] if per_peer else src_refs[t], dst_ref=land_refs[t].at[idx],
                    send_sem=send_sems.at[t * N_PEERS + k], recv_sem=recv_sems.at[t * N_PEERS + k],
                    device_id=to, device_id_type=_MESH)
                cp.wait_send()
                cp.wait_recv()

    outs = pl.pallas_call(
        body, name=name,
        out_shape=(*[pltpu.HBM(s.shape, s.dtype) for s in srcs], *[pltpu.HBM(z.shape, z.dtype) for z in lands]),
        in_specs=[_HBM] * (2 * n_t) + [_SEM, _SEM, _ANY], out_specs=tuple([_HBM] * (2 * n_t)),
        input_output_aliases={i: i for i in range(2 * n_t)},
        compiler_params=pltpu.CompilerParams(has_side_effects=_EFFECT),
    )(*srcs, *lands, send_sems, recv_sems, after)
    me = 4 * lax.axis_index("x") + 2 * lax.axis_index("y") + lax.axis_index("c")
    filled = []
    for src, land in zip(outs[:n_t], outs[n_t:]):
        own = lax.dynamic_index_in_dim(src, me, 0, keepdims=False) if per_peer else src
        filled.append(lax.dynamic_update_index_in_dim(land, own, me, 0))
    return filled


def _adamw_layer(w, m, v, parts, l, prev, *, name):
    depth, n_rows, n_cols = w.shape
    n_parts = parts.shape[0]
    tr = _elem_rows(n_rows, n_cols, 4 * (8 + n_parts))
    c1 = 1.0 / (1.0 - ADAM_B1 ** ADAM_STEP)
    c2 = 1.0 / (1.0 - ADAM_B2 ** ADAM_STEP)
    n_prev = 0 if prev is None else 4

    def body(w_ref, m_ref, v_ref, p_ref, *rest):
        g_ref, d_ref, nm_ref, nv_ref = rest[n_prev:]
        g = p_ref[0].astype(F32)
        for k in range(1, n_parts):
            g = g + p_ref[k].astype(F32)
        nm = ADAM_B1 * m_ref[...] + (1.0 - ADAM_B1) * g
        nv = ADAM_B2 * v_ref[...] + (1.0 - ADAM_B2) * (g * g)
        g_ref[...] = g
        nm_ref[...] = nm
        nv_ref[...] = nv
        d_ref[...] = -ADAM_LR * ((nm * c1) / (jnp.sqrt(nv * c2) + ADAM_EPS) + ADAM_WD * w_ref[...])

    spec = pl.BlockSpec((None, tr, n_cols), lambda i: (l, i, 0))
    return pl.pallas_call(
        body, name=name, grid=(n_rows // tr,),
        in_specs=[spec, spec, spec, pl.BlockSpec((n_parts, tr, n_cols), lambda i: (0, i, 0))] + [_ANY] * n_prev,
        out_specs=[spec] * 4, out_shape=[jax.ShapeDtypeStruct(w.shape, F32)] * 4,
        input_output_aliases={4 + i: i for i in range(n_prev)},
        compiler_params=_cparams(("parallel",)),
    )(w, m, v, parts, *(prev or ()))


def _concat_cols(parts, *, name):
    n_rows = parts[0].shape[0]
    widths = [p.shape[1] for p in parts]
    tr = _row_tile(n_rows, 1408)

    def body(*refs):
        o_ref, off = refs[-1], 0
        for p_ref, w in zip(refs[:-1], widths):
            o_ref[:, off:off + w] = p_ref[...]
            off += w

    return pl.pallas_call(
        body, name=name, grid=(n_rows // tr,), in_specs=[pl.BlockSpec((tr, w), lambda i: (i, 0)) for w in widths],
        out_specs=pl.BlockSpec((tr, sum(widths)), lambda i: (i, 0)),
        out_shape=jax.ShapeDtypeStruct((n_rows, sum(widths)), parts[0].dtype), compiler_params=_cparams(("parallel",)))(*parts)


def _full_weights(g):
    return {n: v if n == "w_up" else v.reshape(N_DEV * v.shape[1], v.shape[2]) for n, v in g.items()}


def _grad_blocks(gb):
    return [g if n == "w_up" else g.reshape(N_DEV, g.shape[0] // N_DEV, g.shape[1]) for n, g in gb.items()]


_SMALL_ROWS = 1096


def _pack_small(d):
    flat = jnp.concatenate([d[n].reshape(-1) for n in SMALL])
    total = N_DEV * _SMALL_ROWS * 128
    assert flat.shape[0] <= total
    return jnp.pad(flat, (0, total - flat.shape[0])).reshape(N_DEV * _SMALL_ROWS, 128)


def _unpack_small(packed, like):
    flat = packed.reshape(-1)
    out, off = {}, 0
    for n in SMALL:
        size = like[n].size
        out[n] = flat[off:off + size].reshape(like[n].shape)
        off += size
    return out


def kernel(x, meta_tokens, norm_mix_g, w_in, q_norm_g, k_norm_g, attn_sinks, ssm_lambda_re, ssm_lambda_im, ssm_log_step, ssm_b_re, ssm_b_im, ssm_c_re, ssm_c_im, ssm_d, w_glu, b_glu, attn_out_g, ssm_out_g, w_out, norm_mlp_g, w_up, w_down, loss_target, m_meta_tokens, m_norm_mix_g, m_w_in, m_q_norm_g, m_k_norm_g, m_attn_sinks, m_ssm_lambda_re, m_ssm_lambda_im, m_ssm_log_step, m_ssm_b_re, m_ssm_b_im, m_ssm_c_re, m_ssm_c_im, m_ssm_d, m_w_glu, m_b_glu, m_attn_out_g, m_ssm_out_g, m_w_out, m_norm_mlp_g, m_w_up, m_w_down, v_meta_tokens, v_norm_mix_g, v_w_in, v_q_norm_g, v_k_norm_g, v_attn_sinks, v_ssm_lambda_re, v_ssm_lambda_im, v_ssm_log_step, v_ssm_b_re, v_ssm_b_im, v_ssm_c_re, v_ssm_c_im, v_ssm_d, v_w_glu, v_b_glu, v_attn_out_g, v_ssm_out_g, v_w_out, v_norm_mlp_g, v_w_up, v_w_down):
    a = dict(locals())
    order = ("meta_tokens", "norm_mix_g", "w_in", "q_norm_g", "k_norm_g", "attn_sinks", "ssm_lambda_re", "ssm_lambda_im",
             "ssm_log_step", "ssm_b_re", "ssm_b_im", "ssm_c_re", "ssm_c_im", "ssm_d", "w_glu", "b_glu", "attn_out_g",
             "ssm_out_g", "w_out", "norm_mlp_g", "w_up", "w_down")

    for n in ("w_in", "m_w_in", "v_w_in"):
        a[n] = jnp.swapaxes(a[n], 1, 2)
    no_dep = jnp.zeros((8, 128), F32)
    sp = {n: a[n] for n in SMALL}
    wb = {}
    for n in BIG:
        depth, r, c = a[n].shape
        wb[n] = _cast_bf16(a[n].reshape(depth * r, c), name="cast_" + n).reshape(depth, r, c)
    meta_all = _all_gather(meta_tokens, name="gather_meta")
    meta = jnp.transpose(meta_all, (1, 0, 2)).reshape(N_META, D_MODEL)

    gathers, exchanges = {}, {}
    updated = {n: None for n in BIG}
    groups = (("w_in",), ("w_glu", "w_out"), ("w_up",), ("w_down",))

    def start_gather(l, after):
        for gi, names in enumerate(groups):
            *handles, after = _send_start([wb[n][l] for n in names], after, per_peer=False, name=f"gather_start_l{l}_g{gi}")
            gathers[l, gi] = handles
        return after

    def weights_for_layer(l, h):
        token = start_gather(0, wb["w_in"]) if l == 0 else h
        if l + 1 < DEPTH:
            token = start_gather(l + 1, token)
        got = {}

        def fetch(name, after):
            if name not in got:
                gi = [name in names for names in groups].index(True)
                lands = _send_wait(gathers.pop((l, gi)), after, per_peer=False, name=f"gather_wait_l{l}_g{gi}")
                got.update(_full_weights(dict(zip(groups[gi], lands))))
            return got[name]

        return fetch, (token,)

    def update_layer(l, after):
        for part in ("a", "b", "c"):
            names, handles = exchanges.pop((l, part))
            recv = _send_wait(handles, after, per_peer=True, name=f"exchange_wait_l{l}_{part}")
            for n, parts in zip(names, recv):
                updated[n] = _adamw_layer(a[n], a["m_" + n], a["v_" + n], parts, l, updated[n], name=f"adamw_{n}_l{l}")

    def early_grads(l, part, gb):
        *handles, token = _send_start(_grad_blocks(gb), no_dep, per_peer=True, name=f"exchange_start_l{l}_{part}")
        exchanges[l, part] = (tuple(gb), handles)
        return (token,)

    def grads_of_layer(l, dh):
        if l + 1 < DEPTH:
            update_layer(l + 1, dh)
        return ()

    loss, dh0, gsmall = _local_step(x[0], loss_target[0], meta, sp, weights_for_layer, early_grads, grads_of_layer)
    loss = lax.psum(loss[0, 0], ("x", "y", "c"))
    grad, delta, new_m, new_v = {}, {}, {}, {}

    dmeta = jnp.transpose(dh0[PAD:BLOCK].reshape(N_META, N_DEV, D_MODEL // N_DEV), (1, 0, 2))
    outs = _adamw(meta_tokens, m_meta_tokens, v_meta_tokens, _exchange(dmeta, name="exchange_meta"), name="adamw_meta_tokens")
    grad["meta_tokens"], delta["meta_tokens"], new_m["meta_tokens"], new_v["meta_tokens"] = outs

    packed = _pack_small({n: jnp.stack(gsmall[n]) for n in SMALL}).reshape(N_DEV, _SMALL_ROWS, 128)
    share = _sum_parts(_exchange(packed, name="exchange_small"), name="sum_small")
    total = _all_gather(share, name="gather_small").reshape(1, N_DEV * _SMALL_ROWS, 128)
    gsum = _unpack_small(total, sp)
    for n in SMALL:
        as2d = lambda v: v.reshape(-1, v.shape[-1])
        outs = _adamw(as2d(a[n]), as2d(a["m_" + n]), as2d(a["v_" + n]), as2d(gsum[n])[None], name="adamw_" + n)
        grad[n], delta[n], new_m[n], new_v[n] = [o.reshape(a[n].shape) for o in outs]

    update_layer(0, outs[0])
    for n in BIG:
        grad[n], delta[n], new_m[n], new_v[n] = [jnp.swapaxes(o, 1, 2) if n == "w_in" else o for o in updated[n]]

    return (loss, dh0[BLOCK:][None], *[grad[n] for n in order], *[delta[n] for n in order],
            *[new_m[n] for n in order], *[new_v[n] for n in order])
```

```python
import math

import jax
import jax.numpy as jnp
from jax import lax
from jax.experimental import pallas as pl
from jax.experimental.pallas import tpu as pltpu

F32 = jnp.float32
BF16 = jnp.bfloat16

N_DEV = 8
D_MODEL = 2048
SEQ = 4096
DEPTH = 4
N_META = 16
HEAD_DIM = 64
ATTN_WIDTH = D_MODEL // 2
N_HEADS = ATTN_WIDTH // HEAD_DIM
N_KV_HEADS = N_HEADS // 4
KV_GROUP = N_HEADS // N_KV_HEADS
KV_WIDTH = N_KV_HEADS * HEAD_DIM
SSM_WIDTH = D_MODEL - ATTN_WIDTH
SSM_GROUP_CH = 16
SSM_GROUPS = SSM_WIDTH // SSM_GROUP_CH
SSM_STATE = 64
WINDOW = 128
BLOCK = 128
PAD = BLOCK - N_META
D_FF = 4 * D_MODEL
IN_WIDTH = ATTN_WIDTH + 2 * KV_WIDTH + SSM_WIDTH
NORM_EPS = 1e-6
NEG_INF = -1e30
ADAM_LR = 0.001
ADAM_B1 = 0.9
ADAM_B2 = 0.999
ADAM_EPS = 1e-08
ADAM_WD = 0.01
ADAM_STEP = 10

VMEM_LIMIT = 56 * 1024 * 1024
_MESH = pl.DeviceIdType.MESH
_ANY = pl.BlockSpec(memory_space=pl.ANY)


def _cparams(sem=None):
    return pltpu.CompilerParams(dimension_semantics=sem, vmem_limit_bytes=VMEM_LIMIT)


def _matmul(a, b, *, mode, tm, tn, tk, outs, epilogue, tiles=(), deps=(), blocked=False, name):
    if mode == "nn":
        m, k = a.shape
        if blocked:
            n = b.shape[0] * b.shape[2]
            assert tn == b.shape[2]
            b_spec = pl.BlockSpec((None, tk, tn), lambda i, j, kk: (j, kk, 0))
        else:
            n = b.shape[1]
            b_spec = pl.BlockSpec((tk, tn), lambda i, j, kk: (kk, j))
        a_spec = pl.BlockSpec((tm, tk), lambda i, j, kk: (i, kk))
        dims = (((1,), (0,)), ((), ()))
    elif mode == "nt":
        m, k = a.shape
        if blocked:
            n = b.shape[1]
            assert tk == b.shape[2] and k == b.shape[0] * b.shape[2]
            b_spec = pl.BlockSpec((None, tn, tk), lambda i, j, kk: (kk, j, 0))
        else:
            n = b.shape[0]
            b_spec = pl.BlockSpec((tn, tk), lambda i, j, kk: (j, kk))
        a_spec = pl.BlockSpec((tm, tk), lambda i, j, kk: (i, kk))
        dims = (((1,), (1,)), ((), ()))
    else:
        (k, m), n = a.shape, b.shape[1]
        a_spec = pl.BlockSpec((tk, tm), lambda i, j, kk: (kk, i))
        b_spec = pl.BlockSpec((tk, tn), lambda i, j, kk: (kk, j))
        dims = (((0,), (0,)), ((), ()))
    assert m % tm == 0 and n % tn == 0 and k % tk == 0, (name, m, n, k, tm, tn, tk)
    nk = k // tk
    n_tiles, n_outs, n_deps = len(tiles), len(outs), len(deps)

    def body(a_ref, b_ref, *rest):
        tile_refs = rest[:n_tiles]
        out_refs = rest[n_tiles + n_deps:n_tiles + n_deps + n_outs]

        def product():
            return lax.dot_general(a_ref[...].astype(BF16), b_ref[...].astype(BF16), dims, preferred_element_type=F32)

        def finish(acc):
            res = epilogue(acc, *[r[...] for r in tile_refs])
            for o_ref, o in zip(out_refs, res):
                o_ref[...] = o.astype(o_ref.dtype)

        if nk == 1:
            finish(product())
            return
        acc_ref = rest[-1]
        kk = pl.program_id(2)

        @pl.when(kk == 0)
        def _():
            acc_ref[...] = jnp.zeros_like(acc_ref)

        acc_ref[...] += product()

        @pl.when(kk == nk - 1)
        def _():
            finish(acc_ref[...])

    tile_spec = pl.BlockSpec((tm, tn), lambda i, j, kk: (i, j))
    if mode == "tn" and blocked:
        out_specs = [pl.BlockSpec((None, tm, tn), lambda i, j, kk: (j, i, 0))] * n_outs
        out_shape = [jax.ShapeDtypeStruct((n // tn, m, tn), dt) for dt in outs]
    else:
        out_specs = [tile_spec] * n_outs
        out_shape = [jax.ShapeDtypeStruct((m, n), dt) for dt in outs]
    return pl.pallas_call(
        body, name=name, grid=(m // tm, n // tn, nk),
        in_specs=[a_spec, b_spec] + [tile_spec] * n_tiles + [_ANY] * n_deps,
        out_specs=out_specs, out_shape=out_shape,
        scratch_shapes=[pltpu.VMEM((tm, tn), F32)] if nk > 1 else [],
        compiler_params=_cparams(("parallel", "parallel", "arbitrary")),
    )(a, b, *tiles, *deps)


def _ident(acc):
    return (acc,)


def _row_tile(n_rows, cap):
    best = BLOCK
    for t in range(BLOCK, cap + 1, BLOCK):
        if n_rows % t == 0:
            best = t
    return best


def _rmsnorm_fwd(xs, gs, *, name, deps=()):
    n_rows, width = xs[0].shape
    n = len(xs)
    tr = _row_tile(n_rows, 384)

    def body(*refs):
        o_ref = refs[-1]
        parts = []
        for x_ref, g_ref in zip(refs[:n], refs[n:2 * n]):
            x = x_ref[...]
            r = lax.rsqrt(jnp.mean(x * x, axis=-1, keepdims=True) + NORM_EPS)
            parts.append(x * r * g_ref[...])
        o_ref[...] = (parts[0] if n == 1 else jnp.concatenate(parts, axis=1)).astype(BF16)

    return pl.pallas_call(
        body, name=name, grid=(n_rows // tr,),
        in_specs=[pl.BlockSpec((tr, width), lambda i: (i, 0))] * n + [pl.BlockSpec((1, width), lambda i: (0, 0))] * n
        + [_ANY] * len(deps),
        out_specs=pl.BlockSpec((tr, n * width), lambda i: (i, 0)),
        out_shape=jax.ShapeDtypeStruct((n_rows, n * width), BF16),
        compiler_params=_cparams(("parallel",)),
    )(*xs, *gs, *deps)


def _rmsnorm_bwd(xs, gs, dy, res, *, name):
    n_rows, width = xs[0].shape
    n = len(xs)
    tr = _row_tile(n_rows, 384)
    has_res = res is not None

    def body(*refs):
        x_refs, g_refs, dy_ref = refs[:n], refs[n:2 * n], refs[2 * n]
        res_ref = refs[2 * n + 1] if has_res else None
        outs = refs[2 * n + 1 + int(has_res):]
        dx_refs, dg_refs = outs[:n], outs[n:2 * n]
        i = pl.program_id(0)
        for c in range(n):
            x = x_refs[c][...]
            d = dy_ref[:, c * width:(c + 1) * width]
            r = lax.rsqrt(jnp.mean(x * x, axis=-1, keepdims=True) + NORM_EPS)
            xh = x * r
            gd = d * g_refs[c][...]
            dx = r * (gd - xh * jnp.mean(gd * xh, axis=-1, keepdims=True))
            if has_res:
                dx = dx + res_ref[...]
                outs[2 * n][...] = dx.astype(BF16)
            dx_refs[c][...] = dx
            part = jnp.sum(d * xh, axis=0, keepdims=True)

            @pl.when(i == 0)
            def _():
                dg_refs[c][...] = part

            @pl.when(i > 0)
            def _():
                dg_refs[c][...] += part

    row_spec = pl.BlockSpec((tr, width), lambda i: (i, 0))
    vec_spec = pl.BlockSpec((1, width), lambda i: (0, 0))
    outs = pl.pallas_call(
        body, name=name, grid=(n_rows // tr,),
        in_specs=[row_spec] * n + [vec_spec] * n + [pl.BlockSpec((tr, n * width), lambda i: (i, 0))] + [row_spec] * int(has_res),
        out_specs=[row_spec] * n + [vec_spec] * n + [row_spec] * int(has_res),
        out_shape=[jax.ShapeDtypeStruct((n_rows, width), F32)] * n + [jax.ShapeDtypeStruct((1, width), F32)] * n
        + [jax.ShapeDtypeStruct((n_rows, width), BF16)] * int(has_res),
        compiler_params=_cparams(("arbitrary",)),
    )(*xs, *gs, dy, *([res] if has_res else []))
    if has_res:
        return outs[:n], outs[n:2 * n], outs[2 * n]
    return outs[:n], outs[n:]


_SCALE = 1.0 / math.sqrt(HEAD_DIM)
_DN_NT = (((1,), (1,)), ((), ()))
_DN_TN = (((0,), (0,)), ((), ()))


def _head_norm(x, g):
    r = lax.rsqrt(jnp.mean(x * x, axis=-1, keepdims=True) + NORM_EPS)
    return x * r * g, r


def _attn_bias():
    rows = KV_GROUP * BLOCK
    r = lax.broadcasted_iota(jnp.int32, (rows, 3 * BLOCK), 0)
    j = lax.broadcasted_iota(jnp.int32, (rows, 3 * BLOCK), 1)
    i = r % BLOCK
    is_meta = j < BLOCK
    dist_band = 2 * BLOCK + i - j
    ok = jnp.where(is_meta, j >= PAD, (dist_band >= 0) & (dist_band < WINDOW))
    dist = jnp.where(is_meta, i - j, dist_band).astype(F32)
    slopes = jnp.asarray([2.0 ** (-8.0 * (h + 1) / N_HEADS) for h in range(N_HEADS)], F32).reshape(N_KV_HEADS, KV_GROUP)
    slope_rows = jnp.repeat(slopes, BLOCK, axis=1)[:, :, None]
    bias = jnp.where(ok[None], -slope_rows * dist[None], NEG_INF)
    neg0 = jnp.where(j[:, :BLOCK] <= i[:, :BLOCK], 0.0, NEG_INF)
    return bias.astype(F32), neg0.astype(F32)


def _slope_col(kv):
    g = lax.broadcasted_iota(jnp.int32, (KV_GROUP * BLOCK, 1), 0) // BLOCK
    col = jnp.zeros((KV_GROUP * BLOCK, 1), F32)
    for gi in range(KV_GROUP):
        col = jnp.where(g == gi, 2.0 ** (-8.0 * (kv * KV_GROUP + gi + 1) / N_HEADS), col)
    return col


def _sink_col(sink_ref, kv):
    g = lax.broadcasted_iota(jnp.int32, (KV_GROUP * BLOCK, 1), 0) // BLOCK
    col = jnp.zeros((KV_GROUP * BLOCK, 1), F32)
    for gi in range(KV_GROUP):
        h = kv * KV_GROUP + gi
        col = jnp.where(g == gi, sink_ref[0:1, h:h + 1], col)
    return col


def _stack_heads(x, kv):
    return jnp.concatenate([x[:, (kv * KV_GROUP + g) * HEAD_DIM:(kv * KV_GROUP + g + 1) * HEAD_DIM]
                            for g in range(KV_GROUP)], axis=0)


def _attn_scores(q_ref, k_refs, gq_ref, gk_ref, sink_ref, bias_ref, neg0_ref, kv, n):
    qs = _stack_heads(q_ref[...], kv)
    kcat = jnp.concatenate([r[:, kv * HEAD_DIM:(kv + 1) * HEAD_DIM] for r in k_refs], axis=0)
    qn, rq = _head_norm(qs, gq_ref[...])
    kn, rk = _head_norm(kcat, gk_ref[...])
    s = lax.dot_general((qn * _SCALE).astype(BF16), kn.astype(BF16), _DN_NT, preferred_element_type=F32)
    first = jnp.where(n == 0, 1.0, 0.0)
    second = jnp.where(n == 1, 1.0, 0.0)
    meta = (s[:, :BLOCK] + bias_ref[kv, :, :BLOCK] + _slope_col(kv) * (-float(BLOCK) * n.astype(F32))
            + first * neg0_ref[...])
    in_prev = jnp.where(lax.broadcasted_iota(jnp.int32, (1, 2 * BLOCK), 1) < BLOCK, 1.0, 0.0)
    band = s[:, BLOCK:] + bias_ref[kv, :, BLOCK:] + NEG_INF * (first + second * in_prev)
    s = jnp.concatenate([meta, band], axis=1)
    return qs, kcat, qn, kn, rq, rk, s, _sink_col(sink_ref, kv)


def _attn_specs():
    kq = ATTN_WIDTH // KV_WIDTH
    q_spec = pl.BlockSpec((BLOCK, ATTN_WIDTH), lambda n: (n, 0))
    kv_specs = []
    for col in (kq, kq + 1):
        kv_specs += [pl.BlockSpec((BLOCK, KV_WIDTH), lambda n, col=col: (0, col)),
                     pl.BlockSpec((BLOCK, KV_WIDTH), lambda n, col=col: (jnp.maximum(n - 1, 0), col)),
                     pl.BlockSpec((BLOCK, KV_WIDTH), lambda n, col=col: (n, col))]
    small = [pl.BlockSpec((1, HEAD_DIM), lambda n: (0, 0)), pl.BlockSpec((1, HEAD_DIM), lambda n: (0, 0)),
             pl.BlockSpec((1, N_HEADS), lambda n: (0, 0)),
             pl.BlockSpec((N_KV_HEADS, KV_GROUP * BLOCK, 3 * BLOCK), lambda n: (0, 0, 0)),
             pl.BlockSpec((KV_GROUP * BLOCK, BLOCK), lambda n: (0, 0))]
    return q_spec, kv_specs, small


def _attn_fwd(proj, gq, gk, sinks, *, name):
    n_rows = proj.shape[0]
    q_spec, kv_specs, small = _attn_specs()

    def body(q_ref, k0, k1, k2, v0, v1, v2, gq_ref, gk_ref, sink_ref, bias_ref, neg0_ref, o_ref, lse_ref):
        n = pl.program_id(0)
        o_parts, lse_parts = [], []
        for kv in range(N_KV_HEADS):
            _, _, _, _, _, _, s, sink = _attn_scores(q_ref, (k0, k1, k2), gq_ref, gk_ref, sink_ref, bias_ref, neg0_ref, kv, n)
            vcat = jnp.concatenate([r[:, kv * HEAD_DIM:(kv + 1) * HEAD_DIM] for r in (v0, v1, v2)], axis=0)
            m = jnp.maximum(jnp.max(s, axis=-1, keepdims=True), sink)
            p = jnp.exp(s - m)
            l = jnp.sum(p, axis=-1, keepdims=True) + jnp.exp(sink - m)
            o = jnp.dot(p.astype(BF16), vcat.astype(BF16), preferred_element_type=F32) / l
            lse = m + jnp.log(l)
            o_parts += [o[g * BLOCK:(g + 1) * BLOCK] for g in range(KV_GROUP)]
            lse_parts += [lse[g * BLOCK:(g + 1) * BLOCK] for g in range(KV_GROUP)]
        o_ref[...] = jnp.concatenate(o_parts, axis=1)
        lse_ref[...] = jnp.concatenate(lse_parts, axis=1)

    return pl.pallas_call(
        body, name=name, grid=(n_rows // BLOCK,),
        in_specs=[q_spec] + kv_specs + small,
        out_specs=[pl.BlockSpec((BLOCK, ATTN_WIDTH), lambda n: (n, 0)), pl.BlockSpec((BLOCK, N_HEADS), lambda n: (n, 0))],
        out_shape=[jax.ShapeDtypeStruct((n_rows, ATTN_WIDTH), F32), jax.ShapeDtypeStruct((n_rows, N_HEADS), F32)],
        compiler_params=_cparams(("parallel",)),
    )(proj, proj, proj, proj, proj, proj, proj, gq, gk, sinks, *_attn_bias())


def _attn_bwd(proj, gq, gk, sinks, o, lse, do, *, name):
    n_rows = proj.shape[0]
    q_spec, kv_specs, small = _attn_specs()

    def body(q_ref, k0, k1, k2, v0, v1, v2, gq_ref, gk_ref, sink_ref, bias_ref, neg0_ref, o_ref, lse_ref, do_ref,
             dq_ref, dkb_ref, dvb_ref, dgq_ref, dgk_ref, dsink_ref, dk_ref, dv_ref):
        n = pl.program_id(0)

        @pl.when(n == 0)
        def _():
            dk_ref[...] = jnp.zeros_like(dk_ref)
            dv_ref[...] = jnp.zeros_like(dv_ref)
            dgq_ref[...] = jnp.zeros_like(dgq_ref)
            dgk_ref[...] = jnp.zeros_like(dgk_ref)
            dsink_ref[...] = jnp.zeros_like(dsink_ref)

        dq_parts, dk_parts, dv_parts, dsink_parts = [], [], [], []
        dgq = jnp.zeros((1, HEAD_DIM), F32)
        dgk = jnp.zeros((1, HEAD_DIM), F32)
        for kv in range(N_KV_HEADS):
            qs, kcat, qn, kn, rq, rk, s, sink = _attn_scores(q_ref, (k0, k1, k2), gq_ref, gk_ref, sink_ref, bias_ref, neg0_ref, kv, n)
            vcat = jnp.concatenate([r[:, kv * HEAD_DIM:(kv + 1) * HEAD_DIM] for r in (v0, v1, v2)], axis=0)
            os_ = _stack_heads(o_ref[...], kv)
            dos = _stack_heads(do_ref[...], kv)
            lse = jnp.concatenate([lse_ref[:, kv * KV_GROUP + g:kv * KV_GROUP + g + 1] for g in range(KV_GROUP)], axis=0)
            p = jnp.exp(s - lse)
            delta = jnp.sum(dos * os_, axis=-1, keepdims=True)
            dp = lax.dot_general(dos.astype(BF16), vcat.astype(BF16), _DN_NT, preferred_element_type=F32)
            ds = (p * (dp - delta)) * _SCALE
            dsink_rows = -jnp.exp(sink - lse) * delta
            dsink_parts += [jnp.sum(dsink_rows[g * BLOCK:(g + 1) * BLOCK], axis=0, keepdims=True) for g in range(KV_GROUP)]
            dv_parts.append(lax.dot_general(p.astype(BF16), dos.astype(BF16), _DN_TN, preferred_element_type=F32))
            dsb = ds.astype(BF16)
            dqn = jnp.dot(dsb, kn.astype(BF16), preferred_element_type=F32)
            dkn = lax.dot_general(dsb, qn.astype(BF16), _DN_TN, preferred_element_type=F32)
            qh = qs * rq
            gd = dqn * gq_ref[...]
            dqs = rq * (gd - qh * jnp.mean(gd * qh, axis=-1, keepdims=True))
            dgq = dgq + jnp.sum(dqn * qh, axis=0, keepdims=True)
            kh = kcat * rk
            gdk = dkn * gk_ref[...]
            dk_parts.append(rk * (gdk - kh * jnp.mean(gdk * kh, axis=-1, keepdims=True)))
            dgk = dgk + jnp.sum(dkn * kh, axis=0, keepdims=True)
            dq_parts += [dqs[g * BLOCK:(g + 1) * BLOCK] for g in range(KV_GROUP)]
        dq_ref[...] = jnp.concatenate(dq_parts, axis=1).astype(BF16)
        dkc = jnp.concatenate(dk_parts, axis=1)
        dvc = jnp.concatenate(dv_parts, axis=1)
        prev = pl.multiple_of(jnp.maximum(n - 1, 0) * BLOCK, BLOCK)
        cur = pl.multiple_of(n * BLOCK, BLOCK)
        for acc_ref, val in ((dk_ref, dkc), (dv_ref, dvc)):
            acc_ref[0:BLOCK, :] += val[0:BLOCK]
            acc_ref[pl.ds(prev, BLOCK), :] += val[BLOCK:2 * BLOCK]
            acc_ref[pl.ds(cur, BLOCK), :] += val[2 * BLOCK:3 * BLOCK]
        dgq_ref[...] += dgq
        dgk_ref[...] += dgk
        dsink_ref[...] += jnp.concatenate(dsink_parts, axis=1)

        @pl.when(n == pl.num_programs(0) - 1)
        def _():
            dkb_ref[...] = dk_ref[...].astype(BF16)
            dvb_ref[...] = dv_ref[...].astype(BF16)

    blk = lambda w: pl.BlockSpec((BLOCK, w), lambda n: (n, 0))
    full = lambda r, w: pl.BlockSpec((r, w), lambda n: (0, 0))
    return pl.pallas_call(
        body, name=name, grid=(n_rows // BLOCK,),
        in_specs=[q_spec] + kv_specs + small + [blk(ATTN_WIDTH), blk(N_HEADS), blk(ATTN_WIDTH)],
        out_specs=[blk(ATTN_WIDTH), full(n_rows, KV_WIDTH), full(n_rows, KV_WIDTH),
                   full(1, HEAD_DIM), full(1, HEAD_DIM), full(1, N_HEADS)],
        out_shape=[jax.ShapeDtypeStruct((n_rows, ATTN_WIDTH), BF16), jax.ShapeDtypeStruct((n_rows, KV_WIDTH), BF16),
                   jax.ShapeDtypeStruct((n_rows, KV_WIDTH), BF16), jax.ShapeDtypeStruct((1, HEAD_DIM), F32),
                   jax.ShapeDtypeStruct((1, HEAD_DIM), F32), jax.ShapeDtypeStruct((1, N_HEADS), F32)],
        scratch_shapes=[pltpu.VMEM((n_rows, KV_WIDTH), F32), pltpu.VMEM((n_rows, KV_WIDTH), F32)],
        compiler_params=_cparams(("arbitrary",)),
    )(proj, proj, proj, proj, proj, proj, proj, gq, gk, sinks, *_attn_bias(), o, lse, do)


SSM_LAGS = 8
SLAB_G = 128 // SSM_GROUP_CH
N_SLABS = SSM_GROUPS // SLAB_G
SLAB_STATE = SLAB_G * SSM_STATE
U_COL = (ATTN_WIDTH + 2 * KV_WIDTH) // 128


def _ssm_prep(lam_re, lam_im, log_step, b_re, b_im, c_re, c_im):
    lam = lax.complex(lam_re, lam_im)
    delta = jnp.exp(log_step)[:, None]
    lam_bar = jnp.exp(lam * delta)
    b_t = lax.complex(jnp.swapaxes(b_re, 1, 2), jnp.swapaxes(b_im, 1, 2))
    b_bar = ((lam_bar - 1.0) / lam)[:, None, :] * b_t
    pw = [jnp.ones_like(lam_bar)]
    for _ in range(SSM_LAGS):
        pw.append(pw[-1] * lam_bar)
    w = jnp.stack(pw[:SSM_LAGS])[:, :, None, :] * b_bar[None]
    wri = jnp.stack([jnp.real(w), jnp.imag(w)], axis=3)
    wc = wri.reshape(SSM_LAGS, N_SLABS, SLAB_G * SSM_GROUP_CH, 2 * SSM_STATE)
    wc = jnp.swapaxes(wc, 0, 1).reshape(N_SLABS, SSM_LAGS * 128, 2 * SSM_STATE)
    cri = jnp.stack([c_re, -c_im], axis=2).reshape(N_SLABS, SLAB_G, SSM_GROUP_CH, 2, 1, SSM_STATE)
    eye = jnp.eye(SLAB_G, dtype=F32).reshape(1, SLAB_G, 1, 1, SLAB_G, 1)
    ct = (cri * eye).reshape(N_SLABS, 128, 2 * SLAB_STATE)
    l8 = pw[SSM_LAGS]
    lam8 = jnp.concatenate([jnp.real(l8).reshape(N_SLABS, 1, SLAB_STATE), jnp.imag(l8).reshape(N_SLABS, 1, SLAB_STATE)], axis=2)
    return wc, ct, lam8


def _row_group():
    return (lax.broadcasted_iota(jnp.int32, (SSM_LAGS * 128, 1), 0) // SSM_GROUP_CH) % SLAB_G


def _spread_groups(wc):
    g_of_row = _row_group()
    return jnp.concatenate([jnp.where(g_of_row == g, wc[:, r * SSM_STATE:(r + 1) * SSM_STATE], 0.0)
                            for r in range(2) for g in range(SLAB_G)], axis=1)


def _gather_groups(dw):
    g_of_row = _row_group()
    parts = []
    for r in range(2):
        acc = jnp.zeros((SSM_LAGS * 128, SSM_STATE), F32)
        for g in range(SLAB_G):
            c0 = r * SLAB_STATE + g * SSM_STATE
            acc = acc + jnp.where(g_of_row == g, dw[:, c0:c0 + SSM_STATE], 0.0)
        parts.append(acc)
    return jnp.concatenate(parts, axis=1)


def _lagged(u, up, t_rows):
    ue = jnp.concatenate([up, u], axis=0)
    return jnp.concatenate([ue[SSM_LAGS - tau:SSM_LAGS - tau + t_rows] for tau in range(SSM_LAGS)], axis=1).astype(BF16)


def _ssm_fwd(proj, wc, cmat, lam8, dvec, *, name):
    n_rows = proj.shape[0]
    tt = _row_tile(n_rows, 1408)
    n_t = n_rows // tt
    sw = 2 * SLAB_STATE
    hs = SLAB_STATE

    def body(u_ref, up_ref, wc_ref, c_ref, l_ref, d_ref, y_ref, x_ref, carry_ref, w_ref):
        t = pl.program_id(1)

        @pl.when(t == 0)
        def _():
            carry_ref[...] = jnp.zeros_like(carry_ref)
            w_ref[...] = _spread_groups(wc_ref[...]).astype(BF16)

        u = u_ref[...]
        up = jnp.where(t > 0, up_ref[...], 0.0)
        x_ref[...] = jnp.dot(_lagged(u, up, tt), w_ref[...], preferred_element_type=F32)
        ar = jnp.broadcast_to(l_ref[:, :hs], (8, hs))
        ai = jnp.broadcast_to(l_ref[:, hs:], (8, hs))

        def step(b, c):
            xr, xi = c
            r0 = pl.multiple_of(b * 8, 8)
            w = x_ref[pl.ds(r0, 8), :]
            nr = w[:, :hs] + ar * xr - ai * xi
            ni = w[:, hs:] + ar * xi + ai * xr
            x_ref[pl.ds(r0, 8), :] = jnp.concatenate([nr, ni], axis=1)
            return nr, ni

        xr, xi = lax.fori_loop(0, tt // 8, step, (carry_ref[:, :hs], carry_ref[:, hs:]), unroll=8)
        carry_ref[...] = jnp.concatenate([xr, xi], axis=1)
        y_ref[...] = lax.dot_general(x_ref[...].astype(BF16), c_ref[...], _DN_NT, preferred_element_type=F32) + d_ref[...] * u

    return pl.pallas_call(
        body, name=name, grid=(N_SLABS, n_t),
        in_specs=[pl.BlockSpec((tt, 128), lambda j, t: (t, U_COL + j)),
                  pl.BlockSpec((8, 128), lambda j, t: (jnp.maximum(t * (tt // 8) - 1, 0), U_COL + j)),
                  pl.BlockSpec((None, SSM_LAGS * 128, 2 * SSM_STATE), lambda j, t: (j, 0, 0)),
                  pl.BlockSpec((None, 128, sw), lambda j, t: (j, 0, 0)),
                  pl.BlockSpec((None, 1, sw), lambda j, t: (j, 0, 0)),
                  pl.BlockSpec((1, 128), lambda j, t: (0, j))],
        out_specs=[pl.BlockSpec((tt, 128), lambda j, t: (t, j)), pl.BlockSpec((tt, sw), lambda j, t: (t, j))],
        out_shape=[jax.ShapeDtypeStruct((n_rows, SSM_WIDTH), F32), jax.ShapeDtypeStruct((n_rows, N_SLABS * sw), F32)],
        scratch_shapes=[pltpu.VMEM((8, sw), F32), pltpu.VMEM((SSM_LAGS * 128, sw), BF16)],
        compiler_params=_cparams(("parallel", "arbitrary")),
    )(proj, proj, wc, cmat, lam8, dvec)


def _ssm_bwd(proj, xs, dy, wc, cmat, lam8, dvec, *, name, deps=()):
    n_rows = proj.shape[0]
    tt = _row_tile(n_rows, 704)
    n_t = n_rows // tt
    sw = 2 * SLAB_STATE
    hs = SLAB_STATE

    def body(u_ref, up_ref, x_ref, xp_ref, dy_ref, wc_ref, c_ref, l_ref, d_ref,
             *rest):
        du_ref, dwc_ref, dc_ref, dl_ref, dd_ref, a_ref, carry_ref, head_ref, w_ref, dw_ref = rest[len(deps):]
        t = pl.program_id(1)
        ti = n_t - 1 - t

        @pl.when(t == 0)
        def _():
            w_ref[...] = _spread_groups(wc_ref[...]).astype(BF16)
            carry_ref[...] = jnp.zeros_like(carry_ref)
            head_ref[...] = jnp.zeros_like(head_ref)
            dw_ref[...] = jnp.zeros_like(dw_ref)
            dc_ref[...] = jnp.zeros_like(dc_ref)
            dl_ref[...] = jnp.zeros_like(dl_ref)
            dd_ref[...] = jnp.zeros_like(dd_ref)

        u = u_ref[...]
        up = jnp.where(ti > 0, up_ref[...], 0.0)
        ucat = _lagged(u, up, tt)
        dyv = dy_ref[...]
        dyb = dyv.astype(BF16)
        a_ref[...] = jnp.dot(dyb, c_ref[...], preferred_element_type=F32)
        lr = jnp.broadcast_to(l_ref[:, :hs], (8, hs))
        li = jnp.broadcast_to(l_ref[:, hs:], (8, hs))

        def step(i, c):
            cr, ci = c
            r0 = pl.multiple_of((tt // 8 - 1 - i) * 8, 8)
            g = a_ref[pl.ds(r0, 8), :]
            nr = g[:, :hs] + lr * cr + li * ci
            ni = g[:, hs:] + lr * ci - li * cr
            a_ref[pl.ds(r0, 8), :] = jnp.concatenate([nr, ni], axis=1)
            return nr, ni

        cr, ci = lax.fori_loop(0, tt // 8, step, (carry_ref[:, :hs], carry_ref[:, hs:]), unroll=8)
        carry_ref[...] = jnp.concatenate([cr, ci], axis=1)

        a = a_ref[...]
        xv = x_ref[...]
        xprev = jnp.where(ti > 0, xp_ref[...], 0.0)
        xsh = jnp.concatenate([xprev, xv[:tt - SSM_LAGS]], axis=0)
        a_re, a_im, x_re, x_im = a[:, :hs], a[:, hs:], xsh[:, :hs], xsh[:, hs:]
        dl_ref[...] += jnp.concatenate([jnp.sum(a_re * x_re + a_im * x_im, axis=0, keepdims=True),
                                        jnp.sum(a_im * x_re - a_re * x_im, axis=0, keepdims=True)], axis=1)
        ab = a.astype(BF16)
        dw_ref[...] += lax.dot_general(ucat, ab, _DN_TN, preferred_element_type=F32)
        duc = lax.dot_general(ab, w_ref[...], _DN_NT, preferred_element_type=F32)
        ext = jnp.concatenate([duc, head_ref[...]], axis=0)
        du = d_ref[...] * dyv
        for tau in range(SSM_LAGS):
            du = du + ext[tau:tau + tt, tau * 128:(tau + 1) * 128]
        head_ref[...] = duc[0:8]
        row = ti * tt + lax.broadcasted_iota(jnp.int32, (tt, 128), 0)
        du_ref[...] = jnp.where(row >= PAD, du, 0.0).astype(BF16)
        dd_ref[...] += jnp.sum(dyv * u, axis=0, keepdims=True)
        dc_ref[...] += lax.dot_general(dyb, xv.astype(BF16), _DN_TN, preferred_element_type=F32)

        @pl.when(t == n_t - 1)
        def _():
            dwc_ref[...] = _gather_groups(dw_ref[...])

    rt = lambda t: n_t - 1 - t
    prev8 = lambda t: jnp.maximum(rt(t) * (tt // 8) - 1, 0)
    return pl.pallas_call(
        body, name=name, grid=(N_SLABS, n_t),
        in_specs=[pl.BlockSpec((tt, 128), lambda j, t: (rt(t), U_COL + j)),
                  pl.BlockSpec((8, 128), lambda j, t: (prev8(t), U_COL + j)),
                  pl.BlockSpec((tt, sw), lambda j, t: (rt(t), j)),
                  pl.BlockSpec((8, sw), lambda j, t: (prev8(t), j)),
                  pl.BlockSpec((tt, 128), lambda j, t: (rt(t), j)),
                  pl.BlockSpec((None, SSM_LAGS * 128, 2 * SSM_STATE), lambda j, t: (j, 0, 0)),
                  pl.BlockSpec((None, 128, sw), lambda j, t: (j, 0, 0)),
                  pl.BlockSpec((None, 1, sw), lambda j, t: (j, 0, 0)),
                  pl.BlockSpec((1, 128), lambda j, t: (0, j))] + [_ANY] * len(deps),
        out_specs=[pl.BlockSpec((tt, 128), lambda j, t: (rt(t), j)),
                   pl.BlockSpec((None, SSM_LAGS * 128, 2 * SSM_STATE), lambda j, t: (j, 0, 0)),
                   pl.BlockSpec((None, 128, sw), lambda j, t: (j, 0, 0)),
                   pl.BlockSpec((None, 1, sw), lambda j, t: (j, 0, 0)),
                   pl.BlockSpec((1, 128), lambda j, t: (0, j))],
        out_shape=[jax.ShapeDtypeStruct((n_rows, SSM_WIDTH), BF16),
                   jax.ShapeDtypeStruct((N_SLABS, SSM_LAGS * 128, 2 * SSM_STATE), F32),
                   jax.ShapeDtypeStruct((N_SLABS, 128, sw), F32),
                   jax.ShapeDtypeStruct((N_SLABS, 1, sw), F32),
                   jax.ShapeDtypeStruct((1, SSM_WIDTH), F32)],
        scratch_shapes=[pltpu.VMEM((tt, sw), F32), pltpu.VMEM((8, sw), F32), pltpu.VMEM((8, sw), F32),
                        pltpu.VMEM((SSM_LAGS * 128, sw), BF16), pltpu.VMEM((SSM_LAGS * 128, sw), F32)],
        compiler_params=_cparams(("parallel", "arbitrary")),
    )(proj, proj, xs, xs, dy, wc, cmat, lam8, dvec, *deps)


_GELU_C = math.sqrt(2.0 / math.pi)
_GELU_A = 0.044715


def _gelu(y):
    th = jnp.tanh(_GELU_C * (y + _GELU_A * y * y * y))
    return 0.5 * y * (1.0 + th), th


def _glu_fwd(y, w, b, *, name):
    n_rows, width = y.shape
    tr = _row_tile(n_rows, 384)

    def body(y_ref, w_ref, b_ref, o_ref):
        g, _ = _gelu(y_ref[...])
        z = jnp.dot(g.astype(BF16), w_ref[...], preferred_element_type=F32) + b_ref[...]
        o_ref[...] = g * jax.nn.sigmoid(z)

    return pl.pallas_call(
        body, name=name, grid=(n_rows // tr,),
        in_specs=[pl.BlockSpec((tr, width), lambda i: (i, 0)), pl.BlockSpec((width, width), lambda i: (0, 0)),
                  pl.BlockSpec((1, width), lambda i: (0, 0))],
        out_specs=pl.BlockSpec((tr, width), lambda i: (i, 0)),
        out_shape=jax.ShapeDtypeStruct((n_rows, width), F32),
        compiler_params=_cparams(("parallel",)),
    )(y, w, b)


def _glu_bwd(y, w, b, dout, *, name):
    n_rows, width = y.shape
    tr = _row_tile(n_rows, 384)

    def body(y_ref, w_ref, b_ref, do_ref, dy_ref, g_ref, dz_ref, db_ref):
        i = pl.program_id(0)
        yv = y_ref[...]
        g, th = _gelu(yv)
        gb = g.astype(BF16)
        z = jnp.dot(gb, w_ref[...], preferred_element_type=F32) + b_ref[...]
        sg = jax.nn.sigmoid(z)
        do = do_ref[...]
        dz = do * g * sg * (1.0 - sg)
        dzb = dz.astype(BF16)
        dg = do * sg + lax.dot_general(dzb, w_ref[...], _DN_NT, preferred_element_type=F32)
        dgelu = 0.5 * (1.0 + th) + 0.5 * yv * (1.0 - th * th) * _GELU_C * (1.0 + 3.0 * _GELU_A * yv * yv)
        dy_ref[...] = dg * dgelu
        g_ref[...] = gb
        dz_ref[...] = dzb
        part = jnp.sum(dz, axis=0, keepdims=True)

        @pl.when(i == 0)
        def _():
            db_ref[...] = part

        @pl.when(i > 0)
        def _():
            db_ref[...] += part

    row = pl.BlockSpec((tr, width), lambda i: (i, 0))
    vec = pl.BlockSpec((1, width), lambda i: (0, 0))
    return pl.pallas_call(
        body, name=name, grid=(n_rows // tr,),
        in_specs=[row, pl.BlockSpec((width, width), lambda i: (0, 0)), vec, row],
        out_specs=[row, row, row, vec],
        out_shape=[jax.ShapeDtypeStruct((n_rows, width), F32), jax.ShapeDtypeStruct((n_rows, width), BF16),
                   jax.ShapeDtypeStruct((n_rows, width), BF16), jax.ShapeDtypeStruct((1, width), F32)],
        compiler_params=_cparams(("arbitrary",)),
    )(y, w, b, dout)


def _loss_head(h, target, *, name):
    n_rows, width = h.shape

    def body(h_ref, t_ref, dh_ref, dhb_ref, loss_ref):
        i = pl.program_id(0)

        @pl.when(i == 0)
        def _():
            dh_ref[...] = jnp.zeros_like(dh_ref)
            dhb_ref[...] = jnp.zeros_like(dhb_ref)
            loss_ref[...] = jnp.zeros_like(loss_ref)

        @pl.when(i > 0)
        def _():
            err = h_ref[...] - t_ref[...]
            dh = err * (1.0 / width)
            dh_ref[...] = dh
            dhb_ref[...] = dh.astype(BF16)
            loss_ref[...] += (0.5 / width) * jnp.sum(err * err, keepdims=True)

    return pl.pallas_call(
        body, name=name, grid=(n_rows // BLOCK,),
        in_specs=[pl.BlockSpec((BLOCK, width), lambda i: (i, 0)),
                  pl.BlockSpec((BLOCK, width), lambda i: (jnp.maximum(i - 1, 0), 0))],
        out_specs=[pl.BlockSpec((BLOCK, width), lambda i: (i, 0)), pl.BlockSpec((BLOCK, width), lambda i: (i, 0)),
                   pl.BlockSpec((1, 1), lambda i: (0, 0))],
        out_shape=[jax.ShapeDtypeStruct((n_rows, width), F32), jax.ShapeDtypeStruct((n_rows, width), BF16),
                   jax.ShapeDtypeStruct((1, 1), F32)],
        compiler_params=_cparams(("arbitrary",)),
    )(h, target)


def _elem_rows(n_rows, n_cols, bytes_per_row_elem):
    lanes = -(-n_cols // 128) * 128
    cap = max(16, (12 * 1024 * 1024) // (lanes * bytes_per_row_elem))
    best = None
    for t in range(16, min(n_rows, cap) + 1, 16):
        if n_rows % t == 0:
            best = t
    return best or n_rows


def _cast_bf16(x, *, name):
    n_rows, n_cols = x.shape
    tr = _elem_rows(n_rows, n_cols, 4)

    def body(x_ref, o_ref):
        o_ref[...] = x_ref[...].astype(BF16)

    spec = pl.BlockSpec((tr, n_cols), lambda i: (i, 0))
    return pl.pallas_call(body, name=name, grid=(n_rows // tr,), in_specs=[spec], out_specs=spec,
                          out_shape=jax.ShapeDtypeStruct(x.shape, BF16), compiler_params=_cparams(("parallel",)))(x)


def _adamw(w, m, v, parts, *, name):
    n_rows, n_cols = w.shape
    n_parts = parts.shape[0]
    tr = _elem_rows(n_rows, n_cols, 4 * (8 + n_parts))
    c1 = 1.0 / (1.0 - ADAM_B1 ** ADAM_STEP)
    c2 = 1.0 / (1.0 - ADAM_B2 ** ADAM_STEP)

    def body(w_ref, m_ref, v_ref, p_ref, g_ref, d_ref, nm_ref, nv_ref):
        g = p_ref[0].astype(F32)
        for k in range(1, n_parts):
            g = g + p_ref[k].astype(F32)
        nm = ADAM_B1 * m_ref[...] + (1.0 - ADAM_B1) * g
        nv = ADAM_B2 * v_ref[...] + (1.0 - ADAM_B2) * (g * g)
        g_ref[...] = g
        nm_ref[...] = nm
        nv_ref[...] = nv
        d_ref[...] = -ADAM_LR * ((nm * c1) / (jnp.sqrt(nv * c2) + ADAM_EPS) + ADAM_WD * w_ref[...])

    spec = pl.BlockSpec((tr, n_cols), lambda i: (i, 0))
    return pl.pallas_call(
        body, name=name, grid=(n_rows // tr,),
        in_specs=[spec, spec, spec, pl.BlockSpec((n_parts, tr, n_cols), lambda i: (0, i, 0))],
        out_specs=[spec] * 4, out_shape=[jax.ShapeDtypeStruct(w.shape, F32)] * 4,
        compiler_params=_cparams(("parallel",)),
    )(w, m, v, parts)


def _sum_parts(parts, *, name):
    n_parts, n_rows, n_cols = parts.shape
    tr = _elem_rows(n_rows, n_cols, 4 * (1 + n_parts))

    def body(p_ref, o_ref):
        g = p_ref[0].astype(F32)
        for k in range(1, n_parts):
            g = g + p_ref[k].astype(F32)
        o_ref[...] = g

    return pl.pallas_call(
        body, name=name, grid=(n_rows // tr,),
        in_specs=[pl.BlockSpec((n_parts, tr, n_cols), lambda i: (0, i, 0))],
        out_specs=pl.BlockSpec((tr, n_cols), lambda i: (i, 0)),
        out_shape=jax.ShapeDtypeStruct((n_rows, n_cols), F32), compiler_params=_cparams(("parallel",)),
    )(parts)


BIG = ("w_in", "w_glu", "w_out", "w_up", "w_down")
SMALL = ("norm_mix_g", "q_norm_g", "k_norm_g", "attn_sinks", "ssm_lambda_re", "ssm_lambda_im", "ssm_log_step",
         "ssm_b_re", "ssm_b_im", "ssm_c_re", "ssm_c_im", "ssm_d", "b_glu", "attn_out_g", "ssm_out_g", "norm_mlp_g")
_SSM_NAMES = ("ssm_lambda_re", "ssm_lambda_im", "ssm_log_step", "ssm_b_re", "ssm_b_im", "ssm_c_re", "ssm_c_im")


def _divisor(n, cands):
    for c in cands:
        if n % c == 0:
            return c
    return n


def _mm(a, b, mode, name, outs=(F32,), epilogue=_ident, tiles=(), deps=(), blocked=False):
    if mode == "nn":
        m, k = a.shape
        n = b.shape[0] * b.shape[2] if blocked else b.shape[1]
    elif mode == "nt":
        m, k = a.shape
        n = b.shape[1] if blocked else b.shape[0]
    else:
        (k, m), n = a.shape, b.shape[1]
    if mode == "tn":
        tm, tn, tk = _divisor(m, (1024, 512)), _divisor(n, (1024, 512)), k
    elif k <= 2560:
        tm, tn, tk = _row_tile(m, 1408), _divisor(n, (1024, 1280, 512)), k
    elif blocked:
        tm, tn, tk = (m // 2 if m % 32 == 0 else m), _divisor(n, (1024, 512)), b.shape[2]
    else:
        tm, tn, tk = _row_tile(m, 1408), _divisor(n, (1024, 512)), _divisor(k, (1024, 512))
    return _matmul(a, b, mode=mode, tm=tm, tn=tn, tk=tk, outs=list(outs), epilogue=epilogue, tiles=tiles, deps=deps,
                   blocked=blocked, name=name)


def _add_tile(acc, res):
    return (acc + res,)


def _relu_sq(acc):
    r = jnp.maximum(acc, 0.0)
    return r, r * r


def _relu_sq_bwd(acc, r):
    return (acc * (2.0 * r.astype(F32)),)


def _row(v):
    return v.reshape(1, -1)


def _layer_fwd(hres, fetch, sp, l, deps=()):
    tag = f"_l{l}"
    wts = {}
    hb = _rmsnorm_fwd([hres], [_row(sp["norm_mix_g"])], name="norm_mix" + tag, deps=deps)
    wts["w_in"] = fetch("w_in", hb)
    proj, = _mm(hb, wts["w_in"], "nt", "proj" + tag)
    gq, gk, sinks = _row(sp["q_norm_g"]), _row(sp["k_norm_g"]), _row(sp["attn_sinks"])
    o, lse = _attn_fwd(proj, gq, gk, sinks, name="attn_fwd" + tag)
    (wc, cmat, lam8), prep_vjp = jax.vjp(_ssm_prep, *[sp[n] for n in _SSM_NAMES])
    cmat = cmat.astype(BF16)
    y, xs = _ssm_fwd(proj, wc, cmat, lam8, _row(sp["ssm_d"]), name="ssm_fwd" + tag)
    wts["w_glu"] = fetch("w_glu", y)
    s = _glu_fwd(y, wts["w_glu"], _row(sp["b_glu"]), name="glu_fwd" + tag)
    mix = _rmsnorm_fwd([o, s], [_row(sp["attn_out_g"]), _row(sp["ssm_out_g"])], name="norm_out" + tag)
    wts["w_out"] = fetch("w_out", mix)
    hres2, = _mm(mix, wts["w_out"], "nn", "out_proj" + tag, epilogue=_add_tile, tiles=(hres,))
    h2 = _rmsnorm_fwd([hres2], [_row(sp["norm_mlp_g"])], name="norm_mlp" + tag)
    wts["w_up"] = fetch("w_up", h2)
    r, act = _mm(h2, wts["w_up"], "nn", "mlp_up" + tag, outs=(BF16, BF16), epilogue=_relu_sq, blocked=True)
    wts["w_down"] = fetch("w_down", act)
    hres3, = _mm(act, wts["w_down"], "nn", "mlp_down" + tag, epilogue=_add_tile, tiles=(hres2,))
    saved = dict(wts=wts, hres=hres, hb=hb, proj=proj, o=o, lse=lse, wc=wc, cmat=cmat, lam8=lam8, prep_vjp=prep_vjp,
                 y=y, xs=xs, s=s, mix=mix, hres2=hres2, h2=h2, r=r, act=act)
    return hres3, saved


def _layer_bwd(dres, dres_b, sp, sv, l, early_grads, deps=()):
    tag = f"_l{l}"
    wts = sv["wts"]
    gb, gs = {}, {}
    d_up, = _mm(dres_b, wts["w_down"], "nt", "mlp_down_dx" + tag, outs=(BF16,), epilogue=_relu_sq_bwd, tiles=(sv["r"],),
                deps=deps)
    gb["w_down"], = _mm(sv["act"], dres_b, "tn", "mlp_down_dw" + tag, outs=(BF16,))
    gb["w_up"], = _mm(sv["h2"], d_up, "tn", "mlp_up_dw" + tag, outs=(BF16,), blocked=True)
    deps = early_grads(l, "a", {n: gb.pop(n) for n in ("w_up", "w_down")})
    dh2, = _mm(d_up, wts["w_up"], "nt", "mlp_up_dx" + tag, blocked=True, deps=deps)
    (dres2,), (dg,), dres2_b = _rmsnorm_bwd([sv["hres2"]], [_row(sp["norm_mlp_g"])], dh2, dres, name="norm_mlp_bwd" + tag)
    gs["norm_mlp_g"] = dg
    dmix, = _mm(dres2_b, wts["w_out"], "nt", "out_proj_dx" + tag)
    gb["w_out"], = _mm(sv["mix"], dres2_b, "tn", "out_proj_dw" + tag, outs=(BF16,))
    (do, ds), (dga, dgs) = _rmsnorm_bwd([sv["o"], sv["s"]], [_row(sp["attn_out_g"]), _row(sp["ssm_out_g"])], dmix, None,
                                        name="norm_out_bwd" + tag)
    gs["attn_out_g"], gs["ssm_out_g"] = dga, dgs
    dy, g_b, dz_b, db = _glu_bwd(sv["y"], wts["w_glu"], _row(sp["b_glu"]), ds, name="glu_bwd" + tag)
    gs["b_glu"] = db
    gb["w_glu"], = _mm(g_b, dz_b, "tn", "glu_dw" + tag, outs=(BF16,))
    deps = early_grads(l, "b", {n: gb.pop(n) for n in ("w_out", "w_glu")})
    du, dwc, dcmat, dlam8, dd = _ssm_bwd(sv["proj"], sv["xs"], dy, sv["wc"], sv["cmat"], sv["lam8"], _row(sp["ssm_d"]),
                                         name="ssm_bwd" + tag, deps=deps)
    gs["ssm_d"] = dd
    for n, g in zip(_SSM_NAMES, sv["prep_vjp"]((dwc, dcmat, dlam8))):
        gs[n] = g
    dq, dk, dv, dgq, dgk, dsinks = _attn_bwd(sv["proj"], _row(sp["q_norm_g"]), _row(sp["k_norm_g"]), _row(sp["attn_sinks"]),
                                             sv["o"], sv["lse"], do, name="attn_bwd" + tag)
    gs["q_norm_g"], gs["k_norm_g"], gs["attn_sinks"] = dgq, dgk, dsinks
    dproj = _concat_cols([dq, dk, dv, du], name="dproj" + tag)
    gb["w_in"], = _mm(dproj, sv["hb"], "tn", "proj_dw" + tag, outs=(BF16,))
    deps = early_grads(l, "c", {"w_in": gb.pop("w_in")})
    dh, = _mm(dproj, wts["w_in"], "nn", "proj_dx" + tag, deps=deps)
    (dres_in,), (dg,), dres_in_b = _rmsnorm_bwd([sv["hres"]], [_row(sp["norm_mix_g"])], dh, dres2, name="norm_mix_bwd" + tag)
    gs["norm_mix_g"] = dg
    return dres_in, dres_in_b, gs


def _local_step(x, target, meta, sp, weights_for_layer, early_grads, grads_of_layer):
    h = jnp.concatenate([jnp.zeros((PAD, x.shape[1]), F32), meta, x], axis=0)
    saved = []
    for l in range(DEPTH):
        fetch, deps = weights_for_layer(l, h)
        h, sv = _layer_fwd(h, fetch, {n: sp[n][l] for n in SMALL}, l, deps)
        saved.append(sv)
    dh, dh_b, loss = _loss_head(h, target, name="loss_head")
    gsmall = {n: [None] * DEPTH for n in SMALL}
    deps = ()
    for l in reversed(range(DEPTH)):
        dh, dh_b, gs = _layer_bwd(dh, dh_b, {n: sp[n][l] for n in SMALL}, saved[l], l, early_grads, deps)
        deps = grads_of_layer(l, dh)
        for n in SMALL:
            gsmall[n][l] = gs[n].reshape(sp[n][l].shape)
    return loss, dh, gsmall


def _all_gather(x, *, name):
    def body(x_ref, out_ref, send_sems, recv_sems, local_sem):
        x, y, c = lax.axis_index("x"), lax.axis_index("y"), lax.axis_index("c")
        me, sibling = (x, y, c), (x, y, 1 - c)
        chips = [(1 - x, y), (x, 1 - y), (1 - x, 1 - y)]

        def slot(px, py, pc):
            return out_ref.at[4 * px + 2 * py + pc]

        def copy(k, block, to, src=None):
            return pltpu.make_async_remote_copy(
                src_ref=slot(*block) if src is None else src, dst_ref=slot(*block),
                send_sem=send_sems.at[k], recv_sem=recv_sems.at[k], device_id=to, device_id_type=_MESH)

        mine = pltpu.make_async_copy(x_ref, slot(*me), local_sem)
        mine.start()
        first = [copy(0, me, sibling, src=x_ref)]
        first += [copy(1 + j, me, (*chip, c), src=x_ref) for j, chip in enumerate(chips)]
        for cp in first:
            cp.start()
        passed = [copy(4 + j, (*chip, c), sibling) for j, chip in enumerate(chips)]
        for j, chip in enumerate(chips):
            copy(1 + j, (*chip, c), me).wait_recv()
            passed[j].start()
        copy(0, sibling, me).wait_recv()
        for j, chip in enumerate(chips):
            copy(4 + j, (*chip, 1 - c), me).wait_recv()
        for cp in first + passed:
            cp.wait_send()
        mine.wait()

    return pl.pallas_call(
        body, name=name, out_shape=jax.ShapeDtypeStruct((N_DEV,) + x.shape, x.dtype),
        in_specs=[_ANY], out_specs=_ANY,
        scratch_shapes=[pltpu.SemaphoreType.DMA((7,)), pltpu.SemaphoreType.DMA((7,)), pltpu.SemaphoreType.DMA],
    )(x)


def _exchange(g, *, name):
    def body(g_ref, r_ref, send_sems, recv_sems, local_sem):
        x, y, c = lax.axis_index("x"), lax.axis_index("y"), lax.axis_index("c")
        me = 4 * x + 2 * y + c
        mine = pltpu.make_async_copy(g_ref.at[me], r_ref.at[me], local_sem)
        mine.start()

        def peer(k):
            px, py, pc = (x + (k >> 2)) % 2, (y + ((k >> 1) & 1)) % 2, (c + (k & 1)) % 2
            return (px, py, pc), 4 * px + 2 * py + pc

        def copy(k, src_block, dst_block):
            to, _ = peer(k)
            return pltpu.make_async_remote_copy(
                src_ref=g_ref.at[src_block], dst_ref=r_ref.at[dst_block],
                send_sem=send_sems.at[k - 1], recv_sem=recv_sems.at[k - 1], device_id=to, device_id_type=_MESH)

        sends = [copy(k, peer(k)[1], me) for k in range(1, N_DEV)]
        for cp in sends:
            cp.start()
        for k in range(1, N_DEV):
            copy(k, me, peer(k)[1]).wait_recv()
        for cp in sends:
            cp.wait_send()
        mine.wait()

    return pl.pallas_call(
        body, name=name, out_shape=jax.ShapeDtypeStruct(g.shape, g.dtype),
        in_specs=[_ANY], out_specs=_ANY,
        scratch_shapes=[pltpu.SemaphoreType.DMA((7,)), pltpu.SemaphoreType.DMA((7,)), pltpu.SemaphoreType.DMA],
    )(g)


_HBM = pl.BlockSpec(memory_space=pltpu.HBM)
_SEM = pl.BlockSpec(memory_space=pltpu.SEMAPHORE)
_EFFECT = pltpu.SideEffectType.DATAFLOW_SIDE_EFFECTING
N_PEERS = N_DEV - 1


def _me_and_peers():
    x, y, c = lax.axis_index("x"), lax.axis_index("y"), lax.axis_index("c")
    peers = []
    for k in range(1, N_DEV):
        px, py, pc = (x + (k >> 2)) % 2, (y + ((k >> 1) & 1)) % 2, (c + (k & 1)) % 2
        peers.append(((px, py, pc), 4 * px + 2 * py + pc))
    return 4 * x + 2 * y + c, peers


def _send_start(srcs, after, *, per_peer, name):
    n_t = len(srcs)
    blks = [s.shape[1:] if per_peer else s.shape for s in srcs]
    lands = [lax.empty((N_DEV,) + b, s.dtype) for b, s in zip(blks, srcs)]

    def body(*refs):
        src_refs, land_refs = refs[:n_t], refs[n_t:2 * n_t]
        send_sems, recv_sems = refs[2 * n_t + 1], refs[2 * n_t + 2]
        token = refs[-1]
        me, peers = _me_and_peers()
        for t in range(n_t):
            for k, (to, idx) in enumerate(peers):
                pltpu.make_async_remote_copy(
                    src_ref=src_refs[t].at[idx] if per_peer else src_refs[t], dst_ref=land_refs[t].at[me],
                    send_sem=send_sems.at[t * N_PEERS + k], recv_sem=recv_sems.at[t * N_PEERS + k],
                    device_id=to, device_id_type=_MESH).start()
        token[...] = jnp.zeros_like(token)

    sems = pltpu.SemaphoreType.DMA((n_t * N_PEERS,))
    outs = pl.pallas_call(
        body, name=name,
        out_shape=(sems, sems, *[pltpu.HBM(s.shape, s.dtype) for s in srcs], *[pltpu.HBM(z.shape, z.dtype) for z in lands],
                   jax.ShapeDtypeStruct((8, 128), F32)),
        in_specs=[_HBM] * (2 * n_t) + [_ANY],
        out_specs=(_SEM, _SEM, *[_HBM] * (2 * n_t), pl.BlockSpec(memory_space=pltpu.VMEM)),
        input_output_aliases={i: 2 + i for i in range(2 * n_t)},
        compiler_params=pltpu.CompilerParams(has_side_effects=_EFFECT),
    )(*[pltpu.with_memory_space_constraint(s, pltpu.HBM) for s in srcs],
      *[pltpu.with_memory_space_constraint(z, pltpu.HBM) for z in lands], after)
    return outs[0], outs[1], list(outs[2:2 + n_t]), list(outs[2 + n_t:2 + 2 * n_t]), outs[-1]


def _send_wait(handles, after, *, per_peer, name):
    send_sems, recv_sems, srcs, lands = handles
    n_t = len(srcs)

    def body(*refs):
        src_refs, land_refs = refs[:n_t], refs[n_t:2 * n_t]
        send_sems, recv_sems = refs[2 * n_t], refs[2 * n_t + 1]
        _, peers = _me_and_peers()
        for t in range(n_t):
            for k, (to, idx) in enumerate(peers):
                cp = pltpu.make_async_remote_copy(
                    src_ref=src_refs[t].at[idx] if per_peer else src_refs[t], dst_ref=land_refs[t].at[idx],
                    send_sem=send_sems.at[t * N_PEERS + k], recv_sem=recv_sems.at[t * N_PEERS + k],
                    device_id=to, device_id_type=_MESH)
                cp.wait_send()
                cp.wait_recv()

    outs = pl.pallas_call(
        body, name=name,
        out_shape=(*[pltpu.HBM(s.shape, s.dtype) for s in srcs], *[pltpu.HBM(z.shape, z.dtype) for z in lands]),
        in_specs=[_HBM] * (2 * n_t) + [_SEM, _SEM, _ANY], out_specs=tuple([_HBM] * (2 * n_t)),
        input_output_aliases={i: i for i in range(2 * n_t)},
        compiler_params=pltpu.CompilerParams(has_side_effects=_EFFECT),
    )(*srcs, *lands, send_sems, recv_sems, after)
    me = 4 * lax.axis_index("x") + 2 * lax.axis_index("y") + lax.axis_index("c")
    filled = []
    for src, land in zip(outs[:n_t], outs[n_t:]):
        own = lax.dynamic_index_in_dim(src, me, 0, keepdims=False) if per_peer else src
        filled.append(lax.dynamic_update_index_in_dim(land, own, me, 0))
    return filled


def _adamw_layer(w, m, v, parts, l, prev, *, name):
    depth, n_rows, n_cols = w.shape
    n_parts = parts.shape[0]
    tr = _elem_rows(n_rows, n_cols, 4 * (8 + n_parts))
    c1 = 1.0 / (1.0 - ADAM_B1 ** ADAM_STEP)
    c2 = 1.0 / (1.0 - ADAM_B2 ** ADAM_STEP)
    n_prev = 0 if prev is None else 4

    def body(w_ref, m_ref, v_ref, p_ref, *rest):
        g_ref, d_ref, nm_ref, nv_ref = rest[n_prev:]
        g = p_ref[0].astype(F32)
        for k in range(1, n_parts):
            g = g + p_ref[k].astype(F32)
        nm = ADAM_B1 * m_ref[...] + (1.0 - ADAM_B1) * g
        nv = ADAM_B2 * v_ref[...] + (1.0 - ADAM_B2) * (g * g)
        g_ref[...] = g
        nm_ref[...] = nm
        nv_ref[...] = nv
        d_ref[...] = -ADAM_LR * ((nm * c1) / (jnp.sqrt(nv * c2) + ADAM_EPS) + ADAM_WD * w_ref[...])

    spec = pl.BlockSpec((None, tr, n_cols), lambda i: (l, i, 0))
    return pl.pallas_call(
        body, name=name, grid=(n_rows // tr,),
        in_specs=[spec, spec, spec, pl.BlockSpec((n_parts, tr, n_cols), lambda i: (0, i, 0))] + [_ANY] * n_prev,
        out_specs=[spec] * 4, out_shape=[jax.ShapeDtypeStruct(w.shape, F32)] * 4,
        input_output_aliases={4 + i: i for i in range(n_prev)},
        compiler_params=_cparams(("parallel",)),
    )(w, m, v, parts, *(prev or ()))


def _concat_cols(parts, *, name):
    n_rows = parts[0].shape[0]
    widths = [p.shape[1] for p in parts]
    tr = _row_tile(n_rows, 1408)

    def body(*refs):
        o_ref, off = refs[-1], 0
        for p_ref, w in zip(refs[:-1], widths):
            o_ref[:, off:off + w] = p_ref[...]
            off += w

    return pl.pallas_call(
        body, name=name, grid=(n_rows // tr,), in_specs=[pl.BlockSpec((tr, w), lambda i: (i, 0)) for w in widths],
        out_specs=pl.BlockSpec((tr, sum(widths)), lambda i: (i, 0)),
        out_shape=jax.ShapeDtypeStruct((n_rows, sum(widths)), parts[0].dtype), compiler_params=_cparams(("parallel",)))(*parts)


def _full_weights(g):
    return {n: v if n == "w_up" else v.reshape(N_DEV * v.shape[1], v.shape[2]) for n, v in g.items()}


def _grad_blocks(gb):
    return [g if n == "w_up" else g.reshape(N_DEV, g.shape[0] // N_DEV, g.shape[1]) for n, g in gb.items()]


_SMALL_ROWS = 1096


def _pack_small(d):
    flat = jnp.concatenate([d[n].reshape(-1) for n in SMALL])
    total = N_DEV * _SMALL_ROWS * 128
    assert flat.shape[0] <= total
    return jnp.pad(flat, (0, total - flat.shape[0])).reshape(N_DEV * _SMALL_ROWS, 128)


def _unpack_small(packed, like):
    flat = packed.reshape(-1)
    out, off = {}, 0
    for n in SMALL:
        size = like[n].size
        out[n] = flat[off:off + size].reshape(like[n].shape)
        off += size
    return out


def kernel(x, meta_tokens, norm_mix_g, w_in, q_norm_g, k_norm_g, attn_sinks, ssm_lambda_re, ssm_lambda_im, ssm_log_step, ssm_b_re, ssm_b_im, ssm_c_re, ssm_c_im, ssm_d, w_glu, b_glu, attn_out_g, ssm_out_g, w_out, norm_mlp_g, w_up, w_down, loss_target, m_meta_tokens, m_norm_mix_g, m_w_in, m_q_norm_g, m_k_norm_g, m_attn_sinks, m_ssm_lambda_re, m_ssm_lambda_im, m_ssm_log_step, m_ssm_b_re, m_ssm_b_im, m_ssm_c_re, m_ssm_c_im, m_ssm_d, m_w_glu, m_b_glu, m_attn_out_g, m_ssm_out_g, m_w_out, m_norm_mlp_g, m_w_up, m_w_down, v_meta_tokens, v_norm_mix_g, v_w_in, v_q_norm_g, v_k_norm_g, v_attn_sinks, v_ssm_lambda_re, v_ssm_lambda_im, v_ssm_log_step, v_ssm_b_re, v_ssm_b_im, v_ssm_c_re, v_ssm_c_im, v_ssm_d, v_w_glu, v_b_glu, v_attn_out_g, v_ssm_out_g, v_w_out, v_norm_mlp_g, v_w_up, v_w_down):
    a = dict(locals())
    order = ("meta_tokens", "norm_mix_g", "w_in", "q_norm_g", "k_norm_g", "attn_sinks", "ssm_lambda_re", "ssm_lambda_im",
             "ssm_log_step", "ssm_b_re", "ssm_b_im", "ssm_c_re", "ssm_c_im", "ssm_d", "w_glu", "b_glu", "attn_out_g",
             "ssm_out_g", "w_out", "norm_mlp_g", "w_up", "w_down")

    for n in ("w_in", "m_w_in", "v_w_in"):
        a[n] = jnp.swapaxes(a[n], 1, 2)
    no_dep = jnp.zeros((8, 128), F32)
    sp = {n: a[n] for n in SMALL}
    wb = {}
    for n in BIG:
        depth, r, c = a[n].shape
        wb[n] = _cast_bf16(a[n].reshape(depth * r, c), name="cast_" + n).reshape(depth, r, c)
    meta_all = _all_gather(meta_tokens, name="gather_meta")
    meta = jnp.transpose(meta_all, (1, 0, 2)).reshape(N_META, D_MODEL)

    gathers, exchanges = {}, {}
    updated = {n: None for n in BIG}
    groups = (("w_in",), ("w_glu", "w_out"), ("w_up",), ("w_down",))

    def start_gather(l, after):
        for gi, names in enumerate(groups):
            *handles, after = _send_start([wb[n][l] for n in names], after, per_peer=False, name=f"gather_start_l{l}_g{gi}")
            gathers[l, gi] = handles
        return after

    def weights_for_layer(l, h):
        token = start_gather(0, wb["w_in"]) if l == 0 else h
        if l + 1 < DEPTH:
            token = start_gather(l + 1, token)
        got = {}

        def fetch(name, after):
            if name not in got:
                gi = [name in names for names in groups].index(True)
                lands = _send_wait(gathers.pop((l, gi)), after, per_peer=False, name=f"gather_wait_l{l}_g{gi}")
                got.update(_full_weights(dict(zip(groups[gi], lands))))
            return got[name]

        return fetch, (token,)

    def update_layer(l, after):
        for part in ("a", "b", "c"):
            names, handles = exchanges.pop((l, part))
            recv = _send_wait(handles, after, per_peer=True, name=f"exchange_wait_l{l}_{part}")
            for n, parts in zip(names, recv):
                updated[n] = _adamw_layer(a[n], a["m_" + n], a["v_" + n], parts, l, updated[n], name=f"adamw_{n}_l{l}")

    def early_grads(l, part, gb):
        *handles, token = _send_start(_grad_blocks(gb), no_dep, per_peer=True, name=f"exchange_start_l{l}_{part}")
        exchanges[l, part] = (tuple(gb), handles)
        return (token,)

    def grads_of_layer(l, dh):
        if l + 1 < DEPTH:
            update_layer(l + 1, dh)
        return ()

    loss, dh0, gsmall = _local_step(x[0], loss_target[0], meta, sp, weights_for_layer, early_grads, grads_of_layer)
    loss = lax.psum(loss[0, 0], ("x", "y", "c"))
    grad, delta, new_m, new_v = {}, {}, {}, {}

    dmeta = jnp.transpose(dh0[PAD:BLOCK].reshape(N_META, N_DEV, D_MODEL // N_DEV), (1, 0, 2))
    outs = _adamw(meta_tokens, m_meta_tokens, v_meta_tokens, _exchange(dmeta, name="exchange_meta"), name="adamw_meta_tokens")
    grad["meta_tokens"], delta["meta_tokens"], new_m["meta_tokens"], new_v["meta_tokens"] = outs

    packed = _pack_small({n: jnp.stack(gsmall[n]) for n in SMALL}).reshape(N_DEV, _SMALL_ROWS, 128)
    share = _sum_parts(_exchange(packed, name="exchange_small"), name="sum_small")
    total = _all_gather(share, name="gather_small").reshape(1, N_DEV * _SMALL_ROWS, 128)
    gsum = _unpack_small(total, sp)
    for n in SMALL:
        as2d = lambda v: v.reshape(-1, v.shape[-1])
        outs = _adamw(as2d(a[n]), as2d(a["m_" + n]), as2d(a["v_" + n]), as2d(gsum[n])[None], name="adamw_" + n)
        grad[n], delta[n], new_m[n], new_v[n] = [o.reshape(a[n].shape) for o in outs]

    update_layer(0, outs[0])
    for n in BIG:
        grad[n], delta[n], new_m[n], new_v[n] = [jnp.swapaxes(o, 1, 2) if n == "w_in" else o for o in updated[n]]

    return (loss, dh0[BLOCK:][None], *[grad[n] for n in order], *[delta[n] for n in order],
            *[new_m[n] for n in order], *[new_v[n] for n in order])
```

```python
import math

import jax
import jax.numpy as jnp
from jax import lax
from jax.experimental import pallas as pl
from jax.experimental.pallas import tpu as pltpu

F32 = jnp.float32
BF16 = jnp.bfloat16

N_DEV = 8
D_MODEL = 2048
SEQ = 4096
DEPTH = 4
N_META = 16
HEAD_DIM = 64
ATTN_WIDTH = D_MODEL // 2
N_HEADS = ATTN_WIDTH // HEAD_DIM
N_KV_HEADS = N_HEADS // 4
KV_GROUP = N_HEADS // N_KV_HEADS
KV_WIDTH = N_KV_HEADS * HEAD_DIM
SSM_WIDTH = D_MODEL - ATTN_WIDTH
SSM_GROUP_CH = 16
SSM_GROUPS = SSM_WIDTH // SSM_GROUP_CH
SSM_STATE = 64
WINDOW = 128
BLOCK = 128
PAD = BLOCK - N_META
D_FF = 4 * D_MODEL
IN_WIDTH = ATTN_WIDTH + 2 * KV_WIDTH + SSM_WIDTH
NORM_EPS = 1e-6
NEG_INF = -1e30
ADAM_LR = 0.001
ADAM_B1 = 0.9
ADAM_B2 = 0.999
ADAM_EPS = 1e-08
ADAM_WD = 0.01
ADAM_STEP = 10

VMEM_LIMIT = 56 * 1024 * 1024
_MESH = pl.DeviceIdType.MESH
_ANY = pl.BlockSpec(memory_space=pl.ANY)


def _cparams(sem=None):
    return pltpu.CompilerParams(dimension_semantics=sem, vmem_limit_bytes=VMEM_LIMIT)


def _matmul(a, b, *, mode, tm, tn, tk, outs, epilogue, tiles=(), deps=(), blocked=False, name):
    if mode == "nn":
        m, k = a.shape
        if blocked:
            n = b.shape[0] * b.shape[2]
            assert tn == b.shape[2]
            b_spec = pl.BlockSpec((None, tk, tn), lambda i, j, kk: (j, kk, 0))
        else:
            n = b.shape[1]
            b_spec = pl.BlockSpec((tk, tn), lambda i, j, kk: (kk, j))
        a_spec = pl.BlockSpec((tm, tk), lambda i, j, kk: (i, kk))
        dims = (((1,), (0,)), ((), ()))
    elif mode == "nt":
        m, k = a.shape
        if blocked:
            n = b.shape[1]
            assert tk == b.shape[2] and k == b.shape[0] * b.shape[2]
            b_spec = pl.BlockSpec((None, tn, tk), lambda i, j, kk: (kk, j, 0))
        else:
            n = b.shape[0]
            b_spec = pl.BlockSpec((tn, tk), lambda i, j, kk: (j, kk))
        a_spec = pl.BlockSpec((tm, tk), lambda i, j, kk: (i, kk))
        dims = (((1,), (1,)), ((), ()))
    else:
        (k, m), n = a.shape, b.shape[1]
        a_spec = pl.BlockSpec((tk, tm), lambda i, j, kk: (kk, i))
        b_spec = pl.BlockSpec((tk, tn), lambda i, j, kk: (kk, j))
        dims = (((0,), (0,)), ((), ()))
    assert m % tm == 0 and n % tn == 0 and k % tk == 0, (name, m, n, k, tm, tn, tk)
    nk = k // tk
    n_tiles, n_outs, n_deps = len(tiles), len(outs), len(deps)

    def body(a_ref, b_ref, *rest):
        tile_refs = rest[:n_tiles]
        out_refs = rest[n_tiles + n_deps:n_tiles + n_deps + n_outs]

        def product():
            return lax.dot_general(a_ref[...].astype(BF16), b_ref[...].astype(BF16), dims, preferred_element_type=F32)

        def finish(acc):
            res = epilogue(acc, *[r[...] for r in tile_refs])
            for o_ref, o in zip(out_refs, res):
                o_ref[...] = o.astype(o_ref.dtype)

        if nk == 1:
            finish(product())
            return
        acc_ref = rest[-1]
        kk = pl.program_id(2)

        @pl.when(kk == 0)
        def _():
            acc_ref[...] = jnp.zeros_like(acc_ref)

        acc_ref[...] += product()

        @pl.when(kk == nk - 1)
        def _():
            finish(acc_ref[...])

    tile_spec = pl.BlockSpec((tm, tn), lambda i, j, kk: (i, j))
    if mode == "tn" and blocked:
        out_specs = [pl.BlockSpec((None, tm, tn), lambda i, j, kk: (j, i, 0))] * n_outs
        out_shape = [jax.ShapeDtypeStruct((n // tn, m, tn), dt) for dt in outs]
    else:
        out_specs = [tile_spec] * n_outs
        out_shape = [jax.ShapeDtypeStruct((m, n), dt) for dt in outs]
    return pl.pallas_call(
        body, name=name, grid=(m // tm, n // tn, nk),
        in_specs=[a_spec, b_spec] + [tile_spec] * n_tiles + [_ANY] * n_deps,
        out_specs=out_specs, out_shape=out_shape,
        scratch_shapes=[pltpu.VMEM((tm, tn), F32)] if nk > 1 else [],
        compiler_params=_cparams(("parallel", "parallel", "arbitrary")),
    )(a, b, *tiles, *deps)


def _ident(acc):
    return (acc,)


def _row_tile(n_rows, cap):
    best = BLOCK
    for t in range(BLOCK, cap + 1, BLOCK):
        if n_rows % t == 0:
            best = t
    return best


def _rmsnorm_fwd(xs, gs, *, name, deps=()):
    n_rows, width = xs[0].shape
    n = len(xs)
    tr = _row_tile(n_rows, 384)

    def body(*refs):
        o_ref = refs[-1]
        parts = []
        for x_ref, g_ref in zip(refs[:n], refs[n:2 * n]):
            x = x_ref[...]
            r = lax.rsqrt(jnp.mean(x * x, axis=-1, keepdims=True) + NORM_EPS)
            parts.append(x * r * g_ref[...])
        o_ref[...] = (parts[0] if n == 1 else jnp.concatenate(parts, axis=1)).astype(BF16)

    return pl.pallas_call(
        body, name=name, grid=(n_rows // tr,),
        in_specs=[pl.BlockSpec((tr, width), lambda i: (i, 0))] * n + [pl.BlockSpec((1, width), lambda i: (0, 0))] * n
        + [_ANY] * len(deps),
        out_specs=pl.BlockSpec((tr, n * width), lambda i: (i, 0)),
        out_shape=jax.ShapeDtypeStruct((n_rows, n * width), BF16),
        compiler_params=_cparams(("parallel",)),
    )(*xs, *gs, *deps)


def _rmsnorm_bwd(xs, gs, dy, res, *, name):
    n_rows, width = xs[0].shape
    n = len(xs)
    tr = _row_tile(n_rows, 384)
    has_res = res is not None

    def body(*refs):
        x_refs, g_refs, dy_ref = refs[:n], refs[n:2 * n], refs[2 * n]
        res_ref = refs[2 * n + 1] if has_res else None
        outs = refs[2 * n + 1 + int(has_res):]
        dx_refs, dg_refs = outs[:n], outs[n:2 * n]
        i = pl.program_id(0)
        for c in range(n):
            x = x_refs[c][...]
            d = dy_ref[:, c * width:(c + 1) * width]
            r = lax.rsqrt(jnp.mean(x * x, axis=-1, keepdims=True) + NORM_EPS)
            xh = x * r
            gd = d * g_refs[c][...]
            dx = r * (gd - xh * jnp.mean(gd * xh, axis=-1, keepdims=True))
            if has_res:
                dx = dx + res_ref[...]
                outs[2 * n][...] = dx.astype(BF16)
            dx_refs[c][...] = dx
            part = jnp.sum(d * xh, axis=0, keepdims=True)

            @pl.when(i == 0)
            def _():
                dg_refs[c][...] = part

            @pl.when(i > 0)
            def _():
                dg_refs[c][...] += part

    row_spec = pl.BlockSpec((tr, width), lambda i: (i, 0))
    vec_spec = pl.BlockSpec((1, width), lambda i: (0, 0))
    outs = pl.pallas_call(
        body, name=name, grid=(n_rows // tr,),
        in_specs=[row_spec] * n + [vec_spec] * n + [pl.BlockSpec((tr, n * width), lambda i: (i, 0))] + [row_spec] * int(has_res),
        out_specs=[row_spec] * n + [vec_spec] * n + [row_spec] * int(has_res),
        out_shape=[jax.ShapeDtypeStruct((n_rows, width), F32)] * n + [jax.ShapeDtypeStruct((1, width), F32)] * n
        + [jax.ShapeDtypeStruct((n_rows, width), BF16)] * int(has_res),
        compiler_params=_cparams(("arbitrary",)),
    )(*xs, *gs, dy, *([res] if has_res else []))
    if has_res:
        return outs[:n], outs[n:2 * n], outs[2 * n]
    return outs[:n], outs[n:]


_SCALE = 1.0 / math.sqrt(HEAD_DIM)
_DN_NT = (((1,), (1,)), ((), ()))
_DN_TN = (((0,), (0,)), ((), ()))


def _head_norm(x, g):
    r = lax.rsqrt(jnp.mean(x * x, axis=-1, keepdims=True) + NORM_EPS)
    return x * r * g, r


def _attn_bias():
    rows = KV_GROUP * BLOCK
    r = lax.broadcasted_iota(jnp.int32, (rows, 3 * BLOCK), 0)
    j = lax.broadcasted_iota(jnp.int32, (rows, 3 * BLOCK), 1)
    i = r % BLOCK
    is_meta = j < BLOCK
    dist_band = 2 * BLOCK + i - j
    ok = jnp.where(is_meta, j >= PAD, (dist_band >= 0) & (dist_band < WINDOW))
    dist = jnp.where(is_meta, i - j, dist_band).astype(F32)
    slopes = jnp.asarray([2.0 ** (-8.0 * (h + 1) / N_HEADS) for h in range(N_HEADS)], F32).reshape(N_KV_HEADS, KV_GROUP)
    slope_rows = jnp.repeat(slopes, BLOCK, axis=1)[:, :, None]
    bias = jnp.where(ok[None], -slope_rows * dist[None], NEG_INF)
    neg0 = jnp.where(j[:, :BLOCK] <= i[:, :BLOCK], 0.0, NEG_INF)
    return bias.astype(F32), neg0.astype(F32)


def _slope_col(kv):
    g = lax.broadcasted_iota(jnp.int32, (KV_GROUP * BLOCK, 1), 0) // BLOCK
    col = jnp.zeros((KV_GROUP * BLOCK, 1), F32)
    for gi in range(KV_GROUP):
        col = jnp.where(g == gi, 2.0 ** (-8.0 * (kv * KV_GROUP + gi + 1) / N_HEADS), col)
    return col


def _sink_col(sink_ref, kv):
    g = lax.broadcasted_iota(jnp.int32, (KV_GROUP * BLOCK, 1), 0) // BLOCK
    col = jnp.zeros((KV_GROUP * BLOCK, 1), F32)
    for gi in range(KV_GROUP):
        h = kv * KV_GROUP + gi
        col = jnp.where(g == gi, sink_ref[0:1, h:h + 1], col)
    return col


def _stack_heads(x, kv):
    return jnp.concatenate([x[:, (kv * KV_GROUP + g) * HEAD_DIM:(kv * KV_GROUP + g + 1) * HEAD_DIM]
                            for g in range(KV_GROUP)], axis=0)


def _attn_scores(q_ref, k_refs, gq_ref, gk_ref, sink_ref, bias_ref, neg0_ref, kv, n):
    qs = _stack_heads(q_ref[...], kv)
    kcat = jnp.concatenate([r[:, kv * HEAD_DIM:(kv + 1) * HEAD_DIM] for r in k_refs], axis=0)
    qn, rq = _head_norm(qs, gq_ref[...])
    kn, rk = _head_norm(kcat, gk_ref[...])
    s = lax.dot_general((qn * _SCALE).astype(BF16), kn.astype(BF16), _DN_NT, preferred_element_type=F32)
    first = jnp.where(n == 0, 1.0, 0.0)
    second = jnp.where(n == 1, 1.0, 0.0)
    meta = (s[:, :BLOCK] + bias_ref[kv, :, :BLOCK] + _slope_col(kv) * (-float(BLOCK) * n.astype(F32))
            + first * neg0_ref[...])
    in_prev = jnp.where(lax.broadcasted_iota(jnp.int32, (1, 2 * BLOCK), 1) < BLOCK, 1.0, 0.0)
    band = s[:, BLOCK:] + bias_ref[kv, :, BLOCK:] + NEG_INF * (first + second * in_prev)
    s = jnp.concatenate([meta, band], axis=1)
    return qs, kcat, qn, kn, rq, rk, s, _sink_col(sink_ref, kv)


def _attn_specs():
    kq = ATTN_WIDTH // KV_WIDTH
    q_spec = pl.BlockSpec((BLOCK, ATTN_WIDTH), lambda n: (n, 0))
    kv_specs = []
    for col in (kq, kq + 1):
        kv_specs += [pl.BlockSpec((BLOCK, KV_WIDTH), lambda n, col=col: (0, col)),
                     pl.BlockSpec((BLOCK, KV_WIDTH), lambda n, col=col: (jnp.maximum(n - 1, 0), col)),
                     pl.BlockSpec((BLOCK, KV_WIDTH), lambda n, col=col: (n, col))]
    small = [pl.BlockSpec((1, HEAD_DIM), lambda n: (0, 0)), pl.BlockSpec((1, HEAD_DIM), lambda n: (0, 0)),
             pl.BlockSpec((1, N_HEADS), lambda n: (0, 0)),
             pl.BlockSpec((N_KV_HEADS, KV_GROUP * BLOCK, 3 * BLOCK), lambda n: (0, 0, 0)),
             pl.BlockSpec((KV_GROUP * BLOCK, BLOCK), lambda n: (0, 0))]
    return q_spec, kv_specs, small


def _attn_fwd(proj, gq, gk, sinks, *, name):
    n_rows = proj.shape[0]
    q_spec, kv_specs, small = _attn_specs()

    def body(q_ref, k0, k1, k2, v0, v1, v2, gq_ref, gk_ref, sink_ref, bias_ref, neg0_ref, o_ref, lse_ref):
        n = pl.program_id(0)
        o_parts, lse_parts = [], []
        for kv in range(N_KV_HEADS):
            _, _, _, _, _, _, s, sink = _attn_scores(q_ref, (k0, k1, k2), gq_ref, gk_ref, sink_ref, bias_ref, neg0_ref, kv, n)
            vcat = jnp.concatenate([r[:, kv * HEAD_DIM:(kv + 1) * HEAD_DIM] for r in (v0, v1, v2)], axis=0)
            m = jnp.maximum(jnp.max(s, axis=-1, keepdims=True), sink)
            p = jnp.exp(s - m)
            l = jnp.sum(p, axis=-1, keepdims=True) + jnp.exp(sink - m)
            o = jnp.dot(p.astype(BF16), vcat.astype(BF16), preferred_element_type=F32) / l
            lse = m + jnp.log(l)
            o_parts += [o[g * BLOCK:(g + 1) * BLOCK] for g in range(KV_GROUP)]
            lse_parts += [lse[g * BLOCK:(g + 1) * BLOCK] for g in range(KV_GROUP)]
        o_ref[...] = jnp.concatenate(o_parts, axis=1)
        lse_ref[...] = jnp.concatenate(lse_parts, axis=1)

    return pl.pallas_call(
        body, name=name, grid=(n_rows // BLOCK,),
        in_specs=[q_spec] + kv_specs + small,
        out_specs=[pl.BlockSpec((BLOCK, ATTN_WIDTH), lambda n: (n, 0)), pl.BlockSpec((BLOCK, N_HEADS), lambda n: (n, 0))],
        out_shape=[jax.ShapeDtypeStruct((n_rows, ATTN_WIDTH), F32), jax.ShapeDtypeStruct((n_rows, N_HEADS), F32)],
        compiler_params=_cparams(("parallel",)),
    )(proj, proj, proj, proj, proj, proj, proj, gq, gk, sinks, *_attn_bias())


def _attn_bwd(proj, gq, gk, sinks, o, lse, do, *, name):
    n_rows = proj.shape[0]
    q_spec, kv_specs, small = _attn_specs()

    def body(q_ref, k0, k1, k2, v0, v1, v2, gq_ref, gk_ref, sink_ref, bias_ref, neg0_ref, o_ref, lse_ref, do_ref,
             dq_ref, dkb_ref, dvb_ref, dgq_ref, dgk_ref, dsink_ref, dk_ref, dv_ref):
        n = pl.program_id(0)

        @pl.when(n == 0)
        def _():
            dk_ref[...] = jnp.zeros_like(dk_ref)
            dv_ref[...] = jnp.zeros_like(dv_ref)
            dgq_ref[...] = jnp.zeros_like(dgq_ref)
            dgk_ref[...] = jnp.zeros_like(dgk_ref)
            dsink_ref[...] = jnp.zeros_like(dsink_ref)

        dq_parts, dk_parts, dv_parts, dsink_parts = [], [], [], []
        dgq = jnp.zeros((1, HEAD_DIM), F32)
        dgk = jnp.zeros((1, HEAD_DIM), F32)
        for kv in range(N_KV_HEADS):
            qs, kcat, qn, kn, rq, rk, s, sink = _attn_scores(q_ref, (k0, k1, k2), gq_ref, gk_ref, sink_ref, bias_ref, neg0_ref, kv, n)
            vcat = jnp.concatenate([r[:, kv * HEAD_DIM:(kv + 1) * HEAD_DIM] for r in (v0, v1, v2)], axis=0)
            os_ = _stack_heads(o_ref[...], kv)
            dos = _stack_heads(do_ref[...], kv)
            lse = jnp.concatenate([lse_ref[:, kv * KV_GROUP + g:kv * KV_GROUP + g + 1] for g in range(KV_GROUP)], axis=0)
            p = jnp.exp(s - lse)
            delta = jnp.sum(dos * os_, axis=-1, keepdims=True)
            dp = lax.dot_general(dos.astype(BF16), vcat.astype(BF16), _DN_NT, preferred_element_type=F32)
            ds = (p * (dp - delta)) * _SCALE
            dsink_rows = -jnp.exp(sink - lse) * delta
            dsink_parts += [jnp.sum(dsink_rows[g * BLOCK:(g + 1) * BLOCK], axis=0, keepdims=True) for g in range(KV_GROUP)]
            dv_parts.append(lax.dot_general(p.astype(BF16), dos.astype(BF16), _DN_TN, preferred_element_type=F32))
            dsb = ds.astype(BF16)
            dqn = jnp.dot(dsb, kn.astype(BF16), preferred_element_type=F32)
            dkn = lax.dot_general(dsb, qn.astype(BF16), _DN_TN, preferred_element_type=F32)
            qh = qs * rq
            gd = dqn * gq_ref[...]
            dqs = rq * (gd - qh * jnp.mean(gd * qh, axis=-1, keepdims=True))
            dgq = dgq + jnp.sum(dqn * qh, axis=0, keepdims=True)
            kh = kcat * rk
            gdk = dkn * gk_ref[...]
            dk_parts.append(rk * (gdk - kh * jnp.mean(gdk * kh, axis=-1, keepdims=True)))
            dgk = dgk + jnp.sum(dkn * kh, axis=0, keepdims=True)
            dq_parts += [dqs[g * BLOCK:(g + 1) * BLOCK] for g in range(KV_GROUP)]
        dq_ref[...] = jnp.concatenate(dq_parts, axis=1).astype(BF16)
        dkc = jnp.concatenate(dk_parts, axis=1)
        dvc = jnp.concatenate(dv_parts, axis=1)
        prev = pl.multiple_of(jnp.maximum(n - 1, 0) * BLOCK, BLOCK)
        cur = pl.multiple_of(n * BLOCK, BLOCK)
        for acc_ref, val in ((dk_ref, dkc), (dv_ref, dvc)):
            acc_ref[0:BLOCK, :] += val[0:BLOCK]
            acc_ref[pl.ds(prev, BLOCK), :] += val[BLOCK:2 * BLOCK]
            acc_ref[pl.ds(cur, BLOCK), :] += val[2 * BLOCK:3 * BLOCK]
        dgq_ref[...] += dgq
        dgk_ref[...] += dgk
        dsink_ref[...] += jnp.concatenate(dsink_parts, axis=1)

        @pl.when(n == pl.num_programs(0) - 1)
        def _():
            dkb_ref[...] = dk_ref[...].astype(BF16)
            dvb_ref[...] = dv_ref[...].astype(BF16)

    blk = lambda w: pl.BlockSpec((BLOCK, w), lambda n: (n, 0))
    full = lambda r, w: pl.BlockSpec((r, w), lambda n: (0, 0))
    return pl.pallas_call(
        body, name=name, grid=(n_rows // BLOCK,),
        in_specs=[q_spec] + kv_specs + small + [blk(ATTN_WIDTH), blk(N_HEADS), blk(ATTN_WIDTH)],
        out_specs=[blk(ATTN_WIDTH), full(n_rows, KV_WIDTH), full(n_rows, KV_WIDTH),
                   full(1, HEAD_DIM), full(1, HEAD_DIM), full(1, N_HEADS)],
        out_shape=[jax.ShapeDtypeStruct((n_rows, ATTN_WIDTH), BF16), jax.ShapeDtypeStruct((n_rows, KV_WIDTH), BF16),
                   jax.ShapeDtypeStruct((n_rows, KV_WIDTH), BF16), jax.ShapeDtypeStruct((1, HEAD_DIM), F32),
                   jax.ShapeDtypeStruct((1, HEAD_DIM), F32), jax.ShapeDtypeStruct((1, N_HEADS), F32)],
        scratch_shapes=[pltpu.VMEM((n_rows, KV_WIDTH), F32), pltpu.VMEM((n_rows, KV_WIDTH), F32)],
        compiler_params=_cparams(("arbitrary",)),
    )(proj, proj, proj, proj, proj, proj, proj, gq, gk, sinks, *_attn_bias(), o, lse, do)


SSM_LAGS = 8
SLAB_G = 128 // SSM_GROUP_CH
N_SLABS = SSM_GROUPS // SLAB_G
SLAB_STATE = SLAB_G * SSM_STATE
U_COL = (ATTN_WIDTH + 2 * KV_WIDTH) // 128


def _ssm_prep(lam_re, lam_im, log_step, b_re, b_im, c_re, c_im):
    lam = lax.complex(lam_re, lam_im)
    delta = jnp.exp(log_step)[:, None]
    lam_bar = jnp.exp(lam * delta)
    b_t = lax.complex(jnp.swapaxes(b_re, 1, 2), jnp.swapaxes(b_im, 1, 2))
    b_bar = ((lam_bar - 1.0) / lam)[:, None, :] * b_t
    pw = [jnp.ones_like(lam_bar)]
    for _ in range(SSM_LAGS):
        pw.append(pw[-1] * lam_bar)
    w = jnp.stack(pw[:SSM_LAGS])[:, :, None, :] * b_bar[None]
    wri = jnp.stack([jnp.real(w), jnp.imag(w)], axis=3)
    wc = wri.reshape(SSM_LAGS, N_SLABS, SLAB_G * SSM_GROUP_CH, 2 * SSM_STATE)
    wc = jnp.swapaxes(wc, 0, 1).reshape(N_SLABS, SSM_LAGS * 128, 2 * SSM_STATE)
    cri = jnp.stack([c_re, -c_im], axis=2).reshape(N_SLABS, SLAB_G, SSM_GROUP_CH, 2, 1, SSM_STATE)
    eye = jnp.eye(SLAB_G, dtype=F32).reshape(1, SLAB_G, 1, 1, SLAB_G, 1)
    ct = (cri * eye).reshape(N_SLABS, 128, 2 * SLAB_STATE)
    l8 = pw[SSM_LAGS]
    lam8 = jnp.concatenate([jnp.real(l8).reshape(N_SLABS, 1, SLAB_STATE), jnp.imag(l8).reshape(N_SLABS, 1, SLAB_STATE)], axis=2)
    return wc, ct, lam8


def _row_group():
    return (lax.broadcasted_iota(jnp.int32, (SSM_LAGS * 128, 1), 0) // SSM_GROUP_CH) % SLAB_G


def _spread_groups(wc):
    g_of_row = _row_group()
    return jnp.concatenate([jnp.where(g_of_row == g, wc[:, r * SSM_STATE:(r + 1) * SSM_STATE], 0.0)
                            for r in range(2) for g in range(SLAB_G)], axis=1)


def _gather_groups(dw):
    g_of_row = _row_group()
    parts = []
    for r in range(2):
        acc = jnp.zeros((SSM_LAGS * 128, SSM_STATE), F32)
        for g in range(SLAB_G):
            c0 = r * SLAB_STATE + g * SSM_STATE
            acc = acc + jnp.where(g_of_row == g, dw[:, c0:c0 + SSM_STATE], 0.0)
        parts.append(acc)
    return jnp.concatenate(parts, axis=1)


def _lagged(u, up, t_rows):
    ue = jnp.concatenate([up, u], axis=0)
    return jnp.concatenate([ue[SSM_LAGS - tau:SSM_LAGS - tau + t_rows] for tau in range(SSM_LAGS)], axis=1).astype(BF16)


def _ssm_fwd(proj, wc, cmat, lam8, dvec, l, *, name):
    n_rows = proj.shape[0]
    tt = _row_tile(n_rows, 1408)
    n_t = n_rows // tt
    sw = 2 * SLAB_STATE
    hs = SLAB_STATE

    def body(u_ref, up_ref, wc_ref, c_ref, l_ref, d_ref, y_ref, x_ref, carry_ref, w_ref):
        t = pl.program_id(1)

        @pl.when(t == 0)
        def _():
            carry_ref[...] = jnp.zeros_like(carry_ref)
            w_ref[...] = _spread_groups(wc_ref[...]).astype(BF16)

        u = u_ref[...]
        up = jnp.where(t > 0, up_ref[...], 0.0)
        x_ref[...] = jnp.dot(_lagged(u, up, tt), w_ref[...], preferred_element_type=F32)
        ar = jnp.broadcast_to(l_ref[:, :hs], (8, hs))
        ai = jnp.broadcast_to(l_ref[:, hs:], (8, hs))

        def step(b, c):
            xr, xi = c
            r0 = pl.multiple_of(b * 8, 8)
            w = x_ref[pl.ds(r0, 8), :]
            nr = w[:, :hs] + ar * xr - ai * xi
            ni = w[:, hs:] + ar * xi + ai * xr
            x_ref[pl.ds(r0, 8), :] = jnp.concatenate([nr, ni], axis=1)
            return nr, ni

        xr, xi = lax.fori_loop(0, tt // 8, step, (carry_ref[:, :hs], carry_ref[:, hs:]), unroll=8)
        carry_ref[...] = jnp.concatenate([xr, xi], axis=1)
        y_ref[...] = lax.dot_general(x_ref[...].astype(BF16), c_ref[...], _DN_NT, preferred_element_type=F32) + d_ref[...] * u

    return pl.pallas_call(
        body, name=name, grid=(N_SLABS, n_t),
        in_specs=[pl.BlockSpec((tt, 128), lambda j, t: (t, U_COL + j)),
                  pl.BlockSpec((8, 128), lambda j, t: (jnp.maximum(t * (tt // 8) - 1, 0), U_COL + j)),
                  pl.BlockSpec((None, None, SSM_LAGS * 128, 2 * SSM_STATE), lambda j, t: (l, j, 0, 0)),
                  pl.BlockSpec((None, None, 128, sw), lambda j, t: (l, j, 0, 0)),
                  pl.BlockSpec((None, None, 1, sw), lambda j, t: (l, j, 0, 0)),
                  pl.BlockSpec((1, 128), lambda j, t: (0, j))],
        out_specs=[pl.BlockSpec((tt, 128), lambda j, t: (t, j)), pl.BlockSpec((tt, sw), lambda j, t: (t, j))],
        out_shape=[jax.ShapeDtypeStruct((n_rows, SSM_WIDTH), F32), jax.ShapeDtypeStruct((n_rows, N_SLABS * sw), F32)],
        scratch_shapes=[pltpu.VMEM((8, sw), F32), pltpu.VMEM((SSM_LAGS * 128, sw), BF16)],
        compiler_params=_cparams(("parallel", "arbitrary")),
    )(proj, proj, wc, cmat, lam8, dvec)


def _ssm_bwd(proj, xs, dy, wc, cmat, lam8, dvec, l, *, name, deps=()):
    n_rows = proj.shape[0]
    tt = _row_tile(n_rows, 704)
    n_t = n_rows // tt
    sw = 2 * SLAB_STATE
    hs = SLAB_STATE

    def body(u_ref, up_ref, x_ref, xp_ref, dy_ref, wc_ref, c_ref, l_ref, d_ref,
             *rest):
        du_ref, dwc_ref, dc_ref, dl_ref, dd_ref, a_ref, carry_ref, head_ref, w_ref, dw_ref = rest[len(deps):]
        t = pl.program_id(1)
        ti = n_t - 1 - t

        @pl.when(t == 0)
        def _():
            w_ref[...] = _spread_groups(wc_ref[...]).astype(BF16)
            carry_ref[...] = jnp.zeros_like(carry_ref)
            head_ref[...] = jnp.zeros_like(head_ref)
            dw_ref[...] = jnp.zeros_like(dw_ref)
            dc_ref[...] = jnp.zeros_like(dc_ref)
            dl_ref[...] = jnp.zeros_like(dl_ref)
            dd_ref[...] = jnp.zeros_like(dd_ref)

        u = u_ref[...]
        up = jnp.where(ti > 0, up_ref[...], 0.0)
        ucat = _lagged(u, up, tt)
        dyv = dy_ref[...]
        dyb = dyv.astype(BF16)
        a_ref[...] = jnp.dot(dyb, c_ref[...], preferred_element_type=F32)
        lr = jnp.broadcast_to(l_ref[:, :hs], (8, hs))
        li = jnp.broadcast_to(l_ref[:, hs:], (8, hs))

        def step(i, c):
            cr, ci = c
            r0 = pl.multiple_of((tt // 8 - 1 - i) * 8, 8)
            g = a_ref[pl.ds(r0, 8), :]
            nr = g[:, :hs] + lr * cr + li * ci
            ni = g[:, hs:] + lr * ci - li * cr
            a_ref[pl.ds(r0, 8), :] = jnp.concatenate([nr, ni], axis=1)
            return nr, ni

        cr, ci = lax.fori_loop(0, tt // 8, step, (carry_ref[:, :hs], carry_ref[:, hs:]), unroll=8)
        carry_ref[...] = jnp.concatenate([cr, ci], axis=1)

        a = a_ref[...]
        xv = x_ref[...]
        xprev = jnp.where(ti > 0, xp_ref[...], 0.0)
        xsh = jnp.concatenate([xprev, xv[:tt - SSM_LAGS]], axis=0)
        a_re, a_im, x_re, x_im = a[:, :hs], a[:, hs:], xsh[:, :hs], xsh[:, hs:]
        dl_ref[...] += jnp.concatenate([jnp.sum(a_re * x_re + a_im * x_im, axis=0, keepdims=True),
                                        jnp.sum(a_im * x_re - a_re * x_im, axis=0, keepdims=True)], axis=1)
        ab = a.astype(BF16)
        dw_ref[...] += lax.dot_general(ucat, ab, _DN_TN, preferred_element_type=F32)
        duc = lax.dot_general(ab, w_ref[...], _DN_NT, preferred_element_type=F32)
        ext = jnp.concatenate([duc, head_ref[...]], axis=0)
        du = d_ref[...] * dyv
        for tau in range(SSM_LAGS):
            du = du + ext[tau:tau + tt, tau * 128:(tau + 1) * 128]
        head_ref[...] = duc[0:8]
        row = ti * tt + lax.broadcasted_iota(jnp.int32, (tt, 128), 0)
        du_ref[...] = jnp.where(row >= PAD, du, 0.0).astype(BF16)
        dd_ref[...] += jnp.sum(dyv * u, axis=0, keepdims=True)
        dc_ref[...] += lax.dot_general(dyb, xv.astype(BF16), _DN_TN, preferred_element_type=F32)

        @pl.when(t == n_t - 1)
        def _():
            dwc_ref[...] = _gather_groups(dw_ref[...])

    rt = lambda t: n_t - 1 - t
    prev8 = lambda t: jnp.maximum(rt(t) * (tt // 8) - 1, 0)
    return pl.pallas_call(
        body, name=name, grid=(N_SLABS, n_t),
        in_specs=[pl.BlockSpec((tt, 128), lambda j, t: (rt(t), U_COL + j)),
                  pl.BlockSpec((8, 128), lambda j, t: (prev8(t), U_COL + j)),
                  pl.BlockSpec((tt, sw), lambda j, t: (rt(t), j)),
                  pl.BlockSpec((8, sw), lambda j, t: (prev8(t), j)),
                  pl.BlockSpec((tt, 128), lambda j, t: (rt(t), j)),
                  pl.BlockSpec((None, None, SSM_LAGS * 128, 2 * SSM_STATE), lambda j, t: (l, j, 0, 0)),
                  pl.BlockSpec((None, None, 128, sw), lambda j, t: (l, j, 0, 0)),
                  pl.BlockSpec((None, None, 1, sw), lambda j, t: (l, j, 0, 0)),
                  pl.BlockSpec((1, 128), lambda j, t: (0, j))] + [_ANY] * len(deps),
        out_specs=[pl.BlockSpec((tt, 128), lambda j, t: (rt(t), j)),
                   pl.BlockSpec((None, SSM_LAGS * 128, 2 * SSM_STATE), lambda j, t: (j, 0, 0)),
                   pl.BlockSpec((None, 128, sw), lambda j, t: (j, 0, 0)),
                   pl.BlockSpec((None, 1, sw), lambda j, t: (j, 0, 0)),
                   pl.BlockSpec((1, 128), lambda j, t: (0, j))],
        out_shape=[jax.ShapeDtypeStruct((n_rows, SSM_WIDTH), BF16),
                   jax.ShapeDtypeStruct((N_SLABS, SSM_LAGS * 128, 2 * SSM_STATE), F32),
                   jax.ShapeDtypeStruct((N_SLABS, 128, sw), F32),
                   jax.ShapeDtypeStruct((N_SLABS, 1, sw), F32),
                   jax.ShapeDtypeStruct((1, SSM_WIDTH), F32)],
        scratch_shapes=[pltpu.VMEM((tt, sw), F32), pltpu.VMEM((8, sw), F32), pltpu.VMEM((8, sw), F32),
                        pltpu.VMEM((SSM_LAGS * 128, sw), BF16), pltpu.VMEM((SSM_LAGS * 128, sw), F32)],
        compiler_params=_cparams(("parallel", "arbitrary")),
    )(proj, proj, xs, xs, dy, wc, cmat, lam8, dvec, *deps)


_GELU_C = math.sqrt(2.0 / math.pi)
_GELU_A = 0.044715


def _gelu(y):
    th = jnp.tanh(_GELU_C * (y + _GELU_A * y * y * y))
    return 0.5 * y * (1.0 + th), th


def _glu_fwd(y, w, b, *, name):
    n_rows, width = y.shape
    tr = _row_tile(n_rows, 384)

    def body(y_ref, w_ref, b_ref, o_ref):
        g, _ = _gelu(y_ref[...])
        z = jnp.dot(g.astype(BF16), w_ref[...], preferred_element_type=F32) + b_ref[...]
        o_ref[...] = g * jax.nn.sigmoid(z)

    return pl.pallas_call(
        body, name=name, grid=(n_rows // tr,),
        in_specs=[pl.BlockSpec((tr, width), lambda i: (i, 0)), pl.BlockSpec((width, width), lambda i: (0, 0)),
                  pl.BlockSpec((1, width), lambda i: (0, 0))],
        out_specs=pl.BlockSpec((tr, width), lambda i: (i, 0)),
        out_shape=jax.ShapeDtypeStruct((n_rows, width), F32),
        compiler_params=_cparams(("parallel",)),
    )(y, w, b)


def _glu_bwd(y, w, b, dout, *, name):
    n_rows, width = y.shape
    tr = _row_tile(n_rows, 384)

    def body(y_ref, w_ref, b_ref, do_ref, dy_ref, g_ref, dz_ref, db_ref):
        i = pl.program_id(0)
        yv = y_ref[...]
        g, th = _gelu(yv)
        gb = g.astype(BF16)
        z = jnp.dot(gb, w_ref[...], preferred_element_type=F32) + b_ref[...]
        sg = jax.nn.sigmoid(z)
        do = do_ref[...]
        dz = do * g * sg * (1.0 - sg)
        dzb = dz.astype(BF16)
        dg = do * sg + lax.dot_general(dzb, w_ref[...], _DN_NT, preferred_element_type=F32)
        dgelu = 0.5 * (1.0 + th) + 0.5 * yv * (1.0 - th * th) * _GELU_C * (1.0 + 3.0 * _GELU_A * yv * yv)
        dy_ref[...] = dg * dgelu
        g_ref[...] = gb
        dz_ref[...] = dzb
        part = jnp.sum(dz, axis=0, keepdims=True)

        @pl.when(i == 0)
        def _():
            db_ref[...] = part

        @pl.when(i > 0)
        def _():
            db_ref[...] += part

    row = pl.BlockSpec((tr, width), lambda i: (i, 0))
    vec = pl.BlockSpec((1, width), lambda i: (0, 0))
    return pl.pallas_call(
        body, name=name, grid=(n_rows // tr,),
        in_specs=[row, pl.BlockSpec((width, width), lambda i: (0, 0)), vec, row],
        out_specs=[row, row, row, vec],
        out_shape=[jax.ShapeDtypeStruct((n_rows, width), F32), jax.ShapeDtypeStruct((n_rows, width), BF16),
                   jax.ShapeDtypeStruct((n_rows, width), BF16), jax.ShapeDtypeStruct((1, width), F32)],
        compiler_params=_cparams(("arbitrary",)),
    )(y, w, b, dout)


def _loss_head(h, target, *, name):
    n_rows, width = h.shape

    def body(h_ref, t_ref, dh_ref, dhb_ref, loss_ref):
        i = pl.program_id(0)

        @pl.when(i == 0)
        def _():
            dh_ref[...] = jnp.zeros_like(dh_ref)
            dhb_ref[...] = jnp.zeros_like(dhb_ref)
            loss_ref[...] = jnp.zeros_like(loss_ref)

        @pl.when(i > 0)
        def _():
            err = h_ref[...] - t_ref[...]
            dh = err * (1.0 / width)
            dh_ref[...] = dh
            dhb_ref[...] = dh.astype(BF16)
            loss_ref[...] += (0.5 / width) * jnp.sum(err * err, keepdims=True)

    return pl.pallas_call(
        body, name=name, grid=(n_rows // BLOCK,),
        in_specs=[pl.BlockSpec((BLOCK, width), lambda i: (i, 0)),
                  pl.BlockSpec((BLOCK, width), lambda i: (jnp.maximum(i - 1, 0), 0))],
        out_specs=[pl.BlockSpec((BLOCK, width), lambda i: (i, 0)), pl.BlockSpec((BLOCK, width), lambda i: (i, 0)),
                   pl.BlockSpec((1, 1), lambda i: (0, 0))],
        out_shape=[jax.ShapeDtypeStruct((n_rows, width), F32), jax.ShapeDtypeStruct((n_rows, width), BF16),
                   jax.ShapeDtypeStruct((1, 1), F32)],
        compiler_params=_cparams(("arbitrary",)),
    )(h, target)


def _elem_rows(n_rows, n_cols, bytes_per_row_elem):
    lanes = -(-n_cols // 128) * 128
    cap = max(16, (12 * 1024 * 1024) // (lanes * bytes_per_row_elem))
    best = None
    for t in range(16, min(n_rows, cap) + 1, 16):
        if n_rows % t == 0:
            best = t
    return best or n_rows


def _cast_bf16(x, *, name):
    n_rows, n_cols = x.shape
    tr = _elem_rows(n_rows, n_cols, 4)

    def body(x_ref, o_ref):
        o_ref[...] = x_ref[...].astype(BF16)

    spec = pl.BlockSpec((tr, n_cols), lambda i: (i, 0))
    return pl.pallas_call(body, name=name, grid=(n_rows // tr,), in_specs=[spec], out_specs=spec,
                          out_shape=jax.ShapeDtypeStruct(x.shape, BF16), compiler_params=_cparams(("parallel",)))(x)


def _adamw(w, m, v, parts, *, name):
    n_rows, n_cols = w.shape
    n_parts = parts.shape[0]
    tr = _elem_rows(n_rows, n_cols, 4 * (8 + n_parts))
    c1 = 1.0 / (1.0 - ADAM_B1 ** ADAM_STEP)
    c2 = 1.0 / (1.0 - ADAM_B2 ** ADAM_STEP)

    def body(w_ref, m_ref, v_ref, p_ref, g_ref, d_ref, nm_ref, nv_ref):
        g = p_ref[0].astype(F32)
        for k in range(1, n_parts):
            g = g + p_ref[k].astype(F32)
        nm = ADAM_B1 * m_ref[...] + (1.0 - ADAM_B1) * g
        nv = ADAM_B2 * v_ref[...] + (1.0 - ADAM_B2) * (g * g)
        g_ref[...] = g
        nm_ref[...] = nm
        nv_ref[...] = nv
        d_ref[...] = -ADAM_LR * ((nm * c1) / (jnp.sqrt(nv * c2) + ADAM_EPS) + ADAM_WD * w_ref[...])

    spec = pl.BlockSpec((tr, n_cols), lambda i: (i, 0))
    return pl.pallas_call(
        body, name=name, grid=(n_rows // tr,),
        in_specs=[spec, spec, spec, pl.BlockSpec((n_parts, tr, n_cols), lambda i: (0, i, 0))],
        out_specs=[spec] * 4, out_shape=[jax.ShapeDtypeStruct(w.shape, F32)] * 4,
        compiler_params=_cparams(("parallel",)),
    )(w, m, v, parts)


def _sum_parts(parts, *, name):
    n_parts, n_rows, n_cols = parts.shape
    tr = _elem_rows(n_rows, n_cols, 4 * (1 + n_parts))

    def body(p_ref, o_ref):
        g = p_ref[0].astype(F32)
        for k in range(1, n_parts):
            g = g + p_ref[k].astype(F32)
        o_ref[...] = g

    return pl.pallas_call(
        body, name=name, grid=(n_rows // tr,),
        in_specs=[pl.BlockSpec((n_parts, tr, n_cols), lambda i: (0, i, 0))],
        out_specs=pl.BlockSpec((tr, n_cols), lambda i: (i, 0)),
        out_shape=jax.ShapeDtypeStruct((n_rows, n_cols), F32), compiler_params=_cparams(("parallel",)),
    )(parts)


BIG = ("w_in", "w_glu", "w_out", "w_up", "w_down")
SMALL = ("norm_mix_g", "q_norm_g", "k_norm_g", "attn_sinks", "ssm_lambda_re", "ssm_lambda_im", "ssm_log_step",
         "ssm_b_re", "ssm_b_im", "ssm_c_re", "ssm_c_im", "ssm_d", "b_glu", "attn_out_g", "ssm_out_g", "norm_mlp_g")
_SSM_NAMES = ("ssm_lambda_re", "ssm_lambda_im", "ssm_log_step", "ssm_b_re", "ssm_b_im", "ssm_c_re", "ssm_c_im")


def _divisor(n, cands):
    for c in cands:
        if n % c == 0:
            return c
    return n


def _mm(a, b, mode, name, outs=(F32,), epilogue=_ident, tiles=(), deps=(), blocked=False):
    if mode == "nn":
        m, k = a.shape
        n = b.shape[0] * b.shape[2] if blocked else b.shape[1]
    elif mode == "nt":
        m, k = a.shape
        n = b.shape[1] if blocked else b.shape[0]
    else:
        (k, m), n = a.shape, b.shape[1]
    if mode == "tn":
        tm, tn, tk = _divisor(m, (1024, 512)), _divisor(n, (1024, 512)), k
    elif k <= 2560:
        tm, tn, tk = _row_tile(m, 1408), _divisor(n, (1024, 1280, 512)), k
    elif blocked:
        tm, tn, tk = (m // 2 if m % 32 == 0 else m), _divisor(n, (1024, 512)), b.shape[2]
    else:
        tm, tn, tk = _row_tile(m, 1408), _divisor(n, (1024, 512)), _divisor(k, (1024, 512))
    return _matmul(a, b, mode=mode, tm=tm, tn=tn, tk=tk, outs=list(outs), epilogue=epilogue, tiles=tiles, deps=deps,
                   blocked=blocked, name=name)


def _add_tile(acc, res):
    return (acc + res,)


def _relu_sq(acc):
    r = jnp.maximum(acc, 0.0)
    return r, r * r


def _relu_sq_bwd(acc, r):
    return (acc * (2.0 * r.astype(F32)),)


def _row(v):
    return v.reshape(1, -1)


def _layer_fwd(hres, fetch, sp, ssm, l, deps=()):
    tag = f"_l{l}"
    wts = {}
    hb = _rmsnorm_fwd([hres], [_row(sp["norm_mix_g"])], name="norm_mix" + tag, deps=deps)
    wts["w_in"] = fetch("w_in", hb)
    proj, = _mm(hb, wts["w_in"], "nt", "proj" + tag)
    gq, gk, sinks = _row(sp["q_norm_g"]), _row(sp["k_norm_g"]), _row(sp["attn_sinks"])
    o, lse = _attn_fwd(proj, gq, gk, sinks, name="attn_fwd" + tag)
    y, xs = _ssm_fwd(proj, *ssm, _row(sp["ssm_d"]), l, name="ssm_fwd" + tag)
    wts["w_glu"] = fetch("w_glu", y)
    s = _glu_fwd(y, wts["w_glu"], _row(sp["b_glu"]), name="glu_fwd" + tag)
    mix = _rmsnorm_fwd([o, s], [_row(sp["attn_out_g"]), _row(sp["ssm_out_g"])], name="norm_out" + tag)
    wts["w_out"] = fetch("w_out", mix)
    hres2, = _mm(mix, wts["w_out"], "nn", "out_proj" + tag, epilogue=_add_tile, tiles=(hres,))
    h2 = _rmsnorm_fwd([hres2], [_row(sp["norm_mlp_g"])], name="norm_mlp" + tag)
    wts["w_up"] = fetch("w_up", h2)
    r, act = _mm(h2, wts["w_up"], "nn", "mlp_up" + tag, outs=(BF16, BF16), epilogue=_relu_sq, blocked=True)
    wts["w_down"] = fetch("w_down", act)
    hres3, = _mm(act, wts["w_down"], "nn", "mlp_down" + tag, epilogue=_add_tile, tiles=(hres2,))
    saved = dict(wts=wts, hres=hres, hb=hb, proj=proj, o=o, lse=lse, ssm=ssm,
                 y=y, xs=xs, s=s, mix=mix, hres2=hres2, h2=h2, r=r, act=act)
    return hres3, saved


def _layer_bwd(dres, dres_b, sp, sv, l, early_grads, deps=()):
    tag = f"_l{l}"
    wts = sv["wts"]
    gb, gs = {}, {}
    d_up, = _mm(dres_b, wts["w_down"], "nt", "mlp_down_dx" + tag, outs=(BF16,), epilogue=_relu_sq_bwd, tiles=(sv["r"],),
                deps=deps)
    gb["w_down"], = _mm(sv["act"], dres_b, "tn", "mlp_down_dw" + tag, outs=(BF16,))
    gb["w_up"], = _mm(sv["h2"], d_up, "tn", "mlp_up_dw" + tag, outs=(BF16,), blocked=True)
    deps = early_grads(l, "a", {n: gb.pop(n) for n in ("w_up", "w_down")})
    dh2, = _mm(d_up, wts["w_up"], "nt", "mlp_up_dx" + tag, blocked=True, deps=deps)
    (dres2,), (dg,), dres2_b = _rmsnorm_bwd([sv["hres2"]], [_row(sp["norm_mlp_g"])], dh2, dres, name="norm_mlp_bwd" + tag)
    gs["norm_mlp_g"] = dg
    dmix, = _mm(dres2_b, wts["w_out"], "nt", "out_proj_dx" + tag)
    gb["w_out"], = _mm(sv["mix"], dres2_b, "tn", "out_proj_dw" + tag, outs=(BF16,))
    (do, ds), (dga, dgs) = _rmsnorm_bwd([sv["o"], sv["s"]], [_row(sp["attn_out_g"]), _row(sp["ssm_out_g"])], dmix, None,
                                        name="norm_out_bwd" + tag)
    gs["attn_out_g"], gs["ssm_out_g"] = dga, dgs
    dy, g_b, dz_b, db = _glu_bwd(sv["y"], wts["w_glu"], _row(sp["b_glu"]), ds, name="glu_bwd" + tag)
    gs["b_glu"] = db
    gb["w_glu"], = _mm(g_b, dz_b, "tn", "glu_dw" + tag, outs=(BF16,))
    deps = early_grads(l, "b", {n: gb.pop(n) for n in ("w_out", "w_glu")})
    du, dwc, dcmat, dlam8, dd = _ssm_bwd(sv["proj"], sv["xs"], dy, *sv["ssm"], _row(sp["ssm_d"]), l,
                                         name="ssm_bwd" + tag, deps=deps)
    gs["ssm_d"] = dd
    gs["ssm_prep"] = (dwc, dcmat, dlam8)
    dq, dk, dv, dgq, dgk, dsinks = _attn_bwd(sv["proj"], _row(sp["q_norm_g"]), _row(sp["k_norm_g"]), _row(sp["attn_sinks"]),
                                             sv["o"], sv["lse"], do, name="attn_bwd" + tag)
    gs["q_norm_g"], gs["k_norm_g"], gs["attn_sinks"] = dgq, dgk, dsinks
    dproj = _concat_cols([dq, dk, dv, du], name="dproj" + tag)
    gb["w_in"], = _mm(dproj, sv["hb"], "tn", "proj_dw" + tag, outs=(BF16,))
    deps = early_grads(l, "c", {"w_in": gb.pop("w_in")})
    dh, = _mm(dproj, wts["w_in"], "nn", "proj_dx" + tag, deps=deps)
    (dres_in,), (dg,), dres_in_b = _rmsnorm_bwd([sv["hres"]], [_row(sp["norm_mix_g"])], dh, dres2, name="norm_mix_bwd" + tag)
    gs["norm_mix_g"] = dg
    return dres_in, dres_in_b, gs


def _local_step(x, target, meta, sp, weights_for_layer, early_grads, grads_of_layer):
    h = jnp.concatenate([jnp.zeros((PAD, x.shape[1]), F32), meta, x], axis=0)
    (wc, ct, lam8), prep_vjp = jax.vjp(jax.vmap(_ssm_prep), *[sp[n] for n in _SSM_NAMES])
    ssm = (wc, ct.astype(BF16), lam8)
    saved = []
    for l in range(DEPTH):
        fetch, deps = weights_for_layer(l, h)
        h, sv = _layer_fwd(h, fetch, {n: sp[n][l] for n in SMALL}, ssm, l, deps)
        saved.append(sv)
    dh, dh_b, loss = _loss_head(h, target, name="loss_head")
    gsmall = {n: [None] * DEPTH for n in SMALL}
    dprep = [None] * DEPTH
    deps = ()
    for l in reversed(range(DEPTH)):
        dh, dh_b, gs = _layer_bwd(dh, dh_b, {n: sp[n][l] for n in SMALL}, saved[l], l, early_grads, deps)
        deps = grads_of_layer(l, dh)
        dprep[l] = gs.pop("ssm_prep")
        for n in gs:
            gsmall[n][l] = gs[n].reshape(sp[n][l].shape)
    for n, g in zip(_SSM_NAMES, prep_vjp(tuple(jnp.stack(c) for c in zip(*dprep)))):
        gsmall[n] = list(g)
    return loss, dh, gsmall


def _all_gather(x, *, name):
    def body(x_ref, out_ref, send_sems, recv_sems, local_sem):
        x, y, c = lax.axis_index("x"), lax.axis_index("y"), lax.axis_index("c")
        me, sibling = (x, y, c), (x, y, 1 - c)
        chips = [(1 - x, y), (x, 1 - y), (1 - x, 1 - y)]

        def slot(px, py, pc):
            return out_ref.at[4 * px + 2 * py + pc]

        def copy(k, block, to, src=None):
            return pltpu.make_async_remote_copy(
                src_ref=slot(*block) if src is None else src, dst_ref=slot(*block),
                send_sem=send_sems.at[k], recv_sem=recv_sems.at[k], device_id=to, device_id_type=_MESH)

        mine = pltpu.make_async_copy(x_ref, slot(*me), local_sem)
        mine.start()
        first = [copy(0, me, sibling, src=x_ref)]
        first += [copy(1 + j, me, (*chip, c), src=x_ref) for j, chip in enumerate(chips)]
        for cp in first:
            cp.start()
        passed = [copy(4 + j, (*chip, c), sibling) for j, chip in enumerate(chips)]
        for j, chip in enumerate(chips):
            copy(1 + j, (*chip, c), me).wait_recv()
            passed[j].start()
        copy(0, sibling, me).wait_recv()
        for j, chip in enumerate(chips):
            copy(4 + j, (*chip, 1 - c), me).wait_recv()
        for cp in first + passed:
            cp.wait_send()
        mine.wait()

    return pl.pallas_call(
        body, name=name, out_shape=jax.ShapeDtypeStruct((N_DEV,) + x.shape, x.dtype),
        in_specs=[_ANY], out_specs=_ANY,
        scratch_shapes=[pltpu.SemaphoreType.DMA((7,)), pltpu.SemaphoreType.DMA((7,)), pltpu.SemaphoreType.DMA],
    )(x)


def _exchange(g, *, name):
    def body(g_ref, r_ref, send_sems, recv_sems, local_sem):
        x, y, c = lax.axis_index("x"), lax.axis_index("y"), lax.axis_index("c")
        me = 4 * x + 2 * y + c
        mine = pltpu.make_async_copy(g_ref.at[me], r_ref.at[me], local_sem)
        mine.start()

        def peer(k):
            px, py, pc = (x + (k >> 2)) % 2, (y + ((k >> 1) & 1)) % 2, (c + (k & 1)) % 2
            return (px, py, pc), 4 * px + 2 * py + pc

        def copy(k, src_block, dst_block):
            to, _ = peer(k)
            return pltpu.make_async_remote_copy(
                src_ref=g_ref.at[src_block], dst_ref=r_ref.at[dst_block],
                send_sem=send_sems.at[k - 1], recv_sem=recv_sems.at[k - 1], device_id=to, device_id_type=_MESH)

        sends = [copy(k, peer(k)[1], me) for k in range(1, N_DEV)]
        for cp in sends:
            cp.start()
        for k in range(1, N_DEV):
            copy(k, me, peer(k)[1]).wait_recv()
        for cp in sends:
            cp.wait_send()
        mine.wait()

    return pl.pallas_call(
        body, name=name, out_shape=jax.ShapeDtypeStruct(g.shape, g.dtype),
        in_specs=[_ANY], out_specs=_ANY,
        scratch_shapes=[pltpu.SemaphoreType.DMA((7,)), pltpu.SemaphoreType.DMA((7,)), pltpu.SemaphoreType.DMA],
    )(g)


_HBM = pl.BlockSpec(memory_space=pltpu.HBM)
_SEM = pl.BlockSpec(memory_space=pltpu.SEMAPHORE)
_EFFECT = pltpu.SideEffectType.DATAFLOW_SIDE_EFFECTING
N_PEERS = N_DEV - 1


def _me_and_peers():
    x, y, c = lax.axis_index("x"), lax.axis_index("y"), lax.axis_index("c")
    peers = []
    for k in range(1, N_DEV):
        px, py, pc = (x + (k >> 2)) % 2, (y + ((k >> 1) & 1)) % 2, (c + (k & 1)) % 2
        peers.append(((px, py, pc), 4 * px + 2 * py + pc))
    return 4 * x + 2 * y + c, peers


def _send_start(srcs, after, *, per_peer, name):
    n_t = len(srcs)
    blks = [s.shape[1:] if per_peer else s.shape for s in srcs]
    lands = [lax.empty((N_DEV,) + b, s.dtype) for b, s in zip(blks, srcs)]

    def body(*refs):
        src_refs, land_refs = refs[:n_t], refs[n_t:2 * n_t]
        send_sems, recv_sems = refs[2 * n_t + 1], refs[2 * n_t + 2]
        token = refs[-1]
        me, peers = _me_and_peers()
        for t in range(n_t):
            for k, (to, idx) in enumerate(peers):
                pltpu.make_async_remote_copy(
                    src_ref=src_refs[t].at[idx] if per_peer else src_refs[t], dst_ref=land_refs[t].at[me],
                    send_sem=send_sems.at[t * N_PEERS + k], recv_sem=recv_sems.at[t * N_PEERS + k],
                    device_id=to, device_id_type=_MESH).start()
        token[...] = jnp.zeros_like(token)

    sems = pltpu.SemaphoreType.DMA((n_t * N_PEERS,))
    outs = pl.pallas_call(
        body, name=name,
        out_shape=(sems, sems, *[pltpu.HBM(s.shape, s.dtype) for s in srcs], *[pltpu.HBM(z.shape, z.dtype) for z in lands],
                   jax.ShapeDtypeStruct((8, 128), F32)),
        in_specs=[_HBM] * (2 * n_t) + [_ANY],
        out_specs=(_SEM, _SEM, *[_HBM] * (2 * n_t), pl.BlockSpec(memory_space=pltpu.VMEM)),
        input_output_aliases={i: 2 + i for i in range(2 * n_t)},
        compiler_params=pltpu.CompilerParams(has_side_effects=_EFFECT),
    )(*[pltpu.with_memory_space_constraint(s, pltpu.HBM) for s in srcs],
      *[pltpu.with_memory_space_constraint(z, pltpu.HBM) for z in lands], after)
    return outs[0], outs[1], list(outs[2:2 + n_t]), list(outs[2 + n_t:2 + 2 * n_t]), outs[-1]


def _send_wait(handles, after, *, per_peer, name):
    send_sems, recv_sems, srcs, lands = handles
    n_t = len(srcs)

    def body(*refs):
        src_refs, land_refs = refs[:n_t], refs[n_t:2 * n_t]
        send_sems, recv_sems = refs[2 * n_t], refs[2 * n_t + 1]
        _, peers = _me_and_peers()
        for t in range(n_t):
            for k, (to, idx) in enumerate(peers):
                cp = pltpu.make_async_remote_copy(
                    src_ref=src_refs[t].at[idx] if per_peer else src_refs[t], dst_ref=land_refs[t].at[idx],
                    send_sem=send_sems.at[t * N_PEERS + k], recv_sem=recv_sems.at[t * N_PEERS + k],
                    device_id=to, device_id_type=_MESH)
                cp.wait_send()
                cp.wait_recv()

    outs = pl.pallas_call(
        body, name=name,
        out_shape=(*[pltpu.HBM(s.shape, s.dtype) for s in srcs], *[pltpu.HBM(z.shape, z.dtype) for z in lands]),
        in_specs=[_HBM] * (2 * n_t) + [_SEM, _SEM, _ANY], out_specs=tuple([_HBM] * (2 * n_t)),
        input_output_aliases={i: i for i in range(2 * n_t)},
        compiler_params=pltpu.CompilerParams(has_side_effects=_EFFECT),
    )(*srcs, *lands, send_sems, recv_sems, after)
    me = 4 * lax.axis_index("x") + 2 * lax.axis_index("y") + lax.axis_index("c")
    filled = []
    for src, land in zip(outs[:n_t], outs[n_t:]):
        own = lax.dynamic_index_in_dim(src, me, 0, keepdims=False) if per_peer else src
        filled.append(lax.dynamic_update_index_in_dim(land, own, me, 0))
    return filled


def _adamw_layer(w, m, v, parts, l, prev, *, name):
    depth, n_rows, n_cols = w.shape
    n_parts = parts.shape[0]
    tr = _elem_rows(n_rows, n_cols, 4 * (8 + n_parts))
    c1 = 1.0 / (1.0 - ADAM_B1 ** ADAM_STEP)
    c2 = 1.0 / (1.0 - ADAM_B2 ** ADAM_STEP)
    n_prev = 0 if prev is None else 4

    def body(w_ref, m_ref, v_ref, p_ref, *rest):
        g_ref, d_ref, nm_ref, nv_ref = rest[n_prev:]
        g = p_ref[0].astype(F32)
        for k in range(1, n_parts):
            g = g + p_ref[k].astype(F32)
        nm = ADAM_B1 * m_ref[...] + (1.0 - ADAM_B1) * g
        nv = ADAM_B2 * v_ref[...] + (1.0 - ADAM_B2) * (g * g)
        g_ref[...] = g
        nm_ref[...] = nm
        nv_ref[...] = nv
        d_ref[...] = -ADAM_LR * ((nm * c1) / (jnp.sqrt(nv * c2) + ADAM_EPS) + ADAM_WD * w_ref[...])

    spec = pl.BlockSpec((None, tr, n_cols), lambda i: (l, i, 0))
    return pl.pallas_call(
        body, name=name, grid=(n_rows // tr,),
        in_specs=[spec, spec, spec, pl.BlockSpec((n_parts, tr, n_cols), lambda i: (0, i, 0))] + [_ANY] * n_prev,
        out_specs=[spec] * 4, out_shape=[jax.ShapeDtypeStruct(w.shape, F32)] * 4,
        input_output_aliases={4 + i: i for i in range(n_prev)},
        compiler_params=_cparams(("parallel",)),
    )(w, m, v, parts, *(prev or ()))


def _concat_cols(parts, *, name):
    n_rows = parts[0].shape[0]
    widths = [p.shape[1] for p in parts]
    tr = _row_tile(n_rows, 1408)

    def body(*refs):
        o_ref, off = refs[-1], 0
        for p_ref, w in zip(refs[:-1], widths):
            o_ref[:, off:off + w] = p_ref[...]
            off += w

    return pl.pallas_call(
        body, name=name, grid=(n_rows // tr,), in_specs=[pl.BlockSpec((tr, w), lambda i: (i, 0)) for w in widths],
        out_specs=pl.BlockSpec((tr, sum(widths)), lambda i: (i, 0)),
        out_shape=jax.ShapeDtypeStruct((n_rows, sum(widths)), parts[0].dtype), compiler_params=_cparams(("parallel",)))(*parts)


def _full_weights(g):
    return {n: v if n == "w_up" else v.reshape(N_DEV * v.shape[1], v.shape[2]) for n, v in g.items()}


def _grad_blocks(gb):
    return [g if n == "w_up" else g.reshape(N_DEV, g.shape[0] // N_DEV, g.shape[1]) for n, g in gb.items()]


_SMALL_ROWS = 1096


def _pack_small(d):
    flat = jnp.concatenate([d[n].reshape(-1) for n in SMALL])
    total = N_DEV * _SMALL_ROWS * 128
    assert flat.shape[0] <= total
    return jnp.pad(flat, (0, total - flat.shape[0])).reshape(N_DEV * _SMALL_ROWS, 128)


def _unpack_small(packed, like):
    flat = packed.reshape(-1)
    out, off = {}, 0
    for n in SMALL:
        size = like[n].size
        out[n] = flat[off:off + size].reshape(like[n].shape)
        off += size
    return out


def kernel(x, meta_tokens, norm_mix_g, w_in, q_norm_g, k_norm_g, attn_sinks, ssm_lambda_re, ssm_lambda_im, ssm_log_step, ssm_b_re, ssm_b_im, ssm_c_re, ssm_c_im, ssm_d, w_glu, b_glu, attn_out_g, ssm_out_g, w_out, norm_mlp_g, w_up, w_down, loss_target, m_meta_tokens, m_norm_mix_g, m_w_in, m_q_norm_g, m_k_norm_g, m_attn_sinks, m_ssm_lambda_re, m_ssm_lambda_im, m_ssm_log_step, m_ssm_b_re, m_ssm_b_im, m_ssm_c_re, m_ssm_c_im, m_ssm_d, m_w_glu, m_b_glu, m_attn_out_g, m_ssm_out_g, m_w_out, m_norm_mlp_g, m_w_up, m_w_down, v_meta_tokens, v_norm_mix_g, v_w_in, v_q_norm_g, v_k_norm_g, v_attn_sinks, v_ssm_lambda_re, v_ssm_lambda_im, v_ssm_log_step, v_ssm_b_re, v_ssm_b_im, v_ssm_c_re, v_ssm_c_im, v_ssm_d, v_w_glu, v_b_glu, v_attn_out_g, v_ssm_out_g, v_w_out, v_norm_mlp_g, v_w_up, v_w_down):
    a = dict(locals())
    order = ("meta_tokens", "norm_mix_g", "w_in", "q_norm_g", "k_norm_g", "attn_sinks", "ssm_lambda_re", "ssm_lambda_im",
             "ssm_log_step", "ssm_b_re", "ssm_b_im", "ssm_c_re", "ssm_c_im", "ssm_d", "w_glu", "b_glu", "attn_out_g",
             "ssm_out_g", "w_out", "norm_mlp_g", "w_up", "w_down")

    for n in ("w_in", "m_w_in", "v_w_in"):
        a[n] = jnp.swapaxes(a[n], 1, 2)
    no_dep = jnp.zeros((8, 128), F32)
    sp = {n: a[n] for n in SMALL}
    wb = {}
    for n in BIG:
        depth, r, c = a[n].shape
        wb[n] = _cast_bf16(a[n].reshape(depth * r, c), name="cast_" + n).reshape(depth, r, c)
    meta_all = _all_gather(meta_tokens, name="gather_meta")
    meta = jnp.transpose(meta_all, (1, 0, 2)).reshape(N_META, D_MODEL)

    gathers, exchanges = {}, {}
    updated = {n: None for n in BIG}
    groups = (("w_in",), ("w_glu", "w_out"), ("w_up",), ("w_down",))

    def start_gather(l, after):
        for gi, names in enumerate(groups):
            *handles, after = _send_start([wb[n][l] for n in names], after, per_peer=False, name=f"gather_start_l{l}_g{gi}")
            gathers[l, gi] = handles
        return after

    def weights_for_layer(l, h):
        token = start_gather(0, wb["w_in"]) if l == 0 else h
        if l + 1 < DEPTH:
            token = start_gather(l + 1, token)
        got = {}

        def fetch(name, after):
            if name not in got:
                gi = [name in names for names in groups].index(True)
                lands = _send_wait(gathers.pop((l, gi)), after, per_peer=False, name=f"gather_wait_l{l}_g{gi}")
                got.update(_full_weights(dict(zip(groups[gi], lands))))
            return got[name]

        return fetch, (token,)

    def update_layer(l, after):
        for part in ("a", "b", "c"):
            names, handles = exchanges.pop((l, part))
            recv = _send_wait(handles, after, per_peer=True, name=f"exchange_wait_l{l}_{part}")
            for n, parts in zip(names, recv):
                updated[n] = _adamw_layer(a[n], a["m_" + n], a["v_" + n], parts, l, updated[n], name=f"adamw_{n}_l{l}")

    def early_grads(l, part, gb):
        *handles, token = _send_start(_grad_blocks(gb), no_dep, per_peer=True, name=f"exchange_start_l{l}_{part}")
        exchanges[l, part] = (tuple(gb), handles)
        return (token,)

    def grads_of_layer(l, dh):
        if l + 1 < DEPTH:
            update_layer(l + 1, dh)
        return ()

    loss, dh0, gsmall = _local_step(x[0], loss_target[0], meta, sp, weights_for_layer, early_grads, grads_of_layer)
    loss = lax.psum(loss[0, 0], ("x", "y", "c"))
    grad, delta, new_m, new_v = {}, {}, {}, {}

    dmeta = jnp.transpose(dh0[PAD:BLOCK].reshape(N_META, N_DEV, D_MODEL // N_DEV), (1, 0, 2))
    outs = _adamw(meta_tokens, m_meta_tokens, v_meta_tokens, _exchange(dmeta, name="exchange_meta"), name="adamw_meta_tokens")
    grad["meta_tokens"], delta["meta_tokens"], new_m["meta_tokens"], new_v["meta_tokens"] = outs

    packed = _pack_small({n: jnp.stack(gsmall[n]) for n in SMALL}).reshape(N_DEV, _SMALL_ROWS, 128)
    share = _sum_parts(_exchange(packed, name="exchange_small"), name="sum_small")
    total = _all_gather(share, name="gather_small").reshape(1, N_DEV * _SMALL_ROWS, 128)
    gsum = _unpack_small(total, sp)
    for n in SMALL:
        as2d = lambda v: v.reshape(-1, v.shape[-1])
        outs = _adamw(as2d(a[n]), as2d(a["m_" + n]), as2d(a["v_" + n]), as2d(gsum[n])[None], name="adamw_" + n)
        grad[n], delta[n], new_m[n], new_v[n] = [o.reshape(a[n].shape) for o in outs]

    update_layer(0, outs[0])
    for n in BIG:
        grad[n], delta[n], new_m[n], new_v[n] = [jnp.swapaxes(o, 1, 2) if n == "w_in" else o for o in updated[n]]

    return (loss, dh0[BLOCK:][None], *[grad[n] for n in order], *[delta[n] for n in order],
            *[new_m[n] for n in order], *[new_v[n] for n in order])
```

```python
import math

import jax
import jax.numpy as jnp
from jax import lax
from jax.experimental import pallas as pl
from jax.experimental.pallas import tpu as pltpu

F32 = jnp.float32
BF16 = jnp.bfloat16

N_DEV = 8
D_MODEL = 2048
SEQ = 4096
DEPTH = 4
N_META = 16
HEAD_DIM = 64
ATTN_WIDTH = D_MODEL // 2
N_HEADS = ATTN_WIDTH // HEAD_DIM
N_KV_HEADS = N_HEADS // 4
KV_GROUP = N_HEADS // N_KV_HEADS
KV_WIDTH = N_KV_HEADS * HEAD_DIM
SSM_WIDTH = D_MODEL - ATTN_WIDTH
SSM_GROUP_CH = 16
SSM_GROUPS = SSM_WIDTH // SSM_GROUP_CH
SSM_STATE = 64
WINDOW = 128
BLOCK = 128
PAD = BLOCK - N_META
D_FF = 4 * D_MODEL
IN_WIDTH = ATTN_WIDTH + 2 * KV_WIDTH + SSM_WIDTH
NORM_EPS = 1e-6
NEG_INF = -1e30
ADAM_LR = 0.001
ADAM_B1 = 0.9
ADAM_B2 = 0.999
ADAM_EPS = 1e-08
ADAM_WD = 0.01
ADAM_STEP = 10

VMEM_LIMIT = 56 * 1024 * 1024
_MESH = pl.DeviceIdType.MESH
_ANY = pl.BlockSpec(memory_space=pl.ANY)


def _cparams(sem=None):
    return pltpu.CompilerParams(dimension_semantics=sem, vmem_limit_bytes=VMEM_LIMIT)


def _matmul(a, b, *, mode, tm, tn, tk, outs, epilogue, tiles=(), deps=(), blocked=False, name):
    if mode == "nn":
        m, k = a.shape
        if blocked:
            n = b.shape[0] * b.shape[2]
            assert tn == b.shape[2]
            b_spec = pl.BlockSpec((None, tk, tn), lambda i, j, kk: (j, kk, 0))
        else:
            n = b.shape[1]
            b_spec = pl.BlockSpec((tk, tn), lambda i, j, kk: (kk, j))
        a_spec = pl.BlockSpec((tm, tk), lambda i, j, kk: (i, kk))
        dims = (((1,), (0,)), ((), ()))
    elif mode == "nt":
        m, k = a.shape
        if blocked:
            n = b.shape[1]
            assert tk == b.shape[2] and k == b.shape[0] * b.shape[2]
            b_spec = pl.BlockSpec((None, tn, tk), lambda i, j, kk: (kk, j, 0))
        else:
            n = b.shape[0]
            b_spec = pl.BlockSpec((tn, tk), lambda i, j, kk: (j, kk))
        a_spec = pl.BlockSpec((tm, tk), lambda i, j, kk: (i, kk))
        dims = (((1,), (1,)), ((), ()))
    else:
        (k, m), n = a.shape, b.shape[1]
        a_spec = pl.BlockSpec((tk, tm), lambda i, j, kk: (kk, i))
        b_spec = pl.BlockSpec((tk, tn), lambda i, j, kk: (kk, j))
        dims = (((0,), (0,)), ((), ()))
    assert m % tm == 0 and n % tn == 0 and k % tk == 0, (name, m, n, k, tm, tn, tk)
    nk = k // tk
    n_tiles, n_outs, n_deps = len(tiles), len(outs), len(deps)

    def body(a_ref, b_ref, *rest):
        tile_refs = rest[:n_tiles]
        out_refs = rest[n_tiles + n_deps:n_tiles + n_deps + n_outs]

        def product():
            return lax.dot_general(a_ref[...].astype(BF16), b_ref[...].astype(BF16), dims, preferred_element_type=F32)

        def finish(acc):
            res = epilogue(acc, *[r[...] for r in tile_refs])
            for o_ref, o in zip(out_refs, res):
                o_ref[...] = o.astype(o_ref.dtype)

        if nk == 1:
            finish(product())
            return
        acc_ref = rest[-1]
        kk = pl.program_id(2)

        @pl.when(kk == 0)
        def _():
            acc_ref[...] = jnp.zeros_like(acc_ref)

        acc_ref[...] += product()

        @pl.when(kk == nk - 1)
        def _():
            finish(acc_ref[...])

    tile_spec = pl.BlockSpec((tm, tn), lambda i, j, kk: (i, j))
    if mode == "tn" and blocked:
        out_specs = [pl.BlockSpec((None, tm, tn), lambda i, j, kk: (j, i, 0))] * n_outs
        out_shape = [jax.ShapeDtypeStruct((n // tn, m, tn), dt) for dt in outs]
    else:
        out_specs = [tile_spec] * n_outs
        out_shape = [jax.ShapeDtypeStruct((m, n), dt) for dt in outs]
    return pl.pallas_call(
        body, name=name, grid=(m // tm, n // tn, nk),
        in_specs=[a_spec, b_spec] + [tile_spec] * n_tiles + [_ANY] * n_deps,
        out_specs=out_specs, out_shape=out_shape,
        scratch_shapes=[pltpu.VMEM((tm, tn), F32)] if nk > 1 else [],
        compiler_params=_cparams(("parallel", "parallel", "arbitrary")),
    )(a, b, *tiles, *deps)


def _ident(acc):
    return (acc,)


def _row_tile(n_rows, cap):
    best = BLOCK
    for t in range(BLOCK, cap + 1, BLOCK):
        if n_rows % t == 0:
            best = t
    return best


def _rmsnorm_fwd(xs, gs, *, name, deps=()):
    n_rows, width = xs[0].shape
    n = len(xs)
    tr = _row_tile(n_rows, 384)

    def body(*refs):
        o_ref = refs[-1]
        parts = []
        for x_ref, g_ref in zip(refs[:n], refs[n:2 * n]):
            x = x_ref[...]
            r = lax.rsqrt(jnp.mean(x * x, axis=-1, keepdims=True) + NORM_EPS)
            parts.append(x * r * g_ref[...])
        o_ref[...] = (parts[0] if n == 1 else jnp.concatenate(parts, axis=1)).astype(BF16)

    return pl.pallas_call(
        body, name=name, grid=(n_rows // tr,),
        in_specs=[pl.BlockSpec((tr, width), lambda i: (i, 0))] * n + [pl.BlockSpec((1, width), lambda i: (0, 0))] * n
        + [_ANY] * len(deps),
        out_specs=pl.BlockSpec((tr, n * width), lambda i: (i, 0)),
        out_shape=jax.ShapeDtypeStruct((n_rows, n * width), BF16),
        compiler_params=_cparams(("parallel",)),
    )(*xs, *gs, *deps)


def _rmsnorm_bwd(xs, gs, dy, res, *, name):
    n_rows, width = xs[0].shape
    n = len(xs)
    tr = _row_tile(n_rows, 384)
    has_res = res is not None

    def body(*refs):
        x_refs, g_refs, dy_ref = refs[:n], refs[n:2 * n], refs[2 * n]
        res_ref = refs[2 * n + 1] if has_res else None
        outs = refs[2 * n + 1 + int(has_res):]
        dx_refs, dg_refs = outs[:n], outs[n:2 * n]
        i = pl.program_id(0)
        for c in range(n):
            x = x_refs[c][...]
            d = dy_ref[:, c * width:(c + 1) * width]
            r = lax.rsqrt(jnp.mean(x * x, axis=-1, keepdims=True) + NORM_EPS)
            xh = x * r
            gd = d * g_refs[c][...]
            dx = r * (gd - xh * jnp.mean(gd * xh, axis=-1, keepdims=True))
            if has_res:
                dx = dx + res_ref[...]
                outs[2 * n][...] = dx.astype(BF16)
            dx_refs[c][...] = dx
            part = jnp.sum(d * xh, axis=0, keepdims=True)

            @pl.when(i == 0)
            def _():
                dg_refs[c][...] = part

            @pl.when(i > 0)
            def _():
                dg_refs[c][...] += part

    row_spec = pl.BlockSpec((tr, width), lambda i: (i, 0))
    vec_spec = pl.BlockSpec((1, width), lambda i: (0, 0))
    outs = pl.pallas_call(
        body, name=name, grid=(n_rows // tr,),
        in_specs=[row_spec] * n + [vec_spec] * n + [pl.BlockSpec((tr, n * width), lambda i: (i, 0))] + [row_spec] * int(has_res),
        out_specs=[row_spec] * n + [vec_spec] * n + [row_spec] * int(has_res),
        out_shape=[jax.ShapeDtypeStruct((n_rows, width), F32)] * n + [jax.ShapeDtypeStruct((1, width), F32)] * n
        + [jax.ShapeDtypeStruct((n_rows, width), BF16)] * int(has_res),
        compiler_params=_cparams(("arbitrary",)),
    )(*xs, *gs, dy, *([res] if has_res else []))
    if has_res:
        return outs[:n], outs[n:2 * n], outs[2 * n]
    return outs[:n], outs[n:]


_SCALE = 1.0 / math.sqrt(HEAD_DIM)
_DN_NT = (((1,), (1,)), ((), ()))
_DN_TN = (((0,), (0,)), ((), ()))


def _head_norm(x, g):
    r = lax.rsqrt(jnp.mean(x * x, axis=-1, keepdims=True) + NORM_EPS)
    return x * r * g, r


def _attn_bias():
    rows = KV_GROUP * BLOCK
    r = lax.broadcasted_iota(jnp.int32, (rows, 3 * BLOCK), 0)
    j = lax.broadcasted_iota(jnp.int32, (rows, 3 * BLOCK), 1)
    i = r % BLOCK
    is_meta = j < BLOCK
    dist_band = 2 * BLOCK + i - j
    ok = jnp.where(is_meta, j >= PAD, (dist_band >= 0) & (dist_band < WINDOW))
    dist = jnp.where(is_meta, i - j, dist_band).astype(F32)
    slopes = jnp.asarray([2.0 ** (-8.0 * (h + 1) / N_HEADS) for h in range(N_HEADS)], F32).reshape(N_KV_HEADS, KV_GROUP)
    slope_rows = jnp.repeat(slopes, BLOCK, axis=1)[:, :, None]
    bias = jnp.where(ok[None], -slope_rows * dist[None], NEG_INF)
    neg0 = jnp.where(j[:, :BLOCK] <= i[:, :BLOCK], 0.0, NEG_INF)
    return bias.astype(F32), neg0.astype(F32)


def _slope_col(kv):
    g = lax.broadcasted_iota(jnp.int32, (KV_GROUP * BLOCK, 1), 0) // BLOCK
    col = jnp.zeros((KV_GROUP * BLOCK, 1), F32)
    for gi in range(KV_GROUP):
        col = jnp.where(g == gi, 2.0 ** (-8.0 * (kv * KV_GROUP + gi + 1) / N_HEADS), col)
    return col


def _sink_col(sink_ref, kv):
    g = lax.broadcasted_iota(jnp.int32, (KV_GROUP * BLOCK, 1), 0) // BLOCK
    col = jnp.zeros((KV_GROUP * BLOCK, 1), F32)
    for gi in range(KV_GROUP):
        h = kv * KV_GROUP + gi
        col = jnp.where(g == gi, sink_ref[0:1, h:h + 1], col)
    return col


def _stack_heads(x, kv):
    return jnp.concatenate([x[:, (kv * KV_GROUP + g) * HEAD_DIM:(kv * KV_GROUP + g + 1) * HEAD_DIM]
                            for g in range(KV_GROUP)], axis=0)


def _attn_scores(q_ref, k_refs, gq_ref, gk_ref, sink_ref, bias_ref, neg0_ref, kv, n):
    qs = _stack_heads(q_ref[...], kv)
    kcat = jnp.concatenate([r[:, kv * HEAD_DIM:(kv + 1) * HEAD_DIM] for r in k_refs], axis=0)
    qn, rq = _head_norm(qs, gq_ref[...])
    kn, rk = _head_norm(kcat, gk_ref[...])
    s = lax.dot_general((qn * _SCALE).astype(BF16), kn.astype(BF16), _DN_NT, preferred_element_type=F32)
    first = jnp.where(n == 0, 1.0, 0.0)
    second = jnp.where(n == 1, 1.0, 0.0)
    meta = (s[:, :BLOCK] + bias_ref[kv, :, :BLOCK] + _slope_col(kv) * (-float(BLOCK) * n.astype(F32))
            + first * neg0_ref[...])
    in_prev = jnp.where(lax.broadcasted_iota(jnp.int32, (1, 2 * BLOCK), 1) < BLOCK, 1.0, 0.0)
    band = s[:, BLOCK:] + bias_ref[kv, :, BLOCK:] + NEG_INF * (first + second * in_prev)
    s = jnp.concatenate([meta, band], axis=1)
    return qs, kcat, qn, kn, rq, rk, s, _sink_col(sink_ref, kv)


def _attn_specs():
    kq = ATTN_WIDTH // KV_WIDTH
    q_spec = pl.BlockSpec((BLOCK, ATTN_WIDTH), lambda n: (n, 0))
    kv_specs = []
    for col in (kq, kq + 1):
        kv_specs += [pl.BlockSpec((BLOCK, KV_WIDTH), lambda n, col=col: (0, col)),
                     pl.BlockSpec((BLOCK, KV_WIDTH), lambda n, col=col: (jnp.maximum(n - 1, 0), col)),
                     pl.BlockSpec((BLOCK, KV_WIDTH), lambda n, col=col: (n, col))]
    small = [pl.BlockSpec((1, HEAD_DIM), lambda n: (0, 0)), pl.BlockSpec((1, HEAD_DIM), lambda n: (0, 0)),
             pl.BlockSpec((1, N_HEADS), lambda n: (0, 0)),
             pl.BlockSpec((N_KV_HEADS, KV_GROUP * BLOCK, 3 * BLOCK), lambda n: (0, 0, 0)),
             pl.BlockSpec((KV_GROUP * BLOCK, BLOCK), lambda n: (0, 0))]
    return q_spec, kv_specs, small


def _attn_fwd(proj, gq, gk, sinks, *, name):
    n_rows = proj.shape[0]
    q_spec, kv_specs, small = _attn_specs()

    def body(q_ref, k0, k1, k2, v0, v1, v2, gq_ref, gk_ref, sink_ref, bias_ref, neg0_ref, o_ref, lse_ref):
        n = pl.program_id(0)
        o_parts, lse_parts = [], []
        for kv in range(N_KV_HEADS):
            _, _, _, _, _, _, s, sink = _attn_scores(q_ref, (k0, k1, k2), gq_ref, gk_ref, sink_ref, bias_ref, neg0_ref, kv, n)
            vcat = jnp.concatenate([r[:, kv * HEAD_DIM:(kv + 1) * HEAD_DIM] for r in (v0, v1, v2)], axis=0)
            m = jnp.maximum(jnp.max(s, axis=-1, keepdims=True), sink)
            p = jnp.exp(s - m)
            l = jnp.sum(p, axis=-1, keepdims=True) + jnp.exp(sink - m)
            o = jnp.dot(p.astype(BF16), vcat.astype(BF16), preferred_element_type=F32) / l
            lse = m + jnp.log(l)
            o_parts += [o[g * BLOCK:(g + 1) * BLOCK] for g in range(KV_GROUP)]
            lse_parts += [lse[g * BLOCK:(g + 1) * BLOCK] for g in range(KV_GROUP)]
        o_ref[...] = jnp.concatenate(o_parts, axis=1)
        lse_ref[...] = jnp.concatenate(lse_parts, axis=1)

    return pl.pallas_call(
        body, name=name, grid=(n_rows // BLOCK,),
        in_specs=[q_spec] + kv_specs + small,
        out_specs=[pl.BlockSpec((BLOCK, ATTN_WIDTH), lambda n: (n, 0)), pl.BlockSpec((BLOCK, N_HEADS), lambda n: (n, 0))],
        out_shape=[jax.ShapeDtypeStruct((n_rows, ATTN_WIDTH), F32), jax.ShapeDtypeStruct((n_rows, N_HEADS), F32)],
        compiler_params=_cparams(("parallel",)),
    )(proj, proj, proj, proj, proj, proj, proj, gq, gk, sinks, *_attn_bias())


def _attn_bwd(proj, gq, gk, sinks, o, lse, do, *, name):
    n_rows = proj.shape[0]
    q_spec, kv_specs, small = _attn_specs()

    def body(q_ref, k0, k1, k2, v0, v1, v2, gq_ref, gk_ref, sink_ref, bias_ref, neg0_ref, o_ref, lse_ref, do_ref,
             dq_ref, dkb_ref, dvb_ref, dgq_ref, dgk_ref, dsink_ref, dk_ref, dv_ref):
        n = pl.program_id(0)

        @pl.when(n == 0)
        def _():
            dk_ref[...] = jnp.zeros_like(dk_ref)
            dv_ref[...] = jnp.zeros_like(dv_ref)
            dgq_ref[...] = jnp.zeros_like(dgq_ref)
            dgk_ref[...] = jnp.zeros_like(dgk_ref)
            dsink_ref[...] = jnp.zeros_like(dsink_ref)

        dq_parts, dk_parts, dv_parts, dsink_parts = [], [], [], []
        dgq = jnp.zeros((1, HEAD_DIM), F32)
        dgk = jnp.zeros((1, HEAD_DIM), F32)
        for kv in range(N_KV_HEADS):
            qs, kcat, qn, kn, rq, rk, s, sink = _attn_scores(q_ref, (k0, k1, k2), gq_ref, gk_ref, sink_ref, bias_ref, neg0_ref, kv, n)
            vcat = jnp.concatenate([r[:, kv * HEAD_DIM:(kv + 1) * HEAD_DIM] for r in (v0, v1, v2)], axis=0)
            os_ = _stack_heads(o_ref[...], kv)
            dos = _stack_heads(do_ref[...], kv)
            lse = jnp.concatenate([lse_ref[:, kv * KV_GROUP + g:kv * KV_GROUP + g + 1] for g in range(KV_GROUP)], axis=0)
            p = jnp.exp(s - lse)
            delta = jnp.sum(dos * os_, axis=-1, keepdims=True)
            dp = lax.dot_general(dos.astype(BF16), vcat.astype(BF16), _DN_NT, preferred_element_type=F32)
            ds = (p * (dp - delta)) * _SCALE
            dsink_rows = -jnp.exp(sink - lse) * delta
            dsink_parts += [jnp.sum(dsink_rows[g * BLOCK:(g + 1) * BLOCK], axis=0, keepdims=True) for g in range(KV_GROUP)]
            dv_parts.append(lax.dot_general(p.astype(BF16), dos.astype(BF16), _DN_TN, preferred_element_type=F32))
            dsb = ds.astype(BF16)
            dqn = jnp.dot(dsb, kn.astype(BF16), preferred_element_type=F32)
            dkn = lax.dot_general(dsb, qn.astype(BF16), _DN_TN, preferred_element_type=F32)
            qh = qs * rq
            gd = dqn * gq_ref[...]
            dqs = rq * (gd - qh * jnp.mean(gd * qh, axis=-1, keepdims=True))
            dgq = dgq + jnp.sum(dqn * qh, axis=0, keepdims=True)
            kh = kcat * rk
            gdk = dkn * gk_ref[...]
            dk_parts.append(rk * (gdk - kh * jnp.mean(gdk * kh, axis=-1, keepdims=True)))
            dgk = dgk + jnp.sum(dkn * kh, axis=0, keepdims=True)
            dq_parts += [dqs[g * BLOCK:(g + 1) * BLOCK] for g in range(KV_GROUP)]
        dq_ref[...] = jnp.concatenate(dq_parts, axis=1).astype(BF16)
        dkc = jnp.concatenate(dk_parts, axis=1)
        dvc = jnp.concatenate(dv_parts, axis=1)
        prev = pl.multiple_of(jnp.maximum(n - 1, 0) * BLOCK, BLOCK)
        cur = pl.multiple_of(n * BLOCK, BLOCK)
        for acc_ref, val in ((dk_ref, dkc), (dv_ref, dvc)):
            acc_ref[0:BLOCK, :] += val[0:BLOCK]
            acc_ref[pl.ds(prev, BLOCK), :] += val[BLOCK:2 * BLOCK]
            acc_ref[pl.ds(cur, BLOCK), :] += val[2 * BLOCK:3 * BLOCK]
        dgq_ref[...] += dgq
        dgk_ref[...] += dgk
        dsink_ref[...] += jnp.concatenate(dsink_parts, axis=1)

        @pl.when(n == pl.num_programs(0) - 1)
        def _():
            dkb_ref[...] = dk_ref[...].astype(BF16)
            dvb_ref[...] = dv_ref[...].astype(BF16)

    blk = lambda w: pl.BlockSpec((BLOCK, w), lambda n: (n, 0))
    full = lambda r, w: pl.BlockSpec((r, w), lambda n: (0, 0))
    return pl.pallas_call(
        body, name=name, grid=(n_rows // BLOCK,),
        in_specs=[q_spec] + kv_specs + small + [blk(ATTN_WIDTH), blk(N_HEADS), blk(ATTN_WIDTH)],
        out_specs=[blk(ATTN_WIDTH), full(n_rows, KV_WIDTH), full(n_rows, KV_WIDTH),
                   full(1, HEAD_DIM), full(1, HEAD_DIM), full(1, N_HEADS)],
        out_shape=[jax.ShapeDtypeStruct((n_rows, ATTN_WIDTH), BF16), jax.ShapeDtypeStruct((n_rows, KV_WIDTH), BF16),
                   jax.ShapeDtypeStruct((n_rows, KV_WIDTH), BF16), jax.ShapeDtypeStruct((1, HEAD_DIM), F32),
                   jax.ShapeDtypeStruct((1, HEAD_DIM), F32), jax.ShapeDtypeStruct((1, N_HEADS), F32)],
        scratch_shapes=[pltpu.VMEM((n_rows, KV_WIDTH), F32), pltpu.VMEM((n_rows, KV_WIDTH), F32)],
        compiler_params=_cparams(("arbitrary",)),
    )(proj, proj, proj, proj, proj, proj, proj, gq, gk, sinks, *_attn_bias(), o, lse, do)


SSM_LAGS = 8
SLAB_G = 128 // SSM_GROUP_CH
N_SLABS = SSM_GROUPS // SLAB_G
SLAB_STATE = SLAB_G * SSM_STATE
U_COL = (ATTN_WIDTH + 2 * KV_WIDTH) // 128


def _ssm_prep(lam_re, lam_im, log_step, b_re, b_im, c_re, c_im):
    lam = lax.complex(lam_re, lam_im)
    delta = jnp.exp(log_step)[:, None]
    lam_bar = jnp.exp(lam * delta)
    b_t = lax.complex(jnp.swapaxes(b_re, 1, 2), jnp.swapaxes(b_im, 1, 2))
    b_bar = ((lam_bar - 1.0) / lam)[:, None, :] * b_t
    pw = [jnp.ones_like(lam_bar)]
    for _ in range(SSM_LAGS):
        pw.append(pw[-1] * lam_bar)
    w = jnp.stack(pw[:SSM_LAGS])[:, :, None, :] * b_bar[None]
    wri = jnp.stack([jnp.real(w), jnp.imag(w)], axis=3)
    wc = wri.reshape(SSM_LAGS, N_SLABS, SLAB_G * SSM_GROUP_CH, 2 * SSM_STATE)
    wc = jnp.swapaxes(wc, 0, 1).reshape(N_SLABS, SSM_LAGS * 128, 2 * SSM_STATE)
    cri = jnp.stack([c_re, -c_im], axis=2).reshape(N_SLABS, SLAB_G, SSM_GROUP_CH, 2, 1, SSM_STATE)
    eye = jnp.eye(SLAB_G, dtype=F32).reshape(1, SLAB_G, 1, 1, SLAB_G, 1)
    ct = (cri * eye).reshape(N_SLABS, 128, 2 * SLAB_STATE)
    l8 = pw[SSM_LAGS]
    lam8 = jnp.concatenate([jnp.real(l8).reshape(N_SLABS, 1, SLAB_STATE), jnp.imag(l8).reshape(N_SLABS, 1, SLAB_STATE)], axis=2)
    return wc, ct, lam8


def _row_group():
    return (lax.broadcasted_iota(jnp.int32, (SSM_LAGS * 128, 1), 0) // SSM_GROUP_CH) % SLAB_G


def _spread_groups(wc):
    g_of_row = _row_group()
    return jnp.concatenate([jnp.where(g_of_row == g, wc[:, r * SSM_STATE:(r + 1) * SSM_STATE], 0.0)
                            for r in range(2) for g in range(SLAB_G)], axis=1)


def _gather_groups(dw):
    g_of_row = _row_group()
    parts = []
    for r in range(2):
        acc = jnp.zeros((SSM_LAGS * 128, SSM_STATE), F32)
        for g in range(SLAB_G):
            c0 = r * SLAB_STATE + g * SSM_STATE
            acc = acc + jnp.where(g_of_row == g, dw[:, c0:c0 + SSM_STATE], 0.0)
        parts.append(acc)
    return jnp.concatenate(parts, axis=1)


def _lagged(u, up, t_rows):
    ue = jnp.concatenate([up, u], axis=0)
    return jnp.concatenate([ue[SSM_LAGS - tau:SSM_LAGS - tau + t_rows] for tau in range(SSM_LAGS)], axis=1).astype(BF16)


def _ssm_fwd(proj, wc, cmat, lam8, dvec, *, name):
    n_rows = proj.shape[0]
    tt = _row_tile(n_rows, 1408)
    n_t = n_rows // tt
    sw = 2 * SLAB_STATE
    hs = SLAB_STATE

    def body(u_ref, up_ref, wc_ref, c_ref, l_ref, d_ref, y_ref, x_ref, carry_ref, w_ref):
        t = pl.program_id(1)

        @pl.when(t == 0)
        def _():
            carry_ref[...] = jnp.zeros_like(carry_ref)
            w_ref[...] = _spread_groups(wc_ref[...]).astype(BF16)

        u = u_ref[...]
        up = jnp.where(t > 0, up_ref[...], 0.0)
        x_ref[...] = jnp.dot(_lagged(u, up, tt), w_ref[...], preferred_element_type=F32)
        ar = jnp.broadcast_to(l_ref[:, :hs], (8, hs))
        ai = jnp.broadcast_to(l_ref[:, hs:], (8, hs))

        def step(b, c):
            xr, xi = c
            r0 = pl.multiple_of(b * 8, 8)
            w = x_ref[pl.ds(r0, 8), :]
            nr = w[:, :hs] + ar * xr - ai * xi
            ni = w[:, hs:] + ar * xi + ai * xr
            x_ref[pl.ds(r0, 8), :] = jnp.concatenate([nr, ni], axis=1)
            return nr, ni

        xr, xi = lax.fori_loop(0, tt // 8, step, (carry_ref[:, :hs], carry_ref[:, hs:]), unroll=8)
        carry_ref[...] = jnp.concatenate([xr, xi], axis=1)
        y_ref[...] = lax.dot_general(x_ref[...].astype(BF16), c_ref[...], _DN_NT, preferred_element_type=F32) + d_ref[...] * u

    return pl.pallas_call(
        body, name=name, grid=(N_SLABS, n_t),
        in_specs=[pl.BlockSpec((tt, 128), lambda j, t: (t, U_COL + j)),
                  pl.BlockSpec((8, 128), lambda j, t: (jnp.maximum(t * (tt // 8) - 1, 0), U_COL + j)),
                  pl.BlockSpec((None, SSM_LAGS * 128, 2 * SSM_STATE), lambda j, t: (j, 0, 0)),
                  pl.BlockSpec((None, 128, sw), lambda j, t: (j, 0, 0)),
                  pl.BlockSpec((None, 1, sw), lambda j, t: (j, 0, 0)),
                  pl.BlockSpec((1, 128), lambda j, t: (0, j))],
        out_specs=[pl.BlockSpec((tt, 128), lambda j, t: (t, j)), pl.BlockSpec((tt, sw), lambda j, t: (t, j))],
        out_shape=[jax.ShapeDtypeStruct((n_rows, SSM_WIDTH), F32), jax.ShapeDtypeStruct((n_rows, N_SLABS * sw), F32)],
        scratch_shapes=[pltpu.VMEM((8, sw), F32), pltpu.VMEM((SSM_LAGS * 128, sw), BF16)],
        compiler_params=_cparams(("parallel", "arbitrary")),
    )(proj, proj, wc, cmat, lam8, dvec)


def _ssm_bwd(proj, xs, dy, wc, cmat, lam8, dvec, *, name, deps=()):
    n_rows = proj.shape[0]
    tt = _row_tile(n_rows, 704)
    n_t = n_rows // tt
    sw = 2 * SLAB_STATE
    hs = SLAB_STATE

    def body(u_ref, up_ref, x_ref, xp_ref, dy_ref, wc_ref, c_ref, l_ref, d_ref,
             *rest):
        du_ref, dwc_ref, dc_ref, dl_ref, dd_ref, a_ref, carry_ref, head_ref, w_ref, dw_ref = rest[len(deps):]
        t = pl.program_id(1)
        ti = n_t - 1 - t

        @pl.when(t == 0)
        def _():
            w_ref[...] = _spread_groups(wc_ref[...]).astype(BF16)
            carry_ref[...] = jnp.zeros_like(carry_ref)
            head_ref[...] = jnp.zeros_like(head_ref)
            dw_ref[...] = jnp.zeros_like(dw_ref)
            dc_ref[...] = jnp.zeros_like(dc_ref)
            dl_ref[...] = jnp.zeros_like(dl_ref)
            dd_ref[...] = jnp.zeros_like(dd_ref)

        u = u_ref[...]
        up = jnp.where(ti > 0, up_ref[...], 0.0)
        ucat = _lagged(u, up, tt)
        dyv = dy_ref[...]
        dyb = dyv.astype(BF16)
        a_ref[...] = jnp.dot(dyb, c_ref[...], preferred_element_type=F32)
        lr = jnp.broadcast_to(l_ref[:, :hs], (8, hs))
        li = jnp.broadcast_to(l_ref[:, hs:], (8, hs))

        def step(i, c):
            cr, ci = c
            r0 = pl.multiple_of((tt // 8 - 1 - i) * 8, 8)
            g = a_ref[pl.ds(r0, 8), :]
            nr = g[:, :hs] + lr * cr + li * ci
            ni = g[:, hs:] + lr * ci - li * cr
            a_ref[pl.ds(r0, 8), :] = jnp.concatenate([nr, ni], axis=1)
            return nr, ni

        cr, ci = lax.fori_loop(0, tt // 8, step, (carry_ref[:, :hs], carry_ref[:, hs:]), unroll=8)
        carry_ref[...] = jnp.concatenate([cr, ci], axis=1)

        a = a_ref[...]
        xv = x_ref[...]
        xprev = jnp.where(ti > 0, xp_ref[...], 0.0)
        xsh = jnp.concatenate([xprev, xv[:tt - SSM_LAGS]], axis=0)
        a_re, a_im, x_re, x_im = a[:, :hs], a[:, hs:], xsh[:, :hs], xsh[:, hs:]
        dl_ref[...] += jnp.concatenate([jnp.sum(a_re * x_re + a_im * x_im, axis=0, keepdims=True),
                                        jnp.sum(a_im * x_re - a_re * x_im, axis=0, keepdims=True)], axis=1)
        ab = a.astype(BF16)
        dw_ref[...] += lax.dot_general(ucat, ab, _DN_TN, preferred_element_type=F32)
        duc = lax.dot_general(ab, w_ref[...], _DN_NT, preferred_element_type=F32)
        ext = jnp.concatenate([duc, head_ref[...]], axis=0)
        du = d_ref[...] * dyv
        for tau in range(SSM_LAGS):
            du = du + ext[tau:tau + tt, tau * 128:(tau + 1) * 128]
        head_ref[...] = duc[0:8]
        row = ti * tt + lax.broadcasted_iota(jnp.int32, (tt, 128), 0)
        du_ref[...] = jnp.where(row >= PAD, du, 0.0).astype(BF16)
        dd_ref[...] += jnp.sum(dyv * u, axis=0, keepdims=True)
        dc_ref[...] += lax.dot_general(dyb, xv.astype(BF16), _DN_TN, preferred_element_type=F32)

        @pl.when(t == n_t - 1)
        def _():
            dwc_ref[...] = _gather_groups(dw_ref[...])

    rt = lambda t: n_t - 1 - t
    prev8 = lambda t: jnp.maximum(rt(t) * (tt // 8) - 1, 0)
    return pl.pallas_call(
        body, name=name, grid=(N_SLABS, n_t),
        in_specs=[pl.BlockSpec((tt, 128), lambda j, t: (rt(t), U_COL + j)),
                  pl.BlockSpec((8, 128), lambda j, t: (prev8(t), U_COL + j)),
                  pl.BlockSpec((tt, sw), lambda j, t: (rt(t), j)),
                  pl.BlockSpec((8, sw), lambda j, t: (prev8(t), j)),
                  pl.BlockSpec((tt, 128), lambda j, t: (rt(t), j)),
                  pl.BlockSpec((None, SSM_LAGS * 128, 2 * SSM_STATE), lambda j, t: (j, 0, 0)),
                  pl.BlockSpec((None, 128, sw), lambda j, t: (j, 0, 0)),
                  pl.BlockSpec((None, 1, sw), lambda j, t: (j, 0, 0)),
                  pl.BlockSpec((1, 128), lambda j, t: (0, j))] + [_ANY] * len(deps),
        out_specs=[pl.BlockSpec((tt, 128), lambda j, t: (rt(t), j)),
                   pl.BlockSpec((None, SSM_LAGS * 128, 2 * SSM_STATE), lambda j, t: (j, 0, 0)),
                   pl.BlockSpec((None, 128, sw), lambda j, t: (j, 0, 0)),
                   pl.BlockSpec((None, 1, sw), lambda j, t: (j, 0, 0)),
                   pl.BlockSpec((1, 128), lambda j, t: (0, j))],
        out_shape=[jax.ShapeDtypeStruct((n_rows, SSM_WIDTH), BF16),
                   jax.ShapeDtypeStruct((N_SLABS, SSM_LAGS * 128, 2 * SSM_STATE), F32),
                   jax.ShapeDtypeStruct((N_SLABS, 128, sw), F32),
                   jax.ShapeDtypeStruct((N_SLABS, 1, sw), F32),
                   jax.ShapeDtypeStruct((1, SSM_WIDTH), F32)],
        scratch_shapes=[pltpu.VMEM((tt, sw), F32), pltpu.VMEM((8, sw), F32), pltpu.VMEM((8, sw), F32),
                        pltpu.VMEM((SSM_LAGS * 128, sw), BF16), pltpu.VMEM((SSM_LAGS * 128, sw), F32)],
        compiler_params=_cparams(("parallel", "arbitrary")),
    )(proj, proj, xs, xs, dy, wc, cmat, lam8, dvec, *deps)


_GELU_C = math.sqrt(2.0 / math.pi)
_GELU_A = 0.044715


def _gelu(y):
    th = jnp.tanh(_GELU_C * (y + _GELU_A * y * y * y))
    return 0.5 * y * (1.0 + th), th


def _glu_fwd(y, w, b, *, name):
    n_rows, width = y.shape
    tr = _row_tile(n_rows, 384)

    def body(y_ref, w_ref, b_ref, o_ref):
        g, _ = _gelu(y_ref[...])
        z = jnp.dot(g.astype(BF16), w_ref[...], preferred_element_type=F32) + b_ref[...]
        o_ref[...] = g * jax.nn.sigmoid(z)

    return pl.pallas_call(
        body, name=name, grid=(n_rows // tr,),
        in_specs=[pl.BlockSpec((tr, width), lambda i: (i, 0)), pl.BlockSpec((width, width), lambda i: (0, 0)),
                  pl.BlockSpec((1, width), lambda i: (0, 0))],
        out_specs=pl.BlockSpec((tr, width), lambda i: (i, 0)),
        out_shape=jax.ShapeDtypeStruct((n_rows, width), F32),
        compiler_params=_cparams(("parallel",)),
    )(y, w, b)


def _glu_bwd(y, w, b, dout, *, name):
    n_rows, width = y.shape
    tr = _row_tile(n_rows, 384)

    def body(y_ref, w_ref, b_ref, do_ref, dy_ref, g_ref, dz_ref, db_ref):
        i = pl.program_id(0)
        yv = y_ref[...]
        g, th = _gelu(yv)
        gb = g.astype(BF16)
        z = jnp.dot(gb, w_ref[...], preferred_element_type=F32) + b_ref[...]
        sg = jax.nn.sigmoid(z)
        do = do_ref[...]
        dz = do * g * sg * (1.0 - sg)
        dzb = dz.astype(BF16)
        dg = do * sg + lax.dot_general(dzb, w_ref[...], _DN_NT, preferred_element_type=F32)
        dgelu = 0.5 * (1.0 + th) + 0.5 * yv * (1.0 - th * th) * _GELU_C * (1.0 + 3.0 * _GELU_A * yv * yv)
        dy_ref[...] = dg * dgelu
        g_ref[...] = gb
        dz_ref[...] = dzb
        part = jnp.sum(dz, axis=0, keepdims=True)

        @pl.when(i == 0)
        def _():
            db_ref[...] = part

        @pl.when(i > 0)
        def _():
            db_ref[...] += part

    row = pl.BlockSpec((tr, width), lambda i: (i, 0))
    vec = pl.BlockSpec((1, width), lambda i: (0, 0))
    return pl.pallas_call(
        body, name=name, grid=(n_rows // tr,),
        in_specs=[row, pl.BlockSpec((width, width), lambda i: (0, 0)), vec, row],
        out_specs=[row, row, row, vec],
        out_shape=[jax.ShapeDtypeStruct((n_rows, width), F32), jax.ShapeDtypeStruct((n_rows, width), BF16),
                   jax.ShapeDtypeStruct((n_rows, width), BF16), jax.ShapeDtypeStruct((1, width), F32)],
        compiler_params=_cparams(("arbitrary",)),
    )(y, w, b, dout)


def _loss_head(h, target, *, name):
    n_rows, width = h.shape

    def body(h_ref, t_ref, dh_ref, dhb_ref, loss_ref):
        i = pl.program_id(0)

        @pl.when(i == 0)
        def _():
            dh_ref[...] = jnp.zeros_like(dh_ref)
            dhb_ref[...] = jnp.zeros_like(dhb_ref)
            loss_ref[...] = jnp.zeros_like(loss_ref)

        @pl.when(i > 0)
        def _():
            err = h_ref[...] - t_ref[...]
            dh = err * (1.0 / width)
            dh_ref[...] = dh
            dhb_ref[...] = dh.astype(BF16)
            loss_ref[...] += (0.5 / width) * jnp.sum(err * err, keepdims=True)

    return pl.pallas_call(
        body, name=name, grid=(n_rows // BLOCK,),
        in_specs=[pl.BlockSpec((BLOCK, width), lambda i: (i, 0)),
                  pl.BlockSpec((BLOCK, width), lambda i: (jnp.maximum(i - 1, 0), 0))],
        out_specs=[pl.BlockSpec((BLOCK, width), lambda i: (i, 0)), pl.BlockSpec((BLOCK, width), lambda i: (i, 0)),
                   pl.BlockSpec((1, 1), lambda i: (0, 0))],
        out_shape=[jax.ShapeDtypeStruct((n_rows, width), F32), jax.ShapeDtypeStruct((n_rows, width), BF16),
                   jax.ShapeDtypeStruct((1, 1), F32)],
        compiler_params=_cparams(("arbitrary",)),
    )(h, target)


def _elem_rows(n_rows, n_cols, bytes_per_row_elem):
    lanes = -(-n_cols // 128) * 128
    cap = max(16, (12 * 1024 * 1024) // (lanes * bytes_per_row_elem))
    best = None
    for t in range(16, min(n_rows, cap) + 1, 16):
        if n_rows % t == 0:
            best = t
    return best or n_rows


def _cast_bf16(x, *, name, deps=()):
    n_rows, n_cols = x.shape
    tr = _elem_rows(n_rows, n_cols, 4)

    def body(x_ref, *rest):
        rest[-1][...] = x_ref[...].astype(BF16)

    spec = pl.BlockSpec((tr, n_cols), lambda i: (i, 0))
    return pl.pallas_call(body, name=name, grid=(n_rows // tr,), in_specs=[spec] + [_ANY] * len(deps), out_specs=spec,
                          out_shape=jax.ShapeDtypeStruct(x.shape, BF16), compiler_params=_cparams(("parallel",)))(x, *deps)


def _adamw(w, m, v, parts, *, name):
    n_rows, n_cols = w.shape
    n_parts = parts.shape[0]
    tr = _elem_rows(n_rows, n_cols, 4 * (8 + n_parts))
    c1 = 1.0 / (1.0 - ADAM_B1 ** ADAM_STEP)
    c2 = 1.0 / (1.0 - ADAM_B2 ** ADAM_STEP)

    def body(w_ref, m_ref, v_ref, p_ref, g_ref, d_ref, nm_ref, nv_ref):
        g = p_ref[0].astype(F32)
        for k in range(1, n_parts):
            g = g + p_ref[k].astype(F32)
        nm = ADAM_B1 * m_ref[...] + (1.0 - ADAM_B1) * g
        nv = ADAM_B2 * v_ref[...] + (1.0 - ADAM_B2) * (g * g)
        g_ref[...] = g
        nm_ref[...] = nm
        nv_ref[...] = nv
        d_ref[...] = -ADAM_LR * ((nm * c1) / (jnp.sqrt(nv * c2) + ADAM_EPS) + ADAM_WD * w_ref[...])

    spec = pl.BlockSpec((tr, n_cols), lambda i: (i, 0))
    return pl.pallas_call(
        body, name=name, grid=(n_rows // tr,),
        in_specs=[spec, spec, spec, pl.BlockSpec((n_parts, tr, n_cols), lambda i: (0, i, 0))],
        out_specs=[spec] * 4, out_shape=[jax.ShapeDtypeStruct(w.shape, F32)] * 4,
        compiler_params=_cparams(("parallel",)),
    )(w, m, v, parts)


def _sum_parts(parts, *, name):
    n_parts, n_rows, n_cols = parts.shape
    tr = _elem_rows(n_rows, n_cols, 4 * (1 + n_parts))

    def body(p_ref, o_ref):
        g = p_ref[0].astype(F32)
        for k in range(1, n_parts):
            g = g + p_ref[k].astype(F32)
        o_ref[...] = g

    return pl.pallas_call(
        body, name=name, grid=(n_rows // tr,),
        in_specs=[pl.BlockSpec((n_parts, tr, n_cols), lambda i: (0, i, 0))],
        out_specs=pl.BlockSpec((tr, n_cols), lambda i: (i, 0)),
        out_shape=jax.ShapeDtypeStruct((n_rows, n_cols), F32), compiler_params=_cparams(("parallel",)),
    )(parts)


BIG = ("w_in", "w_glu", "w_out", "w_up", "w_down")
SMALL = ("norm_mix_g", "q_norm_g", "k_norm_g", "attn_sinks", "ssm_lambda_re", "ssm_lambda_im", "ssm_log_step",
         "ssm_b_re", "ssm_b_im", "ssm_c_re", "ssm_c_im", "ssm_d", "b_glu", "attn_out_g", "ssm_out_g", "norm_mlp_g")
_SSM_NAMES = ("ssm_lambda_re", "ssm_lambda_im", "ssm_log_step", "ssm_b_re", "ssm_b_im", "ssm_c_re", "ssm_c_im")


def _divisor(n, cands):
    for c in cands:
        if n % c == 0:
            return c
    return n


def _mm(a, b, mode, name, outs=(F32,), epilogue=_ident, tiles=(), deps=(), blocked=False):
    if mode == "nn":
        m, k = a.shape
        n = b.shape[0] * b.shape[2] if blocked else b.shape[1]
    elif mode == "nt":
        m, k = a.shape
        n = b.shape[1] if blocked else b.shape[0]
    else:
        (k, m), n = a.shape, b.shape[1]
    if mode == "tn":
        tm, tn, tk = _divisor(m, (1024, 512)), _divisor(n, (1024, 512)), k
    elif k <= 2560:
        tm, tn, tk = _row_tile(m, 1408), _divisor(n, (1024, 1280, 512)), k
    elif blocked:
        tm, tn, tk = (m // 2 if m % 32 == 0 else m), _divisor(n, (1024, 512)), b.shape[2]
    else:
        tm, tn, tk = _row_tile(m, 1408), _divisor(n, (1024, 512)), _divisor(k, (1024, 512))
    return _matmul(a, b, mode=mode, tm=tm, tn=tn, tk=tk, outs=list(outs), epilogue=epilogue, tiles=tiles, deps=deps,
                   blocked=blocked, name=name)


def _add_tile(acc, res):
    return (acc + res,)


def _relu_sq(acc):
    r = jnp.maximum(acc, 0.0)
    return r, r * r


def _relu_sq_bwd(acc, r):
    return (acc * (2.0 * r.astype(F32)),)


def _row(v):
    return v.reshape(1, -1)


def _layer_fwd(hres, fetch, sp, l, deps=()):
    tag = f"_l{l}"
    wts = {}
    hb = _rmsnorm_fwd([hres], [_row(sp["norm_mix_g"])], name="norm_mix" + tag, deps=deps)
    wts["w_in"] = fetch("w_in", hb)
    proj, = _mm(hb, wts["w_in"], "nt", "proj" + tag)
    gq, gk, sinks = _row(sp["q_norm_g"]), _row(sp["k_norm_g"]), _row(sp["attn_sinks"])
    o, lse = _attn_fwd(proj, gq, gk, sinks, name="attn_fwd" + tag)
    (wc, cmat, lam8), prep_vjp = jax.vjp(_ssm_prep, *[sp[n] for n in _SSM_NAMES])
    cmat = cmat.astype(BF16)
    y, xs = _ssm_fwd(proj, wc, cmat, lam8, _row(sp["ssm_d"]), name="ssm_fwd" + tag)
    wts["w_glu"] = fetch("w_glu", y)
    s = _glu_fwd(y, wts["w_glu"], _row(sp["b_glu"]), name="glu_fwd" + tag)
    mix = _rmsnorm_fwd([o, s], [_row(sp["attn_out_g"]), _row(sp["ssm_out_g"])], name="norm_out" + tag)
    wts["w_out"] = fetch("w_out", mix)
    hres2, = _mm(mix, wts["w_out"], "nn", "out_proj" + tag, epilogue=_add_tile, tiles=(hres,))
    h2 = _rmsnorm_fwd([hres2], [_row(sp["norm_mlp_g"])], name="norm_mlp" + tag)
    wts["w_up"] = fetch("w_up", h2)
    r, act = _mm(h2, wts["w_up"], "nn", "mlp_up" + tag, outs=(BF16, BF16), epilogue=_relu_sq, blocked=True)
    wts["w_down"] = fetch("w_down", act)
    hres3, = _mm(act, wts["w_down"], "nn", "mlp_down" + tag, epilogue=_add_tile, tiles=(hres2,))
    saved = dict(wts=wts, hres=hres, hb=hb, proj=proj, o=o, lse=lse, wc=wc, cmat=cmat, lam8=lam8, prep_vjp=prep_vjp,
                 y=y, xs=xs, s=s, mix=mix, hres2=hres2, h2=h2, r=r, act=act)
    return hres3, saved


def _layer_bwd(dres, dres_b, sp, sv, l, early_grads, deps=()):
    tag = f"_l{l}"
    wts = sv["wts"]
    gb, gs = {}, {}
    d_up, = _mm(dres_b, wts["w_down"], "nt", "mlp_down_dx" + tag, outs=(BF16,), epilogue=_relu_sq_bwd, tiles=(sv["r"],),
                deps=deps)
    gb["w_down"], = _mm(sv["act"], dres_b, "tn", "mlp_down_dw" + tag, outs=(BF16,))
    gb["w_up"], = _mm(sv["h2"], d_up, "tn", "mlp_up_dw" + tag, outs=(BF16,), blocked=True)
    deps = early_grads(l, "a", {n: gb.pop(n) for n in ("w_up", "w_down")})
    dh2, = _mm(d_up, wts["w_up"], "nt", "mlp_up_dx" + tag, blocked=True, deps=deps)
    (dres2,), (dg,), dres2_b = _rmsnorm_bwd([sv["hres2"]], [_row(sp["norm_mlp_g"])], dh2, dres, name="norm_mlp_bwd" + tag)
    gs["norm_mlp_g"] = dg
    dmix, = _mm(dres2_b, wts["w_out"], "nt", "out_proj_dx" + tag)
    gb["w_out"], = _mm(sv["mix"], dres2_b, "tn", "out_proj_dw" + tag, outs=(BF16,))
    (do, ds), (dga, dgs) = _rmsnorm_bwd([sv["o"], sv["s"]], [_row(sp["attn_out_g"]), _row(sp["ssm_out_g"])], dmix, None,
                                        name="norm_out_bwd" + tag)
    gs["attn_out_g"], gs["ssm_out_g"] = dga, dgs
    dy, g_b, dz_b, db = _glu_bwd(sv["y"], wts["w_glu"], _row(sp["b_glu"]), ds, name="glu_bwd" + tag)
    gs["b_glu"] = db
    gb["w_glu"], = _mm(g_b, dz_b, "tn", "glu_dw" + tag, outs=(BF16,))
    deps = early_grads(l, "b", {n: gb.pop(n) for n in ("w_out", "w_glu")})
    du, dwc, dcmat, dlam8, dd = _ssm_bwd(sv["proj"], sv["xs"], dy, sv["wc"], sv["cmat"], sv["lam8"], _row(sp["ssm_d"]),
                                         name="ssm_bwd" + tag, deps=deps)
    gs["ssm_d"] = dd
    for n, g in zip(_SSM_NAMES, sv["prep_vjp"]((dwc, dcmat, dlam8))):
        gs[n] = g
    dq, dk, dv, dgq, dgk, dsinks = _attn_bwd(sv["proj"], _row(sp["q_norm_g"]), _row(sp["k_norm_g"]), _row(sp["attn_sinks"]),
                                             sv["o"], sv["lse"], do, name="attn_bwd" + tag)
    gs["q_norm_g"], gs["k_norm_g"], gs["attn_sinks"] = dgq, dgk, dsinks
    dproj = _concat_cols([dq, dk, dv, du], name="dproj" + tag)
    gb["w_in"], = _mm(dproj, sv["hb"], "tn", "proj_dw" + tag, outs=(BF16,))
    deps = early_grads(l, "c", {"w_in": gb.pop("w_in")})
    dh, = _mm(dproj, wts["w_in"], "nn", "proj_dx" + tag, deps=deps)
    (dres_in,), (dg,), dres_in_b = _rmsnorm_bwd([sv["hres"]], [_row(sp["norm_mix_g"])], dh, dres2, name="norm_mix_bwd" + tag)
    gs["norm_mix_g"] = dg
    return dres_in, dres_in_b, gs


def _local_step(x, target, meta, sp, weights_for_layer, early_grads, grads_of_layer):
    h = jnp.concatenate([jnp.zeros((PAD, x.shape[1]), F32), meta, x], axis=0)
    saved = []
    for l in range(DEPTH):
        fetch, deps = weights_for_layer(l, h)
        h, sv = _layer_fwd(h, fetch, {n: sp[n][l] for n in SMALL}, l, deps)
        saved.append(sv)
    dh, dh_b, loss = _loss_head(h, target, name="loss_head")
    gsmall = {n: [None] * DEPTH for n in SMALL}
    deps = ()
    for l in reversed(range(DEPTH)):
        dh, dh_b, gs = _layer_bwd(dh, dh_b, {n: sp[n][l] for n in SMALL}, saved[l], l, early_grads, deps)
        deps = grads_of_layer(l, dh)
        for n in SMALL:
            gsmall[n][l] = gs[n].reshape(sp[n][l].shape)
    return loss, dh, gsmall


def _all_gather(x, *, name, deps=()):
    def body(x_ref, *rest):
        out_ref, send_sems, recv_sems, local_sem = rest[len(deps):]
        x, y, c = lax.axis_index("x"), lax.axis_index("y"), lax.axis_index("c")
        me, sibling = (x, y, c), (x, y, 1 - c)
        chips = [(1 - x, y), (x, 1 - y), (1 - x, 1 - y)]

        def slot(px, py, pc):
            return out_ref.at[4 * px + 2 * py + pc]

        def copy(k, block, to, src=None):
            return pltpu.make_async_remote_copy(
                src_ref=slot(*block) if src is None else src, dst_ref=slot(*block),
                send_sem=send_sems.at[k], recv_sem=recv_sems.at[k], device_id=to, device_id_type=_MESH)

        mine = pltpu.make_async_copy(x_ref, slot(*me), local_sem)
        mine.start()
        first = [copy(0, me, sibling, src=x_ref)]
        first += [copy(1 + j, me, (*chip, c), src=x_ref) for j, chip in enumerate(chips)]
        for cp in first:
            cp.start()
        passed = [copy(4 + j, (*chip, c), sibling) for j, chip in enumerate(chips)]
        for j, chip in enumerate(chips):
            copy(1 + j, (*chip, c), me).wait_recv()
            passed[j].start()
        copy(0, sibling, me).wait_recv()
        for j, chip in enumerate(chips):
            copy(4 + j, (*chip, 1 - c), me).wait_recv()
        for cp in first + passed:
            cp.wait_send()
        mine.wait()

    return pl.pallas_call(
        body, name=name, out_shape=jax.ShapeDtypeStruct((N_DEV,) + x.shape, x.dtype),
        in_specs=[_ANY] * (1 + len(deps)), out_specs=_ANY,
        scratch_shapes=[pltpu.SemaphoreType.DMA((7,)), pltpu.SemaphoreType.DMA((7,)), pltpu.SemaphoreType.DMA],
    )(x, *deps)


def _exchange(g, *, name):
    def body(g_ref, r_ref, send_sems, recv_sems, local_sem):
        x, y, c = lax.axis_index("x"), lax.axis_index("y"), lax.axis_index("c")
        me = 4 * x + 2 * y + c
        mine = pltpu.make_async_copy(g_ref.at[me], r_ref.at[me], local_sem)
        mine.start()

        def peer(k):
            px, py, pc = (x + (k >> 2)) % 2, (y + ((k >> 1) & 1)) % 2, (c + (k & 1)) % 2
            return (px, py, pc), 4 * px + 2 * py + pc

        def copy(k, src_block, dst_block):
            to, _ = peer(k)
            return pltpu.make_async_remote_copy(
                src_ref=g_ref.at[src_block], dst_ref=r_ref.at[dst_block],
                send_sem=send_sems.at[k - 1], recv_sem=recv_sems.at[k - 1], device_id=to, device_id_type=_MESH)

        sends = [copy(k, peer(k)[1], me) for k in range(1, N_DEV)]
        for cp in sends:
            cp.start()
        for k in range(1, N_DEV):
            copy(k, me, peer(k)[1]).wait_recv()
        for cp in sends:
            cp.wait_send()
        mine.wait()

    return pl.pallas_call(
        body, name=name, out_shape=jax.ShapeDtypeStruct(g.shape, g.dtype),
        in_specs=[_ANY], out_specs=_ANY,
        scratch_shapes=[pltpu.SemaphoreType.DMA((7,)), pltpu.SemaphoreType.DMA((7,)), pltpu.SemaphoreType.DMA],
    )(g)


_HBM = pl.BlockSpec(memory_space=pltpu.HBM)
_SEM = pl.BlockSpec(memory_space=pltpu.SEMAPHORE)
_EFFECT = pltpu.SideEffectType.DATAFLOW_SIDE_EFFECTING
N_PEERS = N_DEV - 1


def _me_and_peers():
    x, y, c = lax.axis_index("x"), lax.axis_index("y"), lax.axis_index("c")
    peers = []
    for k in range(1, N_DEV):
        px, py, pc = (x + (k >> 2)) % 2, (y + ((k >> 1) & 1)) % 2, (c + (k & 1)) % 2
        peers.append(((px, py, pc), 4 * px + 2 * py + pc))
    return 4 * x + 2 * y + c, peers


def _send_start(srcs, after, *, per_peer, name):
    n_t = len(srcs)
    blks = [s.shape[1:] if per_peer else s.shape for s in srcs]
    lands = [lax.empty((N_DEV,) + b, s.dtype) for b, s in zip(blks, srcs)]

    def body(*refs):
        src_refs, land_refs = refs[:n_t], refs[n_t:2 * n_t]
        send_sems, recv_sems = refs[2 * n_t + 1], refs[2 * n_t + 2]
        token = refs[-1]
        me, peers = _me_and_peers()
        for t in range(n_t):
            for k, (to, idx) in enumerate(peers):
                pltpu.make_async_remote_copy(
                    src_ref=src_refs[t].at[idx] if per_peer else src_refs[t], dst_ref=land_refs[t].at[me],
                    send_sem=send_sems.at[t * N_PEERS + k], recv_sem=recv_sems.at[t * N_PEERS + k],
                    device_id=to, device_id_type=_MESH).start()
        token[...] = jnp.zeros_like(token)

    sems = pltpu.SemaphoreType.DMA((n_t * N_PEERS,))
    outs = pl.pallas_call(
        body, name=name,
        out_shape=(sems, sems, *[pltpu.HBM(s.shape, s.dtype) for s in srcs], *[pltpu.HBM(z.shape, z.dtype) for z in lands],
                   jax.ShapeDtypeStruct((8, 128), F32)),
        in_specs=[_HBM] * (2 * n_t) + [_ANY],
        out_specs=(_SEM, _SEM, *[_HBM] * (2 * n_t), pl.BlockSpec(memory_space=pltpu.VMEM)),
        input_output_aliases={i: 2 + i for i in range(2 * n_t)},
        compiler_params=pltpu.CompilerParams(has_side_effects=_EFFECT),
    )(*[pltpu.with_memory_space_constraint(s, pltpu.HBM) for s in srcs],
      *[pltpu.with_memory_space_constraint(z, pltpu.HBM) for z in lands], after)
    return outs[0], outs[1], list(outs[2:2 + n_t]), list(outs[2 + n_t:2 + 2 * n_t]), outs[-1]


def _send_wait(handles, after, *, per_peer, name):
    send_sems, recv_sems, srcs, lands = handles
    n_t = len(srcs)

    def body(*refs):
        src_refs, land_refs = refs[:n_t], refs[n_t:2 * n_t]
        send_sems, recv_sems = refs[2 * n_t], refs[2 * n_t + 1]
        _, peers = _me_and_peers()
        for t in range(n_t):
            for k, (to, idx) in enumerate(peers):
                cp = pltpu.make_async_remote_copy(
                    src_ref=src_refs[t].at[idx] if per_peer else src_refs[t], dst_ref=land_refs[t].at[idx],
                    send_sem=send_sems.at[t * N_PEERS + k], recv_sem=recv_sems.at[t * N_PEERS + k],
                    device_id=to, device_id_type=_MESH)
                cp.wait_send()
                cp.wait_recv()

    outs = pl.pallas_call(
        body, name=name,
        out_shape=(*[pltpu.HBM(s.shape, s.dtype) for s in srcs], *[pltpu.HBM(z.shape, z.dtype) for z in lands]),
        in_specs=[_HBM] * (2 * n_t) + [_SEM, _SEM, _ANY], out_specs=tuple([_HBM] * (2 * n_t)),
        input_output_aliases={i: i for i in range(2 * n_t)},
        compiler_params=pltpu.CompilerParams(has_side_effects=_EFFECT),
    )(*srcs, *lands, send_sems, recv_sems, after)
    me = 4 * lax.axis_index("x") + 2 * lax.axis_index("y") + lax.axis_index("c")
    filled = []
    for src, land in zip(outs[:n_t], outs[n_t:]):
        own = lax.dynamic_index_in_dim(src, me, 0, keepdims=False) if per_peer else src
        filled.append(lax.dynamic_update_index_in_dim(land, own, me, 0))
    return filled


def _adamw_layer(w, m, v, parts, l, prev, *, name):
    depth, n_rows, n_cols = w.shape
    n_parts = parts.shape[0]
    tr = _elem_rows(n_rows, n_cols, 4 * (8 + n_parts))
    c1 = 1.0 / (1.0 - ADAM_B1 ** ADAM_STEP)
    c2 = 1.0 / (1.0 - ADAM_B2 ** ADAM_STEP)
    n_prev = 0 if prev is None else 4

    def body(w_ref, m_ref, v_ref, p_ref, *rest):
        g_ref, d_ref, nm_ref, nv_ref = rest[n_prev:]
        g = p_ref[0].astype(F32)
        for k in range(1, n_parts):
            g = g + p_ref[k].astype(F32)
        nm = ADAM_B1 * m_ref[...] + (1.0 - ADAM_B1) * g
        nv = ADAM_B2 * v_ref[...] + (1.0 - ADAM_B2) * (g * g)
        g_ref[...] = g
        nm_ref[...] = nm
        nv_ref[...] = nv
        d_ref[...] = -ADAM_LR * ((nm * c1) / (jnp.sqrt(nv * c2) + ADAM_EPS) + ADAM_WD * w_ref[...])

    spec = pl.BlockSpec((None, tr, n_cols), lambda i: (l, i, 0))
    return pl.pallas_call(
        body, name=name, grid=(n_rows // tr,),
        in_specs=[spec, spec, spec, pl.BlockSpec((n_parts, tr, n_cols), lambda i: (0, i, 0))] + [_ANY] * n_prev,
        out_specs=[spec] * 4, out_shape=[jax.ShapeDtypeStruct(w.shape, F32)] * 4,
        input_output_aliases={4 + i: i for i in range(n_prev)},
        compiler_params=_cparams(("parallel",)),
    )(w, m, v, parts, *(prev or ()))


def _concat_cols(parts, *, name):
    n_rows = parts[0].shape[0]
    widths = [p.shape[1] for p in parts]
    tr = _row_tile(n_rows, 1408)

    def body(*refs):
        o_ref, off = refs[-1], 0
        for p_ref, w in zip(refs[:-1], widths):
            o_ref[:, off:off + w] = p_ref[...]
            off += w

    return pl.pallas_call(
        body, name=name, grid=(n_rows // tr,), in_specs=[pl.BlockSpec((tr, w), lambda i: (i, 0)) for w in widths],
        out_specs=pl.BlockSpec((tr, sum(widths)), lambda i: (i, 0)),
        out_shape=jax.ShapeDtypeStruct((n_rows, sum(widths)), parts[0].dtype), compiler_params=_cparams(("parallel",)))(*parts)


def _full_weights(g):
    return {n: v if n == "w_up" else v.reshape(N_DEV * v.shape[1], v.shape[2]) for n, v in g.items()}


def _grad_blocks(gb):
    return [g if n == "w_up" else g.reshape(N_DEV, g.shape[0] // N_DEV, g.shape[1]) for n, g in gb.items()]


_SMALL_ROWS = 1096


def _pack_small(d):
    flat = jnp.concatenate([d[n].reshape(-1) for n in SMALL])
    total = N_DEV * _SMALL_ROWS * 128
    assert flat.shape[0] <= total
    return jnp.pad(flat, (0, total - flat.shape[0])).reshape(N_DEV * _SMALL_ROWS, 128)


def _unpack_small(packed, like):
    flat = packed.reshape(-1)
    out, off = {}, 0
    for n in SMALL:
        size = like[n].size
        out[n] = flat[off:off + size].reshape(like[n].shape)
        off += size
    return out


def kernel(x, meta_tokens, norm_mix_g, w_in, q_norm_g, k_norm_g, attn_sinks, ssm_lambda_re, ssm_lambda_im, ssm_log_step, ssm_b_re, ssm_b_im, ssm_c_re, ssm_c_im, ssm_d, w_glu, b_glu, attn_out_g, ssm_out_g, w_out, norm_mlp_g, w_up, w_down, loss_target, m_meta_tokens, m_norm_mix_g, m_w_in, m_q_norm_g, m_k_norm_g, m_attn_sinks, m_ssm_lambda_re, m_ssm_lambda_im, m_ssm_log_step, m_ssm_b_re, m_ssm_b_im, m_ssm_c_re, m_ssm_c_im, m_ssm_d, m_w_glu, m_b_glu, m_attn_out_g, m_ssm_out_g, m_w_out, m_norm_mlp_g, m_w_up, m_w_down, v_meta_tokens, v_norm_mix_g, v_w_in, v_q_norm_g, v_k_norm_g, v_attn_sinks, v_ssm_lambda_re, v_ssm_lambda_im, v_ssm_log_step, v_ssm_b_re, v_ssm_b_im, v_ssm_c_re, v_ssm_c_im, v_ssm_d, v_w_glu, v_b_glu, v_attn_out_g, v_ssm_out_g, v_w_out, v_norm_mlp_g, v_w_up, v_w_down):
    a = dict(locals())
    order = ("meta_tokens", "norm_mix_g", "w_in", "q_norm_g", "k_norm_g", "attn_sinks", "ssm_lambda_re", "ssm_lambda_im",
             "ssm_log_step", "ssm_b_re", "ssm_b_im", "ssm_c_re", "ssm_c_im", "ssm_d", "w_glu", "b_glu", "attn_out_g",
             "ssm_out_g", "w_out", "norm_mlp_g", "w_up", "w_down")

    for n in ("w_in", "m_w_in", "v_w_in"):
        a[n] = jnp.swapaxes(a[n], 1, 2)
    no_dep = jnp.zeros((8, 128), F32)
    sp = {n: a[n] for n in SMALL}
    gathers, exchanges = {}, {}
    updated = {n: None for n in BIG}
    groups = (("w_in",), ("w_glu", "w_out"), ("w_up",), ("w_down",))

    def start_group(l, gi, after):
        *handles, token = _send_start([wb[n][l] for n in groups[gi]], after, per_peer=False, name=f"gather_start_l{l}_g{gi}")
        gathers[l, gi] = handles
        return token

    def start_gather(l, after):
        for gi in range(len(groups)):
            after = start_group(l, gi, after)
        return after

    wb, token0 = {}, no_dep
    for gi, names in enumerate(groups):
        for n in names:
            depth, r, c = a[n].shape
            wb[n] = _cast_bf16(a[n].reshape(depth * r, c), name="cast_" + n, deps=(token0,) if gi else ()).reshape(depth, r, c)
        token0 = start_group(0, gi, token0)
    meta_all = _all_gather(meta_tokens, name="gather_meta", deps=(token0,))
    meta = jnp.transpose(meta_all, (1, 0, 2)).reshape(N_META, D_MODEL)

    def weights_for_layer(l, h):
        token = token0 if l == 0 else h
        if l + 1 < DEPTH:
            token = start_gather(l + 1, token)
        got = {}

        def fetch(name, after):
            if name not in got:
                gi = [name in names for names in groups].index(True)
                lands = _send_wait(gathers.pop((l, gi)), after, per_peer=False, name=f"gather_wait_l{l}_g{gi}")
                got.update(_full_weights(dict(zip(groups[gi], lands))))
            return got[name]

        return fetch, (token,)

    def update_layer(l, after):
        for part in ("a", "b", "c"):
            names, handles = exchanges.pop((l, part))
            recv = _send_wait(handles, after, per_peer=True, name=f"exchange_wait_l{l}_{part}")
            for n, parts in zip(names, recv):
                updated[n] = _adamw_layer(a[n], a["m_" + n], a["v_" + n], parts, l, updated[n], name=f"adamw_{n}_l{l}")

    def early_grads(l, part, gb):
        *handles, token = _send_start(_grad_blocks(gb), no_dep, per_peer=True, name=f"exchange_start_l{l}_{part}")
        exchanges[l, part] = (tuple(gb), handles)
        return (token,)

    def grads_of_layer(l, dh):
        if l + 1 < DEPTH:
            update_layer(l + 1, dh)
        return ()

    loss, dh0, gsmall = _local_step(x[0], loss_target[0], meta, sp, weights_for_layer, early_grads, grads_of_layer)
    loss = lax.psum(loss[0, 0], ("x", "y", "c"))
    grad, delta, new_m, new_v = {}, {}, {}, {}

    dmeta = jnp.transpose(dh0[PAD:BLOCK].reshape(N_META, N_DEV, D_MODEL // N_DEV), (1, 0, 2))
    outs = _adamw(meta_tokens, m_meta_tokens, v_meta_tokens, _exchange(dmeta, name="exchange_meta"), name="adamw_meta_tokens")
    grad["meta_tokens"], delta["meta_tokens"], new_m["meta_tokens"], new_v["meta_tokens"] = outs

    packed = _pack_small({n: jnp.stack(gsmall[n]) for n in SMALL}).reshape(N_DEV, _SMALL_ROWS, 128)
    share = _sum_parts(_exchange(packed, name="exchange_small"), name="sum_small")
    total = _all_gather(share, name="gather_small").reshape(1, N_DEV * _SMALL_ROWS, 128)
    gsum = _unpack_small(total, sp)
    for n in SMALL:
        as2d = lambda v: v.reshape(-1, v.shape[-1])
        outs = _adamw(as2d(a[n]), as2d(a["m_" + n]), as2d(a["v_" + n]), as2d(gsum[n])[None], name="adamw_" + n)
        grad[n], delta[n], new_m[n], new_v[n] = [o.reshape(a[n].shape) for o in outs]

    update_layer(0, outs[0])
    for n in BIG:
        grad[n], delta[n], new_m[n], new_v[n] = [jnp.swapaxes(o, 1, 2) if n == "w_in" else o for o in updated[n]]

    return (loss, dh0[BLOCK:][None], *[grad[n] for n in order], *[delta[n] for n in order],
            *[new_m[n] for n in order], *[new_v[n] for n in order])
```

```python
import math

import jax
import jax.numpy as jnp
from jax import lax
from jax.experimental import pallas as pl
from jax.experimental.pallas import tpu as pltpu

F32 = jnp.float32
BF16 = jnp.bfloat16

N_DEV = 8
D_MODEL = 2048
SEQ = 4096
DEPTH = 4
N_META = 16
HEAD_DIM = 64
ATTN_WIDTH = D_MODEL // 2
N_HEADS = ATTN_WIDTH // HEAD_DIM
N_KV_HEADS = N_HEADS // 4
KV_GROUP = N_HEADS // N_KV_HEADS
KV_WIDTH = N_KV_HEADS * HEAD_DIM
SSM_WIDTH = D_MODEL - ATTN_WIDTH
SSM_GROUP_CH = 16
SSM_GROUPS = SSM_WIDTH // SSM_GROUP_CH
SSM_STATE = 64
WINDOW = 128
BLOCK = 128
PAD = BLOCK - N_META
D_FF = 4 * D_MODEL
IN_WIDTH = ATTN_WIDTH + 2 * KV_WIDTH + SSM_WIDTH
NORM_EPS = 1e-6
NEG_INF = -1e30
ADAM_LR = 0.001
ADAM_B1 = 0.9
ADAM_B2 = 0.999
ADAM_EPS = 1e-08
ADAM_WD = 0.01
ADAM_STEP = 10

VMEM_LIMIT = 56 * 1024 * 1024
_MESH = pl.DeviceIdType.MESH
_ANY = pl.BlockSpec(memory_space=pl.ANY)


def _cparams(sem=None):
    return pltpu.CompilerParams(dimension_semantics=sem, vmem_limit_bytes=VMEM_LIMIT)


def _matmul(a, b, *, mode, tm, tn, tk, outs, epilogue, tiles=(), deps=(), blocked=False, name):
    if mode == "nn":
        m, k = a.shape
        if blocked:
            n = b.shape[0] * b.shape[2]
            assert tn == b.shape[2]
            b_spec = pl.BlockSpec((None, tk, tn), lambda i, j, kk: (j, kk, 0))
        else:
            n = b.shape[1]
            b_spec = pl.BlockSpec((tk, tn), lambda i, j, kk: (kk, j))
        a_spec = pl.BlockSpec((tm, tk), lambda i, j, kk: (i, kk))
        dims = (((1,), (0,)), ((), ()))
    elif mode == "nt":
        m, k = a.shape
        if blocked:
            n = b.shape[1]
            assert tk == b.shape[2] and k == b.shape[0] * b.shape[2]
            b_spec = pl.BlockSpec((None, tn, tk), lambda i, j, kk: (kk, j, 0))
        else:
            n = b.shape[0]
            b_spec = pl.BlockSpec((tn, tk), lambda i, j, kk: (j, kk))
        a_spec = pl.BlockSpec((tm, tk), lambda i, j, kk: (i, kk))
        dims = (((1,), (1,)), ((), ()))
    else:
        (k, m), n = a.shape, b.shape[1]
        a_spec = pl.BlockSpec((tk, tm), lambda i, j, kk: (kk, i))
        b_spec = pl.BlockSpec((tk, tn), lambda i, j, kk: (kk, j))
        dims = (((0,), (0,)), ((), ()))
    assert m % tm == 0 and n % tn == 0 and k % tk == 0, (name, m, n, k, tm, tn, tk)
    nk = k // tk
    n_tiles, n_outs, n_deps = len(tiles), len(outs), len(deps)

    def body(a_ref, b_ref, *rest):
        tile_refs = rest[:n_tiles]
        out_refs = rest[n_tiles + n_deps:n_tiles + n_deps + n_outs]

        def product():
            return lax.dot_general(a_ref[...].astype(BF16), b_ref[...].astype(BF16), dims, preferred_element_type=F32)

        def finish(acc):
            res = epilogue(acc, *[r[...] for r in tile_refs])
            for o_ref, o in zip(out_refs, res):
                o_ref[...] = o.astype(o_ref.dtype)

        if nk == 1:
            finish(product())
            return
        acc_ref = rest[-1]
        kk = pl.program_id(2)

        @pl.when(kk == 0)
        def _():
            acc_ref[...] = jnp.zeros_like(acc_ref)

        acc_ref[...] += product()

        @pl.when(kk == nk - 1)
        def _():
            finish(acc_ref[...])

    tile_spec = pl.BlockSpec((tm, tn), lambda i, j, kk: (i, j))
    if mode == "tn" and blocked:
        out_specs = [pl.BlockSpec((None, tm, tn), lambda i, j, kk: (j, i, 0))] * n_outs
        out_shape = [jax.ShapeDtypeStruct((n // tn, m, tn), dt) for dt in outs]
    else:
        out_specs = [tile_spec] * n_outs
        out_shape = [jax.ShapeDtypeStruct((m, n), dt) for dt in outs]
    return pl.pallas_call(
        body, name=name, grid=(m // tm, n // tn, nk),
        in_specs=[a_spec, b_spec] + [tile_spec] * n_tiles + [_ANY] * n_deps,
        out_specs=out_specs, out_shape=out_shape,
        scratch_shapes=[pltpu.VMEM((tm, tn), F32)] if nk > 1 else [],
        compiler_params=_cparams(("parallel", "parallel", "arbitrary")),
    )(a, b, *tiles, *deps)


def _ident(acc):
    return (acc,)


def _row_tile(n_rows, cap):
    best = BLOCK
    for t in range(BLOCK, cap + 1, BLOCK):
        if n_rows % t == 0:
            best = t
    return best


def _rmsnorm_fwd(xs, gs, *, name, deps=()):
    n_rows, width = xs[0].shape
    n = len(xs)
    tr = _row_tile(n_rows, 384)

    def body(*refs):
        o_ref = refs[-1]
        parts = []
        for x_ref, g_ref in zip(refs[:n], refs[n:2 * n]):
            x = x_ref[...]
            r = lax.rsqrt(jnp.mean(x * x, axis=-1, keepdims=True) + NORM_EPS)
            parts.append(x * r * g_ref[...])
        o_ref[...] = (parts[0] if n == 1 else jnp.concatenate(parts, axis=1)).astype(BF16)

    return pl.pallas_call(
        body, name=name, grid=(n_rows // tr,),
        in_specs=[pl.BlockSpec((tr, width), lambda i: (i, 0))] * n + [pl.BlockSpec((1, width), lambda i: (0, 0))] * n
        + [_ANY] * len(deps),
        out_specs=pl.BlockSpec((tr, n * width), lambda i: (i, 0)),
        out_shape=jax.ShapeDtypeStruct((n_rows, n * width), BF16),
        compiler_params=_cparams(("parallel",)),
    )(*xs, *gs, *deps)


def _rmsnorm_bwd(xs, gs, dy, res, *, name):
    n_rows, width = xs[0].shape
    n = len(xs)
    tr = _row_tile(n_rows, 384)
    has_res = res is not None

    def body(*refs):
        x_refs, g_refs, dy_ref = refs[:n], refs[n:2 * n], refs[2 * n]
        res_ref = refs[2 * n + 1] if has_res else None
        outs = refs[2 * n + 1 + int(has_res):]
        dx_refs, dg_refs = outs[:n], outs[n:2 * n]
        i = pl.program_id(0)
        for c in range(n):
            x = x_refs[c][...]
            d = dy_ref[:, c * width:(c + 1) * width]
            r = lax.rsqrt(jnp.mean(x * x, axis=-1, keepdims=True) + NORM_EPS)
            xh = x * r
            gd = d * g_refs[c][...]
            dx = r * (gd - xh * jnp.mean(gd * xh, axis=-1, keepdims=True))
            if has_res:
                dx = dx + res_ref[...]
                outs[2 * n][...] = dx.astype(BF16)
            dx_refs[c][...] = dx
            part = jnp.sum(d * xh, axis=0, keepdims=True)

            @pl.when(i == 0)
            def _():
                dg_refs[c][...] = part

            @pl.when(i > 0)
            def _():
                dg_refs[c][...] += part

    row_spec = pl.BlockSpec((tr, width), lambda i: (i, 0))
    vec_spec = pl.BlockSpec((1, width), lambda i: (0, 0))
    outs = pl.pallas_call(
        body, name=name, grid=(n_rows // tr,),
        in_specs=[row_spec] * n + [vec_spec] * n + [pl.BlockSpec((tr, n * width), lambda i: (i, 0))] + [row_spec] * int(has_res),
        out_specs=[row_spec] * n + [vec_spec] * n + [row_spec] * int(has_res),
        out_shape=[jax.ShapeDtypeStruct((n_rows, width), F32)] * n + [jax.ShapeDtypeStruct((1, width), F32)] * n
        + [jax.ShapeDtypeStruct((n_rows, width), BF16)] * int(has_res),
        compiler_params=_cparams(("arbitrary",)),
    )(*xs, *gs, dy, *([res] if has_res else []))
    if has_res:
        return outs[:n], outs[n:2 * n], outs[2 * n]
    return outs[:n], outs[n:]


_SCALE = 1.0 / math.sqrt(HEAD_DIM)
_DN_NT = (((1,), (1,)), ((), ()))
_DN_TN = (((0,), (0,)), ((), ()))


def _head_norm(x, g):
    r = lax.rsqrt(jnp.mean(x * x, axis=-1, keepdims=True) + NORM_EPS)
    return x * r * g, r


def _attn_bias():
    rows = KV_GROUP * BLOCK
    r = lax.broadcasted_iota(jnp.int32, (rows, 3 * BLOCK), 0)
    j = lax.broadcasted_iota(jnp.int32, (rows, 3 * BLOCK), 1)
    i = r % BLOCK
    is_meta = j < BLOCK
    dist_band = 2 * BLOCK + i - j
    ok = jnp.where(is_meta, j >= PAD, (dist_band >= 0) & (dist_band < WINDOW))
    dist = jnp.where(is_meta, i - j, dist_band).astype(F32)
    slopes = jnp.asarray([2.0 ** (-8.0 * (h + 1) / N_HEADS) for h in range(N_HEADS)], F32).reshape(N_KV_HEADS, KV_GROUP)
    slope_rows = jnp.repeat(slopes, BLOCK, axis=1)[:, :, None]
    bias = jnp.where(ok[None], -slope_rows * dist[None], NEG_INF)
    neg0 = jnp.where(j[:, :BLOCK] <= i[:, :BLOCK], 0.0, NEG_INF)
    return bias.astype(F32), neg0.astype(F32)


def _slope_col(kv):
    g = lax.broadcasted_iota(jnp.int32, (KV_GROUP * BLOCK, 1), 0) // BLOCK
    col = jnp.zeros((KV_GROUP * BLOCK, 1), F32)
    for gi in range(KV_GROUP):
        col = jnp.where(g == gi, 2.0 ** (-8.0 * (kv * KV_GROUP + gi + 1) / N_HEADS), col)
    return col


def _sink_col(sink_ref, kv):
    g = lax.broadcasted_iota(jnp.int32, (KV_GROUP * BLOCK, 1), 0) // BLOCK
    col = jnp.zeros((KV_GROUP * BLOCK, 1), F32)
    for gi in range(KV_GROUP):
        h = kv * KV_GROUP + gi
        col = jnp.where(g == gi, sink_ref[0:1, h:h + 1], col)
    return col


def _stack_heads(x, kv):
    return jnp.concatenate([x[:, (kv * KV_GROUP + g) * HEAD_DIM:(kv * KV_GROUP + g + 1) * HEAD_DIM]
                            for g in range(KV_GROUP)], axis=0)


def _attn_scores(q_ref, k_refs, gq_ref, gk_ref, sink_ref, bias_ref, neg0_ref, kv, n):
    qs = _stack_heads(q_ref[...], kv)
    kcat = jnp.concatenate([r[:, kv * HEAD_DIM:(kv + 1) * HEAD_DIM] for r in k_refs], axis=0)
    qn, rq = _head_norm(qs, gq_ref[...])
    kn, rk = _head_norm(kcat, gk_ref[...])
    s = lax.dot_general((qn * _SCALE).astype(BF16), kn.astype(BF16), _DN_NT, preferred_element_type=F32)
    first = jnp.where(n == 0, 1.0, 0.0)
    second = jnp.where(n == 1, 1.0, 0.0)
    meta = (s[:, :BLOCK] + bias_ref[kv, :, :BLOCK] + _slope_col(kv) * (-float(BLOCK) * n.astype(F32))
            + first * neg0_ref[...])
    in_prev = jnp.where(lax.broadcasted_iota(jnp.int32, (1, 2 * BLOCK), 1) < BLOCK, 1.0, 0.0)
    band = s[:, BLOCK:] + bias_ref[kv, :, BLOCK:] + NEG_INF * (first + second * in_prev)
    s = jnp.concatenate([meta, band], axis=1)
    return qs, kcat, qn, kn, rq, rk, s, _sink_col(sink_ref, kv)


def _attn_specs():
    kq = ATTN_WIDTH // KV_WIDTH
    q_spec = pl.BlockSpec((BLOCK, ATTN_WIDTH), lambda n: (n, 0))
    kv_specs = []
    for col in (kq, kq + 1):
        kv_specs += [pl.BlockSpec((BLOCK, KV_WIDTH), lambda n, col=col: (0, col)),
                     pl.BlockSpec((BLOCK, KV_WIDTH), lambda n, col=col: (jnp.maximum(n - 1, 0), col)),
                     pl.BlockSpec((BLOCK, KV_WIDTH), lambda n, col=col: (n, col))]
    small = [pl.BlockSpec((1, HEAD_DIM), lambda n: (0, 0)), pl.BlockSpec((1, HEAD_DIM), lambda n: (0, 0)),
             pl.BlockSpec((1, N_HEADS), lambda n: (0, 0)),
             pl.BlockSpec((N_KV_HEADS, KV_GROUP * BLOCK, 3 * BLOCK), lambda n: (0, 0, 0)),
             pl.BlockSpec((KV_GROUP * BLOCK, BLOCK), lambda n: (0, 0))]
    return q_spec, kv_specs, small


def _attn_fwd(proj, gq, gk, sinks, *, name):
    n_rows = proj.shape[0]
    q_spec, kv_specs, small = _attn_specs()

    def body(q_ref, k0, k1, k2, v0, v1, v2, gq_ref, gk_ref, sink_ref, bias_ref, neg0_ref, o_ref, lse_ref):
        n = pl.program_id(0)
        o_parts, lse_parts = [], []
        for kv in range(N_KV_HEADS):
            _, _, _, _, _, _, s, sink = _attn_scores(q_ref, (k0, k1, k2), gq_ref, gk_ref, sink_ref, bias_ref, neg0_ref, kv, n)
            vcat = jnp.concatenate([r[:, kv * HEAD_DIM:(kv + 1) * HEAD_DIM] for r in (v0, v1, v2)], axis=0)
            m = jnp.maximum(jnp.max(s, axis=-1, keepdims=True), sink)
            p = jnp.exp(s - m)
            l = jnp.sum(p, axis=-1, keepdims=True) + jnp.exp(sink - m)
            o = jnp.dot(p.astype(BF16), vcat.astype(BF16), preferred_element_type=F32) / l
            lse = m + jnp.log(l)
            o_parts += [o[g * BLOCK:(g + 1) * BLOCK] for g in range(KV_GROUP)]
            lse_parts += [lse[g * BLOCK:(g + 1) * BLOCK] for g in range(KV_GROUP)]
        o_ref[...] = jnp.concatenate(o_parts, axis=1)
        lse_ref[...] = jnp.concatenate(lse_parts, axis=1)

    return pl.pallas_call(
        body, name=name, grid=(n_rows // BLOCK,),
        in_specs=[q_spec] + kv_specs + small,
        out_specs=[pl.BlockSpec((BLOCK, ATTN_WIDTH), lambda n: (n, 0)), pl.BlockSpec((BLOCK, N_HEADS), lambda n: (n, 0))],
        out_shape=[jax.ShapeDtypeStruct((n_rows, ATTN_WIDTH), F32), jax.ShapeDtypeStruct((n_rows, N_HEADS), F32)],
        compiler_params=_cparams(("parallel",)),
    )(proj, proj, proj, proj, proj, proj, proj, gq, gk, sinks, *_attn_bias())


def _attn_bwd(proj, gq, gk, sinks, o, lse, do, *, name):
    n_rows = proj.shape[0]
    q_spec, kv_specs, small = _attn_specs()

    def body(q_ref, k0, k1, k2, v0, v1, v2, gq_ref, gk_ref, sink_ref, bias_ref, neg0_ref, o_ref, lse_ref, do_ref,
             dq_ref, dkb_ref, dvb_ref, dgq_ref, dgk_ref, dsink_ref, dk_ref, dv_ref):
        n = pl.program_id(0)

        @pl.when(n == 0)
        def _():
            dk_ref[...] = jnp.zeros_like(dk_ref)
            dv_ref[...] = jnp.zeros_like(dv_ref)
            dgq_ref[...] = jnp.zeros_like(dgq_ref)
            dgk_ref[...] = jnp.zeros_like(dgk_ref)
            dsink_ref[...] = jnp.zeros_like(dsink_ref)

        dq_parts, dk_parts, dv_parts, dsink_parts = [], [], [], []
        dgq = jnp.zeros((1, HEAD_DIM), F32)
        dgk = jnp.zeros((1, HEAD_DIM), F32)
        for kv in range(N_KV_HEADS):
            qs, kcat, qn, kn, rq, rk, s, sink = _attn_scores(q_ref, (k0, k1, k2), gq_ref, gk_ref, sink_ref, bias_ref, neg0_ref, kv, n)
            vcat = jnp.concatenate([r[:, kv * HEAD_DIM:(kv + 1) * HEAD_DIM] for r in (v0, v1, v2)], axis=0)
            os_ = _stack_heads(o_ref[...], kv)
            dos = _stack_heads(do_ref[...], kv)
            lse = jnp.concatenate([lse_ref[:, kv * KV_GROUP + g:kv * KV_GROUP + g + 1] for g in range(KV_GROUP)], axis=0)
            p = jnp.exp(s - lse)
            delta = jnp.sum(dos * os_, axis=-1, keepdims=True)
            dp = lax.dot_general(dos.astype(BF16), vcat.astype(BF16), _DN_NT, preferred_element_type=F32)
            ds = (p * (dp - delta)) * _SCALE
            dsink_rows = -jnp.exp(sink - lse) * delta
            dsink_parts += [jnp.sum(dsink_rows[g * BLOCK:(g + 1) * BLOCK], axis=0, keepdims=True) for g in range(KV_GROUP)]
            dv_parts.append(lax.dot_general(p.astype(BF16), dos.astype(BF16), _DN_TN, preferred_element_type=F32))
            dsb = ds.astype(BF16)
            dqn = jnp.dot(dsb, kn.astype(BF16), preferred_element_type=F32)
            dkn = lax.dot_general(dsb, qn.astype(BF16), _DN_TN, preferred_element_type=F32)
            qh = qs * rq
            gd = dqn * gq_ref[...]
            dqs = rq * (gd - qh * jnp.mean(gd * qh, axis=-1, keepdims=True))
            dgq = dgq + jnp.sum(dqn * qh, axis=0, keepdims=True)
            kh = kcat * rk
            gdk = dkn * gk_ref[...]
            dk_parts.append(rk * (gdk - kh * jnp.mean(gdk * kh, axis=-1, keepdims=True)))
            dgk = dgk + jnp.sum(dkn * kh, axis=0, keepdims=True)
            dq_parts += [dqs[g * BLOCK:(g + 1) * BLOCK] for g in range(KV_GROUP)]
        dq_ref[...] = jnp.concatenate(dq_parts, axis=1).astype(BF16)
        dkc = jnp.concatenate(dk_parts, axis=1)
        dvc = jnp.concatenate(dv_parts, axis=1)
        prev = pl.multiple_of(jnp.maximum(n - 1, 0) * BLOCK, BLOCK)
        cur = pl.multiple_of(n * BLOCK, BLOCK)
        for acc_ref, val in ((dk_ref, dkc), (dv_ref, dvc)):
            acc_ref[0:BLOCK, :] += val[0:BLOCK]
            acc_ref[pl.ds(prev, BLOCK), :] += val[BLOCK:2 * BLOCK]
            acc_ref[pl.ds(cur, BLOCK), :] += val[2 * BLOCK:3 * BLOCK]
        dgq_ref[...] += dgq
        dgk_ref[...] += dgk
        dsink_ref[...] += jnp.concatenate(dsink_parts, axis=1)

        @pl.when(n == pl.num_programs(0) - 1)
        def _():
            dkb_ref[...] = dk_ref[...].astype(BF16)
            dvb_ref[...] = dv_ref[...].astype(BF16)

    blk = lambda w: pl.BlockSpec((BLOCK, w), lambda n: (n, 0))
    full = lambda r, w: pl.BlockSpec((r, w), lambda n: (0, 0))
    return pl.pallas_call(
        body, name=name, grid=(n_rows // BLOCK,),
        in_specs=[q_spec] + kv_specs + small + [blk(ATTN_WIDTH), blk(N_HEADS), blk(ATTN_WIDTH)],
        out_specs=[blk(ATTN_WIDTH), full(n_rows, KV_WIDTH), full(n_rows, KV_WIDTH),
                   full(1, HEAD_DIM), full(1, HEAD_DIM), full(1, N_HEADS)],
        out_shape=[jax.ShapeDtypeStruct((n_rows, ATTN_WIDTH), BF16), jax.ShapeDtypeStruct((n_rows, KV_WIDTH), BF16),
                   jax.ShapeDtypeStruct((n_rows, KV_WIDTH), BF16), jax.ShapeDtypeStruct((1, HEAD_DIM), F32),
                   jax.ShapeDtypeStruct((1, HEAD_DIM), F32), jax.ShapeDtypeStruct((1, N_HEADS), F32)],
        scratch_shapes=[pltpu.VMEM((n_rows, KV_WIDTH), F32), pltpu.VMEM((n_rows, KV_WIDTH), F32)],
        compiler_params=_cparams(("arbitrary",)),
    )(proj, proj, proj, proj, proj, proj, proj, gq, gk, sinks, *_attn_bias(), o, lse, do)


SSM_LAGS = 8
SLAB_G = 128 // SSM_GROUP_CH
N_SLABS = SSM_GROUPS // SLAB_G
SLAB_STATE = SLAB_G * SSM_STATE
U_COL = (ATTN_WIDTH + 2 * KV_WIDTH) // 128


def _ssm_prep(lam_re, lam_im, log_step, b_re, b_im, c_re, c_im):
    lam = lax.complex(lam_re, lam_im)
    delta = jnp.exp(log_step)[:, None]
    lam_bar = jnp.exp(lam * delta)
    b_t = lax.complex(jnp.swapaxes(b_re, 1, 2), jnp.swapaxes(b_im, 1, 2))
    b_bar = ((lam_bar - 1.0) / lam)[:, None, :] * b_t
    pw = [jnp.ones_like(lam_bar)]
    for _ in range(SSM_LAGS):
        pw.append(pw[-1] * lam_bar)
    w = jnp.stack(pw[:SSM_LAGS])[:, :, None, :] * b_bar[None]
    wri = jnp.stack([jnp.real(w), jnp.imag(w)], axis=3)
    wc = wri.reshape(SSM_LAGS, N_SLABS, SLAB_G * SSM_GROUP_CH, 2 * SSM_STATE)
    wc = jnp.swapaxes(wc, 0, 1).reshape(N_SLABS, SSM_LAGS * 128, 2 * SSM_STATE)
    cri = jnp.stack([c_re, -c_im], axis=2).reshape(N_SLABS, SLAB_G, SSM_GROUP_CH, 2, 1, SSM_STATE)
    eye = jnp.eye(SLAB_G, dtype=F32).reshape(1, SLAB_G, 1, 1, SLAB_G, 1)
    ct = (cri * eye).reshape(N_SLABS, 128, 2 * SLAB_STATE)
    l8 = pw[SSM_LAGS]
    lam8 = jnp.concatenate([jnp.real(l8).reshape(N_SLABS, 1, SLAB_STATE), jnp.imag(l8).reshape(N_SLABS, 1, SLAB_STATE)], axis=2)
    return wc, ct, lam8


def _row_group():
    return (lax.broadcasted_iota(jnp.int32, (SSM_LAGS * 128, 1), 0) // SSM_GROUP_CH) % SLAB_G


def _spread_groups(wc):
    g_of_row = _row_group()
    return jnp.concatenate([jnp.where(g_of_row == g, wc[:, r * SSM_STATE:(r + 1) * SSM_STATE], 0.0)
                            for r in range(2) for g in range(SLAB_G)], axis=1)


def _gather_groups(dw):
    g_of_row = _row_group()
    parts = []
    for r in range(2):
        acc = jnp.zeros((SSM_LAGS * 128, SSM_STATE), F32)
        for g in range(SLAB_G):
            c0 = r * SLAB_STATE + g * SSM_STATE
            acc = acc + jnp.where(g_of_row == g, dw[:, c0:c0 + SSM_STATE], 0.0)
        parts.append(acc)
    return jnp.concatenate(parts, axis=1)


def _lagged(u, up, t_rows):
    ue = jnp.concatenate([up, u], axis=0)
    return jnp.concatenate([ue[SSM_LAGS - tau:SSM_LAGS - tau + t_rows] for tau in range(SSM_LAGS)], axis=1).astype(BF16)


def _ssm_fwd(proj, wc, cmat, lam8, dvec, *, name):
    n_rows = proj.shape[0]
    tt = _row_tile(n_rows, 1408)
    n_t = n_rows // tt
    sw = 2 * SLAB_STATE
    hs = SLAB_STATE

    def body(u_ref, up_ref, wc_ref, c_ref, l_ref, d_ref, y_ref, x_ref, carry_ref, w_ref):
        t = pl.program_id(1)

        @pl.when(t == 0)
        def _():
            carry_ref[...] = jnp.zeros_like(carry_ref)
            w_ref[...] = _spread_groups(wc_ref[...]).astype(BF16)

        u = u_ref[...]
        up = jnp.where(t > 0, up_ref[...], 0.0)
        x_ref[...] = jnp.dot(_lagged(u, up, tt), w_ref[...], preferred_element_type=F32)
        ar = jnp.broadcast_to(l_ref[:, :hs], (8, hs))
        ai = jnp.broadcast_to(l_ref[:, hs:], (8, hs))

        def step(b, c):
            xr, xi = c
            r0 = pl.multiple_of(b * 8, 8)
            w = x_ref[pl.ds(r0, 8), :]
            nr = w[:, :hs] + ar * xr - ai * xi
            ni = w[:, hs:] + ar * xi + ai * xr
            x_ref[pl.ds(r0, 8), :] = jnp.concatenate([nr, ni], axis=1)
            return nr, ni

        xr, xi = lax.fori_loop(0, tt // 8, step, (carry_ref[:, :hs], carry_ref[:, hs:]), unroll=8)
        carry_ref[...] = jnp.concatenate([xr, xi], axis=1)
        y_ref[...] = lax.dot_general(x_ref[...].astype(BF16), c_ref[...], _DN_NT, preferred_element_type=F32) + d_ref[...] * u

    return pl.pallas_call(
        body, name=name, grid=(N_SLABS, n_t),
        in_specs=[pl.BlockSpec((tt, 128), lambda j, t: (t, U_COL + j)),
                  pl.BlockSpec((8, 128), lambda j, t: (jnp.maximum(t * (tt // 8) - 1, 0), U_COL + j)),
                  pl.BlockSpec((None, SSM_LAGS * 128, 2 * SSM_STATE), lambda j, t: (j, 0, 0)),
                  pl.BlockSpec((None, 128, sw), lambda j, t: (j, 0, 0)),
                  pl.BlockSpec((None, 1, sw), lambda j, t: (j, 0, 0)),
                  pl.BlockSpec((1, 128), lambda j, t: (0, j))],
        out_specs=[pl.BlockSpec((tt, 128), lambda j, t: (t, j)), pl.BlockSpec((tt, sw), lambda j, t: (t, j))],
        out_shape=[jax.ShapeDtypeStruct((n_rows, SSM_WIDTH), F32), jax.ShapeDtypeStruct((n_rows, N_SLABS * sw), F32)],
        scratch_shapes=[pltpu.VMEM((8, sw), F32), pltpu.VMEM((SSM_LAGS * 128, sw), BF16)],
        compiler_params=_cparams(("parallel", "arbitrary")),
    )(proj, proj, wc, cmat, lam8, dvec)


def _ssm_bwd(proj, xs, dy, wc, cmat, lam8, dvec, *, name, deps=()):
    n_rows = proj.shape[0]
    tt = _row_tile(n_rows, 704)
    n_t = n_rows // tt
    sw = 2 * SLAB_STATE
    hs = SLAB_STATE

    def body(u_ref, up_ref, x_ref, xp_ref, dy_ref, wc_ref, c_ref, l_ref, d_ref,
             *rest):
        du_ref, dwc_ref, dc_ref, dl_ref, dd_ref, a_ref, carry_ref, head_ref, w_ref, dw_ref = rest[len(deps):]
        t = pl.program_id(1)
        ti = n_t - 1 - t

        @pl.when(t == 0)
        def _():
            w_ref[...] = _spread_groups(wc_ref[...]).astype(BF16)
            carry_ref[...] = jnp.zeros_like(carry_ref)
            head_ref[...] = jnp.zeros_like(head_ref)
            dw_ref[...] = jnp.zeros_like(dw_ref)
            dc_ref[...] = jnp.zeros_like(dc_ref)
            dl_ref[...] = jnp.zeros_like(dl_ref)
            dd_ref[...] = jnp.zeros_like(dd_ref)

        u = u_ref[...]
        up = jnp.where(ti > 0, up_ref[...], 0.0)
        ucat = _lagged(u, up, tt)
        dyv = dy_ref[...]
        dyb = dyv.astype(BF16)
        a_ref[...] = jnp.dot(dyb, c_ref[...], preferred_element_type=F32)
        lr = jnp.broadcast_to(l_ref[:, :hs], (8, hs))
        li = jnp.broadcast_to(l_ref[:, hs:], (8, hs))

        def step(i, c):
            cr, ci = c
            r0 = pl.multiple_of((tt // 8 - 1 - i) * 8, 8)
            g = a_ref[pl.ds(r0, 8), :]
            nr = g[:, :hs] + lr * cr + li * ci
            ni = g[:, hs:] + lr * ci - li * cr
            a_ref[pl.ds(r0, 8), :] = jnp.concatenate([nr, ni], axis=1)
            return nr, ni

        cr, ci = lax.fori_loop(0, tt // 8, step, (carry_ref[:, :hs], carry_ref[:, hs:]), unroll=8)
        carry_ref[...] = jnp.concatenate([cr, ci], axis=1)

        a = a_ref[...]
        xv = x_ref[...]
        xprev = jnp.where(ti > 0, xp_ref[...], 0.0)
        xsh = jnp.concatenate([xprev, xv[:tt - SSM_LAGS]], axis=0)
        a_re, a_im, x_re, x_im = a[:, :hs], a[:, hs:], xsh[:, :hs], xsh[:, hs:]
        dl_ref[...] += jnp.concatenate([jnp.sum(a_re * x_re + a_im * x_im, axis=0, keepdims=True),
                                        jnp.sum(a_im * x_re - a_re * x_im, axis=0, keepdims=True)], axis=1)
        ab = a.astype(BF16)
        dw_ref[...] += lax.dot_general(ucat, ab, _DN_TN, preferred_element_type=F32)
        duc = lax.dot_general(ab, w_ref[...], _DN_NT, preferred_element_type=F32)
        ext = jnp.concatenate([duc, head_ref[...]], axis=0)
        du = d_ref[...] * dyv
        for tau in range(SSM_LAGS):
            du = du + ext[tau:tau + tt, tau * 128:(tau + 1) * 128]
        head_ref[...] = duc[0:8]
        row = ti * tt + lax.broadcasted_iota(jnp.int32, (tt, 128), 0)
        du_ref[...] = jnp.where(row >= PAD, du, 0.0).astype(BF16)
        dd_ref[...] += jnp.sum(dyv * u, axis=0, keepdims=True)
        dc_ref[...] += lax.dot_general(dyb, xv.astype(BF16), _DN_TN, preferred_element_type=F32)

        @pl.when(t == n_t - 1)
        def _():
            dwc_ref[...] = _gather_groups(dw_ref[...])

    rt = lambda t: n_t - 1 - t
    prev8 = lambda t: jnp.maximum(rt(t) * (tt // 8) - 1, 0)
    return pl.pallas_call(
        body, name=name, grid=(N_SLABS, n_t),
        in_specs=[pl.BlockSpec((tt, 128), lambda j, t: (rt(t), U_COL + j)),
                  pl.BlockSpec((8, 128), lambda j, t: (prev8(t), U_COL + j)),
                  pl.BlockSpec((tt, sw), lambda j, t: (rt(t), j)),
                  pl.BlockSpec((8, sw), lambda j, t: (prev8(t), j)),
                  pl.BlockSpec((tt, 128), lambda j, t: (rt(t), j)),
                  pl.BlockSpec((None, SSM_LAGS * 128, 2 * SSM_STATE), lambda j, t: (j, 0, 0)),
                  pl.BlockSpec((None, 128, sw), lambda j, t: (j, 0, 0)),
                  pl.BlockSpec((None, 1, sw), lambda j, t: (j, 0, 0)),
                  pl.BlockSpec((1, 128), lambda j, t: (0, j))] + [_ANY] * len(deps),
        out_specs=[pl.BlockSpec((tt, 128), lambda j, t: (rt(t), j)),
                   pl.BlockSpec((None, SSM_LAGS * 128, 2 * SSM_STATE), lambda j, t: (j, 0, 0)),
                   pl.BlockSpec((None, 128, sw), lambda j, t: (j, 0, 0)),
                   pl.BlockSpec((None, 1, sw), lambda j, t: (j, 0, 0)),
                   pl.BlockSpec((1, 128), lambda j, t: (0, j))],
        out_shape=[jax.ShapeDtypeStruct((n_rows, SSM_WIDTH), BF16),
                   jax.ShapeDtypeStruct((N_SLABS, SSM_LAGS * 128, 2 * SSM_STATE), F32),
                   jax.ShapeDtypeStruct((N_SLABS, 128, sw), F32),
                   jax.ShapeDtypeStruct((N_SLABS, 1, sw), F32),
                   jax.ShapeDtypeStruct((1, SSM_WIDTH), F32)],
        scratch_shapes=[pltpu.VMEM((tt, sw), F32), pltpu.VMEM((8, sw), F32), pltpu.VMEM((8, sw), F32),
                        pltpu.VMEM((SSM_LAGS * 128, sw), BF16), pltpu.VMEM((SSM_LAGS * 128, sw), F32)],
        compiler_params=_cparams(("parallel", "arbitrary")),
    )(proj, proj, xs, xs, dy, wc, cmat, lam8, dvec, *deps)


_GELU_C = math.sqrt(2.0 / math.pi)
_GELU_A = 0.044715


def _gelu(y):
    th = jnp.tanh(_GELU_C * (y + _GELU_A * y * y * y))
    return 0.5 * y * (1.0 + th), th


def _glu_fwd(y, w, b, *, name):
    n_rows, width = y.shape
    tr = _row_tile(n_rows, 384)

    def body(y_ref, w_ref, b_ref, o_ref):
        g, _ = _gelu(y_ref[...])
        z = jnp.dot(g.astype(BF16), w_ref[...], preferred_element_type=F32) + b_ref[...]
        o_ref[...] = g * jax.nn.sigmoid(z)

    return pl.pallas_call(
        body, name=name, grid=(n_rows // tr,),
        in_specs=[pl.BlockSpec((tr, width), lambda i: (i, 0)), pl.BlockSpec((width, width), lambda i: (0, 0)),
                  pl.BlockSpec((1, width), lambda i: (0, 0))],
        out_specs=pl.BlockSpec((tr, width), lambda i: (i, 0)),
        out_shape=jax.ShapeDtypeStruct((n_rows, width), F32),
        compiler_params=_cparams(("parallel",)),
    )(y, w, b)


def _glu_bwd(y, w, b, dout, *, name):
    n_rows, width = y.shape
    tr = _row_tile(n_rows, 384)

    def body(y_ref, w_ref, b_ref, do_ref, dy_ref, g_ref, dz_ref, db_ref):
        i = pl.program_id(0)
        yv = y_ref[...]
        g, th = _gelu(yv)
        gb = g.astype(BF16)
        z = jnp.dot(gb, w_ref[...], preferred_element_type=F32) + b_ref[...]
        sg = jax.nn.sigmoid(z)
        do = do_ref[...]
        dz = do * g * sg * (1.0 - sg)
        dzb = dz.astype(BF16)
        dg = do * sg + lax.dot_general(dzb, w_ref[...], _DN_NT, preferred_element_type=F32)
        dgelu = 0.5 * (1.0 + th) + 0.5 * yv * (1.0 - th * th) * _GELU_C * (1.0 + 3.0 * _GELU_A * yv * yv)
        dy_ref[...] = dg * dgelu
        g_ref[...] = gb
        dz_ref[...] = dzb
        part = jnp.sum(dz, axis=0, keepdims=True)

        @pl.when(i == 0)
        def _():
            db_ref[...] = part

        @pl.when(i > 0)
        def _():
            db_ref[...] += part

    row = pl.BlockSpec((tr, width), lambda i: (i, 0))
    vec = pl.BlockSpec((1, width), lambda i: (0, 0))
    return pl.pallas_call(
        body, name=name, grid=(n_rows // tr,),
        in_specs=[row, pl.BlockSpec((width, width), lambda i: (0, 0)), vec, row],
        out_specs=[row, row, row, vec],
        out_shape=[jax.ShapeDtypeStruct((n_rows, width), F32), jax.ShapeDtypeStruct((n_rows, width), BF16),
                   jax.ShapeDtypeStruct((n_rows, width), BF16), jax.ShapeDtypeStruct((1, width), F32)],
        compiler_params=_cparams(("arbitrary",)),
    )(y, w, b, dout)


def _loss_head(h, target, *, name):
    n_rows, width = h.shape

    def body(h_ref, t_ref, dh_ref, dhb_ref, loss_ref):
        i = pl.program_id(0)

        @pl.when(i == 0)
        def _():
            dh_ref[...] = jnp.zeros_like(dh_ref)
            dhb_ref[...] = jnp.zeros_like(dhb_ref)
            loss_ref[...] = jnp.zeros_like(loss_ref)

        @pl.when(i > 0)
        def _():
            err = h_ref[...] - t_ref[...]
            dh = err * (1.0 / width)
            dh_ref[...] = dh
            dhb_ref[...] = dh.astype(BF16)
            loss_ref[...] += (0.5 / width) * jnp.sum(err * err, keepdims=True)

    return pl.pallas_call(
        body, name=name, grid=(n_rows // BLOCK,),
        in_specs=[pl.BlockSpec((BLOCK, width), lambda i: (i, 0)),
                  pl.BlockSpec((BLOCK, width), lambda i: (jnp.maximum(i - 1, 0), 0))],
        out_specs=[pl.BlockSpec((BLOCK, width), lambda i: (i, 0)), pl.BlockSpec((BLOCK, width), lambda i: (i, 0)),
                   pl.BlockSpec((1, 1), lambda i: (0, 0))],
        out_shape=[jax.ShapeDtypeStruct((n_rows, width), F32), jax.ShapeDtypeStruct((n_rows, width), BF16),
                   jax.ShapeDtypeStruct((1, 1), F32)],
        compiler_params=_cparams(("arbitrary",)),
    )(h, target)


def _elem_rows(n_rows, n_cols, bytes_per_row_elem):
    lanes = -(-n_cols // 128) * 128
    cap = max(16, (12 * 1024 * 1024) // (lanes * bytes_per_row_elem))
    best = None
    for t in range(16, min(n_rows, cap) + 1, 16):
        if n_rows % t == 0:
            best = t
    return best or n_rows


def _cast_bf16(x, *, name, deps=()):
    n_rows, n_cols = x.shape
    tr = _elem_rows(n_rows, n_cols, 4)

    def body(x_ref, *rest):
        rest[-1][...] = x_ref[...].astype(BF16)

    spec = pl.BlockSpec((tr, n_cols), lambda i: (i, 0))
    return pl.pallas_call(body, name=name, grid=(n_rows // tr,), in_specs=[spec] + [_ANY] * len(deps), out_specs=spec,
                          out_shape=jax.ShapeDtypeStruct(x.shape, BF16), compiler_params=_cparams(("parallel",)))(x, *deps)


def _adamw(w, m, v, parts, *, name):
    n_rows, n_cols = w.shape
    n_parts = parts.shape[0]
    tr = _elem_rows(n_rows, n_cols, 4 * (8 + n_parts))
    c1 = 1.0 / (1.0 - ADAM_B1 ** ADAM_STEP)
    c2 = 1.0 / (1.0 - ADAM_B2 ** ADAM_STEP)

    def body(w_ref, m_ref, v_ref, p_ref, g_ref, d_ref, nm_ref, nv_ref):
        g = p_ref[0].astype(F32)
        for k in range(1, n_parts):
            g = g + p_ref[k].astype(F32)
        nm = ADAM_B1 * m_ref[...] + (1.0 - ADAM_B1) * g
        nv = ADAM_B2 * v_ref[...] + (1.0 - ADAM_B2) * (g * g)
        g_ref[...] = g
        nm_ref[...] = nm
        nv_ref[...] = nv
        d_ref[...] = -ADAM_LR * ((nm * c1) / (jnp.sqrt(nv * c2) + ADAM_EPS) + ADAM_WD * w_ref[...])

    spec = pl.BlockSpec((tr, n_cols), lambda i: (i, 0))
    return pl.pallas_call(
        body, name=name, grid=(n_rows // tr,),
        in_specs=[spec, spec, spec, pl.BlockSpec((n_parts, tr, n_cols), lambda i: (0, i, 0))],
        out_specs=[spec] * 4, out_shape=[jax.ShapeDtypeStruct(w.shape, F32)] * 4,
        compiler_params=_cparams(("parallel",)),
    )(w, m, v, parts)


def _sum_parts(parts, *, name):
    n_parts, n_rows, n_cols = parts.shape
    tr = _elem_rows(n_rows, n_cols, 4 * (1 + n_parts))

    def body(p_ref, o_ref):
        g = p_ref[0].astype(F32)
        for k in range(1, n_parts):
            g = g + p_ref[k].astype(F32)
        o_ref[...] = g

    return pl.pallas_call(
        body, name=name, grid=(n_rows // tr,),
        in_specs=[pl.BlockSpec((n_parts, tr, n_cols), lambda i: (0, i, 0))],
        out_specs=pl.BlockSpec((tr, n_cols), lambda i: (i, 0)),
        out_shape=jax.ShapeDtypeStruct((n_rows, n_cols), F32), compiler_params=_cparams(("parallel",)),
    )(parts)


BIG = ("w_in", "w_glu", "w_out", "w_up", "w_down")
SMALL = ("norm_mix_g", "q_norm_g", "k_norm_g", "attn_sinks", "ssm_lambda_re", "ssm_lambda_im", "ssm_log_step",
         "ssm_b_re", "ssm_b_im", "ssm_c_re", "ssm_c_im", "ssm_d", "b_glu", "attn_out_g", "ssm_out_g", "norm_mlp_g")
_SSM_NAMES = ("ssm_lambda_re", "ssm_lambda_im", "ssm_log_step", "ssm_b_re", "ssm_b_im", "ssm_c_re", "ssm_c_im")


def _divisor(n, cands):
    for c in cands:
        if n % c == 0:
            return c
    return n


def _mm(a, b, mode, name, outs=(F32,), epilogue=_ident, tiles=(), deps=(), blocked=False):
    if mode == "nn":
        m, k = a.shape
        n = b.shape[0] * b.shape[2] if blocked else b.shape[1]
    elif mode == "nt":
        m, k = a.shape
        n = b.shape[1] if blocked else b.shape[0]
    else:
        (k, m), n = a.shape, b.shape[1]
    if mode == "tn":
        tm, tn, tk = _divisor(m, (1024, 512)), _divisor(n, (1024, 512)), k
    elif k <= 2560:
        tm, tn, tk = _row_tile(m, 1408), _divisor(n, (1024, 1280, 512)), k
    elif blocked:
        tm, tn, tk = (m // 2 if m % 32 == 0 else m), _divisor(n, (1024, 512)), b.shape[2]
    else:
        tm, tn, tk = _row_tile(m, 1408), _divisor(n, (1024, 512)), _divisor(k, (1024, 512))
    return _matmul(a, b, mode=mode, tm=tm, tn=tn, tk=tk, outs=list(outs), epilogue=epilogue, tiles=tiles, deps=deps,
                   blocked=blocked, name=name)


def _add_tile(acc, res):
    return (acc + res,)


def _relu_sq(acc):
    r = jnp.maximum(acc, 0.0)
    return r, r * r


def _relu_sq_bwd(acc, r):
    return (acc * (2.0 * r.astype(F32)),)


def _row(v):
    return v.reshape(1, -1)


def _layer_fwd(hres, fetch, sp, l, deps=()):
    tag = f"_l{l}"
    wts = {}
    hb = _rmsnorm_fwd([hres], [_row(sp["norm_mix_g"])], name="norm_mix" + tag, deps=deps)
    wts["w_in"] = fetch("w_in", hb)
    proj, = _mm(hb, wts["w_in"], "nt", "proj" + tag)
    gq, gk, sinks = _row(sp["q_norm_g"]), _row(sp["k_norm_g"]), _row(sp["attn_sinks"])
    o, lse = _attn_fwd(proj, gq, gk, sinks, name="attn_fwd" + tag)
    (wc, cmat, lam8), prep_vjp = jax.vjp(_ssm_prep, *[sp[n] for n in _SSM_NAMES])
    cmat = cmat.astype(BF16)
    y, xs = _ssm_fwd(proj, wc, cmat, lam8, _row(sp["ssm_d"]), name="ssm_fwd" + tag)
    wts["w_glu"] = fetch("w_glu", y)
    s = _glu_fwd(y, wts["w_glu"], _row(sp["b_glu"]), name="glu_fwd" + tag)
    mix = _rmsnorm_fwd([o, s], [_row(sp["attn_out_g"]), _row(sp["ssm_out_g"])], name="norm_out" + tag)
    wts["w_out"] = fetch("w_out", mix)
    hres2, = _mm(mix, wts["w_out"], "nn", "out_proj" + tag, epilogue=_add_tile, tiles=(hres,))
    h2 = _rmsnorm_fwd([hres2], [_row(sp["norm_mlp_g"])], name="norm_mlp" + tag)
    wts["w_up"] = fetch("w_up", h2)
    r, act = _mm(h2, wts["w_up"], "nn", "mlp_up" + tag, outs=(BF16, BF16), epilogue=_relu_sq, blocked=True)
    wts["w_down"] = fetch("w_down", act)
    hres3, = _mm(act, wts["w_down"], "nn", "mlp_down" + tag, epilogue=_add_tile, tiles=(hres2,))
    saved = dict(wts=wts, hres=hres, hb=hb, proj=proj, o=o, lse=lse, wc=wc, cmat=cmat, lam8=lam8, prep_vjp=prep_vjp,
                 y=y, xs=xs, s=s, mix=mix, hres2=hres2, h2=h2, r=r, act=act)
    return hres3, saved


def _layer_bwd(dres, dres_b, sp, sv, l, early_grads, deps=()):
    tag = f"_l{l}"
    wts = sv["wts"]
    gb, gs = {}, {}
    d_up, = _mm(dres_b, wts["w_down"], "nt", "mlp_down_dx" + tag, outs=(BF16,), epilogue=_relu_sq_bwd, tiles=(sv["r"],),
                deps=deps)
    gb["w_down"], = _mm(sv["act"], dres_b, "tn", "mlp_down_dw" + tag, outs=(BF16,))
    gb["w_up"], = _mm(sv["h2"], d_up, "tn", "mlp_up_dw" + tag, outs=(BF16,), blocked=True)
    deps = early_grads(l, "a", {n: gb.pop(n) for n in ("w_up", "w_down")})
    dh2, = _mm(d_up, wts["w_up"], "nt", "mlp_up_dx" + tag, blocked=True, deps=deps)
    (dres2,), (dg,), dres2_b = _rmsnorm_bwd([sv["hres2"]], [_row(sp["norm_mlp_g"])], dh2, dres, name="norm_mlp_bwd" + tag)
    gs["norm_mlp_g"] = dg
    dmix, = _mm(dres2_b, wts["w_out"], "nt", "out_proj_dx" + tag)
    gb["w_out"], = _mm(sv["mix"], dres2_b, "tn", "out_proj_dw" + tag, outs=(BF16,))
    (do, ds), (dga, dgs) = _rmsnorm_bwd([sv["o"], sv["s"]], [_row(sp["attn_out_g"]), _row(sp["ssm_out_g"])], dmix, None,
                                        name="norm_out_bwd" + tag)
    gs["attn_out_g"], gs["ssm_out_g"] = dga, dgs
    dy, g_b, dz_b, db = _glu_bwd(sv["y"], wts["w_glu"], _row(sp["b_glu"]), ds, name="glu_bwd" + tag)
    gs["b_glu"] = db
    gb["w_glu"], = _mm(g_b, dz_b, "tn", "glu_dw" + tag, outs=(BF16,))
    deps = early_grads(l, "b", {n: gb.pop(n) for n in ("w_out", "w_glu")})
    du, dwc, dcmat, dlam8, dd = _ssm_bwd(sv["proj"], sv["xs"], dy, sv["wc"], sv["cmat"], sv["lam8"], _row(sp["ssm_d"]),
                                         name="ssm_bwd" + tag, deps=deps)
    gs["ssm_d"] = dd
    for n, g in zip(_SSM_NAMES, sv["prep_vjp"]((dwc, dcmat, dlam8))):
        gs[n] = g
    dq, dk, dv, dgq, dgk, dsinks = _attn_bwd(sv["proj"], _row(sp["q_norm_g"]), _row(sp["k_norm_g"]), _row(sp["attn_sinks"]),
                                             sv["o"], sv["lse"], do, name="attn_bwd" + tag)
    gs["q_norm_g"], gs["k_norm_g"], gs["attn_sinks"] = dgq, dgk, dsinks
    dproj = _concat_cols([dq, dk, dv, du], name="dproj" + tag)
    gb["w_in"], = _mm(dproj, sv["hb"], "tn", "proj_dw" + tag, outs=(BF16,))
    deps = early_grads(l, "c", {"w_in": gb.pop("w_in")})
    dh, = _mm(dproj, wts["w_in"], "nn", "proj_dx" + tag, deps=deps)
    (dres_in,), (dg,), dres_in_b = _rmsnorm_bwd([sv["hres"]], [_row(sp["norm_mix_g"])], dh, dres2, name="norm_mix_bwd" + tag)
    gs["norm_mix_g"] = dg
    return dres_in, dres_in_b, gs


def _local_step(x, target, meta, sp, weights_for_layer, early_grads, grads_of_layer):
    h = jnp.concatenate([jnp.zeros((PAD, x.shape[1]), F32), meta, x], axis=0)
    saved = []
    for l in range(DEPTH):
        fetch, deps = weights_for_layer(l, h)
        h, sv = _layer_fwd(h, fetch, {n: sp[n][l] for n in SMALL}, l, deps)
        saved.append(sv)
    dh, dh_b, loss = _loss_head(h, target, name="loss_head")
    gsmall = {n: [None] * DEPTH for n in SMALL}
    deps = ()
    for l in reversed(range(DEPTH)):
        dh, dh_b, gs = _layer_bwd(dh, dh_b, {n: sp[n][l] for n in SMALL}, saved[l], l, early_grads, deps)
        deps = grads_of_layer(l, dh)
        for n in SMALL:
            gsmall[n][l] = gs[n].reshape(sp[n][l].shape)
    return loss, dh, gsmall


def _all_gather(x, *, name, deps=()):
    def body(x_ref, *rest):
        out_ref, send_sems, recv_sems, local_sem = rest[len(deps):]
        x, y, c = lax.axis_index("x"), lax.axis_index("y"), lax.axis_index("c")
        me, sibling = (x, y, c), (x, y, 1 - c)
        chips = [(1 - x, y), (x, 1 - y), (1 - x, 1 - y)]

        def slot(px, py, pc):
            return out_ref.at[4 * px + 2 * py + pc]

        def copy(k, block, to, src=None):
            return pltpu.make_async_remote_copy(
                src_ref=slot(*block) if src is None else src, dst_ref=slot(*block),
                send_sem=send_sems.at[k], recv_sem=recv_sems.at[k], device_id=to, device_id_type=_MESH)

        mine = pltpu.make_async_copy(x_ref, slot(*me), local_sem)
        mine.start()
        first = [copy(0, me, sibling, src=x_ref)]
        first += [copy(1 + j, me, (*chip, c), src=x_ref) for j, chip in enumerate(chips)]
        for cp in first:
            cp.start()
        passed = [copy(4 + j, (*chip, c), sibling) for j, chip in enumerate(chips)]
        for j, chip in enumerate(chips):
            copy(1 + j, (*chip, c), me).wait_recv()
            passed[j].start()
        copy(0, sibling, me).wait_recv()
        for j, chip in enumerate(chips):
            copy(4 + j, (*chip, 1 - c), me).wait_recv()
        for cp in first + passed:
            cp.wait_send()
        mine.wait()

    return pl.pallas_call(
        body, name=name, out_shape=jax.ShapeDtypeStruct((N_DEV,) + x.shape, x.dtype),
        in_specs=[_ANY] * (1 + len(deps)), out_specs=_ANY,
        scratch_shapes=[pltpu.SemaphoreType.DMA((7,)), pltpu.SemaphoreType.DMA((7,)), pltpu.SemaphoreType.DMA],
    )(x, *deps)


def _exchange(g, *, name):
    def body(g_ref, r_ref, send_sems, recv_sems, local_sem):
        x, y, c = lax.axis_index("x"), lax.axis_index("y"), lax.axis_index("c")
        me = 4 * x + 2 * y + c
        mine = pltpu.make_async_copy(g_ref.at[me], r_ref.at[me], local_sem)
        mine.start()

        def peer(k):
            px, py, pc = (x + (k >> 2)) % 2, (y + ((k >> 1) & 1)) % 2, (c + (k & 1)) % 2
            return (px, py, pc), 4 * px + 2 * py + pc

        def copy(k, src_block, dst_block):
            to, _ = peer(k)
            return pltpu.make_async_remote_copy(
                src_ref=g_ref.at[src_block], dst_ref=r_ref.at[dst_block],
                send_sem=send_sems.at[k - 1], recv_sem=recv_sems.at[k - 1], device_id=to, device_id_type=_MESH)

        sends = [copy(k, peer(k)[1], me) for k in range(1, N_DEV)]
        for cp in sends:
            cp.start()
        for k in range(1, N_DEV):
            copy(k, me, peer(k)[1]).wait_recv()
        for cp in sends:
            cp.wait_send()
        mine.wait()

    return pl.pallas_call(
        body, name=name, out_shape=jax.ShapeDtypeStruct(g.shape, g.dtype),
        in_specs=[_ANY], out_specs=_ANY,
        scratch_shapes=[pltpu.SemaphoreType.DMA((7,)), pltpu.SemaphoreType.DMA((7,)), pltpu.SemaphoreType.DMA],
    )(g)


_HBM = pl.BlockSpec(memory_space=pltpu.HBM)
_SEM = pl.BlockSpec(memory_space=pltpu.SEMAPHORE)
_EFFECT = pltpu.SideEffectType.DATAFLOW_SIDE_EFFECTING
N_PEERS = N_DEV - 1


def _me_and_peers():
    x, y, c = lax.axis_index("x"), lax.axis_index("y"), lax.axis_index("c")
    peers = []
    for k in range(1, N_DEV):
        px, py, pc = (x + (k >> 2)) % 2, (y + ((k >> 1) & 1)) % 2, (c + (k & 1)) % 2
        peers.append(((px, py, pc), 4 * px + 2 * py + pc))
    return 4 * x + 2 * y + c, peers


def _send_start(srcs, after, *, per_peer, name):
    n_t = len(srcs)
    blks = [s.shape[1:] if per_peer else s.shape for s in srcs]
    lands = [lax.empty((N_DEV,) + b, s.dtype) for b, s in zip(blks, srcs)]

    def body(*refs):
        src_refs, land_refs = refs[:n_t], refs[n_t:2 * n_t]
        send_sems, recv_sems = refs[2 * n_t + 1], refs[2 * n_t + 2]
        token = refs[-1]
        me, peers = _me_and_peers()
        for t in range(n_t):
            for k, (to, idx) in enumerate(peers):
                pltpu.make_async_remote_copy(
                    src_ref=src_refs[t].at[idx] if per_peer else src_refs[t], dst_ref=land_refs[t].at[me],
                    send_sem=send_sems.at[t * N_PEERS + k], recv_sem=recv_sems.at[t * N_PEERS + k],
                    device_id=to, device_id_type=_MESH).start()
        token[...] = jnp.zeros_like(token)

    sems = pltpu.SemaphoreType.DMA((n_t * N_PEERS,))
    outs = pl.pallas_call(
        body, name=name,
        out_shape=(sems, sems, *[pltpu.HBM(s.shape, s.dtype) for s in srcs], *[pltpu.HBM(z.shape, z.dtype) for z in lands],
                   jax.ShapeDtypeStruct((8, 128), F32)),
        in_specs=[_HBM] * (2 * n_t) + [_ANY],
        out_specs=(_SEM, _SEM, *[_HBM] * (2 * n_t), pl.BlockSpec(memory_space=pltpu.VMEM)),
        input_output_aliases={i: 2 + i for i in range(2 * n_t)},
        compiler_params=pltpu.CompilerParams(has_side_effects=_EFFECT),
    )(*[pltpu.with_memory_space_constraint(s, pltpu.HBM) for s in srcs],
      *[pltpu.with_memory_space_constraint(z, pltpu.HBM) for z in lands], after)
    return outs[0], outs[1], list(outs[2:2 + n_t]), list(outs[2 + n_t:2 + 2 * n_t]), outs[-1]


def _send_wait(handles, after, *, per_peer, name):
    send_sems, recv_sems, srcs, lands = handles
    n_t = len(srcs)

    def body(*refs):
        src_refs, land_refs = refs[:n_t], refs[n_t:2 * n_t]
        send_sems, recv_sems = refs[2 * n_t], refs[2 * n_t + 1]
        _, peers = _me_and_peers()
        for t in range(n_t):
            for k, (to, idx) in enumerate(peers):
                cp = pltpu.make_async_remote_copy(
                    src_ref=src_refs[t].at[idx] if per_peer else src_refs[t], dst_ref=land_refs[t].at[idx],
                    send_sem=send_sems.at[t * N_PEERS + k], recv_sem=recv_sems.at[t * N_PEERS + k],
                    device_id=to, device_id_type=_MESH)
                cp.wait_send()
                cp.wait_recv()

    outs = pl.pallas_call(
        body, name=name,
        out_shape=(*[pltpu.HBM(s.shape, s.dtype) for s in srcs], *[pltpu.HBM(z.shape, z.dtype) for z in lands]),
        in_specs=[_HBM] * (2 * n_t) + [_SEM, _SEM, _ANY], out_specs=tuple([_HBM] * (2 * n_t)),
        input_output_aliases={i: i for i in range(2 * n_t)},
        compiler_params=pltpu.CompilerParams(has_side_effects=_EFFECT),
    )(*srcs, *lands, send_sems, recv_sems, after)
    me = 4 * lax.axis_index("x") + 2 * lax.axis_index("y") + lax.axis_index("c")
    filled = []
    for src, land in zip(outs[:n_t], outs[n_t:]):
        own = lax.dynamic_index_in_dim(src, me, 0, keepdims=False) if per_peer else src
        filled.append(lax.dynamic_update_index_in_dim(land, own, me, 0))
    return filled


def _adamw_layer(w, m, v, parts, l, prev, *, name):
    depth, n_rows, n_cols = w.shape
    n_parts = parts.shape[0]
    tr = _elem_rows(n_rows, n_cols, 4 * (8 + n_parts))
    c1 = 1.0 / (1.0 - ADAM_B1 ** ADAM_STEP)
    c2 = 1.0 / (1.0 - ADAM_B2 ** ADAM_STEP)
    n_prev = 0 if prev is None else 4

    def body(w_ref, m_ref, v_ref, p_ref, *rest):
        g_ref, d_ref, nm_ref, nv_ref = rest[n_prev:]
        g = p_ref[0].astype(F32)
        for k in range(1, n_parts):
            g = g + p_ref[k].astype(F32)
        nm = ADAM_B1 * m_ref[...] + (1.0 - ADAM_B1) * g
        nv = ADAM_B2 * v_ref[...] + (1.0 - ADAM_B2) * (g * g)
        g_ref[...] = g
        nm_ref[...] = nm
        nv_ref[...] = nv
        d_ref[...] = -ADAM_LR * ((nm * c1) / (jnp.sqrt(nv * c2) + ADAM_EPS) + ADAM_WD * w_ref[...])

    spec = pl.BlockSpec((None, tr, n_cols), lambda i: (l, i, 0))
    return pl.pallas_call(
        body, name=name, grid=(n_rows // tr,),
        in_specs=[spec, spec, spec, pl.BlockSpec((n_parts, tr, n_cols), lambda i: (0, i, 0))] + [_ANY] * n_prev,
        out_specs=[spec] * 4, out_shape=[jax.ShapeDtypeStruct(w.shape, F32)] * 4,
        input_output_aliases={4 + i: i for i in range(n_prev)},
        compiler_params=_cparams(("parallel",)),
    )(w, m, v, parts, *(prev or ()))


def _concat_cols(parts, *, name):
    n_rows = parts[0].shape[0]
    widths = [p.shape[1] for p in parts]
    tr = _row_tile(n_rows, 1408)

    def body(*refs):
        o_ref, off = refs[-1], 0
        for p_ref, w in zip(refs[:-1], widths):
            o_ref[:, off:off + w] = p_ref[...]
            off += w

    return pl.pallas_call(
        body, name=name, grid=(n_rows // tr,), in_specs=[pl.BlockSpec((tr, w), lambda i: (i, 0)) for w in widths],
        out_specs=pl.BlockSpec((tr, sum(widths)), lambda i: (i, 0)),
        out_shape=jax.ShapeDtypeStruct((n_rows, sum(widths)), parts[0].dtype), compiler_params=_cparams(("parallel",)))(*parts)


def _full_weights(g):
    return {n: v if n == "w_up" else v.reshape(N_DEV * v.shape[1], v.shape[2]) for n, v in g.items()}


def _grad_blocks(gb):
    return [g if n == "w_up" else g.reshape(N_DEV, g.shape[0] // N_DEV, g.shape[1]) for n, g in gb.items()]


_SMALL_ROWS = 1096


def _pack_small(d):
    flat = jnp.concatenate([d[n].reshape(-1) for n in SMALL])
    total = N_DEV * _SMALL_ROWS * 128
    assert flat.shape[0] <= total
    return jnp.pad(flat, (0, total - flat.shape[0])).reshape(N_DEV * _SMALL_ROWS, 128)


def _unpack_small(packed, like):
    flat = packed.reshape(-1)
    out, off = {}, 0
    for n in SMALL:
        size = like[n].size
        out[n] = flat[off:off + size].reshape(like[n].shape)
        off += size
    return out


def kernel(x, meta_tokens, norm_mix_g, w_in, q_norm_g, k_norm_g, attn_sinks, ssm_lambda_re, ssm_lambda_im, ssm_log_step, ssm_b_re, ssm_b_im, ssm_c_re, ssm_c_im, ssm_d, w_glu, b_glu, attn_out_g, ssm_out_g, w_out, norm_mlp_g, w_up, w_down, loss_target, m_meta_tokens, m_norm_mix_g, m_w_in, m_q_norm_g, m_k_norm_g, m_attn_sinks, m_ssm_lambda_re, m_ssm_lambda_im, m_ssm_log_step, m_ssm_b_re, m_ssm_b_im, m_ssm_c_re, m_ssm_c_im, m_ssm_d, m_w_glu, m_b_glu, m_attn_out_g, m_ssm_out_g, m_w_out, m_norm_mlp_g, m_w_up, m_w_down, v_meta_tokens, v_norm_mix_g, v_w_in, v_q_norm_g, v_k_norm_g, v_attn_sinks, v_ssm_lambda_re, v_ssm_lambda_im, v_ssm_log_step, v_ssm_b_re, v_ssm_b_im, v_ssm_c_re, v_ssm_c_im, v_ssm_d, v_w_glu, v_b_glu, v_attn_out_g, v_ssm_out_g, v_w_out, v_norm_mlp_g, v_w_up, v_w_down):
    a = dict(locals())
    order = ("meta_tokens", "norm_mix_g", "w_in", "q_norm_g", "k_norm_g", "attn_sinks", "ssm_lambda_re", "ssm_lambda_im",
             "ssm_log_step", "ssm_b_re", "ssm_b_im", "ssm_c_re", "ssm_c_im", "ssm_d", "w_glu", "b_glu", "attn_out_g",
             "ssm_out_g", "w_out", "norm_mlp_g", "w_up", "w_down")

    for n in ("w_in", "m_w_in", "v_w_in"):
        a[n] = jnp.swapaxes(a[n], 1, 2)
    no_dep = jnp.zeros((8, 128), F32)
    sp = {n: a[n] for n in SMALL}
    gathers, exchanges = {}, {}
    updated = {n: None for n in BIG}
    groups = (("w_in",), ("w_glu", "w_out"), ("w_up",), ("w_down",))

    def start_group(l, gi, after):
        *handles, token = _send_start([wb[n][l] for n in groups[gi]], after, per_peer=False, name=f"gather_start_l{l}_g{gi}")
        gathers[l, gi] = handles
        return token

    def start_gather(l, after):
        for gi in range(len(groups)):
            after = start_group(l, gi, after)
        return after

    meta_all = _all_gather(meta_tokens, name="gather_meta")
    meta = jnp.transpose(meta_all, (1, 0, 2)).reshape(N_META, D_MODEL)
    wb, token0 = {}, meta_all
    for gi, names in enumerate(groups):
        for n in names:
            depth, r, c = a[n].shape
            wb[n] = _cast_bf16(a[n].reshape(depth * r, c), name="cast_" + n, deps=(token0,)).reshape(depth, r, c)
        token0 = start_group(0, gi, token0)

    def weights_for_layer(l, h):
        token = token0 if l == 0 else h
        if l + 1 < DEPTH:
            token = start_gather(l + 1, token)
        got = {}

        def fetch(name, after):
            if name not in got:
                gi = [name in names for names in groups].index(True)
                lands = _send_wait(gathers.pop((l, gi)), after, per_peer=False, name=f"gather_wait_l{l}_g{gi}")
                got.update(_full_weights(dict(zip(groups[gi], lands))))
            return got[name]

        return fetch, (token,)

    def update_layer(l, after):
        for part in ("a", "b", "c"):
            names, handles = exchanges.pop((l, part))
            recv = _send_wait(handles, after, per_peer=True, name=f"exchange_wait_l{l}_{part}")
            for n, parts in zip(names, recv):
                updated[n] = _adamw_layer(a[n], a["m_" + n], a["v_" + n], parts, l, updated[n], name=f"adamw_{n}_l{l}")

    def early_grads(l, part, gb):
        *handles, token = _send_start(_grad_blocks(gb), no_dep, per_peer=True, name=f"exchange_start_l{l}_{part}")
        exchanges[l, part] = (tuple(gb), handles)
        return (token,)

    def grads_of_layer(l, dh):
        if l + 1 < DEPTH:
            update_layer(l + 1, dh)
        return ()

    loss, dh0, gsmall = _local_step(x[0], loss_target[0], meta, sp, weights_for_layer, early_grads, grads_of_layer)
    loss = lax.psum(loss[0, 0], ("x", "y", "c"))
    grad, delta, new_m, new_v = {}, {}, {}, {}

    dmeta = jnp.transpose(dh0[PAD:BLOCK].reshape(N_META, N_DEV, D_MODEL // N_DEV), (1, 0, 2))
    outs = _adamw(meta_tokens, m_meta_tokens, v_meta_tokens, _exchange(dmeta, name="exchange_meta"), name="adamw_meta_tokens")
    grad["meta_tokens"], delta["meta_tokens"], new_m["meta_tokens"], new_v["meta_tokens"] = outs

    packed = _pack_small({n: jnp.stack(gsmall[n]) for n in SMALL}).reshape(N_DEV, _SMALL_ROWS, 128)
    share = _sum_parts(_exchange(packed, name="exchange_small"), name="sum_small")
    total = _all_gather(share, name="gather_small").reshape(1, N_DEV * _SMALL_ROWS, 128)
    gsum = _unpack_small(total, sp)
    for n in SMALL:
        as2d = lambda v: v.reshape(-1, v.shape[-1])
        outs = _adamw(as2d(a[n]), as2d(a["m_" + n]), as2d(a["v_" + n]), as2d(gsum[n])[None], name="adamw_" + n)
        grad[n], delta[n], new_m[n], new_v[n] = [o.reshape(a[n].shape) for o in outs]

    update_layer(0, outs[0])
    for n in BIG:
        grad[n], delta[n], new_m[n], new_v[n] = [jnp.swapaxes(o, 1, 2) if n == "w_in" else o for o in updated[n]]

    return (loss, dh0[BLOCK:][None], *[grad[n] for n in order], *[delta[n] for n in order],
            *[new_m[n] for n in order], *[new_v[n] for n in order])
```

```python
import math

import jax
import jax.numpy as jnp
from jax import lax
from jax.experimental import pallas as pl
from jax.experimental.pallas import tpu as pltpu

F32 = jnp.float32
BF16 = jnp.bfloat16

N_DEV = 8
D_MODEL = 2048
SEQ = 4096
DEPTH = 4
N_META = 16
HEAD_DIM = 64
ATTN_WIDTH = D_MODEL // 2
N_HEADS = ATTN_WIDTH // HEAD_DIM
N_KV_HEADS = N_HEADS // 4
KV_GROUP = N_HEADS // N_KV_HEADS
KV_WIDTH = N_KV_HEADS * HEAD_DIM
SSM_WIDTH = D_MODEL - ATTN_WIDTH
SSM_GROUP_CH = 16
SSM_GROUPS = SSM_WIDTH // SSM_GROUP_CH
SSM_STATE = 64
WINDOW = 128
BLOCK = 128
PAD = BLOCK - N_META
D_FF = 4 * D_MODEL
IN_WIDTH = ATTN_WIDTH + 2 * KV_WIDTH + SSM_WIDTH
NORM_EPS = 1e-6
NEG_INF = -1e30
ADAM_LR = 0.001
ADAM_B1 = 0.9
ADAM_B2 = 0.999
ADAM_EPS = 1e-08
ADAM_WD = 0.01
ADAM_STEP = 10

VMEM_LIMIT = 56 * 1024 * 1024
_MESH = pl.DeviceIdType.MESH
_ANY = pl.BlockSpec(memory_space=pl.ANY)


def _cparams(sem=None):
    return pltpu.CompilerParams(dimension_semantics=sem, vmem_limit_bytes=VMEM_LIMIT)


def _matmul(a, b, *, mode, tm, tn, tk, outs, epilogue, tiles=(), deps=(), blocked=False, name):
    if mode == "nn":
        m, k = a.shape
        if blocked:
            n = b.shape[0] * b.shape[2]
            assert tn == b.shape[2]
            b_spec = pl.BlockSpec((None, tk, tn), lambda i, j, kk: (j, kk, 0))
        else:
            n = b.shape[1]
            b_spec = pl.BlockSpec((tk, tn), lambda i, j, kk: (kk, j))
        a_spec = pl.BlockSpec((tm, tk), lambda i, j, kk: (i, kk))
        dims = (((1,), (0,)), ((), ()))
    elif mode == "nt":
        m, k = a.shape
        if blocked:
            n = b.shape[1]
            assert tk == b.shape[2] and k == b.shape[0] * b.shape[2]
            b_spec = pl.BlockSpec((None, tn, tk), lambda i, j, kk: (kk, j, 0))
        else:
            n = b.shape[0]
            b_spec = pl.BlockSpec((tn, tk), lambda i, j, kk: (j, kk))
        a_spec = pl.BlockSpec((tm, tk), lambda i, j, kk: (i, kk))
        dims = (((1,), (1,)), ((), ()))
    else:
        (k, m), n = a.shape, b.shape[1]
        a_spec = pl.BlockSpec((tk, tm), lambda i, j, kk: (kk, i))
        b_spec = pl.BlockSpec((tk, tn), lambda i, j, kk: (kk, j))
        dims = (((0,), (0,)), ((), ()))
    assert m % tm == 0 and n % tn == 0 and k % tk == 0, (name, m, n, k, tm, tn, tk)
    nk = k // tk
    n_tiles, n_outs, n_deps = len(tiles), len(outs), len(deps)

    def body(a_ref, b_ref, *rest):
        tile_refs = rest[:n_tiles]
        out_refs = rest[n_tiles + n_deps:n_tiles + n_deps + n_outs]

        def product():
            return lax.dot_general(a_ref[...].astype(BF16), b_ref[...].astype(BF16), dims, preferred_element_type=F32)

        def finish(acc):
            res = epilogue(acc, *[r[...] for r in tile_refs])
            for o_ref, o in zip(out_refs, res):
                o_ref[...] = o.astype(o_ref.dtype)

        if nk == 1:
            finish(product())
            return
        acc_ref = rest[-1]
        kk = pl.program_id(2)

        @pl.when(kk == 0)
        def _():
            acc_ref[...] = jnp.zeros_like(acc_ref)

        acc_ref[...] += product()

        @pl.when(kk == nk - 1)
        def _():
            finish(acc_ref[...])

    tile_spec = pl.BlockSpec((tm, tn), lambda i, j, kk: (i, j))
    if mode == "tn" and blocked:
        out_specs = [pl.BlockSpec((None, tm, tn), lambda i, j, kk: (j, i, 0))] * n_outs
        out_shape = [jax.ShapeDtypeStruct((n // tn, m, tn), dt) for dt in outs]
    else:
        out_specs = [tile_spec] * n_outs
        out_shape = [jax.ShapeDtypeStruct((m, n), dt) for dt in outs]
    return pl.pallas_call(
        body, name=name, grid=(m // tm, n // tn, nk),
        in_specs=[a_spec, b_spec] + [tile_spec] * n_tiles + [_ANY] * n_deps,
        out_specs=out_specs, out_shape=out_shape,
        scratch_shapes=[pltpu.VMEM((tm, tn), F32)] if nk > 1 else [],
        compiler_params=_cparams(("parallel", "parallel", "arbitrary")),
    )(a, b, *tiles, *deps)


def _ident(acc):
    return (acc,)


def _row_tile(n_rows, cap):
    best = BLOCK
    for t in range(BLOCK, cap + 1, BLOCK):
        if n_rows % t == 0:
            best = t
    return best


def _rmsnorm_fwd(xs, gs, *, name, deps=()):
    n_rows, width = xs[0].shape
    n = len(xs)
    tr = _row_tile(n_rows, 384)

    def body(*refs):
        o_ref = refs[-1]
        parts = []
        for x_ref, g_ref in zip(refs[:n], refs[n:2 * n]):
            x = x_ref[...]
            r = lax.rsqrt(jnp.mean(x * x, axis=-1, keepdims=True) + NORM_EPS)
            parts.append(x * r * g_ref[...])
        o_ref[...] = (parts[0] if n == 1 else jnp.concatenate(parts, axis=1)).astype(BF16)

    return pl.pallas_call(
        body, name=name, grid=(n_rows // tr,),
        in_specs=[pl.BlockSpec((tr, width), lambda i: (i, 0))] * n + [pl.BlockSpec((1, width), lambda i: (0, 0))] * n
        + [_ANY] * len(deps),
        out_specs=pl.BlockSpec((tr, n * width), lambda i: (i, 0)),
        out_shape=jax.ShapeDtypeStruct((n_rows, n * width), BF16),
        compiler_params=_cparams(("parallel",)),
    )(*xs, *gs, *deps)


def _rmsnorm_bwd(xs, gs, dy, res, *, name):
    n_rows, width = xs[0].shape
    n = len(xs)
    tr = _row_tile(n_rows, 384)
    has_res = res is not None

    def body(*refs):
        x_refs, g_refs, dy_ref = refs[:n], refs[n:2 * n], refs[2 * n]
        res_ref = refs[2 * n + 1] if has_res else None
        outs = refs[2 * n + 1 + int(has_res):]
        dx_refs, dg_refs = outs[:n], outs[n:2 * n]
        i = pl.program_id(0)
        for c in range(n):
            x = x_refs[c][...]
            d = dy_ref[:, c * width:(c + 1) * width]
            r = lax.rsqrt(jnp.mean(x * x, axis=-1, keepdims=True) + NORM_EPS)
            xh = x * r
            gd = d * g_refs[c][...]
            dx = r * (gd - xh * jnp.mean(gd * xh, axis=-1, keepdims=True))
            if has_res:
                dx = dx + res_ref[...]
                outs[2 * n][...] = dx.astype(BF16)
            dx_refs[c][...] = dx
            part = jnp.sum(d * xh, axis=0, keepdims=True)

            @pl.when(i == 0)
            def _():
                dg_refs[c][...] = part

            @pl.when(i > 0)
            def _():
                dg_refs[c][...] += part

    row_spec = pl.BlockSpec((tr, width), lambda i: (i, 0))
    vec_spec = pl.BlockSpec((1, width), lambda i: (0, 0))
    outs = pl.pallas_call(
        body, name=name, grid=(n_rows // tr,),
        in_specs=[row_spec] * n + [vec_spec] * n + [pl.BlockSpec((tr, n * width), lambda i: (i, 0))] + [row_spec] * int(has_res),
        out_specs=[row_spec] * n + [vec_spec] * n + [row_spec] * int(has_res),
        out_shape=[jax.ShapeDtypeStruct((n_rows, width), F32)] * n + [jax.ShapeDtypeStruct((1, width), F32)] * n
        + [jax.ShapeDtypeStruct((n_rows, width), BF16)] * int(has_res),
        compiler_params=_cparams(("arbitrary",)),
    )(*xs, *gs, dy, *([res] if has_res else []))
    if has_res:
        return outs[:n], outs[n:2 * n], outs[2 * n]
    return outs[:n], outs[n:]


_SCALE = 1.0 / math.sqrt(HEAD_DIM)
_DN_NT = (((1,), (1,)), ((), ()))
_DN_TN = (((0,), (0,)), ((), ()))


def _head_norm(x, g):
    r = lax.rsqrt(jnp.mean(x * x, axis=-1, keepdims=True) + NORM_EPS)
    return x * r * g, r


def _attn_bias():
    rows = KV_GROUP * BLOCK
    r = lax.broadcasted_iota(jnp.int32, (rows, 3 * BLOCK), 0)
    j = lax.broadcasted_iota(jnp.int32, (rows, 3 * BLOCK), 1)
    i = r % BLOCK
    is_meta = j < BLOCK
    dist_band = 2 * BLOCK + i - j
    ok = jnp.where(is_meta, j >= PAD, (dist_band >= 0) & (dist_band < WINDOW))
    dist = jnp.where(is_meta, i - j, dist_band).astype(F32)
    slopes = jnp.asarray([2.0 ** (-8.0 * (h + 1) / N_HEADS) for h in range(N_HEADS)], F32).reshape(N_KV_HEADS, KV_GROUP)
    slope_rows = jnp.repeat(slopes, BLOCK, axis=1)[:, :, None]
    bias = jnp.where(ok[None], -slope_rows * dist[None], NEG_INF)
    neg0 = jnp.where(j[:, :BLOCK] <= i[:, :BLOCK], 0.0, NEG_INF)
    return bias.astype(F32), neg0.astype(F32)


def _slope_col(kv):
    g = lax.broadcasted_iota(jnp.int32, (KV_GROUP * BLOCK, 1), 0) // BLOCK
    col = jnp.zeros((KV_GROUP * BLOCK, 1), F32)
    for gi in range(KV_GROUP):
        col = jnp.where(g == gi, 2.0 ** (-8.0 * (kv * KV_GROUP + gi + 1) / N_HEADS), col)
    return col


def _sink_col(sink_ref, kv):
    g = lax.broadcasted_iota(jnp.int32, (KV_GROUP * BLOCK, 1), 0) // BLOCK
    col = jnp.zeros((KV_GROUP * BLOCK, 1), F32)
    for gi in range(KV_GROUP):
        h = kv * KV_GROUP + gi
        col = jnp.where(g == gi, sink_ref[0:1, h:h + 1], col)
    return col


def _stack_heads(x, kv):
    return jnp.concatenate([x[:, (kv * KV_GROUP + g) * HEAD_DIM:(kv * KV_GROUP + g + 1) * HEAD_DIM]
                            for g in range(KV_GROUP)], axis=0)


def _attn_scores(q_ref, k_refs, gq_ref, gk_ref, sink_ref, bias_ref, neg0_ref, kv, n):
    qs = _stack_heads(q_ref[...], kv)
    kcat = jnp.concatenate([r[:, kv * HEAD_DIM:(kv + 1) * HEAD_DIM] for r in k_refs], axis=0)
    qn, rq = _head_norm(qs, gq_ref[...])
    kn, rk = _head_norm(kcat, gk_ref[...])
    s = lax.dot_general((qn * _SCALE).astype(BF16), kn.astype(BF16), _DN_NT, preferred_element_type=F32)
    first = jnp.where(n == 0, 1.0, 0.0)
    second = jnp.where(n == 1, 1.0, 0.0)
    meta = (s[:, :BLOCK] + bias_ref[kv, :, :BLOCK] + _slope_col(kv) * (-float(BLOCK) * n.astype(F32))
            + first * neg0_ref[...])
    in_prev = jnp.where(lax.broadcasted_iota(jnp.int32, (1, 2 * BLOCK), 1) < BLOCK, 1.0, 0.0)
    band = s[:, BLOCK:] + bias_ref[kv, :, BLOCK:] + NEG_INF * (first + second * in_prev)
    s = jnp.concatenate([meta, band], axis=1)
    return qs, kcat, qn, kn, rq, rk, s, _sink_col(sink_ref, kv)


def _attn_specs():
    kq = ATTN_WIDTH // KV_WIDTH
    q_spec = pl.BlockSpec((BLOCK, ATTN_WIDTH), lambda n: (n, 0))
    kv_specs = []
    for col in (kq, kq + 1):
        kv_specs += [pl.BlockSpec((BLOCK, KV_WIDTH), lambda n, col=col: (0, col)),
                     pl.BlockSpec((BLOCK, KV_WIDTH), lambda n, col=col: (jnp.maximum(n - 1, 0), col)),
                     pl.BlockSpec((BLOCK, KV_WIDTH), lambda n, col=col: (n, col))]
    small = [pl.BlockSpec((1, HEAD_DIM), lambda n: (0, 0)), pl.BlockSpec((1, HEAD_DIM), lambda n: (0, 0)),
             pl.BlockSpec((1, N_HEADS), lambda n: (0, 0)),
             pl.BlockSpec((N_KV_HEADS, KV_GROUP * BLOCK, 3 * BLOCK), lambda n: (0, 0, 0)),
             pl.BlockSpec((KV_GROUP * BLOCK, BLOCK), lambda n: (0, 0))]
    return q_spec, kv_specs, small


def _attn_fwd(proj, gq, gk, sinks, *, name):
    n_rows = proj.shape[0]
    q_spec, kv_specs, small = _attn_specs()

    def body(q_ref, k0, k1, k2, v0, v1, v2, gq_ref, gk_ref, sink_ref, bias_ref, neg0_ref, o_ref, lse_ref):
        n = pl.program_id(0)
        o_parts, lse_parts = [], []
        for kv in range(N_KV_HEADS):
            _, _, _, _, _, _, s, sink = _attn_scores(q_ref, (k0, k1, k2), gq_ref, gk_ref, sink_ref, bias_ref, neg0_ref, kv, n)
            vcat = jnp.concatenate([r[:, kv * HEAD_DIM:(kv + 1) * HEAD_DIM] for r in (v0, v1, v2)], axis=0)
            m = jnp.maximum(jnp.max(s, axis=-1, keepdims=True), sink)
            p = jnp.exp(s - m)
            l = jnp.sum(p, axis=-1, keepdims=True) + jnp.exp(sink - m)
            o = jnp.dot(p.astype(BF16), vcat.astype(BF16), preferred_element_type=F32) / l
            lse = m + jnp.log(l)
            o_parts += [o[g * BLOCK:(g + 1) * BLOCK] for g in range(KV_GROUP)]
            lse_parts += [lse[g * BLOCK:(g + 1) * BLOCK] for g in range(KV_GROUP)]
        o_ref[...] = jnp.concatenate(o_parts, axis=1)
        lse_ref[...] = jnp.concatenate(lse_parts, axis=1)

    return pl.pallas_call(
        body, name=name, grid=(n_rows // BLOCK,),
        in_specs=[q_spec] + kv_specs + small,
        out_specs=[pl.BlockSpec((BLOCK, ATTN_WIDTH), lambda n: (n, 0)), pl.BlockSpec((BLOCK, N_HEADS), lambda n: (n, 0))],
        out_shape=[jax.ShapeDtypeStruct((n_rows, ATTN_WIDTH), F32), jax.ShapeDtypeStruct((n_rows, N_HEADS), F32)],
        compiler_params=_cparams(("parallel",)),
    )(proj, proj, proj, proj, proj, proj, proj, gq, gk, sinks, *_attn_bias())


def _attn_bwd(proj, gq, gk, sinks, o, lse, do, *, name):
    n_rows = proj.shape[0]
    q_spec, kv_specs, small = _attn_specs()

    def body(q_ref, k0, k1, k2, v0, v1, v2, gq_ref, gk_ref, sink_ref, bias_ref, neg0_ref, o_ref, lse_ref, do_ref,
             dq_ref, dkb_ref, dvb_ref, dgq_ref, dgk_ref, dsink_ref, dk_ref, dv_ref):
        n = pl.program_id(0)

        @pl.when(n == 0)
        def _():
            dk_ref[...] = jnp.zeros_like(dk_ref)
            dv_ref[...] = jnp.zeros_like(dv_ref)
            dgq_ref[...] = jnp.zeros_like(dgq_ref)
            dgk_ref[...] = jnp.zeros_like(dgk_ref)
            dsink_ref[...] = jnp.zeros_like(dsink_ref)

        dq_parts, dk_parts, dv_parts, dsink_parts = [], [], [], []
        dgq = jnp.zeros((1, HEAD_DIM), F32)
        dgk = jnp.zeros((1, HEAD_DIM), F32)
        for kv in range(N_KV_HEADS):
            qs, kcat, qn, kn, rq, rk, s, sink = _attn_scores(q_ref, (k0, k1, k2), gq_ref, gk_ref, sink_ref, bias_ref, neg0_ref, kv, n)
            vcat = jnp.concatenate([r[:, kv * HEAD_DIM:(kv + 1) * HEAD_DIM] for r in (v0, v1, v2)], axis=0)
            os_ = _stack_heads(o_ref[...], kv)
            dos = _stack_heads(do_ref[...], kv)
            lse = jnp.concatenate([lse_ref[:, kv * KV_GROUP + g:kv * KV_GROUP + g + 1] for g in range(KV_GROUP)], axis=0)
            p = jnp.exp(s - lse)
            delta = jnp.sum(dos * os_, axis=-1, keepdims=True)
            dp = lax.dot_general(dos.astype(BF16), vcat.astype(BF16), _DN_NT, preferred_element_type=F32)
            ds = (p * (dp - delta)) * _SCALE
            dsink_rows = -jnp.exp(sink - lse) * delta
            dsink_parts += [jnp.sum(dsink_rows[g * BLOCK:(g + 1) * BLOCK], axis=0, keepdims=True) for g in range(KV_GROUP)]
            dv_parts.append(lax.dot_general(p.astype(BF16), dos.astype(BF16), _DN_TN, preferred_element_type=F32))
            dsb = ds.astype(BF16)
            dqn = jnp.dot(dsb, kn.astype(BF16), preferred_element_type=F32)
            dkn = lax.dot_general(dsb, qn.astype(BF16), _DN_TN, preferred_element_type=F32)
            qh = qs * rq
            gd = dqn * gq_ref[...]
            dqs = rq * (gd - qh * jnp.mean(gd * qh, axis=-1, keepdims=True))
            dgq = dgq + jnp.sum(dqn * qh, axis=0, keepdims=True)
            kh = kcat * rk
            gdk = dkn * gk_ref[...]
            dk_parts.append(rk * (gdk - kh * jnp.mean(gdk * kh, axis=-1, keepdims=True)))
            dgk = dgk + jnp.sum(dkn * kh, axis=0, keepdims=True)
            dq_parts += [dqs[g * BLOCK:(g + 1) * BLOCK] for g in range(KV_GROUP)]
        dq_ref[...] = jnp.concatenate(dq_parts, axis=1).astype(BF16)
        dkc = jnp.concatenate(dk_parts, axis=1)
        dvc = jnp.concatenate(dv_parts, axis=1)
        prev = pl.multiple_of(jnp.maximum(n - 1, 0) * BLOCK, BLOCK)
        cur = pl.multiple_of(n * BLOCK, BLOCK)
        for acc_ref, val in ((dk_ref, dkc), (dv_ref, dvc)):
            acc_ref[0:BLOCK, :] += val[0:BLOCK]
            acc_ref[pl.ds(prev, BLOCK), :] += val[BLOCK:2 * BLOCK]
            acc_ref[pl.ds(cur, BLOCK), :] += val[2 * BLOCK:3 * BLOCK]
        dgq_ref[...] += dgq
        dgk_ref[...] += dgk
        dsink_ref[...] += jnp.concatenate(dsink_parts, axis=1)

        @pl.when(n == pl.num_programs(0) - 1)
        def _():
            dkb_ref[...] = dk_ref[...].astype(BF16)
            dvb_ref[...] = dv_ref[...].astype(BF16)

    blk = lambda w: pl.BlockSpec((BLOCK, w), lambda n: (n, 0))
    full = lambda r, w: pl.BlockSpec((r, w), lambda n: (0, 0))
    return pl.pallas_call(
        body, name=name, grid=(n_rows // BLOCK,),
        in_specs=[q_spec] + kv_specs + small + [blk(ATTN_WIDTH), blk(N_HEADS), blk(ATTN_WIDTH)],
        out_specs=[blk(ATTN_WIDTH), full(n_rows, KV_WIDTH), full(n_rows, KV_WIDTH),
                   full(1, HEAD_DIM), full(1, HEAD_DIM), full(1, N_HEADS)],
        out_shape=[jax.ShapeDtypeStruct((n_rows, ATTN_WIDTH), BF16), jax.ShapeDtypeStruct((n_rows, KV_WIDTH), BF16),
                   jax.ShapeDtypeStruct((n_rows, KV_WIDTH), BF16), jax.ShapeDtypeStruct((1, HEAD_DIM), F32),
                   jax.ShapeDtypeStruct((1, HEAD_DIM), F32), jax.ShapeDtypeStruct((1, N_HEADS), F32)],
        scratch_shapes=[pltpu.VMEM((n_rows, KV_WIDTH), F32), pltpu.VMEM((n_rows, KV_WIDTH), F32)],
        compiler_params=_cparams(("arbitrary",)),
    )(proj, proj, proj, proj, proj, proj, proj, gq, gk, sinks, *_attn_bias(), o, lse, do)


SSM_LAGS = 8
SLAB_G = 128 // SSM_GROUP_CH
N_SLABS = SSM_GROUPS // SLAB_G
SLAB_STATE = SLAB_G * SSM_STATE
U_COL = (ATTN_WIDTH + 2 * KV_WIDTH) // 128


def _ssm_prep(lam_re, lam_im, log_step, b_re, b_im, c_re, c_im):
    lam = lax.complex(lam_re, lam_im)
    delta = jnp.exp(log_step)[:, None]
    lam_bar = jnp.exp(lam * delta)
    b_t = lax.complex(jnp.swapaxes(b_re, 1, 2), jnp.swapaxes(b_im, 1, 2))
    b_bar = ((lam_bar - 1.0) / lam)[:, None, :] * b_t
    pw = [jnp.ones_like(lam_bar)]
    for _ in range(SSM_LAGS):
        pw.append(pw[-1] * lam_bar)
    w = jnp.stack(pw[:SSM_LAGS])[:, :, None, :] * b_bar[None]
    wri = jnp.stack([jnp.real(w), jnp.imag(w)], axis=3)
    wc = wri.reshape(SSM_LAGS, N_SLABS, SLAB_G * SSM_GROUP_CH, 2 * SSM_STATE)
    wc = jnp.swapaxes(wc, 0, 1).reshape(N_SLABS, SSM_LAGS * 128, 2 * SSM_STATE)
    cri = jnp.stack([c_re, -c_im], axis=2).reshape(N_SLABS, SLAB_G, SSM_GROUP_CH, 2, 1, SSM_STATE)
    eye = jnp.eye(SLAB_G, dtype=F32).reshape(1, SLAB_G, 1, 1, SLAB_G, 1)
    ct = (cri * eye).reshape(N_SLABS, 128, 2 * SLAB_STATE)
    l8 = pw[SSM_LAGS]
    lam8 = jnp.concatenate([jnp.real(l8).reshape(N_SLABS, 1, SLAB_STATE), jnp.imag(l8).reshape(N_SLABS, 1, SLAB_STATE)], axis=2)
    return wc, ct, lam8


def _row_group():
    return (lax.broadcasted_iota(jnp.int32, (SSM_LAGS * 128, 1), 0) // SSM_GROUP_CH) % SLAB_G


def _spread_groups(wc):
    g_of_row = _row_group()
    return jnp.concatenate([jnp.where(g_of_row == g, wc[:, r * SSM_STATE:(r + 1) * SSM_STATE], 0.0)
                            for r in range(2) for g in range(SLAB_G)], axis=1)


def _gather_groups(dw):
    g_of_row = _row_group()
    parts = []
    for r in range(2):
        acc = jnp.zeros((SSM_LAGS * 128, SSM_STATE), F32)
        for g in range(SLAB_G):
            c0 = r * SLAB_STATE + g * SSM_STATE
            acc = acc + jnp.where(g_of_row == g, dw[:, c0:c0 + SSM_STATE], 0.0)
        parts.append(acc)
    return jnp.concatenate(parts, axis=1)


def _lagged(u, up, t_rows):
    ue = jnp.concatenate([up, u], axis=0)
    return jnp.concatenate([ue[SSM_LAGS - tau:SSM_LAGS - tau + t_rows] for tau in range(SSM_LAGS)], axis=1).astype(BF16)


def _ssm_fwd(proj, wc, cmat, lam8, dvec, *, name):
    n_rows = proj.shape[0]
    tt = _row_tile(n_rows, 1408)
    n_t = n_rows // tt
    sw = 2 * SLAB_STATE
    hs = SLAB_STATE

    def body(u_ref, up_ref, wc_ref, c_ref, l_ref, d_ref, y_ref, x_ref, carry_ref, w_ref):
        t = pl.program_id(1)

        @pl.when(t == 0)
        def _():
            carry_ref[...] = jnp.zeros_like(carry_ref)
            w_ref[...] = _spread_groups(wc_ref[...]).astype(BF16)

        u = u_ref[...]
        up = jnp.where(t > 0, up_ref[...], 0.0)
        x_ref[...] = jnp.dot(_lagged(u, up, tt), w_ref[...], preferred_element_type=F32)
        ar = jnp.broadcast_to(l_ref[:, :hs], (8, hs))
        ai = jnp.broadcast_to(l_ref[:, hs:], (8, hs))

        def step(b, c):
            xr, xi = c
            r0 = pl.multiple_of(b * 8, 8)
            w = x_ref[pl.ds(r0, 8), :]
            nr = w[:, :hs] + ar * xr - ai * xi
            ni = w[:, hs:] + ar * xi + ai * xr
            x_ref[pl.ds(r0, 8), :] = jnp.concatenate([nr, ni], axis=1)
            return nr, ni

        xr, xi = lax.fori_loop(0, tt // 8, step, (carry_ref[:, :hs], carry_ref[:, hs:]), unroll=8)
        carry_ref[...] = jnp.concatenate([xr, xi], axis=1)
        y_ref[...] = lax.dot_general(x_ref[...].astype(BF16), c_ref[...], _DN_NT, preferred_element_type=F32) + d_ref[...] * u

    return pl.pallas_call(
        body, name=name, grid=(N_SLABS, n_t),
        in_specs=[pl.BlockSpec((tt, 128), lambda j, t: (t, U_COL + j)),
                  pl.BlockSpec((8, 128), lambda j, t: (jnp.maximum(t * (tt // 8) - 1, 0), U_COL + j)),
                  pl.BlockSpec((None, SSM_LAGS * 128, 2 * SSM_STATE), lambda j, t: (j, 0, 0)),
                  pl.BlockSpec((None, 128, sw), lambda j, t: (j, 0, 0)),
                  pl.BlockSpec((None, 1, sw), lambda j, t: (j, 0, 0)),
                  pl.BlockSpec((1, 128), lambda j, t: (0, j))],
        out_specs=[pl.BlockSpec((tt, 128), lambda j, t: (t, j)), pl.BlockSpec((tt, sw), lambda j, t: (t, j))],
        out_shape=[jax.ShapeDtypeStruct((n_rows, SSM_WIDTH), F32), jax.ShapeDtypeStruct((n_rows, N_SLABS * sw), F32)],
        scratch_shapes=[pltpu.VMEM((8, sw), F32), pltpu.VMEM((SSM_LAGS * 128, sw), BF16)],
        compiler_params=_cparams(("parallel", "arbitrary")),
    )(proj, proj, wc, cmat, lam8, dvec)


def _ssm_bwd(proj, xs, dy, wc, cmat, lam8, dvec, *, name, deps=()):
    n_rows = proj.shape[0]
    tt = _row_tile(n_rows, 704)
    n_t = n_rows // tt
    sw = 2 * SLAB_STATE
    hs = SLAB_STATE

    def body(u_ref, up_ref, x_ref, xp_ref, dy_ref, wc_ref, c_ref, l_ref, d_ref,
             *rest):
        du_ref, dwc_ref, dc_ref, dl_ref, dd_ref, a_ref, carry_ref, head_ref, w_ref, dw_ref = rest[len(deps):]
        t = pl.program_id(1)
        ti = n_t - 1 - t

        @pl.when(t == 0)
        def _():
            w_ref[...] = _spread_groups(wc_ref[...]).astype(BF16)
            carry_ref[...] = jnp.zeros_like(carry_ref)
            head_ref[...] = jnp.zeros_like(head_ref)
            dw_ref[...] = jnp.zeros_like(dw_ref)
            dc_ref[...] = jnp.zeros_like(dc_ref)
            dl_ref[...] = jnp.zeros_like(dl_ref)
            dd_ref[...] = jnp.zeros_like(dd_ref)

        u = u_ref[...]
        up = jnp.where(ti > 0, up_ref[...], 0.0)
        ucat = _lagged(u, up, tt)
        dyv = dy_ref[...]
        dyb = dyv.astype(BF16)
        a_ref[...] = jnp.dot(dyb, c_ref[...], preferred_element_type=F32)
        lr = jnp.broadcast_to(l_ref[:, :hs], (8, hs))
        li = jnp.broadcast_to(l_ref[:, hs:], (8, hs))

        def step(i, c):
            cr, ci = c
            r0 = pl.multiple_of((tt // 8 - 1 - i) * 8, 8)
            g = a_ref[pl.ds(r0, 8), :]
            nr = g[:, :hs] + lr * cr + li * ci
            ni = g[:, hs:] + lr * ci - li * cr
            a_ref[pl.ds(r0, 8), :] = jnp.concatenate([nr, ni], axis=1)
            return nr, ni

        cr, ci = lax.fori_loop(0, tt // 8, step, (carry_ref[:, :hs], carry_ref[:, hs:]), unroll=8)
        carry_ref[...] = jnp.concatenate([cr, ci], axis=1)

        a = a_ref[...]
        xv = x_ref[...]
        xprev = jnp.where(ti > 0, xp_ref[...], 0.0)
        xsh = jnp.concatenate([xprev, xv[:tt - SSM_LAGS]], axis=0)
        a_re, a_im, x_re, x_im = a[:, :hs], a[:, hs:], xsh[:, :hs], xsh[:, hs:]
        dl_ref[...] += jnp.concatenate([jnp.sum(a_re * x_re + a_im * x_im, axis=0, keepdims=True),
                                        jnp.sum(a_im * x_re - a_re * x_im, axis=0, keepdims=True)], axis=1)
        ab = a.astype(BF16)
        dw_ref[...] += lax.dot_general(ucat, ab, _DN_TN, preferred_element_type=F32)
        duc = lax.dot_general(ab, w_ref[...], _DN_NT, preferred_element_type=F32)
        ext = jnp.concatenate([duc, head_ref[...]], axis=0)
        du = d_ref[...] * dyv
        for tau in range(SSM_LAGS):
            du = du + ext[tau:tau + tt, tau * 128:(tau + 1) * 128]
        head_ref[...] = duc[0:8]
        row = ti * tt + lax.broadcasted_iota(jnp.int32, (tt, 128), 0)
        du_ref[...] = jnp.where(row >= PAD, du, 0.0).astype(BF16)
        dd_ref[...] += jnp.sum(dyv * u, axis=0, keepdims=True)
        dc_ref[...] += lax.dot_general(dyb, xv.astype(BF16), _DN_TN, preferred_element_type=F32)

        @pl.when(t == n_t - 1)
        def _():
            dwc_ref[...] = _gather_groups(dw_ref[...])

    rt = lambda t: n_t - 1 - t
    prev8 = lambda t: jnp.maximum(rt(t) * (tt // 8) - 1, 0)
    return pl.pallas_call(
        body, name=name, grid=(N_SLABS, n_t),
        in_specs=[pl.BlockSpec((tt, 128), lambda j, t: (rt(t), U_COL + j)),
                  pl.BlockSpec((8, 128), lambda j, t: (prev8(t), U_COL + j)),
                  pl.BlockSpec((tt, sw), lambda j, t: (rt(t), j)),
                  pl.BlockSpec((8, sw), lambda j, t: (prev8(t), j)),
                  pl.BlockSpec((tt, 128), lambda j, t: (rt(t), j)),
                  pl.BlockSpec((None, SSM_LAGS * 128, 2 * SSM_STATE), lambda j, t: (j, 0, 0)),
                  pl.BlockSpec((None, 128, sw), lambda j, t: (j, 0, 0)),
                  pl.BlockSpec((None, 1, sw), lambda j, t: (j, 0, 0)),
                  pl.BlockSpec((1, 128), lambda j, t: (0, j))] + [_ANY] * len(deps),
        out_specs=[pl.BlockSpec((tt, 128), lambda j, t: (rt(t), j)),
                   pl.BlockSpec((None, SSM_LAGS * 128, 2 * SSM_STATE), lambda j, t: (j, 0, 0)),
                   pl.BlockSpec((None, 128, sw), lambda j, t: (j, 0, 0)),
                   pl.BlockSpec((None, 1, sw), lambda j, t: (j, 0, 0)),
                   pl.BlockSpec((1, 128), lambda j, t: (0, j))],
        out_shape=[jax.ShapeDtypeStruct((n_rows, SSM_WIDTH), BF16),
                   jax.ShapeDtypeStruct((N_SLABS, SSM_LAGS * 128, 2 * SSM_STATE), F32),
                   jax.ShapeDtypeStruct((N_SLABS, 128, sw), F32),
                   jax.ShapeDtypeStruct((N_SLABS, 1, sw), F32),
                   jax.ShapeDtypeStruct((1, SSM_WIDTH), F32)],
        scratch_shapes=[pltpu.VMEM((tt, sw), F32), pltpu.VMEM((8, sw), F32), pltpu.VMEM((8, sw), F32),
                        pltpu.VMEM((SSM_LAGS * 128, sw), BF16), pltpu.VMEM((SSM_LAGS * 128, sw), F32)],
        compiler_params=_cparams(("parallel", "arbitrary")),
    )(proj, proj, xs, xs, dy, wc, cmat, lam8, dvec, *deps)


_GELU_C = math.sqrt(2.0 / math.pi)
_GELU_A = 0.044715


def _gelu(y):
    th = jnp.tanh(_GELU_C * (y + _GELU_A * y * y * y))
    return 0.5 * y * (1.0 + th), th


def _glu_fwd(y, w, b, *, name):
    n_rows, width = y.shape
    tr = _row_tile(n_rows, 384)

    def body(y_ref, w_ref, b_ref, o_ref):
        g, _ = _gelu(y_ref[...])
        z = jnp.dot(g.astype(BF16), w_ref[...], preferred_element_type=F32) + b_ref[...]
        o_ref[...] = g * jax.nn.sigmoid(z)

    return pl.pallas_call(
        body, name=name, grid=(n_rows // tr,),
        in_specs=[pl.BlockSpec((tr, width), lambda i: (i, 0)), pl.BlockSpec((width, width), lambda i: (0, 0)),
                  pl.BlockSpec((1, width), lambda i: (0, 0))],
        out_specs=pl.BlockSpec((tr, width), lambda i: (i, 0)),
        out_shape=jax.ShapeDtypeStruct((n_rows, width), F32),
        compiler_params=_cparams(("parallel",)),
    )(y, w, b)


def _glu_bwd(y, w, b, dout, *, name):
    n_rows, width = y.shape
    tr = _row_tile(n_rows, 384)

    def body(y_ref, w_ref, b_ref, do_ref, dy_ref, g_ref, dz_ref, db_ref):
        i = pl.program_id(0)
        yv = y_ref[...]
        g, th = _gelu(yv)
        gb = g.astype(BF16)
        z = jnp.dot(gb, w_ref[...], preferred_element_type=F32) + b_ref[...]
        sg = jax.nn.sigmoid(z)
        do = do_ref[...]
        dz = do * g * sg * (1.0 - sg)
        dzb = dz.astype(BF16)
        dg = do * sg + lax.dot_general(dzb, w_ref[...], _DN_NT, preferred_element_type=F32)
        dgelu = 0.5 * (1.0 + th) + 0.5 * yv * (1.0 - th * th) * _GELU_C * (1.0 + 3.0 * _GELU_A * yv * yv)
        dy_ref[...] = dg * dgelu
        g_ref[...] = gb
        dz_ref[...] = dzb
        part = jnp.sum(dz, axis=0, keepdims=True)

        @pl.when(i == 0)
        def _():
            db_ref[...] = part

        @pl.when(i > 0)
        def _():
            db_ref[...] += part

    row = pl.BlockSpec((tr, width), lambda i: (i, 0))
    vec = pl.BlockSpec((1, width), lambda i: (0, 0))
    return pl.pallas_call(
        body, name=name, grid=(n_rows // tr,),
        in_specs=[row, pl.BlockSpec((width, width), lambda i: (0, 0)), vec, row],
        out_specs=[row, row, row, vec],
        out_shape=[jax.ShapeDtypeStruct((n_rows, width), F32), jax.ShapeDtypeStruct((n_rows, width), BF16),
                   jax.ShapeDtypeStruct((n_rows, width), BF16), jax.ShapeDtypeStruct((1, width), F32)],
        compiler_params=_cparams(("arbitrary",)),
    )(y, w, b, dout)


def _loss_head(h, target, *, name):
    n_rows, width = h.shape

    def body(h_ref, t_ref, dh_ref, dhb_ref, loss_ref):
        i = pl.program_id(0)

        @pl.when(i == 0)
        def _():
            dh_ref[...] = jnp.zeros_like(dh_ref)
            dhb_ref[...] = jnp.zeros_like(dhb_ref)
            loss_ref[...] = jnp.zeros_like(loss_ref)

        @pl.when(i > 0)
        def _():
            err = h_ref[...] - t_ref[...]
            dh = err * (1.0 / width)
            dh_ref[...] = dh
            dhb_ref[...] = dh.astype(BF16)
            loss_ref[...] += (0.5 / width) * jnp.sum(err * err, keepdims=True)

    return pl.pallas_call(
        body, name=name, grid=(n_rows // BLOCK,),
        in_specs=[pl.BlockSpec((BLOCK, width), lambda i: (i, 0)),
                  pl.BlockSpec((BLOCK, width), lambda i: (jnp.maximum(i - 1, 0), 0))],
        out_specs=[pl.BlockSpec((BLOCK, width), lambda i: (i, 0)), pl.BlockSpec((BLOCK, width), lambda i: (i, 0)),
                   pl.BlockSpec((1, 1), lambda i: (0, 0))],
        out_shape=[jax.ShapeDtypeStruct((n_rows, width), F32), jax.ShapeDtypeStruct((n_rows, width), BF16),
                   jax.ShapeDtypeStruct((1, 1), F32)],
        compiler_params=_cparams(("arbitrary",)),
    )(h, target)


def _elem_rows(n_rows, n_cols, bytes_per_row_elem):
    lanes = -(-n_cols // 128) * 128
    cap = max(16, (12 * 1024 * 1024) // (lanes * bytes_per_row_elem))
    best = None
    for t in range(16, min(n_rows, cap) + 1, 16):
        if n_rows % t == 0:
            best = t
    return best or n_rows


def _cast_bf16(x, *, name, deps=()):
    n_rows, n_cols = x.shape
    tr = _elem_rows(n_rows, n_cols, 4)

    def body(x_ref, *rest):
        rest[-1][...] = x_ref[...].astype(BF16)

    spec = pl.BlockSpec((tr, n_cols), lambda i: (i, 0))
    return pl.pallas_call(body, name=name, grid=(n_rows // tr,), in_specs=[spec] + [_ANY] * len(deps), out_specs=spec,
                          out_shape=jax.ShapeDtypeStruct(x.shape, BF16), compiler_params=_cparams(("parallel",)))(x, *deps)


def _adamw(w, m, v, parts, *, name):
    n_rows, n_cols = w.shape
    n_parts = parts.shape[0]
    tr = _elem_rows(n_rows, n_cols, 4 * (8 + n_parts))
    c1 = 1.0 / (1.0 - ADAM_B1 ** ADAM_STEP)
    c2 = 1.0 / (1.0 - ADAM_B2 ** ADAM_STEP)

    def body(w_ref, m_ref, v_ref, p_ref, g_ref, d_ref, nm_ref, nv_ref):
        g = p_ref[0].astype(F32)
        for k in range(1, n_parts):
            g = g + p_ref[k].astype(F32)
        nm = ADAM_B1 * m_ref[...] + (1.0 - ADAM_B1) * g
        nv = ADAM_B2 * v_ref[...] + (1.0 - ADAM_B2) * (g * g)
        g_ref[...] = g
        nm_ref[...] = nm
        nv_ref[...] = nv
        d_ref[...] = -ADAM_LR * ((nm * c1) / (jnp.sqrt(nv * c2) + ADAM_EPS) + ADAM_WD * w_ref[...])

    spec = pl.BlockSpec((tr, n_cols), lambda i: (i, 0))
    return pl.pallas_call(
        body, name=name, grid=(n_rows // tr,),
        in_specs=[spec, spec, spec, pl.BlockSpec((n_parts, tr, n_cols), lambda i: (0, i, 0))],
        out_specs=[spec] * 4, out_shape=[jax.ShapeDtypeStruct(w.shape, F32)] * 4,
        compiler_params=_cparams(("parallel",)),
    )(w, m, v, parts)


def _sum_parts(parts, *, name):
    n_parts, n_rows, n_cols = parts.shape
    tr = _elem_rows(n_rows, n_cols, 4 * (1 + n_parts))

    def body(p_ref, o_ref):
        g = p_ref[0].astype(F32)
        for k in range(1, n_parts):
            g = g + p_ref[k].astype(F32)
        o_ref[...] = g

    return pl.pallas_call(
        body, name=name, grid=(n_rows // tr,),
        in_specs=[pl.BlockSpec((n_parts, tr, n_cols), lambda i: (0, i, 0))],
        out_specs=pl.BlockSpec((tr, n_cols), lambda i: (i, 0)),
        out_shape=jax.ShapeDtypeStruct((n_rows, n_cols), F32), compiler_params=_cparams(("parallel",)),
    )(parts)


BIG = ("w_in", "w_glu", "w_out", "w_up", "w_down")
SMALL = ("norm_mix_g", "q_norm_g", "k_norm_g", "attn_sinks", "ssm_lambda_re", "ssm_lambda_im", "ssm_log_step",
         "ssm_b_re", "ssm_b_im", "ssm_c_re", "ssm_c_im", "ssm_d", "b_glu", "attn_out_g", "ssm_out_g", "norm_mlp_g")
_SSM_NAMES = ("ssm_lambda_re", "ssm_lambda_im", "ssm_log_step", "ssm_b_re", "ssm_b_im", "ssm_c_re", "ssm_c_im")


def _divisor(n, cands):
    for c in cands:
        if n % c == 0:
            return c
    return n


def _mm(a, b, mode, name, outs=(F32,), epilogue=_ident, tiles=(), deps=(), blocked=False):
    if mode == "nn":
        m, k = a.shape
        n = b.shape[0] * b.shape[2] if blocked else b.shape[1]
    elif mode == "nt":
        m, k = a.shape
        n = b.shape[1] if blocked else b.shape[0]
    else:
        (k, m), n = a.shape, b.shape[1]
    if mode == "tn":
        tm, tn, tk = _divisor(m, (1024, 512)), _divisor(n, (1024, 512)), k
    elif k <= 2560:
        tm, tn, tk = _row_tile(m, 1408), _divisor(n, (1024, 1280, 512)), k
    elif blocked:
        tm, tn, tk = (m // 2 if m % 32 == 0 else m), _divisor(n, (1024, 512)), b.shape[2]
    else:
        tm, tn, tk = _row_tile(m, 1408), _divisor(n, (1024, 512)), _divisor(k, (1024, 512))
    return _matmul(a, b, mode=mode, tm=tm, tn=tn, tk=tk, outs=list(outs), epilogue=epilogue, tiles=tiles, deps=deps,
                   blocked=blocked, name=name)


def _add_tile(acc, res):
    return (acc + res,)


def _relu_sq(acc):
    r = jnp.maximum(acc, 0.0)
    return r, r * r


def _relu_sq_bwd(acc, r):
    return (acc * (2.0 * r.astype(F32)),)


def _row(v):
    return v.reshape(1, -1)


def _layer_fwd(hres, fetch, sp, l, deps=()):
    tag = f"_l{l}"
    wts = {}
    hb = _rmsnorm_fwd([hres], [_row(sp["norm_mix_g"])], name="norm_mix" + tag, deps=deps)
    wts["w_in"] = fetch("w_in", hb)
    proj, = _mm(hb, wts["w_in"], "nt", "proj" + tag)
    gq, gk, sinks = _row(sp["q_norm_g"]), _row(sp["k_norm_g"]), _row(sp["attn_sinks"])
    o, lse = _attn_fwd(proj, gq, gk, sinks, name="attn_fwd" + tag)
    (wc, cmat, lam8), prep_vjp = jax.vjp(_ssm_prep, *[sp[n] for n in _SSM_NAMES])
    cmat = cmat.astype(BF16)
    y, xs = _ssm_fwd(proj, wc, cmat, lam8, _row(sp["ssm_d"]), name="ssm_fwd" + tag)
    wts["w_glu"] = fetch("w_glu", y)
    s = _glu_fwd(y, wts["w_glu"], _row(sp["b_glu"]), name="glu_fwd" + tag)
    mix = _rmsnorm_fwd([o, s], [_row(sp["attn_out_g"]), _row(sp["ssm_out_g"])], name="norm_out" + tag)
    wts["w_out"] = fetch("w_out", mix)
    hres2, = _mm(mix, wts["w_out"], "nn", "out_proj" + tag, epilogue=_add_tile, tiles=(hres,))
    h2 = _rmsnorm_fwd([hres2], [_row(sp["norm_mlp_g"])], name="norm_mlp" + tag)
    wts["w_up"] = fetch("w_up", h2)
    r, act = _mm(h2, wts["w_up"], "nn", "mlp_up" + tag, outs=(BF16, BF16), epilogue=_relu_sq, blocked=True)
    wts["w_down"] = fetch("w_down", act)
    hres3, = _mm(act, wts["w_down"], "nn", "mlp_down" + tag, epilogue=_add_tile, tiles=(hres2,))
    saved = dict(wts=wts, hres=hres, hb=hb, proj=proj, o=o, lse=lse, wc=wc, cmat=cmat, lam8=lam8, prep_vjp=prep_vjp,
                 y=y, xs=xs, s=s, mix=mix, hres2=hres2, h2=h2, r=r, act=act)
    return hres3, saved


def _layer_bwd(dres, dres_b, sp, sv, l, early_grads, deps=()):
    tag = f"_l{l}"
    wts = sv["wts"]
    gb, gs = {}, {}
    d_up, = _mm(dres_b, wts["w_down"], "nt", "mlp_down_dx" + tag, outs=(BF16,), epilogue=_relu_sq_bwd, tiles=(sv["r"],),
                deps=deps)
    gb["w_down"], = _mm(sv["act"], dres_b, "tn", "mlp_down_dw" + tag, outs=(BF16,))
    gb["w_up"], = _mm(sv["h2"], d_up, "tn", "mlp_up_dw" + tag, outs=(BF16,), blocked=True)
    deps = early_grads(l, "a", {n: gb.pop(n) for n in ("w_up", "w_down")})
    dh2, = _mm(d_up, wts["w_up"], "nt", "mlp_up_dx" + tag, blocked=True, deps=deps)
    (dres2,), (dg,), dres2_b = _rmsnorm_bwd([sv["hres2"]], [_row(sp["norm_mlp_g"])], dh2, dres, name="norm_mlp_bwd" + tag)
    gs["norm_mlp_g"] = dg
    dmix, = _mm(dres2_b, wts["w_out"], "nt", "out_proj_dx" + tag)
    gb["w_out"], = _mm(sv["mix"], dres2_b, "tn", "out_proj_dw" + tag, outs=(BF16,))
    (do, ds), (dga, dgs) = _rmsnorm_bwd([sv["o"], sv["s"]], [_row(sp["attn_out_g"]), _row(sp["ssm_out_g"])], dmix, None,
                                        name="norm_out_bwd" + tag)
    gs["attn_out_g"], gs["ssm_out_g"] = dga, dgs
    dy, g_b, dz_b, db = _glu_bwd(sv["y"], wts["w_glu"], _row(sp["b_glu"]), ds, name="glu_bwd" + tag)
    gs["b_glu"] = db
    gb["w_glu"], = _mm(g_b, dz_b, "tn", "glu_dw" + tag, outs=(BF16,))
    deps = early_grads(l, "b", {n: gb.pop(n) for n in ("w_out", "w_glu")})
    du, dwc, dcmat, dlam8, dd = _ssm_bwd(sv["proj"], sv["xs"], dy, sv["wc"], sv["cmat"], sv["lam8"], _row(sp["ssm_d"]),
                                         name="ssm_bwd" + tag, deps=deps)
    gs["ssm_d"] = dd
    for n, g in zip(_SSM_NAMES, sv["prep_vjp"]((dwc, dcmat, dlam8))):
        gs[n] = g
    dq, dk, dv, dgq, dgk, dsinks = _attn_bwd(sv["proj"], _row(sp["q_norm_g"]), _row(sp["k_norm_g"]), _row(sp["attn_sinks"]),
                                             sv["o"], sv["lse"], do, name="attn_bwd" + tag)
    gs["q_norm_g"], gs["k_norm_g"], gs["attn_sinks"] = dgq, dgk, dsinks
    dproj = _concat_cols([dq, dk, dv, du], name="dproj" + tag)
    gb["w_in"], = _mm(dproj, sv["hb"], "tn", "proj_dw" + tag, outs=(BF16,))
    deps = early_grads(l, "c", {"w_in": gb.pop("w_in")})
    dh, = _mm(dproj, wts["w_in"], "nn", "proj_dx" + tag, deps=deps)
    (dres_in,), (dg,), dres_in_b = _rmsnorm_bwd([sv["hres"]], [_row(sp["norm_mix_g"])], dh, dres2, name="norm_mix_bwd" + tag)
    gs["norm_mix_g"] = dg
    return dres_in, dres_in_b, gs


def _local_step(x, target, meta, sp, weights_for_layer, early_grads, grads_of_layer):
    h = jnp.concatenate([jnp.zeros((PAD, x.shape[1]), F32), meta, x], axis=0)
    saved = []
    for l in range(DEPTH):
        fetch, deps = weights_for_layer(l, h)
        h, sv = _layer_fwd(h, fetch, {n: sp[n][l] for n in SMALL}, l, deps)
        saved.append(sv)
    dh, dh_b, loss = _loss_head(h, target, name="loss_head")
    gsmall = {n: [None] * DEPTH for n in SMALL}
    deps = ()
    for l in reversed(range(DEPTH)):
        dh, dh_b, gs = _layer_bwd(dh, dh_b, {n: sp[n][l] for n in SMALL}, saved[l], l, early_grads, deps)
        deps = grads_of_layer(l, dh)
        for n in SMALL:
            gsmall[n][l] = gs[n].reshape(sp[n][l].shape)
    return loss, dh, gsmall


def _all_gather(x, *, name, deps=()):
    def body(x_ref, *rest):
        out_ref, send_sems, recv_sems, local_sem = rest[len(deps):]
        x, y, c = lax.axis_index("x"), lax.axis_index("y"), lax.axis_index("c")
        me, sibling = (x, y, c), (x, y, 1 - c)
        chips = [(1 - x, y), (x, 1 - y), (1 - x, 1 - y)]

        def slot(px, py, pc):
            return out_ref.at[4 * px + 2 * py + pc]

        def copy(k, block, to, src=None):
            return pltpu.make_async_remote_copy(
                src_ref=slot(*block) if src is None else src, dst_ref=slot(*block),
                send_sem=send_sems.at[k], recv_sem=recv_sems.at[k], device_id=to, device_id_type=_MESH)

        mine = pltpu.make_async_copy(x_ref, slot(*me), local_sem)
        mine.start()
        first = [copy(0, me, sibling, src=x_ref)]
        first += [copy(1 + j, me, (*chip, c), src=x_ref) for j, chip in enumerate(chips)]
        for cp in first:
            cp.start()
        passed = [copy(4 + j, (*chip, c), sibling) for j, chip in enumerate(chips)]
        for j, chip in enumerate(chips):
            copy(1 + j, (*chip, c), me).wait_recv()
            passed[j].start()
        copy(0, sibling, me).wait_recv()
        for j, chip in enumerate(chips):
            copy(4 + j, (*chip, 1 - c), me).wait_recv()
        for cp in first + passed:
            cp.wait_send()
        mine.wait()

    return pl.pallas_call(
        body, name=name, out_shape=jax.ShapeDtypeStruct((N_DEV,) + x.shape, x.dtype),
        in_specs=[_ANY] * (1 + len(deps)), out_specs=_ANY,
        scratch_shapes=[pltpu.SemaphoreType.DMA((7,)), pltpu.SemaphoreType.DMA((7,)), pltpu.SemaphoreType.DMA],
    )(x, *deps)


def _exchange(g, *, name):
    def body(g_ref, r_ref, send_sems, recv_sems, local_sem):
        x, y, c = lax.axis_index("x"), lax.axis_index("y"), lax.axis_index("c")
        me = 4 * x + 2 * y + c
        mine = pltpu.make_async_copy(g_ref.at[me], r_ref.at[me], local_sem)
        mine.start()

        def peer(k):
            px, py, pc = (x + (k >> 2)) % 2, (y + ((k >> 1) & 1)) % 2, (c + (k & 1)) % 2
            return (px, py, pc), 4 * px + 2 * py + pc

        def copy(k, src_block, dst_block):
            to, _ = peer(k)
            return pltpu.make_async_remote_copy(
                src_ref=g_ref.at[src_block], dst_ref=r_ref.at[dst_block],
                send_sem=send_sems.at[k - 1], recv_sem=recv_sems.at[k - 1], device_id=to, device_id_type=_MESH)

        sends = [copy(k, peer(k)[1], me) for k in range(1, N_DEV)]
        for cp in sends:
            cp.start()
        for k in range(1, N_DEV):
            copy(k, me, peer(k)[1]).wait_recv()
        for cp in sends:
            cp.wait_send()
        mine.wait()

    return pl.pallas_call(
        body, name=name, out_shape=jax.ShapeDtypeStruct(g.shape, g.dtype),
        in_specs=[_ANY], out_specs=_ANY,
        scratch_shapes=[pltpu.SemaphoreType.DMA((7,)), pltpu.SemaphoreType.DMA((7,)), pltpu.SemaphoreType.DMA],
    )(g)


_HBM = pl.BlockSpec(memory_space=pltpu.HBM)
_SEM = pl.BlockSpec(memory_space=pltpu.SEMAPHORE)
_EFFECT = pltpu.SideEffectType.DATAFLOW_SIDE_EFFECTING
N_PEERS = N_DEV - 1


def _me_and_peers():
    x, y, c = lax.axis_index("x"), lax.axis_index("y"), lax.axis_index("c")
    peers = []
    for k in range(1, N_DEV):
        px, py, pc = (x + (k >> 2)) % 2, (y + ((k >> 1) & 1)) % 2, (c + (k & 1)) % 2
        peers.append(((px, py, pc), 4 * px + 2 * py + pc))
    return 4 * x + 2 * y + c, peers


def _send_start(srcs, after, *, per_peer, name):
    n_t = len(srcs)
    blks = [s.shape[1:] if per_peer else s.shape for s in srcs]
    lands = [lax.empty((N_DEV,) + b, s.dtype) for b, s in zip(blks, srcs)]

    def body(*refs):
        src_refs, land_refs = refs[:n_t], refs[n_t:2 * n_t]
        send_sems, recv_sems = refs[2 * n_t + 1], refs[2 * n_t + 2]
        token = refs[-1]
        me, peers = _me_and_peers()
        for t in range(n_t):
            for k, (to, idx) in enumerate(peers):
                pltpu.make_async_remote_copy(
                    src_ref=src_refs[t].at[idx] if per_peer else src_refs[t], dst_ref=land_refs[t].at[me],
                    send_sem=send_sems.at[t * N_PEERS + k], recv_sem=recv_sems.at[t * N_PEERS + k],
                    device_id=to, device_id_type=_MESH).start()
        token[...] = jnp.zeros_like(token)

    sems = pltpu.SemaphoreType.DMA((n_t * N_PEERS,))
    outs = pl.pallas_call(
        body, name=name,
        out_shape=(sems, sems, *[pltpu.HBM(s.shape, s.dtype) for s in srcs], *[pltpu.HBM(z.shape, z.dtype) for z in lands],
                   jax.ShapeDtypeStruct((8, 128), F32)),
        in_specs=[_HBM] * (2 * n_t) + [_ANY],
        out_specs=(_SEM, _SEM, *[_HBM] * (2 * n_t), pl.BlockSpec(memory_space=pltpu.VMEM)),
        input_output_aliases={i: 2 + i for i in range(2 * n_t)},
        compiler_params=pltpu.CompilerParams(has_side_effects=_EFFECT),
    )(*[pltpu.with_memory_space_constraint(s, pltpu.HBM) for s in srcs],
      *[pltpu.with_memory_space_constraint(z, pltpu.HBM) for z in lands], after)
    return outs[0], outs[1], list(outs[2:2 + n_t]), list(outs[2 + n_t:2 + 2 * n_t]), outs[-1]


def _send_wait(handles, after, *, per_peer, name):
    send_sems, recv_sems, srcs, lands = handles
    n_t = len(srcs)

    def body(*refs):
        src_refs, land_refs = refs[:n_t], refs[n_t:2 * n_t]
        send_sems, recv_sems = refs[2 * n_t], refs[2 * n_t + 1]
        _, peers = _me_and_peers()
        for t in range(n_t):
            for k, (to, idx) in enumerate(peers):
                cp = pltpu.make_async_remote_copy(
                    src_ref=src_refs[t].at[idx] if per_peer else src_refs[t], dst_ref=land_refs[t].at[idx],
                    send_sem=send_sems.at[t * N_PEERS + k], recv_sem=recv_sems.at[t * N_PEERS + k],
                    device_id=to, device_id_type=_MESH)
                cp.wait_send()
                cp.wait_recv()

    outs = pl.pallas_call(
        body, name=name,
        out_shape=(*[pltpu.HBM(s.shape, s.dtype) for s in srcs], *[pltpu.HBM(z.shape, z.dtype) for z in lands]),
        in_specs=[_HBM] * (2 * n_t) + [_SEM, _SEM, _ANY], out_specs=tuple([_HBM] * (2 * n_t)),
        input_output_aliases={i: i for i in range(2 * n_t)},
        compiler_params=pltpu.CompilerParams(has_side_effects=_EFFECT),
    )(*srcs, *lands, send_sems, recv_sems, after)
    me = 4 * lax.axis_index("x") + 2 * lax.axis_index("y") + lax.axis_index("c")
    filled = []
    for src, land in zip(outs[:n_t], outs[n_t:]):
        own = lax.dynamic_index_in_dim(src, me, 0, keepdims=False) if per_peer else src
        filled.append(lax.dynamic_update_index_in_dim(land, own, me, 0))
    return filled


def _adamw_layer(w, m, v, parts, l, prev, *, name):
    depth, n_rows, n_cols = w.shape
    n_parts = parts.shape[0]
    tr = _elem_rows(n_rows, n_cols, 4 * (8 + n_parts))
    c1 = 1.0 / (1.0 - ADAM_B1 ** ADAM_STEP)
    c2 = 1.0 / (1.0 - ADAM_B2 ** ADAM_STEP)
    n_prev = 0 if prev is None else 4

    def body(w_ref, m_ref, v_ref, p_ref, *rest):
        g_ref, d_ref, nm_ref, nv_ref = rest[n_prev:]
        g = p_ref[0].astype(F32)
        for k in range(1, n_parts):
            g = g + p_ref[k].astype(F32)
        nm = ADAM_B1 * m_ref[...] + (1.0 - ADAM_B1) * g
        nv = ADAM_B2 * v_ref[...] + (1.0 - ADAM_B2) * (g * g)
        g_ref[...] = g
        nm_ref[...] = nm
        nv_ref[...] = nv
        d_ref[...] = -ADAM_LR * ((nm * c1) / (jnp.sqrt(nv * c2) + ADAM_EPS) + ADAM_WD * w_ref[...])

    spec = pl.BlockSpec((None, tr, n_cols), lambda i: (l, i, 0))
    return pl.pallas_call(
        body, name=name, grid=(n_rows // tr,),
        in_specs=[spec, spec, spec, pl.BlockSpec((n_parts, tr, n_cols), lambda i: (0, i, 0))] + [_ANY] * n_prev,
        out_specs=[spec] * 4, out_shape=[jax.ShapeDtypeStruct(w.shape, F32)] * 4,
        input_output_aliases={4 + i: i for i in range(n_prev)},
        compiler_params=_cparams(("parallel",)),
    )(w, m, v, parts, *(prev or ()))


def _concat_cols(parts, *, name):
    n_rows = parts[0].shape[0]
    widths = [p.shape[1] for p in parts]
    tr = _row_tile(n_rows, 1408)

    def body(*refs):
        o_ref, off = refs[-1], 0
        for p_ref, w in zip(refs[:-1], widths):
            o_ref[:, off:off + w] = p_ref[...]
            off += w

    return pl.pallas_call(
        body, name=name, grid=(n_rows // tr,), in_specs=[pl.BlockSpec((tr, w), lambda i: (i, 0)) for w in widths],
        out_specs=pl.BlockSpec((tr, sum(widths)), lambda i: (i, 0)),
        out_shape=jax.ShapeDtypeStruct((n_rows, sum(widths)), parts[0].dtype), compiler_params=_cparams(("parallel",)))(*parts)


def _full_weights(g):
    return {n: v if n == "w_up" else v.reshape(N_DEV * v.shape[1], v.shape[2]) for n, v in g.items()}


def _grad_blocks(gb):
    return [g if n == "w_up" else g.reshape(N_DEV, g.shape[0] // N_DEV, g.shape[1]) for n, g in gb.items()]


_SMALL_ROWS = 1096


def _pack_small(d):
    flat = jnp.concatenate([d[n].reshape(-1) for n in SMALL])
    total = N_DEV * _SMALL_ROWS * 128
    assert flat.shape[0] <= total
    return jnp.pad(flat, (0, total - flat.shape[0])).reshape(N_DEV * _SMALL_ROWS, 128)


def _unpack_small(packed, like):
    flat = packed.reshape(-1)
    out, off = {}, 0
    for n in SMALL:
        size = like[n].size
        out[n] = flat[off:off + size].reshape(like[n].shape)
        off += size
    return out


def kernel(x, meta_tokens, norm_mix_g, w_in, q_norm_g, k_norm_g, attn_sinks, ssm_lambda_re, ssm_lambda_im, ssm_log_step, ssm_b_re, ssm_b_im, ssm_c_re, ssm_c_im, ssm_d, w_glu, b_glu, attn_out_g, ssm_out_g, w_out, norm_mlp_g, w_up, w_down, loss_target, m_meta_tokens, m_norm_mix_g, m_w_in, m_q_norm_g, m_k_norm_g, m_attn_sinks, m_ssm_lambda_re, m_ssm_lambda_im, m_ssm_log_step, m_ssm_b_re, m_ssm_b_im, m_ssm_c_re, m_ssm_c_im, m_ssm_d, m_w_glu, m_b_glu, m_attn_out_g, m_ssm_out_g, m_w_out, m_norm_mlp_g, m_w_up, m_w_down, v_meta_tokens, v_norm_mix_g, v_w_in, v_q_norm_g, v_k_norm_g, v_attn_sinks, v_ssm_lambda_re, v_ssm_lambda_im, v_ssm_log_step, v_ssm_b_re, v_ssm_b_im, v_ssm_c_re, v_ssm_c_im, v_ssm_d, v_w_glu, v_b_glu, v_attn_out_g, v_ssm_out_g, v_w_out, v_norm_mlp_g, v_w_up, v_w_down):
    a = dict(locals())
    order = ("meta_tokens", "norm_mix_g", "w_in", "q_norm_g", "k_norm_g", "attn_sinks", "ssm_lambda_re", "ssm_lambda_im",
             "ssm_log_step", "ssm_b_re", "ssm_b_im", "ssm_c_re", "ssm_c_im", "ssm_d", "w_glu", "b_glu", "attn_out_g",
             "ssm_out_g", "w_out", "norm_mlp_g", "w_up", "w_down")

    for n in ("w_in", "m_w_in", "v_w_in"):
        a[n] = jnp.swapaxes(a[n], 1, 2)
    no_dep = jnp.zeros((8, 128), F32)
    sp = {n: a[n] for n in SMALL}
    gathers, exchanges = {}, {}
    updated = {n: None for n in BIG}
    groups = (("w_in",), ("w_glu", "w_out"), ("w_up",), ("w_down",))

    def start_group(l, gi, after):
        *handles, token = _send_start([wb[n][l] for n in groups[gi]], after, per_peer=False, name=f"gather_start_l{l}_g{gi}")
        gathers[l, gi] = handles
        return token

    def start_gather(l, after):
        for gi in range(len(groups)):
            after = start_group(l, gi, after)
        return after

    meta_all = _all_gather(meta_tokens, name="gather_meta")
    meta = jnp.transpose(meta_all, (1, 0, 2)).reshape(N_META, D_MODEL)
    wb, token0 = {}, meta_all
    for gi, names in enumerate(groups):
        for n in names:
            depth, r, c = a[n].shape
            wb[n] = _cast_bf16(a[n].reshape(depth * r, c), name="cast_" + n, deps=(token0,)).reshape(depth, r, c)
        token0 = start_group(0, gi, token0)

    def weights_for_layer(l, h):
        token = token0 if l == 0 else h
        if l + 1 < DEPTH:
            token = start_gather(l + 1, token)
        got = {}

        def fetch(name, after):
            if name not in got:
                gi = [name in names for names in groups].index(True)
                lands = _send_wait(gathers.pop((l, gi)), after, per_peer=False, name=f"gather_wait_l{l}_g{gi}")
                got.update(_full_weights(dict(zip(groups[gi], lands))))
            return got[name]

        return fetch, (token,)

    def update_layer(l, after):
        for part in ("a", "b", "c"):
            names, handles = exchanges.pop((l, part))
            recv = _send_wait(handles, after, per_peer=True, name=f"exchange_wait_l{l}_{part}")
            for n, parts in zip(names, recv):
                updated[n] = _adamw_layer(a[n], a["m_" + n], a["v_" + n], parts, l, updated[n], name=f"adamw_{n}_l{l}")

    def early_grads(l, part, gb):
        *handles, token = _send_start(_grad_blocks(gb), no_dep, per_peer=True, name=f"exchange_start_l{l}_{part}")
        exchanges[l, part] = (tuple(gb), handles)
        return (token,)

    def grads_of_layer(l, dh):
        if l + 1 < DEPTH:
            update_layer(l + 1, dh)
        return ()

    loss, dh0, gsmall = _local_step(x[0], loss_target[0], meta, sp, weights_for_layer, early_grads, grads_of_layer)
    loss = lax.psum(loss[0, 0], ("x", "y", "c"))
    grad, delta, new_m, new_v = {}, {}, {}, {}

    dmeta = jnp.transpose(dh0[PAD:BLOCK].reshape(N_META, N_DEV, D_MODEL // N_DEV), (1, 0, 2))
    outs = _adamw(meta_tokens, m_meta_tokens, v_meta_tokens, _exchange(dmeta, name="exchange_meta"), name="adamw_meta_tokens")
    grad["meta_tokens"], delta["meta_tokens"], new_m["meta_tokens"], new_v["meta_tokens"] = outs

    packed = _pack_small({n: jnp.stack(gsmall[n]) for n in SMALL}).reshape(N_DEV, _SMALL_ROWS, 128)
    *small_handles, token = _send_start([packed], no_dep, per_peer=True, name="exchange_start_small")
    update_layer(0, token)
    recv, = _send_wait(small_handles, updated["w_in"][0], per_peer=True, name="exchange_wait_small")
    share = _sum_parts(recv, name="sum_small")
    total = _all_gather(share, name="gather_small").reshape(1, N_DEV * _SMALL_ROWS, 128)
    gsum = _unpack_small(total, sp)
    for n in SMALL:
        as2d = lambda v: v.reshape(-1, v.shape[-1])
        outs = _adamw(as2d(a[n]), as2d(a["m_" + n]), as2d(a["v_" + n]), as2d(gsum[n])[None], name="adamw_" + n)
        grad[n], delta[n], new_m[n], new_v[n] = [o.reshape(a[n].shape) for o in outs]

    for n in BIG:
        grad[n], delta[n], new_m[n], new_v[n] = [jnp.swapaxes(o, 1, 2) if n == "w_in" else o for o in updated[n]]

    return (loss, dh0[BLOCK:][None], *[grad[n] for n in order], *[delta[n] for n in order],
            *[new_m[n] for n in order], *[new_v[n] for n in order])
```
